```python
import math
import jax, jax.numpy as jnp
from jax import lax
import numpy as np

D_MODEL = 4096
BATCH = 4
SEQ = 2048
DEPTH = 1
DEC_BATCH = 16
DEC_SEQ = 64
PAST_LEN = 1024

CHUNK = 64
MIX_WIDTH = D_MODEL
M_INNER = MIX_WIDTH // 2
M_HEADDIM = 64
M_HEADS = M_INNER // M_HEADDIM
M_GROUPS = 8
M_DSTATE = 128
CONV_W = 4
CONV_DIM = M_INNER + 2 * M_GROUPS * M_DSTATE
R_WIDTH = MIX_WIDTH - M_INNER
R_HEADSIZE = 64
R_HEADS = R_WIDTH // R_HEADSIZE
R_LORA_W = 96
R_LORA_A = 96
R_LORA_G = 256
R_COLS = 3 * R_WIDTH + R_LORA_W + R_LORA_A + R_LORA_G
IN_COLS = M_INNER + CONV_DIM + M_HEADS + R_COLS
E_GROUPS = 4
E_PER_GROUP = 8
N_EXPERTS = E_GROUPS * E_PER_GROUP
TOP_K = 2
D_EXPERT = D_MODEL // 4
MOE_BLOCK = 128
NORM_EPS = 1e-6
M_NORM_EPS = 1e-5
R_LN_EPS = 64e-5

kernel_name = 'hymba_ssd_rwkv7_hmoe_adaln_stream_step'


def _rmsnorm(x, g, eps=NORM_EPS):
    xf = x.astype(jnp.float32)
    y = xf * lax.rsqrt(jnp.mean(xf * xf, axis=-1, keepdims=True) + eps)
    return (y * g.astype(jnp.float32)).astype(x.dtype)


def _ssd_chunked(xh, dt, a, bm, cm, s0):
    f32 = jnp.float32
    b, t, h, p = xh.shape
    g, n = bm.shape[2], bm.shape[3]
    j = h // g
    l = min(CHUNK, t)
    nc = t // l
    da = (dt * a).reshape(b, nc, l, g, j)
    xdt = (xh.astype(f32) * dt[..., None]).reshape(b, nc, l, g, j, p)
    bc = bm.astype(f32).reshape(b, nc, l, g, n)
    cc = cm.astype(f32).reshape(b, nc, l, g, n)
    cum = jnp.cumsum(da, axis=2)
    causal = jnp.tril(jnp.ones((l, l), dtype=bool))
    seg = cum[:, :, :, None] - cum[:, :, None, :]
    decay = jnp.exp(jnp.where(causal[None, None, :, :, None, None], seg, -jnp.inf))
    scores = jnp.einsum('bclgn,bcsgn->bclsg', cc, bc)
    y_diag = jnp.einsum('bclsgj,bcsgjp->bclgjp', scores[..., None] * decay, xdt)
    to_end = jnp.exp(cum[:, :, -1:] - cum)
    chunk_states = jnp.einsum('bclgn,bclgjp->bcgjpn', bc, xdt * to_end[..., None])
    chunk_decay = jnp.exp(cum[:, :, -1])

    def step(s, inp):
        dec, st = inp
        return s * dec[..., None, None] + st, s

    s_init = s0.astype(f32).reshape(b, g, j, p, n)
    s_fin, s_in = lax.scan(step, s_init, (jnp.moveaxis(chunk_decay, 1, 0), jnp.moveaxis(chunk_states, 1, 0)))
    s_in = jnp.moveaxis(s_in, 0, 1)
    y_off = jnp.einsum('bclgn,bcgjpn->bclgjp', cc, s_in) * jnp.exp(cum)[..., None]
    y = (y_diag + y_off).reshape(b, t, h, p)
    return y, s_fin.reshape(b, h, p, n).astype(s0.dtype)


def _mamba2_group(z, xbc, dt_raw, conv_state, ssm_state, conv_w, conv_b, dt_bias, a_log, d_skip, norm_g):
    f32 = jnp.float32
    bsz, t, _ = xbc.shape
    full = jnp.concatenate([conv_state.astype(xbc.dtype), xbc], axis=1)
    conv = conv_b + full[:, 0:t] * conv_w[0]
    for tap in range(1, CONV_W):
        conv = conv + full[:, tap:tap + t] * conv_w[tap]
    new_conv = full[:, t:]
    u = jax.nn.silu(conv)
    nb = M_GROUPS * M_DSTATE
    xh = u[..., :M_INNER].reshape(bsz, t, M_HEADS, M_HEADDIM)
    bm = u[..., M_INNER:M_INNER + nb].reshape(bsz, t, M_GROUPS, M_DSTATE)
    cm = u[..., M_INNER + nb:].reshape(bsz, t, M_GROUPS, M_DSTATE)
    dt = jax.nn.softplus(dt_raw.astype(f32) + dt_bias.astype(f32))
    a = -jnp.exp(a_log.astype(f32))
    y, new_ssm = _ssd_chunked(xh, dt, a, bm, cm, ssm_state)
    y = y + xh.astype(f32) * d_skip.astype(f32)[:, None]
    y = y.reshape(bsz, t, M_INNER) * jax.nn.silu(z.astype(f32))
    yg = y.reshape(bsz, t, M_GROUPS, M_INNER // M_GROUPS)
    yg = yg * lax.rsqrt(jnp.mean(yg * yg, axis=-1, keepdims=True) + M_NORM_EPS)
    out = yg.reshape(bsz, t, M_INNER) * norm_g.astype(f32)
    return out.astype(z.dtype), new_conv, new_ssm


def _wkv7_scan(r, w, k, v, a, b, s0):
    def step(s, inp):
        r_t, w_t, k_t, v_t, a_t, b_t = inp
        sa = jnp.einsum('zhvk,zhk->zhv', s, a_t)
        s = s * w_t[:, :, None, :] + sa[..., None] * b_t[:, :, None, :] + v_t[..., None] * k_t[:, :, None, :]
        return s, jnp.einsum('zhvk,zhk->zhv', s, r_t)

    seq_first = tuple(jnp.moveaxis(u, 1, 0) for u in (r, w, k, v, a, b))
    s_fin, ys = lax.scan(step, s0.astype(jnp.float32), seq_first)
    return jnp.moveaxis(ys, 0, 1), s_fin.astype(s0.dtype)


def _rwkv7_group(pr, shift_state, wkv_state, shift_mu, w0, w_up, a0, a_up, g_up, k_k, k_a, r_k, ln_g, ln_b):
    f32 = jnp.float32
    bsz, t, _ = pr.shape
    prev = jnp.concatenate([shift_state.astype(pr.dtype), pr[:, :-1]], axis=1)
    new_shift = pr[:, -1:]
    xs = pr + (prev - pr) * shift_mu
    r = xs[..., :R_WIDTH]
    k = xs[..., R_WIDTH:2 * R_WIDTH]
    v = xs[..., 2 * R_WIDTH:3 * R_WIDTH]
    o = 3 * R_WIDTH
    wd = xs[..., o:o + R_LORA_W]
    o = o + R_LORA_W
    ad = xs[..., o:o + R_LORA_A]
    o = o + R_LORA_A
    gd = xs[..., o:o + R_LORA_G]

    def hd(u):
        return u.astype(f32).reshape(u.shape[:-1] + (R_HEADS, R_HEADSIZE))

    w_raw = -jax.nn.softplus(-(w0 + jnp.tanh(wd) @ w_up).astype(f32)) - 0.5
    decay = jnp.exp(-jnp.exp(w_raw))
    a = jax.nn.sigmoid((a0 + ad @ a_up).astype(f32))
    g = jax.nn.sigmoid(gd) @ g_up
    rh, vh, ah, wh = hd(r), hd(v), hd(a), hd(decay)
    kk = hd(k * k_k)
    kk = kk * lax.rsqrt(jnp.maximum(jnp.sum(kk * kk, axis=-1, keepdims=True), 1e-24))
    kh = hd(k) * (1.0 + (ah - 1.0) * hd(k_a))
    y, new_wkv = _wkv7_scan(rh, wh, kh, vh, -kk, kk * ah, wkv_state)
    mu = jnp.mean(y, axis=-1, keepdims=True)
    var = jnp.mean(jnp.square(y - mu), axis=-1, keepdims=True)
    yn = ((y - mu) * lax.rsqrt(var + R_LN_EPS)).reshape(bsz, t, R_WIDTH) * ln_g.astype(f32) + ln_b.astype(f32)
    bonus = (jnp.sum(rh * kh * r_k.astype(f32), axis=-1, keepdims=True) * vh).reshape(bsz, t, R_WIDTH)
    out = (yn + bonus) * g.astype(f32)
    return out.astype(pr.dtype), new_shift, new_wkv


def _hier_moe(h, w_grp, b_grp, w_erouter, b_erouter, e_gate, e_up, e_down):
    f32 = jnp.float32
    bsz, t, d = h.shape
    xs = h.reshape(-1, d)
    n = xs.shape[0]
    glog = (xs @ w_grp + b_grp).astype(f32)
    gprob = jax.nn.softmax(glog, axis=-1)
    gsel = jnp.argmax(glog, axis=-1)
    elog = (xs @ w_erouter + b_erouter).astype(f32).reshape(n, E_GROUPS, E_PER_GROUP)
    elog_g = jnp.take_along_axis(elog, gsel[:, None, None], axis=1)[:, 0]
    eprob = jax.nn.softmax(elog_g, axis=-1)
    topv, topi = lax.top_k(eprob, TOP_K)
    pg = jnp.take_along_axis(gprob, gsel[:, None], axis=1)
    wts = topv / jnp.sum(topv, axis=-1, keepdims=True) * pg
    eid = (gsel[:, None] * E_PER_GROUP + topi).astype(jnp.int32)

    m = n * TOP_K
    e_flat = eid.reshape(m)
    tok = jnp.repeat(jnp.arange(n, dtype=jnp.int32), TOP_K)
    w_flat = wts.reshape(m)
    order = jnp.argsort(e_flat)
    e_sorted = e_flat[order]
    counts = jnp.bincount(e_flat, length=N_EXPERTS).astype(jnp.int32)
    padded = (counts + MOE_BLOCK - 1) // MOE_BLOCK * MOE_BLOCK
    start = jnp.cumsum(counts) - counts
    pend = jnp.cumsum(padded)
    pstart = pend - padded
    dest = pstart[e_sorted] + jnp.arange(m, dtype=jnp.int32) - start[e_sorted]
    n_blocks = -(-(m + N_EXPERTS * (MOE_BLOCK - 1)) // MOE_BLOCK)
    rows = n_blocks * MOE_BLOCK
    row_tok = jnp.full((rows,), n, dtype=jnp.int32).at[dest].set(tok[order])
    row_w = jnp.zeros((rows,), h.dtype).at[dest].set(w_flat[order].astype(h.dtype))
    blk_e = jnp.minimum(jnp.searchsorted(pend, jnp.arange(n_blocks, dtype=jnp.int32) * MOE_BLOCK, side='right'), N_EXPERTS - 1)
    xpad = jnp.concatenate([xs, jnp.zeros((1, d), xs.dtype)], axis=0)
    xb = xpad[row_tok].reshape(n_blocks, MOE_BLOCK, d)

    def expert_block(args):
        xblk, e = args
        hid = jax.nn.silu(xblk @ e_gate[e]) * (xblk @ e_up[e])
        return hid @ e_down[e]

    yb = lax.map(expert_block, (xb, blk_e)).reshape(rows, d)
    out = jnp.zeros((n + 1, d), h.dtype).at[row_tok].add(yb * row_w[:, None])[:n]
    return out.reshape(bsz, t, d)


def _layer(x, c, conv_s, ssm_s, shift_s, wkv_s, norm1_g, w_mod, b_mod, w_in, conv_w, conv_b, dt_bias, a_log,
           d_skip, m_norm_g, shift_mu, w0, w_up, a0, a_up, g_up, k_k, k_a, r_k, ln_x_g, ln_x_b, w_out,
           norm2_g, w_grp, b_grp, w_erouter, b_erouter, e_gate, e_up, e_down):
    mod = (jax.nn.silu(c) @ w_mod + b_mod)[:, None, :]
    shift1, scale1, gate1, shift2, scale2, gate2 = jnp.split(mod, 6, axis=-1)
    h = _rmsnorm(x, norm1_g) * (1.0 + scale1) + shift1
    proj = h @ w_in
    o1 = M_INNER
    o2 = o1 + CONV_DIM
    o3 = o2 + M_HEADS
    ym, new_conv, new_ssm = _mamba2_group(proj[..., :o1], proj[..., o1:o2], proj[..., o2:o3], conv_s, ssm_s,
                                          conv_w, conv_b, dt_bias, a_log, d_skip, m_norm_g)
    yr, new_shift, new_wkv = _rwkv7_group(proj[..., o3:], shift_s, wkv_s, shift_mu, w0, w_up, a0, a_up, g_up,
                                          k_k, k_a, r_k, ln_x_g, ln_x_b)
    x = x + gate1 * (jnp.concatenate([ym, yr], axis=-1) @ w_out)
    h2 = _rmsnorm(x, norm2_g) * (1.0 + scale2) + shift2
    x = x + gate2 * _hier_moe(h2, w_grp, b_grp, w_erouter, b_erouter, e_gate, e_up, e_down)
    return x, new_conv, new_ssm, new_shift, new_wkv


def setup_inputs(seed: int = 0) -> dict:
    key = jax.random.key(seed)
    ks = iter(jax.random.split(key, 48))
    f32 = jnp.float32
    L, D = DEPTH, D_MODEL

    def nrm(shape, scale=1.0):
        return scale * jax.random.normal(next(ks), shape, f32)

    def gain(shape):
        return 1.0 + nrm(shape, 0.02)

    x_prompt = nrm((BATCH, SEQ, D))
    x_sample = nrm((DEC_BATCH, DEC_SEQ, D))
    state_conv = nrm((L, DEC_BATCH, CONV_W - 1, CONV_DIM))
    state_ssm = nrm((L, DEC_BATCH, M_HEADS, M_HEADDIM, M_DSTATE), 0.1)
    state_shift = nrm((L, DEC_BATCH, 1, R_COLS))
    state_wkv = nrm((L, DEC_BATCH, R_HEADS, R_HEADSIZE, R_HEADSIZE), 0.1)
    c_prompt = nrm((BATCH, D))
    c_sample = nrm((DEC_BATCH, D))
    norm1_g = gain((L, D))
    w_mod = nrm((L, D, 6 * D), 0.5 * D ** -0.5)
    b_mod = nrm((L, 6 * D), 0.01)
    w_in = nrm((L, D, IN_COLS), D ** -0.5)
    conv_w = nrm((L, CONV_W, CONV_DIM), CONV_W ** -0.5)
    conv_b = nrm((L, CONV_DIM), 0.01)
    dt0 = jnp.exp(jax.random.uniform(next(ks), (L, M_HEADS), f32, math.log(1e-3), math.log(1e-1)))
    dt_bias = dt0 + jnp.log(-jnp.expm1(-dt0))
    a_log = jnp.log(jax.random.uniform(next(ks), (L, M_HEADS), f32, 1.0, 16.0))
    d_skip = 1.0 + nrm((L, M_HEADS), 0.1)
    m_norm_g = gain((L, M_INNER))
    shift_mu = jax.random.uniform(next(ks), (L, R_COLS), f32)
    w0 = jax.random.uniform(next(ks), (L, R_WIDTH), f32, -6.0, -1.0)
    w_up = nrm((L, R_LORA_W, R_WIDTH), 0.5 * R_LORA_W ** -0.5)
    a0 = nrm((L, R_WIDTH), 0.1)
    a_up = nrm((L, R_LORA_A, R_WIDTH), 0.5 * R_LORA_A ** -0.5)
    g_up = nrm((L, R_LORA_G, R_WIDTH), R_LORA_G ** -0.5)
    k_k = 0.85 + nrm((L, R_WIDTH), 0.05)
    k_a = 1.0 + nrm((L, R_WIDTH), 0.05)
    r_k = nrm((L, R_HEADS, R_HEADSIZE), 0.1)
    ln_x_g = gain((L, R_WIDTH))
    ln_x_b = nrm((L, R_WIDTH), 0.01)
    w_out = nrm((L, MIX_WIDTH, D), MIX_WIDTH ** -0.5)
    norm2_g = gain((L, D))
    w_grp = nrm((L, D, E_GROUPS), D ** -0.5)
    b_grp = nrm((L, E_GROUPS), 0.01)
    w_erouter = nrm((L, D, N_EXPERTS), D ** -0.5)
    b_erouter = nrm((L, N_EXPERTS), 0.01)
    e_gate = nrm((L, N_EXPERTS, D, D_EXPERT), D ** -0.5)
    e_up = nrm((L, N_EXPERTS, D, D_EXPERT), D ** -0.5)
    e_down = nrm((L, N_EXPERTS, D_EXPERT, D), D_EXPERT ** -0.5)
    final_norm_g = gain((D,))
    return {'x_prompt': x_prompt, 'x_sample': x_sample, 'state_conv': state_conv, 'state_ssm': state_ssm,
            'state_shift': state_shift, 'state_wkv': state_wkv, 'c_prompt': c_prompt, 'c_sample': c_sample,
            'norm1_g': norm1_g, 'w_mod': w_mod, 'b_mod': b_mod, 'w_in': w_in, 'conv_w': conv_w,
            'conv_b': conv_b, 'dt_bias': dt_bias, 'a_log': a_log, 'd_skip': d_skip, 'm_norm_g': m_norm_g,
            'shift_mu': shift_mu, 'w0': w0, 'w_up': w_up, 'a0': a0, 'a_up': a_up, 'g_up': g_up,
            'k_k': k_k, 'k_a': k_a, 'r_k': r_k, 'ln_x_g': ln_x_g, 'ln_x_b': ln_x_b, 'w_out': w_out,
            'norm2_g': norm2_g, 'w_grp': w_grp, 'b_grp': b_grp, 'w_erouter': w_erouter,
            'b_erouter': b_erouter, 'e_gate': e_gate, 'e_up': e_up, 'e_down': e_down,
            'final_norm_g': final_norm_g}


def reference(x_prompt, x_sample, state_conv, state_ssm, state_shift, state_wkv, c_prompt, c_sample,
              norm1_g, w_mod, b_mod, w_in, conv_w, conv_b, dt_bias, a_log, d_skip, m_norm_g, shift_mu,
              w0, w_up, a0, a_up, g_up, k_k, k_a, r_k, ln_x_g, ln_x_b, w_out, norm2_g, w_grp, b_grp,
              w_erouter, b_erouter, e_gate, e_up, e_down, final_norm_g):
    layer_weights = (norm1_g, w_mod, b_mod, w_in, conv_w, conv_b, dt_bias, a_log, d_skip, m_norm_g,
                     shift_mu, w0, w_up, a0, a_up, g_up, k_k, k_a, r_k, ln_x_g, ln_x_b, w_out,
                     norm2_g, w_grp, b_grp, w_erouter, b_erouter, e_gate, e_up, e_down)
    bp = x_prompt.shape[0]
    sdt = x_prompt.dtype
    hp, hs = x_prompt, x_sample
    conv_p, ssm_p, shift_p, wkv_p = [], [], [], []
    conv_s, ssm_s, shift_s, wkv_s = [], [], [], []
    for layer in range(DEPTH):
        lw = tuple(w[layer] for w in layer_weights)
        hp, cp, sp, shp, wp = _layer(hp, c_prompt,
                                     jnp.zeros((bp, CONV_W - 1, CONV_DIM), sdt),
                                     jnp.zeros((bp, M_HEADS, M_HEADDIM, M_DSTATE), sdt),
                                     jnp.zeros((bp, 1, R_COLS), sdt),
                                     jnp.zeros((bp, R_HEADS, R_HEADSIZE, R_HEADSIZE), sdt), *lw)
        hs, cs, ss, shs, ws = _layer(hs, c_sample, state_conv[layer], state_ssm[layer],
                                     state_shift[layer], state_wkv[layer], *lw)
        conv_p.append(cp)
        ssm_p.append(sp)
        shift_p.append(shp)
        wkv_p.append(wp)
        conv_s.append(cs)
        ssm_s.append(ss)
        shift_s.append(shs)
        wkv_s.append(ws)
    y_prompt = _rmsnorm(hp, final_norm_g)
    y_sample = _rmsnorm(hs, final_norm_g)
    return (y_prompt, y_sample,
            jnp.stack(conv_p), jnp.stack(ssm_p), jnp.stack(shift_p), jnp.stack(wkv_p),
            jnp.stack(conv_s), jnp.stack(ssm_s), jnp.stack(shift_s), jnp.stack(wkv_s))
```

```python
import functools
from typing import NamedTuple

import jax
import jax.numpy as jnp
from jax import lax
from jax.experimental import pallas as pl
from jax.experimental.pallas import tpu as pltpu

F32 = jnp.float32
BF16 = jnp.bfloat16
HI = lax.Precision.HIGHEST

LANES = 128
CHUNK = 64
HEAD = 64
PAIR = 2 * HEAD
NORM_EPS = 1e-6
M_NORM_EPS = 1e-5
R_LN_EPS = 64e-5
CONV_W = 4
E_GROUPS = 4
E_PER_GROUP = 8
TOP_K = 2
VMEM_LIMIT = 56 * 1024 * 1024


class Cfg(NamedTuple):
    d: int
    bp: int
    tp: int
    bs: int
    ts: int
    mi: int
    rw: int
    lw: int
    la: int
    lg: int
    nc: int

    @property
    def cpp(self):
        return self.tp // CHUNK

    @property
    def cps(self):
        return self.ts // CHUNK

    @property
    def n_chunks(self):
        return self.bp * self.cpp + self.bs * self.cps

    @property
    def n_seq(self):
        return self.bp + self.bs

    @property
    def n_rows(self):
        return self.n_chunks * CHUNK

    @property
    def groups(self):
        return self.mi // 256

    @property
    def nb(self):
        return self.groups * LANES

    @property
    def cd(self):
        return self.mi + 2 * self.nb

    @property
    def mh(self):
        return self.mi // HEAD

    @property
    def rh(self):
        return self.rw // HEAD

    @property
    def lp(self):
        return 2 * LANES + self.lg

    @property
    def o_r(self):
        return self.mi + self.cd

    @property
    def o_lora(self):
        return self.o_r + 3 * self.rw

    @property
    def o_dt(self):
        return self.o_lora + self.lp


def _seq_of_chunk(c, cfg):
    npc = cfg.bp * cfg.cpp
    return jnp.where(c < npc, c // cfg.cpp, cfg.bp + (c - npc) // cfg.cps)


def _is_first_chunk(c, cfg):
    npc = cfg.bp * cfg.cpp
    return jnp.where(c < npc, c % cfg.cpp == 0, (c - npc) % cfg.cps == 0)


def _silu(x):
    return x * jax.nn.sigmoid(x)


def _nt(a, b, **kw):
    return lax.dot_general(a, b, (((1,), (1,)), ((), ())), preferred_element_type=F32, **kw)


def _tn(a, b, **kw):
    return lax.dot_general(a, b, (((0,), (0,)), ((), ())), preferred_element_type=F32, **kw)


def _params(*sem):
    return pltpu.CompilerParams(dimension_semantics=sem, vmem_limit_bytes=VMEM_LIMIT)


def _mod_kernel(c_ref, w_ref, b_ref, o_ref):
    o_ref[...] = jnp.dot(_silu(c_ref[...]), w_ref[...], preferred_element_type=F32) + b_ref[...]


def _modulation(c_all, w_mod, b_mod):
    s, d = c_all.shape
    cols = w_mod.shape[1]
    tn = 512
    return pl.pallas_call(
        _mod_kernel,
        grid=(cols // tn,),
        in_specs=[pl.BlockSpec((s, d), lambda j: (0, 0)),
                  pl.BlockSpec((d, tn), lambda j: (0, j)),
                  pl.BlockSpec((1, tn), lambda j: (0, j))],
        out_specs=pl.BlockSpec((s, tn), lambda j: (0, j)),
        out_shape=jax.ShapeDtypeStruct((s, cols), F32),
        compiler_params=_params("parallel"),
        name="modulation",
    )(c_all, w_mod, b_mod.reshape(1, cols))


def _modulated_norm(x, g, scale, shift):
    y = x * lax.rsqrt(jnp.mean(x * x, axis=-1, keepdims=True) + NORM_EPS) * g
    return y * (1.0 + scale) + shift


def _inproj_kernel(x_ref, shift_ref, scale_ref, g_ref, w_ref, o_ref, h_scr, *, cfg, tm):
    i = pl.program_id(0)

    @pl.when(pl.program_id(1) == 0)
    def _():
        for s in range(tm // CHUNK):
            seq = _seq_of_chunk(i * (tm // CHUNK) + s, cfg)
            rows = slice(s * CHUNK, (s + 1) * CHUNK)
            h = _modulated_norm(x_ref[rows, :], g_ref[...], scale_ref[pl.ds(seq, 1), :], shift_ref[pl.ds(seq, 1), :])
            h_scr[rows, :] = h.astype(BF16)

    o_ref[...] = jnp.dot(h_scr[...], w_ref[...], preferred_element_type=F32)


def _inproj(x_all, shift, scale, g, w_bf16, cfg, tm=512, tn=1024):
    n, d = x_all.shape
    nc = w_bf16.shape[1]
    s = shift.shape[0]
    return pl.pallas_call(
        functools.partial(_inproj_kernel, cfg=cfg, tm=tm),
        grid=(n // tm, nc // tn),
        in_specs=[pl.BlockSpec((tm, d), lambda i, j: (i, 0)),
                  pl.BlockSpec((s, d), lambda i, j: (0, 0)),
                  pl.BlockSpec((s, d), lambda i, j: (0, 0)),
                  pl.BlockSpec((1, d), lambda i, j: (0, 0)),
                  pl.BlockSpec((d, tn), lambda i, j: (0, j))],
        out_specs=pl.BlockSpec((tm, tn), lambda i, j: (i, j)),
        out_shape=jax.ShapeDtypeStruct((n, nc), F32),
        scratch_shapes=[pltpu.VMEM((tm, d), BF16)],
        compiler_params=_params("parallel", "arbitrary"),
        name="inproj",
    )(x_all, shift, scale, g, w_bf16)


def _mamba_kernel(zx_ref, dt_ref, cst_ref, sst_ref, cw_ref, cb_ref, dtb_ref, alog_ref, dsk_ref, ng_ref, exp_ref,
                  ym_ref, ssm_ref, buf, u_scr, *, cfg):
    c = pl.program_id(0)
    mi, nb, cd = cfg.mi, cfg.nb, cfg.cd
    L = CHUNK

    @pl.when(_is_first_chunk(c, cfg))
    def _():
        buf[0:8, :] = cst_ref[0]
        ssm_ref[0] = sst_ref[0]

    xbc = zx_ref[:, mi:mi + cd]
    buf[8:8 + L, :] = xbc
    conv = (cb_ref[...] + buf[5:5 + L, :] * cw_ref[0:1, :] + buf[6:6 + L, :] * cw_ref[1:2, :]
            + buf[7:7 + L, :] * cw_ref[2:3, :] + xbc * cw_ref[3:4, :])
    buf[0:8, :] = buf[L:L + 8, :]
    u_scr[...] = _silu(conv)

    dt = jax.nn.softplus(dt_ref[...] + dtb_ref[...])
    da = dt * (-jnp.exp(alog_ref[...]))
    row = lax.broadcasted_iota(jnp.int32, (L, 256), 0)
    pos = lax.broadcasted_iota(jnp.int32, (L, 256), 1) % HEAD
    causal = row >= pos
    diag = row == pos
    tri = (lax.broadcasted_iota(jnp.int32, (L, L), 0) >= lax.broadcasted_iota(jnp.int32, (L, L), 1)).astype(F32)
    lane = lax.broadcasted_iota(jnp.int32, (L, PAIR), 1)
    row8 = lax.broadcasted_iota(jnp.int32, (8, 256), 0)
    ones8 = jnp.ones((8, LANES), F32)

    for g in range(cfg.groups):
        cs = slice(g * 256, (g + 1) * 256)
        xh = u_scr[:, cs]
        bg = u_scr[:, mi + g * LANES:mi + (g + 1) * LANES]
        cg = u_scr[:, mi + nb + g * LANES:mi + nb + (g + 1) * LANES]
        eg = exp_ref[:, cs]
        dte = jnp.dot(dt, eg, precision=HI, preferred_element_type=F32)
        dae = jnp.dot(da, eg, precision=HI, preferred_element_type=F32)
        cum = jnp.dot(tri, dae, precision=HI, preferred_element_type=F32)
        cum_row = jnp.sum(jnp.where(diag, cum, 0.0), axis=0, keepdims=True)
        last = cum[L - 1:L, :]
        decay = jnp.exp(jnp.where(causal, cum - cum_row, -jnp.inf))
        xdt = xh * dte
        sc2 = _nt(cg, jnp.concatenate([bg, bg], axis=0))
        ys = []
        for q in range(2):
            ps = slice(q * PAIR, (q + 1) * PAIR)
            xq = xdt[:, ps]
            rhs = jnp.concatenate([jnp.where(lane < HEAD, xq, 0.0), jnp.where(lane >= HEAD, xq, 0.0)], axis=0)
            ys.append(jnp.dot(decay[:, ps] * sc2, rhs, preferred_element_type=F32))
        y = jnp.concatenate(ys, axis=1)
        sg = ssm_ref[0, cs, :]
        y = y + _nt(cg, sg) * jnp.exp(cum)
        new = _tn(xdt * jnp.exp(last - cum), bg)
        dcol = _tn(jnp.where(row8 == 0, jnp.exp(last), 0.0), ones8, precision=HI)
        ssm_ref[0, cs, :] = sg * dcol + new
        y = y + xh * dsk_ref[:, cs]
        y = y * _silu(zx_ref[:, cs])
        y = y * lax.rsqrt(jnp.mean(y * y, axis=-1, keepdims=True) + M_NORM_EPS)
        ym_ref[:, cs] = (y * ng_ref[:, cs]).astype(BF16)


def _mamba(proj, conv_state8, ssm_state, cw, cb, dtb, alog, dsk, ng, expand, cfg):
    n = proj.shape[0]
    mi, cd = cfg.mi, cfg.cd
    nst = ssm_state.shape[-1]
    seq = lambda c: _seq_of_chunk(c, cfg)
    full = lambda a: pl.BlockSpec(a.shape, lambda c: (0,) * a.ndim)
    return pl.pallas_call(
        functools.partial(_mamba_kernel, cfg=cfg),
        grid=(cfg.n_chunks,),
        in_specs=[pl.BlockSpec((CHUNK, mi + cd), lambda c: (c, 0)),
                  pl.BlockSpec((CHUNK, LANES), lambda c: (c, cfg.o_dt // LANES)),
                  pl.BlockSpec((1, 8, cd), lambda c: (seq(c), 0, 0)),
                  pl.BlockSpec((1, mi, nst), lambda c: (seq(c), 0, 0)),
                  full(cw), full(cb), full(dtb), full(alog), full(dsk), full(ng), full(expand)],
        out_specs=[pl.BlockSpec((CHUNK, mi), lambda c: (c, 0)),
                   pl.BlockSpec((1, mi, nst), lambda c: (seq(c), 0, 0))],
        out_shape=[jax.ShapeDtypeStruct((n, mi), BF16),
                   jax.ShapeDtypeStruct((cfg.n_seq, mi, nst), F32)],
        scratch_shapes=[pltpu.VMEM((CHUNK + 8, cd), F32), pltpu.VMEM((CHUNK, cd), F32)],
        compiler_params=_params("arbitrary"),
        name="mamba",
    )(proj, proj, conv_state8, ssm_state, cw, cb, dtb, alog, dsk, ng, expand)


def _seg_sum(x, ob):
    parts = [jnp.dot(x[:, p * PAIR:(p + 1) * PAIR], ob, preferred_element_type=F32) for p in range(x.shape[1] // PAIR)]
    return jnp.concatenate(parts, axis=1)


def _split_hi_lo(x):
    hi = x.astype(BF16)
    lo = (x - hi.astype(F32)).astype(BF16)
    return hi, lo


def _rwkv_kernel(rkv_ref, lora_ref, sh_ref, wkv_in_ref, mu_ref, w0_ref, wup_ref, a0_ref, aup_ref, gup_ref,
                 kk_ref, ka_ref, rk_ref, lng_ref, lnb_ref,
                 yr_ref, wkv_ref, pbuf, lbuf, w_s, a_s, b_s, k_s, r_s, y_s, vt_s, *, cfg):
    c = pl.program_id(0)
    rw = cfg.rw
    L = CHUNK
    npair = rw // PAIR

    @pl.when(_is_first_chunk(c, cfg))
    def _():
        pbuf[0:8, :] = sh_ref[0, :, 0:3 * rw]
        lbuf[0:8, :] = sh_ref[0, :, 3 * rw:]
        wkv_ref[0] = wkv_in_ref[0]

    pr = rkv_ref[...]
    pl_ = lora_ref[...]
    pbuf[8:8 + L, :] = pr
    lbuf[8:8 + L, :] = pl_
    xs = pr + (pbuf[7:7 + L, :] - pr) * mu_ref[:, 0:3 * rw]
    xl = pl_ + (lbuf[7:7 + L, :] - pl_) * mu_ref[:, 3 * rw:]
    pbuf[0:8, :] = pbuf[L:L + 8, :]
    lbuf[0:8, :] = lbuf[L:L + 8, :]

    r = xs[:, 0:rw]
    k = xs[:, rw:2 * rw]
    v = xs[:, 2 * rw:3 * rw]
    wd = xl[:, 0:LANES]
    ad = xl[:, LANES:2 * LANES]
    gd = xl[:, 2 * LANES:]

    lane = lax.broadcasted_iota(jnp.int32, (PAIR, PAIR), 1)
    rowi = lax.broadcasted_iota(jnp.int32, (PAIR, PAIR), 0)
    ob = ((lane // HEAD) == (rowi // HEAD)).astype(F32)

    w_raw = -jax.nn.softplus(-(w0_ref[...] + jnp.dot(jnp.tanh(wd), wup_ref[...], preferred_element_type=F32))) - 0.5
    w_s[...] = jnp.exp(-jnp.exp(w_raw))
    a = jax.nn.sigmoid(a0_ref[...] + jnp.dot(ad, aup_ref[...], preferred_element_type=F32))
    kk = k * kk_ref[...]
    kk = kk * lax.rsqrt(jnp.maximum(_seg_sum(kk * kk, ob), 1e-24))
    kh = k * (1.0 + (a - 1.0) * ka_ref[...])
    a_s[...] = -kk
    b_s[...] = kk * a
    k_s[...] = kh
    r_s[...] = r
    for p in range(npair):
        vt = v[:, p * PAIR:(p + 1) * PAIR].T
        hi = vt.astype(BF16).astype(F32)
        vt_s[p] = jnp.concatenate([hi, vt - hi], axis=1).astype(BF16)

    ob2 = jnp.concatenate([ob, ob], axis=0).astype(BF16)
    sub = lax.broadcasted_iota(jnp.int32, (HEAD, PAIR), 0)
    lane_h = lax.broadcasted_iota(jnp.int32, (HEAD, PAIR), 1)
    diag = sub == (lane_h % HEAD)
    low = lane_h < HEAD

    def reduce_k(x):
        hi, lo = _split_hi_lo(x)
        return jnp.dot(jnp.concatenate([hi, lo], axis=1), ob2, preferred_element_type=F32)

    def steps(blk, carry):
        t0 = pl.multiple_of(blk * 8, 8)
        sels = [((rowi % HEAD) == t0 + i).astype(BF16) for i in range(8)]
        for p in range(npair):
            ps = slice(p * PAIR, (p + 1) * PAIR)
            rows = pl.ds(t0, 8)
            w8, a8, b8, k8, r8 = w_s[rows, ps], a_s[rows, ps], b_s[rows, ps], k_s[rows, ps], r_s[rows, ps]
            s = wkv_ref[0, p]
            ys = []
            for i in range(8):
                sa = reduce_k(s * a8[i:i + 1, :])
                vb = jnp.dot(vt_s[p], sels[i], preferred_element_type=F32)
                vcol = jnp.where(low, vb[0:HEAD, :], vb[HEAD:, :])
                s = s * w8[i:i + 1, :] + sa * b8[i:i + 1, :] + vcol * k8[i:i + 1, :]
                yb = reduce_k(s * r8[i:i + 1, :])
                ys.append(jnp.sum(jnp.where(diag, yb, 0.0), axis=0, keepdims=True))
            wkv_ref[0, p] = s
            y_s[rows, ps] = jnp.concatenate(ys, axis=0)
        return carry

    lax.fori_loop(0, L // 8, steps, 0)

    y = y_s[...]
    mu = _seg_sum(y, ob) * (1.0 / HEAD)
    dlt = y - mu
    var = _seg_sum(dlt * dlt, ob) * (1.0 / HEAD)
    yn = dlt * lax.rsqrt(var + R_LN_EPS) * lng_ref[...] + lnb_ref[...]
    bonus = _seg_sum(r * k_s[...] * rk_ref[...], ob) * v
    g = jnp.dot(jax.nn.sigmoid(gd), gup_ref[...], preferred_element_type=F32)
    yr_ref[...] = ((yn + bonus) * g).astype(BF16)


def _rwkv(proj, shift8, wkv_pairs, mu, w0, wup, a0, aup, gup, kk, ka, rk, lng, lnb, cfg):
    n = proj.shape[0]
    rw, lp = cfg.rw, cfg.lp
    npair = rw // PAIR
    seq = lambda c: _seq_of_chunk(c, cfg)
    full = lambda a: pl.BlockSpec(a.shape, lambda c: (0,) * a.ndim)
    row_scr = pltpu.VMEM((CHUNK, rw), F32)
    return pl.pallas_call(
        functools.partial(_rwkv_kernel, cfg=cfg),
        grid=(cfg.n_chunks,),
        in_specs=[pl.BlockSpec((CHUNK, 3 * rw), lambda c: (c, cfg.o_r // (3 * rw))),
                  pl.BlockSpec((CHUNK, lp), lambda c: (c, cfg.o_lora // lp)),
                  pl.BlockSpec((1, 8, 3 * rw + lp), lambda c: (seq(c), 0, 0)),
                  pl.BlockSpec((1, npair, HEAD, PAIR), lambda c: (seq(c), 0, 0, 0)),
                  full(mu), full(w0), full(wup), full(a0), full(aup), full(gup),
                  full(kk), full(ka), full(rk), full(lng), full(lnb)],
        out_specs=[pl.BlockSpec((CHUNK, rw), lambda c: (c, 0)),
                   pl.BlockSpec((1, npair, HEAD, PAIR), lambda c: (seq(c), 0, 0, 0))],
        out_shape=[jax.ShapeDtypeStruct((n, rw), BF16),
                   jax.ShapeDtypeStruct((cfg.n_seq, npair, HEAD, PAIR), F32)],
        scratch_shapes=[pltpu.VMEM((CHUNK + 8, 3 * rw), F32), pltpu.VMEM((CHUNK + 8, lp), F32),
                        row_scr, row_scr, row_scr, row_scr, row_scr, row_scr,
                        pltpu.VMEM((npair, PAIR, PAIR), BF16)],
        compiler_params=_params("arbitrary"),
        name="rwkv",
    )(proj, proj, shift8, wkv_pairs, mu, w0, wup, a0, aup, gup, kk, ka, rk, lng, lnb)


def _outproj_kernel(ym_ref, yr_ref, wa_ref, wb_ref, x_ref, gate_ref, o_ref, *, cfg, tm):
    i = pl.program_id(0)
    acc = (jnp.dot(ym_ref[...], wa_ref[...], preferred_element_type=F32)
           + jnp.dot(yr_ref[...], wb_ref[...], preferred_element_type=F32))
    for s in range(tm // CHUNK):
        seq = _seq_of_chunk(i * (tm // CHUNK) + s, cfg)
        rows = slice(s * CHUNK, (s + 1) * CHUNK)
        o_ref[rows, :] = x_ref[rows, :] + gate_ref[pl.ds(seq, 1), :] * acc[rows, :]


def _outproj(ym, yr, wa, wb, x_all, gate, cfg, tm=512, tn=1024):
    n, d = x_all.shape
    ka, kb = ym.shape[1], yr.shape[1]
    s = gate.shape[0]
    return pl.pallas_call(
        functools.partial(_outproj_kernel, cfg=cfg, tm=tm),
        grid=(n // tm, d // tn),
        in_specs=[pl.BlockSpec((tm, ka), lambda i, j: (i, 0)),
                  pl.BlockSpec((tm, kb), lambda i, j: (i, 0)),
                  pl.BlockSpec((ka, tn), lambda i, j: (0, j)),
                  pl.BlockSpec((kb, tn), lambda i, j: (0, j)),
                  pl.BlockSpec((tm, tn), lambda i, j: (i, j)),
                  pl.BlockSpec((s, tn), lambda i, j: (0, j))],
        out_specs=pl.BlockSpec((tm, tn), lambda i, j: (i, j)),
        out_shape=jax.ShapeDtypeStruct((n, d), F32),
        compiler_params=_params("parallel", "parallel"),
        name="outproj",
    )(ym, yr, wa, wb, x_all, gate)


def _router_kernel(x_ref, shift_ref, scale_ref, g_ref, wr_ref, br_ref, h_ref, lg_ref, *, cfg, tm):
    i = pl.program_id(0)
    for s in range(tm // CHUNK):
        seq = _seq_of_chunk(i * (tm // CHUNK) + s, cfg)
        rows = slice(s * CHUNK, (s + 1) * CHUNK)
        h = _modulated_norm(x_ref[rows, :], g_ref[...], scale_ref[pl.ds(seq, 1), :], shift_ref[pl.ds(seq, 1), :])
        h_ref[rows, :] = h.astype(BF16)
        lg_ref[rows, :] = jnp.dot(h, wr_ref[...], precision=HI, preferred_element_type=F32) + br_ref[...]


def _router(x1, shift, scale, g, wr, br, cfg, tm=256):
    n, d = x1.shape
    s = shift.shape[0]
    full = lambda a: pl.BlockSpec(a.shape, lambda i: (0,) * a.ndim)
    return pl.pallas_call(
        functools.partial(_router_kernel, cfg=cfg, tm=tm),
        grid=(n // tm,),
        in_specs=[pl.BlockSpec((tm, d), lambda i: (i, 0)), full(shift), full(scale), full(g), full(wr), full(br)],
        out_specs=[pl.BlockSpec((tm, d), lambda i: (i, 0)), pl.BlockSpec((tm, LANES), lambda i: (i, 0))],
        out_shape=[jax.ShapeDtypeStruct((n, d), BF16), jax.ShapeDtypeStruct((n, LANES), F32)],
        compiler_params=_params("parallel"),
        name="router",
    )(x1, shift, scale, g, wr, br)


def _expert_up_kernel(be_ref, nu_ref, x_ref, wg_ref, wu_ref, h_ref):
    @pl.when(pl.program_id(0) < nu_ref[0])
    def _():
        x = x_ref[...]
        gate = jnp.dot(x, wg_ref[0], preferred_element_type=F32)
        up = jnp.dot(x, wu_ref[0], preferred_element_type=F32)
        h_ref[...] = (_silu(gate) * up).astype(BF16)

    @pl.when(pl.program_id(0) >= nu_ref[0])
    def _():
        h_ref[...] = jnp.zeros_like(h_ref)


def _expert_down_kernel(be_ref, nu_ref, h_ref, wd_ref, y_ref):
    @pl.when(pl.program_id(0) < nu_ref[0])
    def _():
        y_ref[...] = jnp.dot(h_ref[...], wd_ref[0], preferred_element_type=F32)

    @pl.when(pl.program_id(0) >= nu_ref[0])
    def _():
        y_ref[...] = jnp.zeros_like(y_ref)


def _experts(xb, blk_e, n_used, wg, wu, wd, tm):
    rows, d = xb.shape
    f = wg.shape[2]
    nblk = rows // tm
    hid = pl.pallas_call(
        _expert_up_kernel,
        grid_spec=pltpu.PrefetchScalarGridSpec(
            num_scalar_prefetch=2, grid=(nblk,),
            in_specs=[pl.BlockSpec((tm, d), lambda i, be, nu: (i, 0)),
                      pl.BlockSpec((1, d, f), lambda i, be, nu: (be[i], 0, 0)),
                      pl.BlockSpec((1, d, f), lambda i, be, nu: (be[i], 0, 0))],
            out_specs=pl.BlockSpec((tm, f), lambda i, be, nu: (i, 0))),
        out_shape=jax.ShapeDtypeStruct((rows, f), BF16),
        compiler_params=_params("arbitrary"),
        name="expert_up",
    )(blk_e, n_used, xb, wg, wu)
    return pl.pallas_call(
        _expert_down_kernel,
        grid_spec=pltpu.PrefetchScalarGridSpec(
            num_scalar_prefetch=2, grid=(nblk,),
            in_specs=[pl.BlockSpec((tm, f), lambda i, be, nu: (i, 0)),
                      pl.BlockSpec((1, f, d), lambda i, be, nu: (be[i], 0, 0))],
            out_specs=pl.BlockSpec((tm, d), lambda i, be, nu: (i, 0))),
        out_shape=jax.ShapeDtypeStruct((rows, d), F32),
        compiler_params=_params("arbitrary"),
        name="expert_down",
    )(blk_e, n_used, hid, wd)


def _final_kernel(x_ref, moe_ref, gate_ref, g_ref, o_ref, *, cfg, tm):
    i = pl.program_id(0)
    for s in range(tm // CHUNK):
        seq = _seq_of_chunk(i * (tm // CHUNK) + s, cfg)
        rows = slice(s * CHUNK, (s + 1) * CHUNK)
        x = x_ref[rows, :] + gate_ref[pl.ds(seq, 1), :] * moe_ref[rows, :]
        o_ref[rows, :] = x * lax.rsqrt(jnp.mean(x * x, axis=-1, keepdims=True) + NORM_EPS) * g_ref[...]


def _final(x1, moe, gate, g, cfg, tm=256):
    n, d = x1.shape
    full = lambda a: pl.BlockSpec(a.shape, lambda i: (0,) * a.ndim)
    return pl.pallas_call(
        functools.partial(_final_kernel, cfg=cfg, tm=tm),
        grid=(n // tm,),
        in_specs=[pl.BlockSpec((tm, d), lambda i: (i, 0)), pl.BlockSpec((tm, d), lambda i: (i, 0)), full(gate), full(g)],
        out_specs=pl.BlockSpec((tm, d), lambda i: (i, 0)),
        out_shape=jax.ShapeDtypeStruct((n, d), F32),
        compiler_params=_params("parallel"),
        name="final_norm",
    )(x1, moe, gate, g)


def _dispatch(logits, n_experts, tm):
    n = logits.shape[0]
    glog = logits[:, :E_GROUPS]
    elog = logits[:, E_GROUPS:E_GROUPS + n_experts].reshape(n, E_GROUPS, E_PER_GROUP)
    gprob = jax.nn.softmax(glog, axis=-1)
    gsel = jnp.argmax(glog, axis=-1)
    elog_g = jnp.take_along_axis(elog, gsel[:, None, None], axis=1)[:, 0]
    eprob = jax.nn.softmax(elog_g, axis=-1)
    topv, topi = lax.top_k(eprob, TOP_K)
    pg = jnp.take_along_axis(gprob, gsel[:, None], axis=1)
    wts = topv / jnp.sum(topv, axis=-1, keepdims=True) * pg
    eid = (gsel[:, None] * E_PER_GROUP + topi).astype(jnp.int32)

    m = n * TOP_K
    e_flat = eid.reshape(m)
    order = jnp.argsort(e_flat)
    e_sorted = e_flat[order]
    counts = jnp.bincount(e_flat, length=n_experts).astype(jnp.int32)
    padded = (counts + tm - 1) // tm * tm
    start = jnp.cumsum(counts) - counts
    pend = jnp.cumsum(padded)
    pstart = pend - padded
    dest_sorted = pstart[e_sorted] + jnp.arange(m, dtype=jnp.int32) - start[e_sorted]
    n_blocks = -(-(m + n_experts * (tm - 1)) // tm)
    rows = n_blocks * tm
    tok_sorted = (order // TOP_K).astype(jnp.int32)
    row_tok = jnp.full((rows,), n, jnp.int32).at[dest_sorted].set(tok_sorted)
    slot_pos = jnp.zeros((m,), jnp.int32).at[order].set(dest_sorted).reshape(n, TOP_K)
    n_used = (pend[-1] // tm).astype(jnp.int32).reshape(1)
    blk = jnp.arange(n_blocks, dtype=jnp.int32)
    blk_e = jnp.minimum(jnp.searchsorted(pend, blk * tm, side='right'), n_experts - 1).astype(jnp.int32)
    last_e = blk_e[jnp.maximum(n_used[0] - 1, 0)]
    blk_e = jnp.where(blk < n_used[0], blk_e, last_e)
    return row_tok, slot_pos, wts, blk_e, n_used


def _pad_rcols(a, cfg):
    o = 3 * cfg.rw
    z = lambda w: jnp.zeros(a.shape[:-1] + (w,), a.dtype)
    return jnp.concatenate([a[..., :o + cfg.lw], z(LANES - cfg.lw),
                            a[..., o + cfg.lw:o + cfg.lw + cfg.la], z(LANES - cfg.la),
                            a[..., o + cfg.lw + cfg.la:]], axis=-1)


def _unpad_rcols(a, cfg):
    o = 3 * cfg.rw
    return jnp.concatenate([a[..., :o + cfg.lw], a[..., o + LANES:o + LANES + cfg.la], a[..., o + 2 * LANES:]], axis=-1)


def _pad_rows(a, rows):
    return jnp.concatenate([a, jnp.zeros((rows - a.shape[0],) + a.shape[1:], a.dtype)], axis=0)


def kernel(x_prompt, x_sample, state_conv, state_ssm, state_shift, state_wkv, c_prompt, c_sample, norm1_g, w_mod, b_mod, w_in, conv_w, conv_b, dt_bias, a_log, d_skip, m_norm_g, shift_mu, w0, w_up, a0, a_up, g_up, k_k, k_a, r_k, ln_x_g, ln_x_b, w_out, norm2_g, w_grp, b_grp, w_erouter, b_erouter, e_gate, e_up, e_down, final_norm_g):
    assert w_mod.shape[0] == 1, "single-layer trunk"
    bp, tp, d = x_prompt.shape
    bs, ts, _ = x_sample.shape
    mi = m_norm_g.shape[-1]
    rw = w0.shape[-1]
    lw, la, lg = w_up.shape[1], a_up.shape[1], g_up.shape[1]
    tn_in = 1024
    cfg0 = Cfg(d, bp, tp, bs, ts, mi, rw, lw, la, lg, 0)
    nc = -(-(cfg0.o_dt + LANES) // tn_in) * tn_in
    cfg = cfg0._replace(nc=nc)
    assert tp % CHUNK == 0 and ts % CHUNK == 0 and cfg.o_r == 3 * rw and cfg.o_lora % cfg.lp == 0
    assert state_conv.shape[-1] == cfg.cd and lw <= LANES and la <= LANES and cfg.mh <= LANES
    n = cfg.n_rows
    n_seq = cfg.n_seq
    mh, rh = cfg.mh, cfg.rh
    npair = rw // PAIR
    n_experts = e_gate.shape[1]

    x_all = jnp.concatenate([x_prompt.reshape(bp * tp, d), x_sample.reshape(bs * ts, d)], axis=0)
    c_all = jnp.concatenate([c_prompt, c_sample], axis=0)

    mod = _modulation(c_all, w_mod[0], b_mod[0])
    shift1, scale1, gate1, shift2, scale2, gate2 = [mod[:, i * d:(i + 1) * d] for i in range(6)]

    wi = w_in[0]
    o2 = mi + cfg.cd
    o3 = o2 + mh
    zc = lambda w: jnp.zeros((d, w), wi.dtype)
    w_proj = jnp.concatenate([wi[:, :o2], _pad_rcols(wi[:, o3:], cfg), wi[:, o2:o3], zc(nc - cfg.o_dt - mh)],
                             axis=1).astype(BF16)
    proj = _inproj(x_all, shift1, scale1, norm1_g, w_proj, cfg, tn=tn_in)

    zeros = lambda b, *s: jnp.zeros((b,) + s, F32)
    conv0 = jnp.concatenate([zeros(bp, CONV_W - 1, cfg.cd), state_conv[0]], axis=0)
    conv8 = jnp.concatenate([zeros(n_seq, 8 - (CONV_W - 1), cfg.cd), conv0], axis=1)
    nst = state_ssm.shape[-1]
    ssm0 = jnp.concatenate([zeros(bp, mi, nst), state_ssm[0].reshape(bs, mi, nst)], axis=0)
    lane_pad = lambda a: jnp.concatenate([a.reshape(1, -1), jnp.zeros((1, LANES - a.shape[-1]), F32)], axis=1)
    expand = (jnp.arange(LANES)[:, None] == (jnp.arange(mi)[None, :] // HEAD)).astype(F32)
    ym, ssm_new = _mamba(proj, conv8, ssm0, conv_w[0], conv_b[0].reshape(1, -1), lane_pad(dt_bias[0]),
                         lane_pad(a_log[0]), jnp.repeat(d_skip[0], HEAD).reshape(1, mi), m_norm_g[0].reshape(1, mi),
                         expand, cfg)

    sh0 = jnp.concatenate([zeros(bp, 1, state_shift.shape[-1]), state_shift[0]], axis=0)
    sh8 = jnp.concatenate([zeros(n_seq, 7, 3 * rw + cfg.lp), _pad_rcols(sh0, cfg)], axis=1)
    to_pairs = lambda s: s.reshape(-1, npair, 2, HEAD, HEAD).transpose(0, 1, 3, 2, 4).reshape(-1, npair, HEAD, PAIR)
    from_pairs = lambda s: s.reshape(-1, npair, HEAD, 2, HEAD).transpose(0, 1, 3, 2, 4).reshape(-1, rh, HEAD, HEAD)
    wkv0 = jnp.concatenate([zeros(bp, npair, HEAD, PAIR), to_pairs(state_wkv[0])], axis=0)
    row = lambda a: a.reshape(1, -1)
    yr, wkv_new = _rwkv(proj, sh8, wkv0, _pad_rcols(row(shift_mu[0]), cfg), row(w0[0]),
                        _pad_rows(w_up[0], LANES), row(a0[0]), _pad_rows(a_up[0], LANES), g_up[0],
                        row(k_k[0]), row(k_a[0]), row(r_k[0]), row(ln_x_g[0]), row(ln_x_b[0]), cfg)

    wo = w_out[0].astype(BF16)
    x1 = _outproj(ym, yr, wo[:mi], wo[mi:], x_all, gate1, cfg)
    wr = jnp.concatenate([w_grp[0], w_erouter[0], jnp.zeros((d, LANES - E_GROUPS - n_experts), F32)], axis=1)
    br = lane_pad(jnp.concatenate([b_grp[0], b_erouter[0]]))
    h2, logits = _router(x1, shift2, scale2, norm2_g, wr, br, cfg)

    tm_e = 256
    row_tok, slot_pos, wts, blk_e, n_used = _dispatch(logits, n_experts, tm_e)
    xb = jnp.concatenate([h2, jnp.zeros((1, d), BF16)], axis=0)[row_tok]
    yb = _experts(xb, blk_e, n_used, e_gate[0].astype(BF16), e_up[0].astype(BF16), e_down[0].astype(BF16), tm_e)
    moe = yb[slot_pos[:, 0]] * wts[:, 0:1] + yb[slot_pos[:, 1]] * wts[:, 1:2]
    y_all = _final(x1, moe, gate2, final_norm_g.reshape(1, d), cfg)

    np_ = bp * tp
    y_prompt = y_all[:np_].reshape(bp, tp, d)
    y_sample = y_all[np_:].reshape(bs, ts, d)
    pp = proj[:np_].reshape(bp, tp, nc)
    ps = proj[np_:].reshape(bs, ts, nc)
    conv_of = lambda p: p[:, -(CONV_W - 1):, mi:mi + cfg.cd][None]
    shift_of = lambda p: _unpad_rcols(p[:, -1:, cfg.o_r:cfg.o_dt], cfg)[None]
    ssm_new = ssm_new.reshape(n_seq, mh, HEAD, nst)
    wkv_new = from_pairs(wkv_new)
    return (y_prompt, y_sample,
            conv_of(pp), ssm_new[:bp][None], shift_of(pp), wkv_new[:bp][None],
            conv_of(ps), ssm_new[bp:][None], shift_of(ps), wkv_new[bp:][None])
```

```python
import functools
from typing import NamedTuple

import jax
import jax.numpy as jnp
from jax import lax
from jax.experimental import pallas as pl
from jax.experimental.pallas import tpu as pltpu

F32 = jnp.float32
BF16 = jnp.bfloat16
HI = lax.Precision.HIGHEST

LANES = 128
CHUNK = 64
HEAD = 64
PAIR = 2 * HEAD
NORM_EPS = 1e-6
M_NORM_EPS = 1e-5
R_LN_EPS = 64e-5
CONV_W = 4
E_GROUPS = 4
E_PER_GROUP = 8
TOP_K = 2
VMEM_LIMIT = 56 * 1024 * 1024


class Cfg(NamedTuple):
    d: int
    bp: int
    tp: int
    bs: int
    ts: int
    mi: int
    rw: int
    lw: int
    la: int
    lg: int
    nc: int

    @property
    def cpp(self):
        return self.tp // CHUNK

    @property
    def cps(self):
        return self.ts // CHUNK

    @property
    def n_chunks(self):
        return self.bp * self.cpp + self.bs * self.cps

    @property
    def n_seq(self):
        return self.bp + self.bs

    @property
    def n_rows(self):
        return self.n_chunks * CHUNK

    @property
    def groups(self):
        return self.mi // 256

    @property
    def nb(self):
        return self.groups * LANES

    @property
    def cd(self):
        return self.mi + 2 * self.nb

    @property
    def mh(self):
        return self.mi // HEAD

    @property
    def rh(self):
        return self.rw // HEAD

    @property
    def lp(self):
        return 2 * LANES + self.lg

    @property
    def o_r(self):
        return self.mi + self.cd

    @property
    def o_lora(self):
        return self.o_r + 3 * self.rw

    @property
    def o_dt(self):
        return self.o_lora + self.lp


def _seq_of_chunk(c, cfg):
    npc = cfg.bp * cfg.cpp
    return jnp.where(c < npc, c // cfg.cpp, cfg.bp + (c - npc) // cfg.cps)


def _is_first_chunk(c, cfg):
    npc = cfg.bp * cfg.cpp
    return jnp.where(c < npc, c % cfg.cpp == 0, (c - npc) % cfg.cps == 0)


def _silu(x):
    return x * jax.nn.sigmoid(x)


def _nt(a, b, **kw):
    return lax.dot_general(a, b, (((1,), (1,)), ((), ())), preferred_element_type=F32, **kw)


def _tn(a, b, **kw):
    return lax.dot_general(a, b, (((0,), (0,)), ((), ())), preferred_element_type=F32, **kw)


def _params(*sem):
    return pltpu.CompilerParams(dimension_semantics=sem, vmem_limit_bytes=VMEM_LIMIT)


def _mod_kernel(c_ref, w_ref, b_ref, o_ref):
    o_ref[...] = jnp.dot(_silu(c_ref[...]), w_ref[...], preferred_element_type=F32) + b_ref[...]


def _modulation(c_all, w_mod, b_mod):
    s, d = c_all.shape
    cols = w_mod.shape[1]
    tn = 512
    return pl.pallas_call(
        _mod_kernel,
        grid=(cols // tn,),
        in_specs=[pl.BlockSpec((s, d), lambda j: (0, 0)),
                  pl.BlockSpec((d, tn), lambda j: (0, j)),
                  pl.BlockSpec((1, tn), lambda j: (0, j))],
        out_specs=pl.BlockSpec((s, tn), lambda j: (0, j)),
        out_shape=jax.ShapeDtypeStruct((s, cols), F32),
        compiler_params=_params("parallel"),
        name="modulation",
    )(c_all, w_mod, b_mod.reshape(1, cols))


def _modulated_norm(x, g, scale, shift):
    y = x * lax.rsqrt(jnp.mean(x * x, axis=-1, keepdims=True) + NORM_EPS) * g
    return y * (1.0 + scale) + shift


def _inproj_kernel(x_ref, shift_ref, scale_ref, g_ref, w_ref, o_ref, h_scr, *, cfg, tm):
    i = pl.program_id(0)

    @pl.when(pl.program_id(1) == 0)
    def _():
        for s in range(tm // CHUNK):
            seq = _seq_of_chunk(i * (tm // CHUNK) + s, cfg)
            rows = slice(s * CHUNK, (s + 1) * CHUNK)
            h = _modulated_norm(x_ref[rows, :], g_ref[...], scale_ref[pl.ds(seq, 1), :], shift_ref[pl.ds(seq, 1), :])
            h_scr[rows, :] = h.astype(BF16)

    o_ref[...] = jnp.dot(h_scr[...], w_ref[...], preferred_element_type=F32)


def _inproj(x_all, shift, scale, g, w_bf16, cfg, tm=512, tn=1024):
    n, d = x_all.shape
    nc = w_bf16.shape[1]
    s = shift.shape[0]
    return pl.pallas_call(
        functools.partial(_inproj_kernel, cfg=cfg, tm=tm),
        grid=(n // tm, nc // tn),
        in_specs=[pl.BlockSpec((tm, d), lambda i, j: (i, 0)),
                  pl.BlockSpec((s, d), lambda i, j: (0, 0)),
                  pl.BlockSpec((s, d), lambda i, j: (0, 0)),
                  pl.BlockSpec((1, d), lambda i, j: (0, 0)),
                  pl.BlockSpec((d, tn), lambda i, j: (0, j))],
        out_specs=pl.BlockSpec((tm, tn), lambda i, j: (i, j)),
        out_shape=jax.ShapeDtypeStruct((n, nc), F32),
        scratch_shapes=[pltpu.VMEM((tm, d), BF16)],
        compiler_params=_params("parallel", "arbitrary"),
        name="inproj",
    )(x_all, shift, scale, g, w_bf16)


def _mamba_kernel(zx_ref, dt_ref, cst_ref, sst_ref, cw_ref, cb_ref, dtb_ref, alog_ref, dsk_ref, ng_ref, exp_ref,
                  ym_ref, ssm_ref, buf, u_scr, *, cfg):
    c = pl.program_id(0)
    mi, nb, cd = cfg.mi, cfg.nb, cfg.cd
    L = CHUNK

    @pl.when(_is_first_chunk(c, cfg))
    def _():
        buf[0:8, :] = cst_ref[0]
        ssm_ref[0] = sst_ref[0]

    xbc = zx_ref[:, mi:mi + cd]
    buf[8:8 + L, :] = xbc
    conv = (cb_ref[...] + buf[5:5 + L, :] * cw_ref[0:1, :] + buf[6:6 + L, :] * cw_ref[1:2, :]
            + buf[7:7 + L, :] * cw_ref[2:3, :] + xbc * cw_ref[3:4, :])
    buf[0:8, :] = buf[L:L + 8, :]
    u_scr[...] = _silu(conv)

    dt = jax.nn.softplus(dt_ref[...] + dtb_ref[...])
    da = dt * (-jnp.exp(alog_ref[...]))
    row = lax.broadcasted_iota(jnp.int32, (L, 256), 0)
    pos = lax.broadcasted_iota(jnp.int32, (L, 256), 1) % HEAD
    causal = row >= pos
    diag = row == pos
    tri2 = (lax.broadcasted_iota(jnp.int32, (L, 2 * L), 0)
            >= lax.broadcasted_iota(jnp.int32, (L, 2 * L), 1) % L).astype(BF16)
    lane = lax.broadcasted_iota(jnp.int32, (L, PAIR), 1)
    row16 = lax.broadcasted_iota(jnp.int32, (16, 256), 0)
    ones16 = jnp.ones((16, LANES), BF16)
    dt_hl = jnp.concatenate(_split_hi_lo(dt), axis=1)
    da_hl = jnp.concatenate(_split_hi_lo(da), axis=1)

    for g in range(cfg.groups):
        cs = slice(g * 256, (g + 1) * 256)
        xh = u_scr[:, cs]
        bg = u_scr[:, mi + g * LANES:mi + (g + 1) * LANES]
        cg = u_scr[:, mi + nb + g * LANES:mi + nb + (g + 1) * LANES]
        eg = exp_ref[:, cs]
        dte = jnp.dot(dt_hl, eg, preferred_element_type=F32)
        dae = jnp.dot(da_hl, eg, preferred_element_type=F32)
        cum = jnp.dot(tri2, jnp.concatenate(_split_hi_lo(dae), axis=0), preferred_element_type=F32)
        cum_row = jnp.sum(jnp.where(diag, cum, 0.0), axis=0, keepdims=True)
        last = cum[L - 1:L, :]
        decay = jnp.exp(jnp.where(causal, cum - cum_row, -jnp.inf))
        xdt = xh * dte
        sc2 = _nt(cg, jnp.concatenate([bg, bg], axis=0))
        ys = []
        for q in range(2):
            ps = slice(q * PAIR, (q + 1) * PAIR)
            xq = xdt[:, ps]
            rhs = jnp.concatenate([jnp.where(lane < HEAD, xq, 0.0), jnp.where(lane >= HEAD, xq, 0.0)], axis=0)
            ys.append(jnp.dot(decay[:, ps] * sc2, rhs, preferred_element_type=F32))
        y = jnp.concatenate(ys, axis=1)
        sg = ssm_ref[0, cs, :]
        y = y + _nt(cg, sg) * jnp.exp(cum)
        new = _tn(xdt * jnp.exp(last - cum), bg)
        e_last = jnp.exp(last)
        e_hi = e_last.astype(BF16).astype(F32)
        e_rows = jnp.where(row16 == 0, e_hi, jnp.where(row16 == 1, e_last - e_hi, 0.0)).astype(BF16)
        dcol = _tn(e_rows, ones16)
        ssm_ref[0, cs, :] = sg * dcol + new
        y = y + xh * dsk_ref[:, cs]
        y = y * _silu(zx_ref[:, cs])
        y = y * lax.rsqrt(jnp.mean(y * y, axis=-1, keepdims=True) + M_NORM_EPS)
        ym_ref[:, cs] = (y * ng_ref[:, cs]).astype(BF16)


def _mamba(proj, conv_state8, ssm_state, cw, cb, dtb, alog, dsk, ng, expand, cfg):
    n = proj.shape[0]
    mi, cd = cfg.mi, cfg.cd
    nst = ssm_state.shape[-1]
    seq = lambda c: _seq_of_chunk(c, cfg)
    full = lambda a: pl.BlockSpec(a.shape, lambda c: (0,) * a.ndim)
    return pl.pallas_call(
        functools.partial(_mamba_kernel, cfg=cfg),
        grid=(cfg.n_chunks,),
        in_specs=[pl.BlockSpec((CHUNK, mi + cd), lambda c: (c, 0)),
                  pl.BlockSpec((CHUNK, LANES), lambda c: (c, cfg.o_dt // LANES)),
                  pl.BlockSpec((1, 8, cd), lambda c: (seq(c), 0, 0)),
                  pl.BlockSpec((1, mi, nst), lambda c: (seq(c), 0, 0)),
                  full(cw), full(cb), full(dtb), full(alog), full(dsk), full(ng), full(expand)],
        out_specs=[pl.BlockSpec((CHUNK, mi), lambda c: (c, 0)),
                   pl.BlockSpec((1, mi, nst), lambda c: (seq(c), 0, 0))],
        out_shape=[jax.ShapeDtypeStruct((n, mi), BF16),
                   jax.ShapeDtypeStruct((cfg.n_seq, mi, nst), F32)],
        scratch_shapes=[pltpu.VMEM((CHUNK + 8, cd), F32), pltpu.VMEM((CHUNK, cd), F32)],
        compiler_params=_params("arbitrary"),
        name="mamba",
    )(proj, proj, conv_state8, ssm_state, cw, cb, dtb, alog, dsk, ng, expand)


def _seg_sum(x, ob):
    parts = [jnp.dot(x[:, p * PAIR:(p + 1) * PAIR], ob, preferred_element_type=F32) for p in range(x.shape[1] // PAIR)]
    return jnp.concatenate(parts, axis=1)


def _split_hi_lo(x):
    hi = x.astype(BF16)
    lo = (x - hi.astype(F32)).astype(BF16)
    return hi, lo


def _rwkv_kernel(rkv_ref, lora_ref, sh_ref, wkv_in_ref, mu_ref, w0_ref, wup_ref, a0_ref, aup_ref, gup_ref,
                 kk_ref, ka_ref, rk_ref, lng_ref, lnb_ref,
                 yr_ref, wkv_ref, pbuf, lbuf, w_s, a_s, b_s, k_s, y_s, vt_s, ut_s, y0_s, g_s, *, cfg):
    c = pl.program_id(0)
    rw = cfg.rw
    L = CHUNK
    npair = rw // PAIR

    @pl.when(_is_first_chunk(c, cfg))
    def _():
        pbuf[0:8, :] = sh_ref[0, :, 0:3 * rw]
        lbuf[0:8, :] = sh_ref[0, :, 3 * rw:]
        wkv_ref[0] = wkv_in_ref[0]

    pr = rkv_ref[...]
    pl_ = lora_ref[...]
    pbuf[8:8 + L, :] = pr
    lbuf[8:8 + L, :] = pl_
    xs = pr + (pbuf[7:7 + L, :] - pr) * mu_ref[:, 0:3 * rw]
    xl = pl_ + (lbuf[7:7 + L, :] - pl_) * mu_ref[:, 3 * rw:]
    pbuf[0:8, :] = pbuf[L:L + 8, :]
    lbuf[0:8, :] = lbuf[L:L + 8, :]

    r = xs[:, 0:rw]
    k = xs[:, rw:2 * rw]
    v = xs[:, 2 * rw:3 * rw]
    wd = xl[:, 0:LANES]
    ad = xl[:, LANES:2 * LANES]
    gd = xl[:, 2 * LANES:]

    lane = lax.broadcasted_iota(jnp.int32, (PAIR, PAIR), 1)
    rowi = lax.broadcasted_iota(jnp.int32, (PAIR, PAIR), 0)
    ob = ((lane // HEAD) == (rowi // HEAD)).astype(F32)

    w_raw = -jax.nn.softplus(-(w0_ref[...] + jnp.dot(jnp.tanh(wd), wup_ref[...], preferred_element_type=F32))) - 0.5
    logw = -jnp.exp(w_raw)
    w_s[...] = jnp.exp(logw)
    a = jax.nn.sigmoid(a0_ref[...] + jnp.dot(ad, aup_ref[...], preferred_element_type=F32))
    kk = k * kk_ref[...]
    kk = kk * lax.rsqrt(jnp.maximum(_seg_sum(kk * kk, ob), 1e-24))
    kh = k * (1.0 + (a - 1.0) * ka_ref[...])
    a_s[...] = -kk
    b_s[...] = kk * a
    k_s[...] = kh

    tri = (lax.broadcasted_iota(jnp.int32, (L, L), 0) >= lax.broadcasted_iota(jnp.int32, (L, L), 1)).astype(F32)
    cum = jnp.dot(tri, logw, precision=HI, preferred_element_type=F32)
    p_inv = jnp.exp(-cum)
    r_p = r * jnp.exp(cum)
    b_p = b_s[...] * p_inv
    k_p = kh * p_inv
    lo_lanes = lane < HEAD
    g_row = lax.broadcasted_iota(jnp.int32, (2 * PAIR, PAIR), 0)
    g_lane = lax.broadcasted_iota(jnp.int32, (2 * PAIR, PAIR), 1)
    g_keep = (g_row % HEAD) <= (g_lane % HEAD)

    def by_head(x):
        x2 = jnp.concatenate([x, x], axis=0)
        return jnp.where(lo_lanes == (rowi < HEAD), x2, 0.0).astype(BF16)

    for p in range(npair):
        ps = slice(p * PAIR, (p + 1) * PAIR)
        hs = slice(p * HEAD, (p + 1) * HEAD)
        r_bd = by_head(r_p[:, ps])
        y0_s[hs, :] = _nt(wkv_ref[0, p].astype(BF16), r_bd)
        gram = _nt(jnp.concatenate([by_head(b_p[:, ps]), by_head(k_p[:, ps])], axis=0), r_bd)
        g_s[p] = jnp.where(g_keep, gram, 0.0).astype(BF16)
        vt = v[:, ps].T
        vjt = jnp.concatenate([vt[0:HEAD], vt[HEAD:]], axis=1)
        hi = vjt.astype(BF16).astype(F32)
        vt_s[hs, :] = jnp.concatenate([hi, vjt - hi], axis=1).astype(BF16)

    ob_b = ob.astype(BF16)
    ob2 = jnp.concatenate([ob_b, ob_b], axis=0)
    lane_t = lax.broadcasted_iota(jnp.int32, (HEAD, PAIR), 1) % HEAD
    crow = lax.broadcasted_iota(jnp.int32, (2 * PAIR, 2 * PAIR), 0)
    clane = lax.broadcasted_iota(jnp.int32, (2 * PAIR, 2 * PAIR), 1)
    same_head = ((crow % PAIR) // HEAD) == ((clane % PAIR) // HEAD)
    c_dt = (crow % HEAD) - clane // PAIR

    def steps(blk, carry):
        t0 = pl.multiple_of(blk * 8, 8)
        rows = pl.ds(t0, 8)
        w8, a8, b8, k8 = w_s[rows, :], a_s[rows, :], b_s[rows, :], k_s[rows, :]
        s = [wkv_ref[0, p] for p in range(npair)]
        ut = [ut_s[p * HEAD:(p + 1) * HEAD, :] for p in range(npair)]
        vcol2 = None
        for i in range(8):
            row = lambda x8, p: x8[i:i + 1, p * PAIR:(p + 1) * PAIR]
            if i % 2 == 0:
                sel = (same_head & (c_dt == t0 + i)).astype(BF16)
                vcol2 = jnp.dot(vt_s[...], sel, preferred_element_type=F32)
            vcol = vcol2[:, (i % 2) * PAIR:(i % 2 + 1) * PAIR]
            lhs = []
            for p in range(npair):
                hi, lo = _split_hi_lo(s[p] * row(a8, p))
                lhs.append(jnp.concatenate([hi, lo], axis=1))
            sa = jnp.dot(jnp.concatenate(lhs, axis=0), ob2, preferred_element_type=F32)
            for p in range(npair):
                hs = slice(p * HEAD, (p + 1) * HEAD)
                s[p] = s[p] * row(w8, p) + sa[hs] * row(b8, p) + vcol[hs] * row(k8, p)
                ut[p] = jnp.where(lane_t == t0 + i, sa[hs], ut[p])
        for p in range(npair):
            wkv_ref[0, p] = s[p]
            ut_s[p * HEAD:(p + 1) * HEAD, :] = ut[p]
        return carry

    ut_s[...] = jnp.zeros_like(ut_s)
    lax.fori_loop(0, L // 8, steps, 0)

    for p in range(npair):
        hs = slice(p * HEAD, (p + 1) * HEAD)
        lhs = jnp.concatenate([ut_s[hs, :].astype(BF16), vt_s[hs, 0:PAIR]], axis=1)
        yt = y0_s[hs, :] + jnp.dot(lhs, g_s[p], preferred_element_type=F32)
        ytt = yt.T
        y_s[:, p * PAIR:(p + 1) * PAIR] = jnp.concatenate([ytt[0:L], ytt[L:]], axis=1)

    y = y_s[...]
    mu = _seg_sum(y, ob) * (1.0 / HEAD)
    dlt = y - mu
    var = _seg_sum(dlt * dlt, ob) * (1.0 / HEAD)
    yn = dlt * lax.rsqrt(var + R_LN_EPS) * lng_ref[...] + lnb_ref[...]
    bonus = _seg_sum(r * k_s[...] * rk_ref[...], ob) * v
    g = jnp.dot(jax.nn.sigmoid(gd), gup_ref[...], preferred_element_type=F32)
    yr_ref[...] = ((yn + bonus) * g).astype(BF16)


def _rwkv(proj, shift8, wkv_pairs, mu, w0, wup, a0, aup, gup, kk, ka, rk, lng, lnb, cfg):
    n = proj.shape[0]
    rw, lp = cfg.rw, cfg.lp
    npair = rw // PAIR
    seq = lambda c: _seq_of_chunk(c, cfg)
    full = lambda a: pl.BlockSpec(a.shape, lambda c: (0,) * a.ndim)
    row_scr = pltpu.VMEM((CHUNK, rw), F32)
    return pl.pallas_call(
        functools.partial(_rwkv_kernel, cfg=cfg),
        grid=(cfg.n_chunks,),
        in_specs=[pl.BlockSpec((CHUNK, 3 * rw), lambda c: (c, cfg.o_r // (3 * rw))),
                  pl.BlockSpec((CHUNK, lp), lambda c: (c, cfg.o_lora // lp)),
                  pl.BlockSpec((1, 8, 3 * rw + lp), lambda c: (seq(c), 0, 0)),
                  pl.BlockSpec((1, npair, HEAD, PAIR), lambda c: (seq(c), 0, 0, 0)),
                  full(mu), full(w0), full(wup), full(a0), full(aup), full(gup),
                  full(kk), full(ka), full(rk), full(lng), full(lnb)],
        out_specs=[pl.BlockSpec((CHUNK, rw), lambda c: (c, 0)),
                   pl.BlockSpec((1, npair, HEAD, PAIR), lambda c: (seq(c), 0, 0, 0))],
        out_shape=[jax.ShapeDtypeStruct((n, rw), BF16),
                   jax.ShapeDtypeStruct((cfg.n_seq, npair, HEAD, PAIR), F32)],
        scratch_shapes=[pltpu.VMEM((CHUNK + 8, 3 * rw), F32), pltpu.VMEM((CHUNK + 8, lp), F32),
                        row_scr, row_scr, row_scr, row_scr, row_scr,
                        pltpu.VMEM((npair * HEAD, 2 * PAIR), BF16), pltpu.VMEM((npair * HEAD, PAIR), F32),
                        pltpu.VMEM((npair * HEAD, PAIR), F32), pltpu.VMEM((npair, 2 * PAIR, PAIR), BF16)],
        compiler_params=_params("arbitrary"),
        name="rwkv",
    )(proj, proj, shift8, wkv_pairs, mu, w0, wup, a0, aup, gup, kk, ka, rk, lng, lnb)


def _outproj_kernel(ym_ref, yr_ref, wa_ref, wb_ref, x_ref, gate_ref, o_ref, *, cfg, tm):
    i = pl.program_id(0)
    acc = (jnp.dot(ym_ref[...], wa_ref[...], preferred_element_type=F32)
           + jnp.dot(yr_ref[...], wb_ref[...], preferred_element_type=F32))
    for s in range(tm // CHUNK):
        seq = _seq_of_chunk(i * (tm // CHUNK) + s, cfg)
        rows = slice(s * CHUNK, (s + 1) * CHUNK)
        o_ref[rows, :] = x_ref[rows, :] + gate_ref[pl.ds(seq, 1), :] * acc[rows, :]


def _outproj(ym, yr, wa, wb, x_all, gate, cfg, tm=512, tn=1024):
    n, d = x_all.shape
    ka, kb = ym.shape[1], yr.shape[1]
    s = gate.shape[0]
    return pl.pallas_call(
        functools.partial(_outproj_kernel, cfg=cfg, tm=tm),
        grid=(n // tm, d // tn),
        in_specs=[pl.BlockSpec((tm, ka), lambda i, j: (i, 0)),
                  pl.BlockSpec((tm, kb), lambda i, j: (i, 0)),
                  pl.BlockSpec((ka, tn), lambda i, j: (0, j)),
                  pl.BlockSpec((kb, tn), lambda i, j: (0, j)),
                  pl.BlockSpec((tm, tn), lambda i, j: (i, j)),
                  pl.BlockSpec((s, tn), lambda i, j: (0, j))],
        out_specs=pl.BlockSpec((tm, tn), lambda i, j: (i, j)),
        out_shape=jax.ShapeDtypeStruct((n, d), F32),
        compiler_params=_params("parallel", "parallel"),
        name="outproj",
    )(ym, yr, wa, wb, x_all, gate)


def _router_kernel(x_ref, shift_ref, scale_ref, g_ref, wr_ref, br_ref, h_ref, lg_ref, *, cfg, tm):
    i = pl.program_id(0)
    for s in range(tm // CHUNK):
        seq = _seq_of_chunk(i * (tm // CHUNK) + s, cfg)
        rows = slice(s * CHUNK, (s + 1) * CHUNK)
        h = _modulated_norm(x_ref[rows, :], g_ref[...], scale_ref[pl.ds(seq, 1), :], shift_ref[pl.ds(seq, 1), :])
        h_ref[rows, :] = h.astype(BF16)
        lg_ref[rows, :] = jnp.dot(h, wr_ref[...], precision=HI, preferred_element_type=F32) + br_ref[...]


def _router(x1, shift, scale, g, wr, br, cfg, tm=256):
    n, d = x1.shape
    s = shift.shape[0]
    full = lambda a: pl.BlockSpec(a.shape, lambda i: (0,) * a.ndim)
    return pl.pallas_call(
        functools.partial(_router_kernel, cfg=cfg, tm=tm),
        grid=(n // tm,),
        in_specs=[pl.BlockSpec((tm, d), lambda i: (i, 0)), full(shift), full(scale), full(g), full(wr), full(br)],
        out_specs=[pl.BlockSpec((tm, d), lambda i: (i, 0)), pl.BlockSpec((tm, LANES), lambda i: (i, 0))],
        out_shape=[jax.ShapeDtypeStruct((n, d), BF16), jax.ShapeDtypeStruct((n, LANES), F32)],
        compiler_params=_params("parallel"),
        name="router",
    )(x1, shift, scale, g, wr, br)


def _expert_up_kernel(be_ref, nu_ref, x_ref, wg_ref, wu_ref, h_ref):
    @pl.when(pl.program_id(0) < nu_ref[0])
    def _():
        x = x_ref[...]
        gate = jnp.dot(x, wg_ref[0], preferred_element_type=F32)
        up = jnp.dot(x, wu_ref[0], preferred_element_type=F32)
        h_ref[...] = (_silu(gate) * up).astype(BF16)

    @pl.when(pl.program_id(0) >= nu_ref[0])
    def _():
        h_ref[...] = jnp.zeros_like(h_ref)


def _expert_down_kernel(be_ref, nu_ref, h_ref, wd_ref, y_ref):
    @pl.when(pl.program_id(0) < nu_ref[0])
    def _():
        y_ref[...] = jnp.dot(h_ref[...], wd_ref[0], preferred_element_type=F32)

    @pl.when(pl.program_id(0) >= nu_ref[0])
    def _():
        y_ref[...] = jnp.zeros_like(y_ref)


def _experts(xb, blk_e, n_used, wg, wu, wd, tm):
    rows, d = xb.shape
    f = wg.shape[2]
    nblk = rows // tm
    hid = pl.pallas_call(
        _expert_up_kernel,
        grid_spec=pltpu.PrefetchScalarGridSpec(
            num_scalar_prefetch=2, grid=(nblk,),
            in_specs=[pl.BlockSpec((tm, d), lambda i, be, nu: (i, 0)),
                      pl.BlockSpec((1, d, f), lambda i, be, nu: (be[i], 0, 0)),
                      pl.BlockSpec((1, d, f), lambda i, be, nu: (be[i], 0, 0))],
            out_specs=pl.BlockSpec((tm, f), lambda i, be, nu: (i, 0))),
        out_shape=jax.ShapeDtypeStruct((rows, f), BF16),
        compiler_params=_params("arbitrary"),
        name="expert_up",
    )(blk_e, n_used, xb, wg, wu)
    return pl.pallas_call(
        _expert_down_kernel,
        grid_spec=pltpu.PrefetchScalarGridSpec(
            num_scalar_prefetch=2, grid=(nblk,),
            in_specs=[pl.BlockSpec((tm, f), lambda i, be, nu: (i, 0)),
                      pl.BlockSpec((1, f, d), lambda i, be, nu: (be[i], 0, 0))],
            out_specs=pl.BlockSpec((tm, d), lambda i, be, nu: (i, 0))),
        out_shape=jax.ShapeDtypeStruct((rows, d), F32),
        compiler_params=_params("arbitrary"),
        name="expert_down",
    )(blk_e, n_used, hid, wd)


def _final_kernel(x_ref, moe_ref, gate_ref, g_ref, o_ref, *, cfg, tm):
    i = pl.program_id(0)
    for s in range(tm // CHUNK):
        seq = _seq_of_chunk(i * (tm // CHUNK) + s, cfg)
        rows = slice(s * CHUNK, (s + 1) * CHUNK)
        x = x_ref[rows, :] + gate_ref[pl.ds(seq, 1), :] * moe_ref[rows, :]
        o_ref[rows, :] = x * lax.rsqrt(jnp.mean(x * x, axis=-1, keepdims=True) + NORM_EPS) * g_ref[...]


def _final(x1, moe, gate, g, cfg, tm=256):
    n, d = x1.shape
    full = lambda a: pl.BlockSpec(a.shape, lambda i: (0,) * a.ndim)
    return pl.pallas_call(
        functools.partial(_final_kernel, cfg=cfg, tm=tm),
        grid=(n // tm,),
        in_specs=[pl.BlockSpec((tm, d), lambda i: (i, 0)), pl.BlockSpec((tm, d), lambda i: (i, 0)), full(gate), full(g)],
        out_specs=pl.BlockSpec((tm, d), lambda i: (i, 0)),
        out_shape=jax.ShapeDtypeStruct((n, d), F32),
        compiler_params=_params("parallel"),
        name="final_norm",
    )(x1, moe, gate, g)


def _dispatch(logits, n_experts, tm):
    n = logits.shape[0]
    glog = logits[:, :E_GROUPS]
    elog = logits[:, E_GROUPS:E_GROUPS + n_experts].reshape(n, E_GROUPS, E_PER_GROUP)
    gprob = jax.nn.softmax(glog, axis=-1)
    gsel = jnp.argmax(glog, axis=-1)
    elog_g = jnp.take_along_axis(elog, gsel[:, None, None], axis=1)[:, 0]
    eprob = jax.nn.softmax(elog_g, axis=-1)
    topv, topi = lax.top_k(eprob, TOP_K)
    pg = jnp.take_along_axis(gprob, gsel[:, None], axis=1)
    wts = topv / jnp.sum(topv, axis=-1, keepdims=True) * pg
    eid = (gsel[:, None] * E_PER_GROUP + topi).astype(jnp.int32)

    m = n * TOP_K
    e_flat = eid.reshape(m)
    order = jnp.argsort(e_flat)
    e_sorted = e_flat[order]
    counts = jnp.bincount(e_flat, length=n_experts).astype(jnp.int32)
    padded = (counts + tm - 1) // tm * tm
    start = jnp.cumsum(counts) - counts
    pend = jnp.cumsum(padded)
    pstart = pend - padded
    dest_sorted = pstart[e_sorted] + jnp.arange(m, dtype=jnp.int32) - start[e_sorted]
    n_blocks = -(-(m + n_experts * (tm - 1)) // tm)
    rows = n_blocks * tm
    tok_sorted = (order // TOP_K).astype(jnp.int32)
    row_tok = jnp.full((rows,), n, jnp.int32).at[dest_sorted].set(tok_sorted)
    slot_pos = jnp.zeros((m,), jnp.int32).at[order].set(dest_sorted).reshape(n, TOP_K)
    n_used = (pend[-1] // tm).astype(jnp.int32).reshape(1)
    blk = jnp.arange(n_blocks, dtype=jnp.int32)
    blk_e = jnp.minimum(jnp.searchsorted(pend, blk * tm, side='right'), n_experts - 1).astype(jnp.int32)
    last_e = blk_e[jnp.maximum(n_used[0] - 1, 0)]
    blk_e = jnp.where(blk < n_used[0], blk_e, last_e)
    return row_tok, slot_pos, wts, blk_e, n_used


def _pad_rcols(a, cfg):
    o = 3 * cfg.rw
    z = lambda w: jnp.zeros(a.shape[:-1] + (w,), a.dtype)
    return jnp.concatenate([a[..., :o + cfg.lw], z(LANES - cfg.lw),
                            a[..., o + cfg.lw:o + cfg.lw + cfg.la], z(LANES - cfg.la),
                            a[..., o + cfg.lw + cfg.la:]], axis=-1)


def _unpad_rcols(a, cfg):
    o = 3 * cfg.rw
    return jnp.concatenate([a[..., :o + cfg.lw], a[..., o + LANES:o + LANES + cfg.la], a[..., o + 2 * LANES:]], axis=-1)


def _pad_rows(a, rows):
    return jnp.concatenate([a, jnp.zeros((rows - a.shape[0],) + a.shape[1:], a.dtype)], axis=0)


def kernel(x_prompt, x_sample, state_conv, state_ssm, state_shift, state_wkv, c_prompt, c_sample, norm1_g, w_mod, b_mod, w_in, conv_w, conv_b, dt_bias, a_log, d_skip, m_norm_g, shift_mu, w0, w_up, a0, a_up, g_up, k_k, k_a, r_k, ln_x_g, ln_x_b, w_out, norm2_g, w_grp, b_grp, w_erouter, b_erouter, e_gate, e_up, e_down, final_norm_g):
    assert w_mod.shape[0] == 1, "single-layer trunk"
    bp, tp, d = x_prompt.shape
    bs, ts, _ = x_sample.shape
    mi = m_norm_g.shape[-1]
    rw = w0.shape[-1]
    lw, la, lg = w_up.shape[1], a_up.shape[1], g_up.shape[1]
    tn_in = 1024
    cfg0 = Cfg(d, bp, tp, bs, ts, mi, rw, lw, la, lg, 0)
    nc = -(-(cfg0.o_dt + LANES) // tn_in) * tn_in
    cfg = cfg0._replace(nc=nc)
    assert tp % CHUNK == 0 and ts % CHUNK == 0 and cfg.o_r == 3 * rw and cfg.o_lora % cfg.lp == 0
    assert state_conv.shape[-1] == cfg.cd and lw <= LANES and la <= LANES and cfg.mh <= LANES
    n = cfg.n_rows
    n_seq = cfg.n_seq
    mh, rh = cfg.mh, cfg.rh
    npair = rw // PAIR
    n_experts = e_gate.shape[1]

    x_all = jnp.concatenate([x_prompt.reshape(bp * tp, d), x_sample.reshape(bs * ts, d)], axis=0)
    c_all = jnp.concatenate([c_prompt, c_sample], axis=0)

    mod = _modulation(c_all, w_mod[0], b_mod[0])
    shift1, scale1, gate1, shift2, scale2, gate2 = [mod[:, i * d:(i + 1) * d] for i in range(6)]

    wi = w_in[0]
    o2 = mi + cfg.cd
    o3 = o2 + mh
    zc = lambda w: jnp.zeros((d, w), wi.dtype)
    w_proj = jnp.concatenate([wi[:, :o2], _pad_rcols(wi[:, o3:], cfg), wi[:, o2:o3], zc(nc - cfg.o_dt - mh)],
                             axis=1).astype(BF16)
    proj = _inproj(x_all, shift1, scale1, norm1_g, w_proj, cfg, tn=tn_in)

    zeros = lambda b, *s: jnp.zeros((b,) + s, F32)
    conv0 = jnp.concatenate([zeros(bp, CONV_W - 1, cfg.cd), state_conv[0]], axis=0)
    conv8 = jnp.concatenate([zeros(n_seq, 8 - (CONV_W - 1), cfg.cd), conv0], axis=1)
    nst = state_ssm.shape[-1]
    ssm0 = jnp.concatenate([zeros(bp, mi, nst), state_ssm[0].reshape(bs, mi, nst)], axis=0)
    lane_pad = lambda a: jnp.concatenate([a.reshape(1, -1), jnp.zeros((1, LANES - a.shape[-1]), F32)], axis=1)
    expand = ((jnp.arange(2 * LANES)[:, None] % LANES) == (jnp.arange(mi)[None, :] // HEAD)).astype(BF16)
    ym, ssm_new = _mamba(proj, conv8, ssm0, conv_w[0], conv_b[0].reshape(1, -1), lane_pad(dt_bias[0]),
                         lane_pad(a_log[0]), jnp.repeat(d_skip[0], HEAD).reshape(1, mi), m_norm_g[0].reshape(1, mi),
                         expand, cfg)

    sh0 = jnp.concatenate([zeros(bp, 1, state_shift.shape[-1]), state_shift[0]], axis=0)
    sh8 = jnp.concatenate([zeros(n_seq, 7, 3 * rw + cfg.lp), _pad_rcols(sh0, cfg)], axis=1)
    to_pairs = lambda s: s.reshape(-1, npair, 2, HEAD, HEAD).transpose(0, 1, 3, 2, 4).reshape(-1, npair, HEAD, PAIR)
    from_pairs = lambda s: s.reshape(-1, npair, HEAD, 2, HEAD).transpose(0, 1, 3, 2, 4).reshape(-1, rh, HEAD, HEAD)
    wkv0 = jnp.concatenate([zeros(bp, npair, HEAD, PAIR), to_pairs(state_wkv[0])], axis=0)
    row = lambda a: a.reshape(1, -1)
    yr, wkv_new = _rwkv(proj, sh8, wkv0, _pad_rcols(row(shift_mu[0]), cfg), row(w0[0]),
                        _pad_rows(w_up[0], LANES), row(a0[0]), _pad_rows(a_up[0], LANES), g_up[0],
                        row(k_k[0]), row(k_a[0]), row(r_k[0]), row(ln_x_g[0]), row(ln_x_b[0]), cfg)

    wo = w_out[0].astype(BF16)
    x1 = _outproj(ym, yr, wo[:mi], wo[mi:], x_all, gate1, cfg)
    wr = jnp.concatenate([w_grp[0], w_erouter[0], jnp.zeros((d, LANES - E_GROUPS - n_experts), F32)], axis=1)
    br = lane_pad(jnp.concatenate([b_grp[0], b_erouter[0]]))
    h2, logits = _router(x1, shift2, scale2, norm2_g, wr, br, cfg)

    tm_e = 256
    row_tok, slot_pos, wts, blk_e, n_used = _dispatch(logits, n_experts, tm_e)
    xb = jnp.concatenate([h2, jnp.zeros((1, d), BF16)], axis=0)[row_tok]
    yb = _experts(xb, blk_e, n_used, e_gate[0].astype(BF16), e_up[0].astype(BF16), e_down[0].astype(BF16), tm_e)
    moe = yb[slot_pos[:, 0]] * wts[:, 0:1] + yb[slot_pos[:, 1]] * wts[:, 1:2]
    y_all = _final(x1, moe, gate2, final_norm_g.reshape(1, d), cfg)

    np_ = bp * tp
    y_prompt = y_all[:np_].reshape(bp, tp, d)
    y_sample = y_all[np_:].reshape(bs, ts, d)
    tails = lambda b, t, base: jnp.stack(
        [lax.slice(proj, (base + (i + 1) * t - (CONV_W - 1), 0), (base + (i + 1) * t, nc)) for i in range(b)])
    pp = tails(bp, tp, 0)
    ps = tails(bs, ts, np_)
    conv_of = lambda p: p[:, :, mi:mi + cfg.cd][None]
    shift_of = lambda p: _unpad_rcols(p[:, -1:, cfg.o_r:cfg.o_dt], cfg)[None]
    ssm_new = ssm_new.reshape(n_seq, mh, HEAD, nst)
    wkv_new = from_pairs(wkv_new)
    return (y_prompt, y_sample,
            conv_of(pp), ssm_new[:bp][None], shift_of(pp), wkv_new[:bp][None],
            conv_of(ps), ssm_new[bp:][None], shift_of(ps), wkv_new[bp:][None])
```

```python
import functools
from typing import NamedTuple

import jax
import jax.numpy as jnp
from jax import lax
from jax.experimental import pallas as pl
from jax.experimental.pallas import tpu as pltpu

F32 = jnp.float32
BF16 = jnp.bfloat16
HI = lax.Precision.HIGHEST

LANES = 128
CHUNK = 64
HEAD = 64
PAIR = 2 * HEAD
NORM_EPS = 1e-6
M_NORM_EPS = 1e-5
R_LN_EPS = 64e-5
CONV_W = 4
E_GROUPS = 4
E_PER_GROUP = 8
TOP_K = 2
VMEM_LIMIT = 56 * 1024 * 1024
EXP_NEG_HALF = 0.6065306597126334
SA_GROUP = 8


class Cfg(NamedTuple):
    d: int
    bp: int
    tp: int
    bs: int
    ts: int
    mi: int
    rw: int
    lw: int
    la: int
    lg: int
    nc: int

    @property
    def cpp(self):
        return self.tp // CHUNK

    @property
    def cps(self):
        return self.ts // CHUNK

    @property
    def n_chunks(self):
        return self.bp * self.cpp + self.bs * self.cps

    @property
    def n_seq(self):
        return self.bp + self.bs

    @property
    def n_rows(self):
        return self.n_chunks * CHUNK

    @property
    def groups(self):
        return self.mi // 256

    @property
    def nb(self):
        return self.groups * LANES

    @property
    def cd(self):
        return self.mi + 2 * self.nb

    @property
    def mh(self):
        return self.mi // HEAD

    @property
    def rh(self):
        return self.rw // HEAD

    @property
    def lp(self):
        return 2 * LANES + self.lg

    @property
    def o_r(self):
        return self.mi + self.cd

    @property
    def o_lora(self):
        return self.o_r + 3 * self.rw

    @property
    def o_dt(self):
        return self.o_lora + self.lp


def _seq_of_chunk(c, cfg):
    npc = cfg.bp * cfg.cpp
    return jnp.where(c < npc, c // cfg.cpp, cfg.bp + (c - npc) // cfg.cps)


def _is_first_chunk(c, cfg):
    npc = cfg.bp * cfg.cpp
    return jnp.where(c < npc, c % cfg.cpp == 0, (c - npc) % cfg.cps == 0)


def _silu(x):
    return x * jax.nn.sigmoid(x)


def _nt(a, b, **kw):
    return lax.dot_general(a, b, (((1,), (1,)), ((), ())), preferred_element_type=F32, **kw)


def _tn(a, b, **kw):
    return lax.dot_general(a, b, (((0,), (0,)), ((), ())), preferred_element_type=F32, **kw)


def _params(*sem):
    return pltpu.CompilerParams(dimension_semantics=sem, vmem_limit_bytes=VMEM_LIMIT)


def _mod_kernel(c_ref, w_ref, b_ref, o_ref):
    o_ref[...] = jnp.dot(_silu(c_ref[...]), w_ref[...], preferred_element_type=F32) + b_ref[...]


def _modulation(c_all, w_mod, b_mod):
    s, d = c_all.shape
    cols = w_mod.shape[1]
    tn = 512
    return pl.pallas_call(
        _mod_kernel,
        grid=(cols // tn,),
        in_specs=[pl.BlockSpec((s, d), lambda j: (0, 0)),
                  pl.BlockSpec((d, tn), lambda j: (0, j)),
                  pl.BlockSpec((1, tn), lambda j: (0, j))],
        out_specs=pl.BlockSpec((s, tn), lambda j: (0, j)),
        out_shape=jax.ShapeDtypeStruct((s, cols), F32),
        compiler_params=_params("parallel"),
        name="modulation",
    )(c_all, w_mod, b_mod.reshape(1, cols))


def _modulated_norm(x, g, scale, shift):
    y = x * lax.rsqrt(jnp.mean(x * x, axis=-1, keepdims=True) + NORM_EPS) * g
    return y * (1.0 + scale) + shift


def _prompt_tiles(cfg, tm):
    assert (cfg.bp * cfg.tp) % tm == 0 and (cfg.bs * cfg.ts) % tm == 0 and tm % CHUNK == 0
    return cfg.bp * cfg.tp // tm


def _row_specs(cfg, tm, width, col, buffers=2):
    npt = _prompt_tiles(cfg, tm)
    mode = dict(pipeline_mode=pl.Buffered(buffers)) if buffers != 2 else {}
    return [pl.BlockSpec((tm, width), lambda i, j: (jnp.minimum(i, npt - 1), jnp.where(i < npt, col(j), 0)), **mode),
            pl.BlockSpec((tm, width), lambda i, j: (jnp.maximum(i - npt, 0), jnp.where(i >= npt, col(j), 0)), **mode)]


def _inproj_kernel(xp_ref, xs_ref, shift_ref, scale_ref, g_ref, w_ref, o_ref, h_scr, *, cfg, tm):
    i = pl.program_id(0)
    first_col = pl.program_id(1) == 0
    npt = _prompt_tiles(cfg, tm)

    def norm_rows(x_ref):
        for s in range(tm // CHUNK):
            seq = _seq_of_chunk(i * (tm // CHUNK) + s, cfg)
            rows = slice(s * CHUNK, (s + 1) * CHUNK)
            h = _modulated_norm(x_ref[rows, :], g_ref[...], scale_ref[pl.ds(seq, 1), :], shift_ref[pl.ds(seq, 1), :])
            h_scr[rows, :] = h.astype(BF16)

    pl.when(jnp.logical_and(first_col, i < npt))(lambda: norm_rows(xp_ref))
    pl.when(jnp.logical_and(first_col, i >= npt))(lambda: norm_rows(xs_ref))
    o_ref[...] = jnp.dot(h_scr[...], w_ref[...], preferred_element_type=F32)


def _inproj(x_p, x_s, shift, scale, g, w_bf16, cfg, tm=512, tn=1024):
    d = x_p.shape[1]
    n = cfg.n_rows
    nc = w_bf16.shape[1]
    s = shift.shape[0]
    return pl.pallas_call(
        functools.partial(_inproj_kernel, cfg=cfg, tm=tm),
        grid=(n // tm, nc // tn),
        in_specs=_row_specs(cfg, tm, d, lambda j: 0, buffers=1) + [
                  pl.BlockSpec((s, d), lambda i, j: (0, 0)),
                  pl.BlockSpec((s, d), lambda i, j: (0, 0)),
                  pl.BlockSpec((1, d), lambda i, j: (0, 0)),
                  pl.BlockSpec((d, tn), lambda i, j: (0, j))],
        out_specs=pl.BlockSpec((tm, tn), lambda i, j: (i, j)),
        out_shape=jax.ShapeDtypeStruct((n, nc), F32),
        scratch_shapes=[pltpu.VMEM((tm, d), BF16)],
        compiler_params=_params("parallel", "arbitrary"),
        name="inproj",
    )(x_p, x_s, shift, scale, g, w_bf16)


def _mamba_kernel(zx_ref, dt_ref, cst_ref, sst_ref, cw_ref, cb_ref, dtb_ref, alog_ref, dsk_ref, ng_ref, exp_ref,
                  ym_ref, ssm_ref, buf, u_scr, *, cfg):
    c = pl.program_id(0)
    mi, nb, cd = cfg.mi, cfg.nb, cfg.cd
    L = CHUNK

    @pl.when(_is_first_chunk(c, cfg))
    def _():
        buf[0:8, :] = cst_ref[0]
        ssm_ref[0] = sst_ref[0]

    xbc = zx_ref[:, mi:mi + cd]
    buf[8:8 + L, :] = xbc
    conv = (cb_ref[...] + buf[5:5 + L, :] * cw_ref[0:1, :] + buf[6:6 + L, :] * cw_ref[1:2, :]
            + buf[7:7 + L, :] * cw_ref[2:3, :] + xbc * cw_ref[3:4, :])
    buf[0:8, :] = buf[L:L + 8, :]
    u_scr[...] = _silu(conv)

    dt = jax.nn.softplus(dt_ref[...] + dtb_ref[...])
    da = dt * (-jnp.exp(alog_ref[...]))
    row = lax.broadcasted_iota(jnp.int32, (L, 256), 0)
    pos = lax.broadcasted_iota(jnp.int32, (L, 256), 1) % HEAD
    causal = row >= pos
    diag = row == pos
    tri2 = (lax.broadcasted_iota(jnp.int32, (L, 2 * L), 0)
            >= lax.broadcasted_iota(jnp.int32, (L, 2 * L), 1) % L).astype(BF16)
    lane = lax.broadcasted_iota(jnp.int32, (L, PAIR), 1)
    row16 = lax.broadcasted_iota(jnp.int32, (16, 256), 0)
    ones16 = jnp.ones((16, LANES), BF16)
    dt_hl = jnp.concatenate(_split_hi_lo(dt), axis=1)
    da_hl = jnp.concatenate(_split_hi_lo(da), axis=1)

    for g in range(cfg.groups):
        cs = slice(g * 256, (g + 1) * 256)
        xh = u_scr[:, cs]
        bg = u_scr[:, mi + g * LANES:mi + (g + 1) * LANES]
        cg = u_scr[:, mi + nb + g * LANES:mi + nb + (g + 1) * LANES]
        eg = exp_ref[:, cs]
        dte = jnp.dot(dt_hl, eg, preferred_element_type=F32)
        dae = jnp.dot(da_hl, eg, preferred_element_type=F32)
        cum = jnp.dot(tri2, jnp.concatenate(_split_hi_lo(dae), axis=0), preferred_element_type=F32)
        cum_row = jnp.sum(jnp.where(diag, cum, 0.0), axis=0, keepdims=True)
        last = cum[L - 1:L, :]
        decay = jnp.exp(jnp.where(causal, cum - cum_row, -jnp.inf))
        xdt = xh * dte
        sc2 = _nt(cg, jnp.concatenate([bg, bg], axis=0))
        ys = []
        for q in range(2):
            ps = slice(q * PAIR, (q + 1) * PAIR)
            xq = xdt[:, ps]
            rhs = jnp.concatenate([jnp.where(lane < HEAD, xq, 0.0), jnp.where(lane >= HEAD, xq, 0.0)], axis=0)
            ys.append(jnp.dot(decay[:, ps] * sc2, rhs, preferred_element_type=F32))
        y = jnp.concatenate(ys, axis=1)
        sg = ssm_ref[0, cs, :]
        y = y + _nt(cg, sg) * jnp.exp(cum)
        new = _tn(xdt * jnp.exp(last - cum), bg)
        e_last = jnp.exp(last)
        e_hi = e_last.astype(BF16).astype(F32)
        e_rows = jnp.where(row16 == 0, e_hi, jnp.where(row16 == 1, e_last - e_hi, 0.0)).astype(BF16)
        dcol = _tn(e_rows, ones16)
        ssm_ref[0, cs, :] = sg * dcol + new
        y = y + xh * dsk_ref[:, cs]
        y = y * _silu(zx_ref[:, cs])
        y = y * lax.rsqrt(jnp.mean(y * y, axis=-1, keepdims=True) + M_NORM_EPS)
        ym_ref[:, cs] = (y * ng_ref[:, cs]).astype(BF16)


def _mamba(proj, conv_state8, ssm_state, cw, cb, dtb, alog, dsk, ng, expand, cfg):
    n = proj.shape[0]
    mi, cd = cfg.mi, cfg.cd
    nst = ssm_state.shape[-1]
    seq = lambda c: _seq_of_chunk(c, cfg)
    full = lambda a: pl.BlockSpec(a.shape, lambda c: (0,) * a.ndim)
    return pl.pallas_call(
        functools.partial(_mamba_kernel, cfg=cfg),
        grid=(cfg.n_chunks,),
        in_specs=[pl.BlockSpec((CHUNK, mi + cd), lambda c: (c, 0)),
                  pl.BlockSpec((CHUNK, LANES), lambda c: (c, cfg.o_dt // LANES)),
                  pl.BlockSpec((1, 8, cd), lambda c: (seq(c), 0, 0)),
                  pl.BlockSpec((1, mi, nst), lambda c: (seq(c), 0, 0)),
                  full(cw), full(cb), full(dtb), full(alog), full(dsk), full(ng), full(expand)],
        out_specs=[pl.BlockSpec((CHUNK, mi), lambda c: (c, 0)),
                   pl.BlockSpec((1, mi, nst), lambda c: (seq(c), 0, 0))],
        out_shape=[jax.ShapeDtypeStruct((n, mi), BF16),
                   jax.ShapeDtypeStruct((cfg.n_seq, mi, nst), F32)],
        scratch_shapes=[pltpu.VMEM((CHUNK + 8, cd), F32), pltpu.VMEM((CHUNK, cd), F32)],
        compiler_params=_params("arbitrary"),
        name="mamba",
    )(proj, proj, conv_state8, ssm_state, cw, cb, dtb, alog, dsk, ng, expand)


def _seg_sum(x, ob):
    parts = [jnp.dot(x[:, p * PAIR:(p + 1) * PAIR], ob, preferred_element_type=F32) for p in range(x.shape[1] // PAIR)]
    return jnp.concatenate(parts, axis=1)


def _split_hi_lo(x):
    hi = x.astype(BF16)
    lo = (x - hi.astype(F32)).astype(BF16)
    return hi, lo


def _rwkv_kernel(rkv_ref, lora_ref, sh_ref, wkv_in_ref, mu_ref, w0_ref, wup_ref, a0_ref, aup_ref, gup_ref,
                 kk_ref, ka_ref, rk_ref, lng_ref, lnb_ref,
                 yr_ref, wkv_ref, pbuf, lbuf, w_s, a_s, b_s, k_s, y_s, vt_s, ut_s, y0_s, g_s, *, cfg):
    c = pl.program_id(0)
    rw = cfg.rw
    L = CHUNK
    npair = rw // PAIR

    @pl.when(_is_first_chunk(c, cfg))
    def _():
        pbuf[0:8, :] = sh_ref[0, :, 0:3 * rw]
        lbuf[0:8, :] = sh_ref[0, :, 3 * rw:]
        wkv_ref[0] = wkv_in_ref[0]

    pr = rkv_ref[...]
    pl_ = lora_ref[...]
    pbuf[8:8 + L, :] = pr
    lbuf[8:8 + L, :] = pl_
    xs = pr + (pbuf[7:7 + L, :] - pr) * mu_ref[:, 0:3 * rw]
    xl = pl_ + (lbuf[7:7 + L, :] - pl_) * mu_ref[:, 3 * rw:]
    pbuf[0:8, :] = pbuf[L:L + 8, :]
    lbuf[0:8, :] = lbuf[L:L + 8, :]

    r = xs[:, 0:rw]
    k = xs[:, rw:2 * rw]
    v = xs[:, 2 * rw:3 * rw]
    wd = xl[:, 0:LANES]
    ad = xl[:, LANES:2 * LANES]
    gd = xl[:, 2 * LANES:]

    lane = lax.broadcasted_iota(jnp.int32, (PAIR, PAIR), 1)
    rowi = lax.broadcasted_iota(jnp.int32, (PAIR, PAIR), 0)
    ob = ((lane // HEAD) == (rowi // HEAD)).astype(F32)

    w_z = w0_ref[...] + jnp.dot(jnp.tanh(wd), wup_ref[...], preferred_element_type=F32)
    logw = -(jax.nn.sigmoid(w_z) * EXP_NEG_HALF)
    w_s[...] = jnp.exp(logw)
    a = jax.nn.sigmoid(a0_ref[...] + jnp.dot(ad, aup_ref[...], preferred_element_type=F32))
    kk = k * kk_ref[...]
    kk = kk * lax.rsqrt(jnp.maximum(_seg_sum(kk * kk, ob), 1e-24))
    kh = k * (1.0 + (a - 1.0) * ka_ref[...])
    a_s[...] = -kk
    b_s[...] = kk * a
    k_s[...] = kh

    tri2 = (lax.broadcasted_iota(jnp.int32, (L, 2 * L), 0)
            >= lax.broadcasted_iota(jnp.int32, (L, 2 * L), 1) % L).astype(BF16)
    cum = jnp.dot(tri2, jnp.concatenate(_split_hi_lo(logw), axis=0), preferred_element_type=F32)
    p_inv = jnp.exp(-cum)
    r_p = r * jnp.exp(cum)
    b_p = b_s[...] * p_inv
    k_p = kh * p_inv
    lo_lanes = lane < HEAD
    g_row = lax.broadcasted_iota(jnp.int32, (2 * PAIR, PAIR), 0)
    g_lane = lax.broadcasted_iota(jnp.int32, (2 * PAIR, PAIR), 1)
    g_keep = (g_row % HEAD) <= (g_lane % HEAD)

    def by_head(x):
        x2 = jnp.concatenate([x, x], axis=0)
        return jnp.where(lo_lanes == (rowi < HEAD), x2, 0.0).astype(BF16)

    for p in range(npair):
        ps = slice(p * PAIR, (p + 1) * PAIR)
        hs = slice(p * HEAD, (p + 1) * HEAD)
        r_bd = by_head(r_p[:, ps])
        y0_s[hs, :] = _nt(wkv_ref[0, p].astype(BF16), r_bd)
        gram = _nt(jnp.concatenate([by_head(b_p[:, ps]), by_head(k_p[:, ps])], axis=0), r_bd)
        g_s[p] = jnp.where(g_keep, gram, 0.0).astype(BF16)
        vt = v[:, ps].T
        vjt = jnp.concatenate([vt[0:HEAD], vt[HEAD:]], axis=1)
        hi = vjt.astype(BF16).astype(F32)
        vt_s[hs, :] = jnp.concatenate([hi, vjt - hi], axis=1).astype(BF16)

    ob_b = ob.astype(BF16)
    ob2 = jnp.concatenate([ob_b, ob_b], axis=0)
    lane_t = lax.broadcasted_iota(jnp.int32, (HEAD, PAIR), 1) % HEAD
    crow = lax.broadcasted_iota(jnp.int32, (2 * PAIR, 2 * PAIR), 0)
    clane = lax.broadcasted_iota(jnp.int32, (2 * PAIR, 2 * PAIR), 1)
    same_head = ((crow % PAIR) // HEAD) == ((clane % PAIR) // HEAD)
    c_dt = (crow % HEAD) - clane // PAIR

    def steps(blk, carry):
        t0 = pl.multiple_of(blk * 8, 8)
        rows = pl.ds(t0, 8)
        w8, a8, b8, k8 = w_s[rows, :], a_s[rows, :], b_s[rows, :], k_s[rows, :]
        s = [wkv_ref[0, p] for p in range(npair)]
        ut = [ut_s[p * HEAD:(p + 1) * HEAD, :] for p in range(npair)]
        vcol2 = None
        for i in range(8):
            row = lambda x8, p: x8[i:i + 1, p * PAIR:(p + 1) * PAIR]
            if i % 2 == 0:
                sel = (same_head & (c_dt == t0 + i)).astype(BF16)
                vcol2 = jnp.dot(vt_s[...], sel, preferred_element_type=F32)
            vcol = vcol2[:, (i % 2) * PAIR:(i % 2 + 1) * PAIR]
            for p0 in range(0, npair, SA_GROUP):
                group = range(p0, min(p0 + SA_GROUP, npair))
                lhs = []
                for p in group:
                    hi, lo = _split_hi_lo(s[p] * row(a8, p))
                    lhs.append(jnp.concatenate([hi, lo], axis=1))
                sa = jnp.dot(jnp.concatenate(lhs, axis=0), ob2, preferred_element_type=F32)
                for q, p in enumerate(group):
                    sa_p = sa[q * HEAD:(q + 1) * HEAD]
                    s[p] = s[p] * row(w8, p) + sa_p * row(b8, p) + vcol[p * HEAD:(p + 1) * HEAD] * row(k8, p)
                    ut[p] = jnp.where(lane_t == t0 + i, sa_p, ut[p])
        for p in range(npair):
            wkv_ref[0, p] = s[p]
            ut_s[p * HEAD:(p + 1) * HEAD, :] = ut[p]
        return carry

    ut_s[...] = jnp.zeros_like(ut_s)
    lax.fori_loop(0, L // 8, steps, 0)

    for p in range(npair):
        hs = slice(p * HEAD, (p + 1) * HEAD)
        lhs = jnp.concatenate([ut_s[hs, :].astype(BF16), vt_s[hs, 0:PAIR]], axis=1)
        yt = y0_s[hs, :] + jnp.dot(lhs, g_s[p], preferred_element_type=F32)
        ytt = yt.T
        y_s[:, p * PAIR:(p + 1) * PAIR] = jnp.concatenate([ytt[0:L], ytt[L:]], axis=1)

    y = y_s[...]
    mu = _seg_sum(y, ob) * (1.0 / HEAD)
    dlt = y - mu
    var = _seg_sum(dlt * dlt, ob) * (1.0 / HEAD)
    yn = dlt * lax.rsqrt(var + R_LN_EPS) * lng_ref[...] + lnb_ref[...]
    bonus = _seg_sum(r * k_s[...] * rk_ref[...], ob) * v
    g = jnp.dot(jax.nn.sigmoid(gd), gup_ref[...], preferred_element_type=F32)
    yr_ref[...] = ((yn + bonus) * g).astype(BF16)


def _rwkv(proj, shift8, wkv_pairs, mu, w0, wup, a0, aup, gup, kk, ka, rk, lng, lnb, cfg):
    n = proj.shape[0]
    rw, lp = cfg.rw, cfg.lp
    npair = rw // PAIR
    seq = lambda c: _seq_of_chunk(c, cfg)
    full = lambda a: pl.BlockSpec(a.shape, lambda c: (0,) * a.ndim)
    row_scr = pltpu.VMEM((CHUNK, rw), F32)
    return pl.pallas_call(
        functools.partial(_rwkv_kernel, cfg=cfg),
        grid=(cfg.n_chunks,),
        in_specs=[pl.BlockSpec((CHUNK, 3 * rw), lambda c: (c, cfg.o_r // (3 * rw))),
                  pl.BlockSpec((CHUNK, lp), lambda c: (c, cfg.o_lora // lp)),
                  pl.BlockSpec((1, 8, 3 * rw + lp), lambda c: (seq(c), 0, 0)),
                  pl.BlockSpec((1, npair, HEAD, PAIR), lambda c: (seq(c), 0, 0, 0)),
                  full(mu), full(w0), full(wup), full(a0), full(aup), full(gup),
                  full(kk), full(ka), full(rk), full(lng), full(lnb)],
        out_specs=[pl.BlockSpec((CHUNK, rw), lambda c: (c, 0)),
                   pl.BlockSpec((1, npair, HEAD, PAIR), lambda c: (seq(c), 0, 0, 0))],
        out_shape=[jax.ShapeDtypeStruct((n, rw), BF16),
                   jax.ShapeDtypeStruct((cfg.n_seq, npair, HEAD, PAIR), F32)],
        scratch_shapes=[pltpu.VMEM((CHUNK + 8, 3 * rw), F32), pltpu.VMEM((CHUNK + 8, lp), F32),
                        row_scr, row_scr, row_scr, row_scr, row_scr,
                        pltpu.VMEM((npair * HEAD, 2 * PAIR), BF16), pltpu.VMEM((npair * HEAD, PAIR), F32),
                        pltpu.VMEM((npair * HEAD, PAIR), F32), pltpu.VMEM((npair, 2 * PAIR, PAIR), BF16)],
        compiler_params=_params("arbitrary"),
        name="rwkv",
    )(proj, proj, shift8, wkv_pairs, mu, w0, wup, a0, aup, gup, kk, ka, rk, lng, lnb)


def _outproj_kernel(ym_ref, yr_ref, wa_ref, wb_ref, xp_ref, xs_ref, gate_ref, o_ref, *, cfg, tm):
    i = pl.program_id(0)
    acc = (jnp.dot(ym_ref[...], wa_ref[...], preferred_element_type=F32)
           + jnp.dot(yr_ref[...], wb_ref[...], preferred_element_type=F32))

    def residual(x_ref):
        for s in range(tm // CHUNK):
            seq = _seq_of_chunk(i * (tm // CHUNK) + s, cfg)
            rows = slice(s * CHUNK, (s + 1) * CHUNK)
            o_ref[rows, :] = x_ref[rows, :] + gate_ref[pl.ds(seq, 1), :] * acc[rows, :]

    npt = _prompt_tiles(cfg, tm)
    pl.when(i < npt)(lambda: residual(xp_ref))
    pl.when(i >= npt)(lambda: residual(xs_ref))


def _outproj(ym, yr, wa, wb, x_p, x_s, gate, cfg, tm=512, tn=1024):
    n, d = cfg.n_rows, x_p.shape[1]
    ka, kb = ym.shape[1], yr.shape[1]
    s = gate.shape[0]
    return pl.pallas_call(
        functools.partial(_outproj_kernel, cfg=cfg, tm=tm),
        grid=(n // tm, d // tn),
        in_specs=[pl.BlockSpec((tm, ka), lambda i, j: (i, 0)),
                  pl.BlockSpec((tm, kb), lambda i, j: (i, 0)),
                  pl.BlockSpec((ka, tn), lambda i, j: (0, j)),
                  pl.BlockSpec((kb, tn), lambda i, j: (0, j))]
                 + _row_specs(cfg, tm, tn, lambda j: j)
                 + [pl.BlockSpec((s, tn), lambda i, j: (0, j))],
        out_specs=pl.BlockSpec((tm, tn), lambda i, j: (i, j)),
        out_shape=jax.ShapeDtypeStruct((n, d), F32),
        compiler_params=_params("parallel", "parallel"),
        name="outproj",
    )(ym, yr, wa, wb, x_p, x_s, gate)


def _router_kernel(x_ref, shift_ref, scale_ref, g_ref, wr_ref, br_ref, h_ref, lg_ref, *, cfg, tm):
    i = pl.program_id(0)
    for s in range(tm // CHUNK):
        seq = _seq_of_chunk(i * (tm // CHUNK) + s, cfg)
        rows = slice(s * CHUNK, (s + 1) * CHUNK)
        h = _modulated_norm(x_ref[rows, :], g_ref[...], scale_ref[pl.ds(seq, 1), :], shift_ref[pl.ds(seq, 1), :])
        h_ref[rows, :] = h.astype(BF16)
        lg_ref[rows, :] = jnp.dot(h, wr_ref[...], precision=HI, preferred_element_type=F32) + br_ref[...]


def _router(x1, shift, scale, g, wr, br, cfg, tm=256):
    n, d = x1.shape
    s = shift.shape[0]
    full = lambda a: pl.BlockSpec(a.shape, lambda i: (0,) * a.ndim)
    return pl.pallas_call(
        functools.partial(_router_kernel, cfg=cfg, tm=tm),
        grid=(n // tm,),
        in_specs=[pl.BlockSpec((tm, d), lambda i: (i, 0)), full(shift), full(scale), full(g), full(wr), full(br)],
        out_specs=[pl.BlockSpec((tm, d), lambda i: (i, 0)), pl.BlockSpec((tm, LANES), lambda i: (i, 0))],
        out_shape=[jax.ShapeDtypeStruct((n, d), BF16), jax.ShapeDtypeStruct((n, LANES), F32)],
        compiler_params=_params("parallel"),
        name="router",
    )(x1, shift, scale, g, wr, br)


def _new_weights(be_ref):
    i = pl.program_id(1)
    return jnp.logical_or(i == 0, be_ref[i] != be_ref[jnp.maximum(i - 1, 0)])


def _expert_up_kernel(be_ref, nu_ref, x_ref, wg_ref, wu_ref, h_ref, wg_s, wu_s):
    i = pl.program_id(1)

    @pl.when(_new_weights(be_ref))
    def _():
        wg_s[...] = wg_ref[0].astype(BF16)
        wu_s[...] = wu_ref[0].astype(BF16)

    @pl.when(i < nu_ref[0])
    def _():
        x = x_ref[...]
        gate = jnp.dot(x, wg_s[...], preferred_element_type=F32)
        up = jnp.dot(x, wu_s[...], preferred_element_type=F32)
        h_ref[...] = (_silu(gate) * up).astype(BF16)

    @pl.when(i >= nu_ref[0])
    def _():
        h_ref[...] = jnp.zeros_like(h_ref)


def _expert_down_kernel(be_ref, nu_ref, h_ref, wd_ref, y_ref, wd_s):
    i = pl.program_id(1)

    @pl.when(_new_weights(be_ref))
    def _():
        wd_s[...] = wd_ref[0].astype(BF16)

    @pl.when(i < nu_ref[0])
    def _():
        y_ref[...] = jnp.dot(h_ref[...], wd_s[...], preferred_element_type=F32)

    @pl.when(i >= nu_ref[0])
    def _():
        y_ref[...] = jnp.zeros_like(y_ref)


def _experts(xb, blk_e, n_used, wg, wu, wd, tm, tf=256, tn=1024):
    rows, d = xb.shape
    f = wg.shape[2]
    nblk = rows // tm
    hid = pl.pallas_call(
        _expert_up_kernel,
        grid_spec=pltpu.PrefetchScalarGridSpec(
            num_scalar_prefetch=2, grid=(f // tf, nblk),
            in_specs=[pl.BlockSpec((tm, d), lambda j, i, be, nu: (i, 0)),
                      pl.BlockSpec((1, d, tf), lambda j, i, be, nu: (be[i], 0, j)),
                      pl.BlockSpec((1, d, tf), lambda j, i, be, nu: (be[i], 0, j))],
            out_specs=pl.BlockSpec((tm, tf), lambda j, i, be, nu: (i, j)),
            scratch_shapes=[pltpu.VMEM((d, tf), BF16), pltpu.VMEM((d, tf), BF16)]),
        out_shape=jax.ShapeDtypeStruct((rows, f), BF16),
        compiler_params=_params("arbitrary", "arbitrary"),
        name="expert_up",
    )(blk_e, n_used, xb, wg, wu)
    return pl.pallas_call(
        _expert_down_kernel,
        grid_spec=pltpu.PrefetchScalarGridSpec(
            num_scalar_prefetch=2, grid=(d // tn, nblk),
            in_specs=[pl.BlockSpec((tm, f), lambda j, i, be, nu: (i, 0)),
                      pl.BlockSpec((1, f, tn), lambda j, i, be, nu: (be[i], 0, j))],
            out_specs=pl.BlockSpec((tm, tn), lambda j, i, be, nu: (i, j)),
            scratch_shapes=[pltpu.VMEM((f, tn), BF16)]),
        out_shape=jax.ShapeDtypeStruct((rows, d), F32),
        compiler_params=_params("arbitrary", "arbitrary"),
        name="expert_down",
    )(blk_e, n_used, hid, wd)


def _final_kernel(x_ref, moe_ref, gate_ref, g_ref, o_ref, *, cfg, tm, tile0):
    i = pl.program_id(0) + tile0
    for s in range(tm // CHUNK):
        seq = _seq_of_chunk(i * (tm // CHUNK) + s, cfg)
        rows = slice(s * CHUNK, (s + 1) * CHUNK)
        x = x_ref[rows, :] + gate_ref[pl.ds(seq, 1), :] * moe_ref[rows, :]
        o_ref[rows, :] = x * lax.rsqrt(jnp.mean(x * x, axis=-1, keepdims=True) + NORM_EPS) * g_ref[...]


def _final(x1, moe, gate, g, cfg, row0, nrows, tm=256):
    d = x1.shape[1]
    assert row0 % tm == 0 and nrows % tm == 0
    tile0 = row0 // tm
    full = lambda a: pl.BlockSpec(a.shape, lambda i: (0,) * a.ndim)
    return pl.pallas_call(
        functools.partial(_final_kernel, cfg=cfg, tm=tm, tile0=tile0),
        grid=(nrows // tm,),
        in_specs=[pl.BlockSpec((tm, d), lambda i: (i + tile0, 0)), pl.BlockSpec((tm, d), lambda i: (i + tile0, 0)),
                  full(gate), full(g)],
        out_specs=pl.BlockSpec((tm, d), lambda i: (i, 0)),
        out_shape=jax.ShapeDtypeStruct((nrows, d), F32),
        compiler_params=_params("parallel"),
        name="final_norm",
    )(x1, moe, gate, g)


def _dispatch(logits, n_experts, tm):
    n = logits.shape[0]
    glog = logits[:, :E_GROUPS]
    elog = logits[:, E_GROUPS:E_GROUPS + n_experts].reshape(n, E_GROUPS, E_PER_GROUP)
    gprob = jax.nn.softmax(glog, axis=-1)
    gsel = jnp.argmax(glog, axis=-1)
    elog_g = jnp.take_along_axis(elog, gsel[:, None, None], axis=1)[:, 0]
    eprob = jax.nn.softmax(elog_g, axis=-1)
    topv, topi = lax.top_k(eprob, TOP_K)
    pg = jnp.take_along_axis(gprob, gsel[:, None], axis=1)
    wts = topv / jnp.sum(topv, axis=-1, keepdims=True) * pg
    eid = (gsel[:, None] * E_PER_GROUP + topi).astype(jnp.int32)

    m = n * TOP_K
    e_flat = eid.reshape(m)
    order = jnp.argsort(e_flat)
    e_sorted = e_flat[order]
    counts = jnp.bincount(e_flat, length=n_experts).astype(jnp.int32)
    padded = (counts + tm - 1) // tm * tm
    start = jnp.cumsum(counts) - counts
    pend = jnp.cumsum(padded)
    pstart = pend - padded
    dest_sorted = pstart[e_sorted] + jnp.arange(m, dtype=jnp.int32) - start[e_sorted]
    n_blocks = -(-(m + n_experts * (tm - 1)) // tm)
    rows = n_blocks * tm
    tok_sorted = (order // TOP_K).astype(jnp.int32)
    row_tok = jnp.zeros((rows,), jnp.int32).at[dest_sorted].set(tok_sorted)
    slot_pos = jnp.zeros((m,), jnp.int32).at[order].set(dest_sorted).reshape(n, TOP_K)
    n_used = (pend[-1] // tm).astype(jnp.int32).reshape(1)
    blk = jnp.arange(n_blocks, dtype=jnp.int32)
    blk_e = jnp.minimum(jnp.searchsorted(pend, blk * tm, side='right'), n_experts - 1).astype(jnp.int32)
    last_e = blk_e[jnp.maximum(n_used[0] - 1, 0)]
    blk_e = jnp.where(blk < n_used[0], blk_e, last_e)
    return row_tok, slot_pos, wts, blk_e, n_used


def _pad_rcols(a, cfg):
    o = 3 * cfg.rw
    z = lambda w: jnp.zeros(a.shape[:-1] + (w,), a.dtype)
    return jnp.concatenate([a[..., :o + cfg.lw], z(LANES - cfg.lw),
                            a[..., o + cfg.lw:o + cfg.lw + cfg.la], z(LANES - cfg.la),
                            a[..., o + cfg.lw + cfg.la:]], axis=-1)


def _unpad_rcols(a, cfg):
    o = 3 * cfg.rw
    return jnp.concatenate([a[..., :o + cfg.lw], a[..., o + LANES:o + LANES + cfg.la], a[..., o + 2 * LANES:]], axis=-1)


def _pad_rows(a, rows):
    return jnp.concatenate([a, jnp.zeros((rows - a.shape[0],) + a.shape[1:], a.dtype)], axis=0)


def kernel(x_prompt, x_sample, state_conv, state_ssm, state_shift, state_wkv, c_prompt, c_sample, norm1_g, w_mod, b_mod, w_in, conv_w, conv_b, dt_bias, a_log, d_skip, m_norm_g, shift_mu, w0, w_up, a0, a_up, g_up, k_k, k_a, r_k, ln_x_g, ln_x_b, w_out, norm2_g, w_grp, b_grp, w_erouter, b_erouter, e_gate, e_up, e_down, final_norm_g):
    assert w_mod.shape[0] == 1, "single-layer trunk"
    bp, tp, d = x_prompt.shape
    bs, ts, _ = x_sample.shape
    mi = m_norm_g.shape[-1]
    rw = w0.shape[-1]
    lw, la, lg = w_up.shape[1], a_up.shape[1], g_up.shape[1]
    tn_in = 1024
    cfg0 = Cfg(d, bp, tp, bs, ts, mi, rw, lw, la, lg, 0)
    nc = -(-(cfg0.o_dt + LANES) // tn_in) * tn_in
    cfg = cfg0._replace(nc=nc)
    assert tp % CHUNK == 0 and ts % CHUNK == 0 and cfg.o_r == 3 * rw and cfg.o_lora % cfg.lp == 0
    assert state_conv.shape[-1] == cfg.cd and lw <= LANES and la <= LANES and cfg.mh <= LANES
    n = cfg.n_rows
    n_seq = cfg.n_seq
    mh, rh = cfg.mh, cfg.rh
    npair = rw // PAIR
    n_experts = e_gate.shape[1]

    x_p = x_prompt.reshape(bp * tp, d)
    x_s = x_sample.reshape(bs * ts, d)
    c_all = jnp.concatenate([c_prompt, c_sample], axis=0)

    mod = _modulation(c_all, w_mod[0], b_mod[0])
    shift1, scale1, gate1, shift2, scale2, gate2 = [mod[:, i * d:(i + 1) * d] for i in range(6)]

    wi = w_in[0]
    o2 = mi + cfg.cd
    o3 = o2 + mh
    o4 = o3 + 3 * rw + lw
    w_proj = jnp.zeros((d, nc), BF16)
    for dst, lo, hi in ((0, 0, o2), (cfg.o_r, o3, o4), (cfg.o_lora + LANES, o4, o4 + la),
                        (cfg.o_lora + 2 * LANES, o4 + la, wi.shape[1]), (cfg.o_dt, o2, o3)):
        w_proj = lax.dynamic_update_slice(w_proj, wi[:, lo:hi].astype(BF16), (0, dst))
    proj = _inproj(x_p, x_s, shift1, scale1, norm1_g, w_proj, cfg, tn=tn_in)

    zeros = lambda b, *s: jnp.zeros((b,) + s, F32)
    conv0 = jnp.concatenate([zeros(bp, CONV_W - 1, cfg.cd), state_conv[0]], axis=0)
    conv8 = jnp.concatenate([zeros(n_seq, 8 - (CONV_W - 1), cfg.cd), conv0], axis=1)
    nst = state_ssm.shape[-1]
    ssm0 = jnp.concatenate([zeros(bp, mi, nst), state_ssm[0].reshape(bs, mi, nst)], axis=0)
    lane_pad = lambda a: jnp.concatenate([a.reshape(1, -1), jnp.zeros((1, LANES - a.shape[-1]), F32)], axis=1)
    expand = ((jnp.arange(2 * LANES)[:, None] % LANES) == (jnp.arange(mi)[None, :] // HEAD)).astype(BF16)
    ym, ssm_new = _mamba(proj, conv8, ssm0, conv_w[0], conv_b[0].reshape(1, -1), lane_pad(dt_bias[0]),
                         lane_pad(a_log[0]), jnp.repeat(d_skip[0], HEAD).reshape(1, mi), m_norm_g[0].reshape(1, mi),
                         expand, cfg)

    sh0 = jnp.concatenate([zeros(bp, 1, state_shift.shape[-1]), state_shift[0]], axis=0)
    sh8 = jnp.concatenate([zeros(n_seq, 7, 3 * rw + cfg.lp), _pad_rcols(sh0, cfg)], axis=1)
    to_pairs = lambda s: s.reshape(-1, npair, 2, HEAD, HEAD).transpose(0, 1, 3, 2, 4).reshape(-1, npair, HEAD, PAIR)
    from_pairs = lambda s: s.reshape(-1, npair, HEAD, 2, HEAD).transpose(0, 1, 3, 2, 4).reshape(-1, rh, HEAD, HEAD)
    wkv0 = jnp.concatenate([zeros(bp, npair, HEAD, PAIR), to_pairs(state_wkv[0])], axis=0)
    row = lambda a: a.reshape(1, -1)
    yr, wkv_new = _rwkv(proj, sh8, wkv0, _pad_rcols(row(shift_mu[0]), cfg), row(w0[0]),
                        _pad_rows(w_up[0], LANES), row(a0[0]), _pad_rows(a_up[0], LANES), g_up[0],
                        row(k_k[0]), row(k_a[0]), row(r_k[0]), row(ln_x_g[0]), row(ln_x_b[0]), cfg)

    wo = w_out[0].astype(BF16)
    x1 = _outproj(ym, yr, wo[:mi], wo[mi:], x_p, x_s, gate1, cfg)
    wr = jnp.concatenate([w_grp[0], w_erouter[0], jnp.zeros((d, LANES - E_GROUPS - n_experts), F32)], axis=1)
    br = lane_pad(jnp.concatenate([b_grp[0], b_erouter[0]]))
    h2, logits = _router(x1, shift2, scale2, norm2_g, wr, br, cfg)

    tm_e = 256
    row_tok, slot_pos, wts, blk_e, n_used = _dispatch(logits, n_experts, tm_e)
    yb = _experts(h2[row_tok], blk_e, n_used, e_gate[0], e_up[0], e_down[0], tm_e)
    moe = yb[slot_pos[:, 0]] * wts[:, 0:1] + yb[slot_pos[:, 1]] * wts[:, 1:2]

    np_ = bp * tp
    fg = final_norm_g.reshape(1, d)
    y_prompt = _final(x1, moe, gate2, fg, cfg, 0, np_).reshape(bp, tp, d)
    y_sample = _final(x1, moe, gate2, fg, cfg, np_, bs * ts).reshape(bs, ts, d)
    tails = lambda b, t, base: jnp.stack(
        [lax.slice(proj, (base + (i + 1) * t - (CONV_W - 1), 0), (base + (i + 1) * t, nc)) for i in range(b)])
    pp = tails(bp, tp, 0)
    ps = tails(bs, ts, np_)
    conv_of = lambda p: p[:, :, mi:mi + cfg.cd][None]
    shift_of = lambda p: _unpad_rcols(p[:, -1:, cfg.o_r:cfg.o_dt], cfg)[None]
    ssm_new = ssm_new.reshape(n_seq, mh, HEAD, nst)
    wkv_new = from_pairs(wkv_new)
    return (y_prompt, y_sample,
            conv_of(pp), ssm_new[:bp][None], shift_of(pp), wkv_new[:bp][None],
            conv_of(ps), ssm_new[bp:][None], shift_of(ps), wkv_new[bp:][None])
```

```python
import functools
from typing import NamedTuple

import jax
import jax.numpy as jnp
from jax import lax
from jax.experimental import pallas as pl
from jax.experimental.pallas import tpu as pltpu

F32 = jnp.float32
BF16 = jnp.bfloat16
HI = lax.Precision.HIGHEST

LANES = 128
CHUNK = 64
HEAD = 64
PAIR = 2 * HEAD
NORM_EPS = 1e-6
M_NORM_EPS = 1e-5
R_LN_EPS = 64e-5
CONV_W = 4
E_GROUPS = 4
E_PER_GROUP = 8
TOP_K = 2
VMEM_LIMIT = 56 * 1024 * 1024
E_SUB = 256
EXP_NEG_HALF = 0.6065306597126334
SA_GROUP = 8


class Cfg(NamedTuple):
    d: int
    bp: int
    tp: int
    bs: int
    ts: int
    mi: int
    rw: int
    lw: int
    la: int
    lg: int

    @property
    def cpp(self):
        return self.tp // CHUNK

    @property
    def cps(self):
        return self.ts // CHUNK

    @property
    def n_chunks(self):
        return self.bp * self.cpp + self.bs * self.cps

    @property
    def n_seq(self):
        return self.bp + self.bs

    @property
    def n_rows(self):
        return self.n_chunks * CHUNK

    @property
    def groups(self):
        return self.mi // 256

    @property
    def nb(self):
        return self.groups * LANES

    @property
    def cd(self):
        return self.mi + 2 * self.nb

    @property
    def mh(self):
        return self.mi // HEAD

    @property
    def rh(self):
        return self.rw // HEAD

    @property
    def o_rw(self):
        return self.mi + self.cd

    @property
    def r_cols(self):
        return 3 * self.rw + self.lw + self.la + self.lg

    @property
    def tail(self):
        return -(-(self.lw + self.la + self.lg) // LANES) * LANES

    @property
    def win(self):
        return 3 * self.rw + self.tail

    @property
    def o_dt(self):
        return self.o_rw + self.win

    @property
    def nc(self):
        return self.o_dt + LANES


def _seq_of_chunk(c, cfg):
    npc = cfg.bp * cfg.cpp
    return jnp.where(c < npc, c // cfg.cpp, cfg.bp + (c - npc) // cfg.cps)


def _is_first_chunk(c, cfg):
    npc = cfg.bp * cfg.cpp
    return jnp.where(c < npc, c % cfg.cpp == 0, (c - npc) % cfg.cps == 0)


def _silu(x):
    return x * jax.nn.sigmoid(x)


def _nt(a, b, **kw):
    return lax.dot_general(a, b, (((1,), (1,)), ((), ())), preferred_element_type=F32, **kw)


def _tn(a, b, **kw):
    return lax.dot_general(a, b, (((0,), (0,)), ((), ())), preferred_element_type=F32, **kw)


def _params(*sem):
    return pltpu.CompilerParams(dimension_semantics=sem, vmem_limit_bytes=VMEM_LIMIT)


def _mod_kernel(c_ref, w_ref, b_ref, o_ref):
    o_ref[...] = jnp.dot(_silu(c_ref[...]), w_ref[...], preferred_element_type=F32) + b_ref[...]


def _modulation(c_all, w_mod, b_mod):
    s, d = c_all.shape
    cols = w_mod.shape[1]
    tn = 512
    return pl.pallas_call(
        _mod_kernel,
        grid=(cols // tn,),
        in_specs=[pl.BlockSpec((s, d), lambda j: (0, 0)),
                  pl.BlockSpec((d, tn), lambda j: (0, j)),
                  pl.BlockSpec((1, tn), lambda j: (0, j))],
        out_specs=pl.BlockSpec((s, tn), lambda j: (0, j)),
        out_shape=jax.ShapeDtypeStruct((s, cols), F32),
        compiler_params=_params("parallel"),
        name="modulation",
    )(c_all, w_mod, b_mod.reshape(1, cols))


def _modulated_norm(x, g, scale, shift):
    y = x * lax.rsqrt(jnp.mean(x * x, axis=-1, keepdims=True) + NORM_EPS) * g
    return y * (1.0 + scale) + shift


def _prompt_tiles(cfg, tm):
    assert (cfg.bp * cfg.tp) % tm == 0 and (cfg.bs * cfg.ts) % tm == 0 and tm % CHUNK == 0
    return cfg.bp * cfg.tp // tm


def _row_specs(cfg, tm, width, col, buffers=2):
    npt = _prompt_tiles(cfg, tm)
    mode = dict(pipeline_mode=pl.Buffered(buffers)) if buffers != 2 else {}
    return [pl.BlockSpec((tm, width), lambda i, j: (jnp.minimum(i, npt - 1), jnp.where(i < npt, col(j), 0)), **mode),
            pl.BlockSpec((tm, width), lambda i, j: (jnp.maximum(i - npt, 0), jnp.where(i >= npt, col(j), 0)), **mode)]


def _inproj_kernel(xp_ref, xs_ref, shift_ref, scale_ref, g_ref, w_ref, o_ref, h_scr, *, cfg, tm):
    i = pl.program_id(0)
    first_col = pl.program_id(1) == 0
    npt = _prompt_tiles(cfg, tm)

    def norm_rows(x_ref):
        for s in range(tm // CHUNK):
            seq = _seq_of_chunk(i * (tm // CHUNK) + s, cfg)
            rows = slice(s * CHUNK, (s + 1) * CHUNK)
            h = _modulated_norm(x_ref[rows, :], g_ref[...], scale_ref[pl.ds(seq, 1), :], shift_ref[pl.ds(seq, 1), :])
            h_scr[rows, :] = h.astype(BF16)

    pl.when(jnp.logical_and(first_col, i < npt))(lambda: norm_rows(xp_ref))
    pl.when(jnp.logical_and(first_col, i >= npt))(lambda: norm_rows(xs_ref))
    o_ref[...] = jnp.dot(h_scr[...], w_ref[...], preferred_element_type=F32)


def _inproj(x_p, x_s, shift, scale, g, w_bf16, cfg, tm=512, tn=1024):
    d = x_p.shape[1]
    n = cfg.n_rows
    nc = w_bf16.shape[1]
    s = shift.shape[0]
    return pl.pallas_call(
        functools.partial(_inproj_kernel, cfg=cfg, tm=tm),
        grid=(n // tm, pl.cdiv(nc, tn)),
        in_specs=_row_specs(cfg, tm, d, lambda j: 0, buffers=1) + [
                  pl.BlockSpec((s, d), lambda i, j: (0, 0)),
                  pl.BlockSpec((s, d), lambda i, j: (0, 0)),
                  pl.BlockSpec((1, d), lambda i, j: (0, 0)),
                  pl.BlockSpec((d, tn), lambda i, j: (0, j))],
        out_specs=pl.BlockSpec((tm, tn), lambda i, j: (i, j)),
        out_shape=jax.ShapeDtypeStruct((n, nc), F32),
        scratch_shapes=[pltpu.VMEM((tm, d), BF16)],
        compiler_params=_params("parallel", "arbitrary"),
        name="inproj",
    )(x_p, x_s, shift, scale, g, w_bf16)


def _mamba_kernel(zx_ref, dt_ref, cst_ref, sst_ref, cw_ref, cb_ref, dtb_ref, alog_ref, dsk_ref, ng_ref, exp_ref,
                  ym_ref, ssm_ref, buf, u_scr, *, cfg):
    c = pl.program_id(0)
    mi, nb, cd = cfg.mi, cfg.nb, cfg.cd
    L = CHUNK

    @pl.when(_is_first_chunk(c, cfg))
    def _():
        buf[0:8, :] = cst_ref[0]
        ssm_ref[0] = sst_ref[0]

    xbc = zx_ref[:, mi:mi + cd]
    buf[8:8 + L, :] = xbc
    conv = (cb_ref[...] + buf[5:5 + L, :] * cw_ref[0:1, :] + buf[6:6 + L, :] * cw_ref[1:2, :]
            + buf[7:7 + L, :] * cw_ref[2:3, :] + xbc * cw_ref[3:4, :])
    buf[0:8, :] = buf[L:L + 8, :]
    u_scr[...] = _silu(conv)

    dt = jax.nn.softplus(dt_ref[...] + dtb_ref[...])
    da = dt * (-jnp.exp(alog_ref[...]))
    row = lax.broadcasted_iota(jnp.int32, (L, 256), 0)
    pos = lax.broadcasted_iota(jnp.int32, (L, 256), 1) % HEAD
    causal = row >= pos
    diag = row == pos
    tri2 = (lax.broadcasted_iota(jnp.int32, (L, 2 * L), 0)
            >= lax.broadcasted_iota(jnp.int32, (L, 2 * L), 1) % L).astype(BF16)
    lane = lax.broadcasted_iota(jnp.int32, (L, PAIR), 1)
    row16 = lax.broadcasted_iota(jnp.int32, (16, 256), 0)
    ones16 = jnp.ones((16, LANES), BF16)
    dt_hl = jnp.concatenate(_split_hi_lo(dt), axis=1)
    da_hl = jnp.concatenate(_split_hi_lo(da), axis=1)

    for g in range(cfg.groups):
        cs = slice(g * 256, (g + 1) * 256)
        xh = u_scr[:, cs]
        bg = u_scr[:, mi + g * LANES:mi + (g + 1) * LANES]
        cg = u_scr[:, mi + nb + g * LANES:mi + nb + (g + 1) * LANES]
        eg = exp_ref[:, cs]
        dte = jnp.dot(dt_hl, eg, preferred_element_type=F32)
        dae = jnp.dot(da_hl, eg, preferred_element_type=F32)
        cum = jnp.dot(tri2, jnp.concatenate(_split_hi_lo(dae), axis=0), preferred_element_type=F32)
        cum_row = jnp.sum(jnp.where(diag, cum, 0.0), axis=0, keepdims=True)
        last = cum[L - 1:L, :]
        decay = jnp.exp(jnp.where(causal, cum - cum_row, -jnp.inf))
        xdt = xh * dte
        sc2 = _nt(cg, jnp.concatenate([bg, bg], axis=0))
        ys = []
        for q in range(2):
            ps = slice(q * PAIR, (q + 1) * PAIR)
            xq = xdt[:, ps]
            rhs = jnp.concatenate([jnp.where(lane < HEAD, xq, 0.0), jnp.where(lane >= HEAD, xq, 0.0)], axis=0)
            ys.append(jnp.dot(decay[:, ps] * sc2, rhs, preferred_element_type=F32))
        y = jnp.concatenate(ys, axis=1)
        sg = ssm_ref[0, cs, :]
        y = y + _nt(cg, sg) * jnp.exp(cum)
        new = _tn(xdt * jnp.exp(last - cum), bg)
        e_last = jnp.exp(last)
        e_hi = e_last.astype(BF16).astype(F32)
        e_rows = jnp.where(row16 == 0, e_hi, jnp.where(row16 == 1, e_last - e_hi, 0.0)).astype(BF16)
        dcol = _tn(e_rows, ones16)
        ssm_ref[0, cs, :] = sg * dcol + new
        y = y + xh * dsk_ref[:, cs]
        y = y * _silu(zx_ref[:, cs])
        y = y * lax.rsqrt(jnp.mean(y * y, axis=-1, keepdims=True) + M_NORM_EPS)
        ym_ref[:, cs] = (y * ng_ref[:, cs]).astype(BF16)


def _mamba(proj, conv_state8, ssm_state, cw, cb, dtb, alog, dsk, ng, expand, cfg):
    n = proj.shape[0]
    mi, cd = cfg.mi, cfg.cd
    nst = ssm_state.shape[-1]
    seq = lambda c: _seq_of_chunk(c, cfg)
    full = lambda a: pl.BlockSpec(a.shape, lambda c: (0,) * a.ndim)
    return pl.pallas_call(
        functools.partial(_mamba_kernel, cfg=cfg),
        grid=(cfg.n_chunks,),
        in_specs=[pl.BlockSpec((CHUNK, mi + cd), lambda c: (c, 0)),
                  pl.BlockSpec((CHUNK, LANES), lambda c: (c, cfg.o_dt // LANES)),
                  pl.BlockSpec((1, 8, cd), lambda c: (seq(c), 0, 0)),
                  pl.BlockSpec((1, mi, nst), lambda c: (seq(c), 0, 0)),
                  full(cw), full(cb), full(dtb), full(alog), full(dsk), full(ng), full(expand)],
        out_specs=[pl.BlockSpec((CHUNK, mi), lambda c: (c, 0)),
                   pl.BlockSpec((1, mi, nst), lambda c: (seq(c), 0, 0))],
        out_shape=[jax.ShapeDtypeStruct((n, mi), BF16),
                   jax.ShapeDtypeStruct((cfg.n_seq, mi, nst), F32)],
        scratch_shapes=[pltpu.VMEM((CHUNK + 8, cd), F32), pltpu.VMEM((CHUNK, cd), F32)],
        compiler_params=_params("arbitrary"),
        name="mamba",
    )(proj, proj, conv_state8, ssm_state, cw, cb, dtb, alog, dsk, ng, expand)


def _seg_sum(x, ob):
    parts = [jnp.dot(x[:, p * PAIR:(p + 1) * PAIR], ob, preferred_element_type=F32) for p in range(x.shape[1] // PAIR)]
    return jnp.concatenate(parts, axis=1)


def _split_hi_lo(x):
    hi = x.astype(BF16)
    lo = (x - hi.astype(F32)).astype(BF16)
    return hi, lo


def _rwkv_kernel(rkv_ref, lora_ref, sh_ref, wkv_in_ref, mu_ref, w0_ref, wup_ref, a0_ref, aup_ref, gup_ref,
                 kk_ref, ka_ref, rk_ref, lng_ref, lnb_ref,
                 yr_ref, wkv_ref, pbuf, lbuf, w_s, a_s, b_s, k_s, y_s, vt_s, ut_s, y0_s, g_s, *, cfg):
    c = pl.program_id(0)
    rw = cfg.rw
    L = CHUNK
    npair = rw // PAIR

    @pl.when(_is_first_chunk(c, cfg))
    def _():
        pbuf[0:8, :] = sh_ref[0, :, 0:3 * rw]
        lbuf[0:8, :] = sh_ref[0, :, 3 * rw:]
        wkv_ref[0] = wkv_in_ref[0]

    pr = rkv_ref[...]
    pl_ = lora_ref[...]
    pbuf[8:8 + L, :] = pr
    lbuf[8:8 + L, :] = pl_
    xs = pr + (pbuf[7:7 + L, :] - pr) * mu_ref[:, 0:3 * rw]
    xl = pl_ + (lbuf[7:7 + L, :] - pl_) * mu_ref[:, 3 * rw:]
    pbuf[0:8, :] = pbuf[L:L + 8, :]
    lbuf[0:8, :] = lbuf[L:L + 8, :]

    r = xs[:, 0:rw]
    k = xs[:, rw:2 * rw]
    v = xs[:, 2 * rw:3 * rw]
    wd = xl[:, 0:LANES]
    ad = xl[:, cfg.lw:cfg.lw + LANES]
    gd = xl[:, cfg.lw + cfg.la:cfg.lw + cfg.la + cfg.lg]

    lane = lax.broadcasted_iota(jnp.int32, (PAIR, PAIR), 1)
    rowi = lax.broadcasted_iota(jnp.int32, (PAIR, PAIR), 0)
    ob = ((lane // HEAD) == (rowi // HEAD)).astype(F32)

    w_z = w0_ref[...] + jnp.dot(jnp.tanh(wd), wup_ref[...], preferred_element_type=F32)
    logw = -(jax.nn.sigmoid(w_z) * EXP_NEG_HALF)
    w_s[...] = jnp.exp(logw)
    a = jax.nn.sigmoid(a0_ref[...] + jnp.dot(ad, aup_ref[...], preferred_element_type=F32))
    kk = k * kk_ref[...]
    kk = kk * lax.rsqrt(jnp.maximum(_seg_sum(kk * kk, ob), 1e-24))
    kh = k * (1.0 + (a - 1.0) * ka_ref[...])
    a_s[...] = -kk
    b_s[...] = kk * a
    k_s[...] = kh

    tri2 = (lax.broadcasted_iota(jnp.int32, (L, 2 * L), 0)
            >= lax.broadcasted_iota(jnp.int32, (L, 2 * L), 1) % L).astype(BF16)
    cum = jnp.dot(tri2, jnp.concatenate(_split_hi_lo(logw), axis=0), preferred_element_type=F32)
    p_inv = jnp.exp(-cum)
    r_p = r * jnp.exp(cum)
    b_p = b_s[...] * p_inv
    k_p = kh * p_inv
    lo_lanes = lane < HEAD
    g_row = lax.broadcasted_iota(jnp.int32, (2 * PAIR, PAIR), 0)
    g_lane = lax.broadcasted_iota(jnp.int32, (2 * PAIR, PAIR), 1)
    g_keep = (g_row % HEAD) <= (g_lane % HEAD)

    def by_head(x):
        x2 = jnp.concatenate([x, x], axis=0)
        return jnp.where(lo_lanes == (rowi < HEAD), x2, 0.0).astype(BF16)

    for p in range(npair):
        ps = slice(p * PAIR, (p + 1) * PAIR)
        hs = slice(p * HEAD, (p + 1) * HEAD)
        r_bd = by_head(r_p[:, ps])
        y0_s[hs, :] = _nt(wkv_ref[0, p].astype(BF16), r_bd)
        gram = _nt(jnp.concatenate([by_head(b_p[:, ps]), by_head(k_p[:, ps])], axis=0), r_bd)
        g_s[p] = jnp.where(g_keep, gram, 0.0).astype(BF16)
        vt = v[:, ps].T
        vjt = jnp.concatenate([vt[0:HEAD], vt[HEAD:]], axis=1)
        hi = vjt.astype(BF16).astype(F32)
        vt_s[hs, :] = jnp.concatenate([hi, vjt - hi], axis=1).astype(BF16)

    ob_b = ob.astype(BF16)
    ob2 = jnp.concatenate([ob_b, ob_b], axis=0)
    lane_t = lax.broadcasted_iota(jnp.int32, (HEAD, PAIR), 1) % HEAD
    crow = lax.broadcasted_iota(jnp.int32, (2 * PAIR, 2 * PAIR), 0)
    clane = lax.broadcasted_iota(jnp.int32, (2 * PAIR, 2 * PAIR), 1)
    same_head = ((crow % PAIR) // HEAD) == ((clane % PAIR) // HEAD)
    c_dt = (crow % HEAD) - clane // PAIR

    def steps(blk, carry):
        t0 = pl.multiple_of(blk * 8, 8)
        rows = pl.ds(t0, 8)
        w8, a8, b8, k8 = w_s[rows, :], a_s[rows, :], b_s[rows, :], k_s[rows, :]
        s = [wkv_ref[0, p] for p in range(npair)]
        ut = [ut_s[p * HEAD:(p + 1) * HEAD, :] for p in range(npair)]
        vcol2 = None
        for i in range(8):
            row = lambda x8, p: x8[i:i + 1, p * PAIR:(p + 1) * PAIR]
            if i % 2 == 0:
                sel = (same_head & (c_dt == t0 + i)).astype(BF16)
                vcol2 = jnp.dot(vt_s[...], sel, preferred_element_type=F32)
            vcol = vcol2[:, (i % 2) * PAIR:(i % 2 + 1) * PAIR]
            for p0 in range(0, npair, SA_GROUP):
                group = range(p0, min(p0 + SA_GROUP, npair))
                lhs = []
                for p in group:
                    hi, lo = _split_hi_lo(s[p] * row(a8, p))
                    lhs.append(jnp.concatenate([hi, lo], axis=1))
                sa = jnp.dot(jnp.concatenate(lhs, axis=0), ob2, preferred_element_type=F32)
                for q, p in enumerate(group):
                    sa_p = sa[q * HEAD:(q + 1) * HEAD]
                    s[p] = s[p] * row(w8, p) + sa_p * row(b8, p) + vcol[p * HEAD:(p + 1) * HEAD] * row(k8, p)
                    ut[p] = jnp.where(lane_t == t0 + i, sa_p, ut[p])
        for p in range(npair):
            wkv_ref[0, p] = s[p]
            ut_s[p * HEAD:(p + 1) * HEAD, :] = ut[p]
        return carry

    ut_s[...] = jnp.zeros_like(ut_s)
    lax.fori_loop(0, L // 8, steps, 0)

    for p in range(npair):
        hs = slice(p * HEAD, (p + 1) * HEAD)
        lhs = jnp.concatenate([ut_s[hs, :].astype(BF16), vt_s[hs, 0:PAIR]], axis=1)
        yt = y0_s[hs, :] + jnp.dot(lhs, g_s[p], preferred_element_type=F32)
        ytt = yt.T
        y_s[:, p * PAIR:(p + 1) * PAIR] = jnp.concatenate([ytt[0:L], ytt[L:]], axis=1)

    y = y_s[...]
    mu = _seg_sum(y, ob) * (1.0 / HEAD)
    dlt = y - mu
    var = _seg_sum(dlt * dlt, ob) * (1.0 / HEAD)
    yn = dlt * lax.rsqrt(var + R_LN_EPS) * lng_ref[...] + lnb_ref[...]
    bonus = _seg_sum(r * k_s[...] * rk_ref[...], ob) * v
    g = jnp.dot(jax.nn.sigmoid(gd), gup_ref[...], preferred_element_type=F32)
    yr_ref[...] = ((yn + bonus) * g).astype(BF16)


def _rwkv(proj, shift8, wkv_pairs, mu, w0, wup, a0, aup, gup, kk, ka, rk, lng, lnb, cfg):
    n = proj.shape[0]
    rw, lp = cfg.rw, cfg.tail
    assert cfg.o_rw % (3 * rw) == 0 and (cfg.o_rw + 3 * rw) % lp == 0
    npair = rw // PAIR
    seq = lambda c: _seq_of_chunk(c, cfg)
    full = lambda a: pl.BlockSpec(a.shape, lambda c: (0,) * a.ndim)
    row_scr = pltpu.VMEM((CHUNK, rw), F32)
    return pl.pallas_call(
        functools.partial(_rwkv_kernel, cfg=cfg),
        grid=(cfg.n_chunks,),
        in_specs=[pl.BlockSpec((CHUNK, 3 * rw), lambda c: (c, cfg.o_rw // (3 * rw))),
                  pl.BlockSpec((CHUNK, lp), lambda c: (c, (cfg.o_rw + 3 * rw) // lp)),
                  pl.BlockSpec((1, 8, 3 * rw + lp), lambda c: (seq(c), 0, 0)),
                  pl.BlockSpec((1, npair, HEAD, PAIR), lambda c: (seq(c), 0, 0, 0)),
                  full(mu), full(w0), full(wup), full(a0), full(aup), full(gup),
                  full(kk), full(ka), full(rk), full(lng), full(lnb)],
        out_specs=[pl.BlockSpec((CHUNK, rw), lambda c: (c, 0)),
                   pl.BlockSpec((1, npair, HEAD, PAIR), lambda c: (seq(c), 0, 0, 0))],
        out_shape=[jax.ShapeDtypeStruct((n, rw), BF16),
                   jax.ShapeDtypeStruct((cfg.n_seq, npair, HEAD, PAIR), F32)],
        scratch_shapes=[pltpu.VMEM((CHUNK + 8, 3 * rw), F32), pltpu.VMEM((CHUNK + 8, lp), F32),
                        row_scr, row_scr, row_scr, row_scr, row_scr,
                        pltpu.VMEM((npair * HEAD, 2 * PAIR), BF16), pltpu.VMEM((npair * HEAD, PAIR), F32),
                        pltpu.VMEM((npair * HEAD, PAIR), F32), pltpu.VMEM((npair, 2 * PAIR, PAIR), BF16)],
        compiler_params=_params("arbitrary"),
        name="rwkv",
    )(proj, proj, shift8, wkv_pairs, mu, w0, wup, a0, aup, gup, kk, ka, rk, lng, lnb)


def _outproj_kernel(ym_ref, yr_ref, wa_ref, wb_ref, xp_ref, xs_ref, gate_ref, o_ref, *, cfg, tm):
    i = pl.program_id(0)
    acc = (jnp.dot(ym_ref[...], wa_ref[...], preferred_element_type=F32)
           + jnp.dot(yr_ref[...], wb_ref[...], preferred_element_type=F32))

    def residual(x_ref):
        for s in range(tm // CHUNK):
            seq = _seq_of_chunk(i * (tm // CHUNK) + s, cfg)
            rows = slice(s * CHUNK, (s + 1) * CHUNK)
            o_ref[rows, :] = x_ref[rows, :] + gate_ref[pl.ds(seq, 1), :] * acc[rows, :]

    npt = _prompt_tiles(cfg, tm)
    pl.when(i < npt)(lambda: residual(xp_ref))
    pl.when(i >= npt)(lambda: residual(xs_ref))


def _outproj(ym, yr, wa, wb, x_p, x_s, gate, cfg, tm=512, tn=1024):
    n, d = cfg.n_rows, x_p.shape[1]
    ka, kb = ym.shape[1], yr.shape[1]
    s = gate.shape[0]
    return pl.pallas_call(
        functools.partial(_outproj_kernel, cfg=cfg, tm=tm),
        grid=(n // tm, d // tn),
        in_specs=[pl.BlockSpec((tm, ka), lambda i, j: (i, 0)),
                  pl.BlockSpec((tm, kb), lambda i, j: (i, 0)),
                  pl.BlockSpec((ka, tn), lambda i, j: (0, j)),
                  pl.BlockSpec((kb, tn), lambda i, j: (0, j))]
                 + _row_specs(cfg, tm, tn, lambda j: j)
                 + [pl.BlockSpec((s, tn), lambda i, j: (0, j))],
        out_specs=pl.BlockSpec((tm, tn), lambda i, j: (i, j)),
        out_shape=jax.ShapeDtypeStruct((n, d), F32),
        compiler_params=_params("parallel", "parallel"),
        name="outproj",
    )(ym, yr, wa, wb, x_p, x_s, gate)


def _router_kernel(x_ref, shift_ref, scale_ref, g_ref, wr_ref, br_ref, h_ref, lg_ref, *, cfg, tm):
    i = pl.program_id(0)
    for s in range(tm // CHUNK):
        seq = _seq_of_chunk(i * (tm // CHUNK) + s, cfg)
        rows = slice(s * CHUNK, (s + 1) * CHUNK)
        h = _modulated_norm(x_ref[rows, :], g_ref[...], scale_ref[pl.ds(seq, 1), :], shift_ref[pl.ds(seq, 1), :])
        h_ref[rows, :] = h.astype(BF16)
        lg_ref[rows, :] = jnp.dot(h, wr_ref[...], precision=HI, preferred_element_type=F32) + br_ref[...]


def _router(x1, shift, scale, g, wr, br, cfg, tm=256):
    n, d = x1.shape
    s = shift.shape[0]
    full = lambda a: pl.BlockSpec(a.shape, lambda i: (0,) * a.ndim)
    return pl.pallas_call(
        functools.partial(_router_kernel, cfg=cfg, tm=tm),
        grid=(n // tm,),
        in_specs=[pl.BlockSpec((tm, d), lambda i: (i, 0)), full(shift), full(scale), full(g), full(wr), full(br)],
        out_specs=[pl.BlockSpec((tm, d), lambda i: (i, 0)), pl.BlockSpec((tm, LANES), lambda i: (i, 0))],
        out_shape=[jax.ShapeDtypeStruct((n, d), BF16), jax.ShapeDtypeStruct((n, LANES), F32)],
        compiler_params=_params("parallel"),
        name="router",
    )(x1, shift, scale, g, wr, br)


def _new_weights(be_ref):
    i = pl.program_id(1)
    return jnp.logical_or(i == 0, be_ref[i] != be_ref[jnp.maximum(i - 1, 0)])


def _for_valid_rows(nv_ref, out_ref, compute):
    nv = nv_ref[pl.program_id(1)]
    for s in range(out_ref.shape[0] // E_SUB):
        rows = slice(s * E_SUB, (s + 1) * E_SUB)

        @pl.when(nv > s * E_SUB)
        def _():
            out_ref[rows, :] = compute(rows)

        @pl.when(nv <= s * E_SUB)
        def _():
            out_ref[rows, :] = jnp.zeros((E_SUB, out_ref.shape[1]), out_ref.dtype)


def _expert_up_kernel(be_ref, nv_ref, x_ref, wg_ref, wu_ref, h_ref, wg_s, wu_s):
    @pl.when(_new_weights(be_ref))
    def _():
        wg_s[...] = wg_ref[0].astype(BF16)
        wu_s[...] = wu_ref[0].astype(BF16)

    def hidden(rows):
        x = x_ref[rows, :]
        gate = jnp.dot(x, wg_s[...], preferred_element_type=F32)
        up = jnp.dot(x, wu_s[...], preferred_element_type=F32)
        return (_silu(gate) * up).astype(BF16)

    _for_valid_rows(nv_ref, h_ref, hidden)


def _expert_down_kernel(be_ref, nv_ref, h_ref, wd_ref, y_ref, wd_s):
    @pl.when(_new_weights(be_ref))
    def _():
        wd_s[...] = wd_ref[0].astype(BF16)

    _for_valid_rows(nv_ref, y_ref, lambda rows: jnp.dot(h_ref[rows, :], wd_s[...], preferred_element_type=F32))


def _experts(xb, blk_e, n_valid, wg, wu, wd, tm, tf=256, tn=1024):
    rows, d = xb.shape
    f = wg.shape[2]
    nblk = rows // tm
    assert tm % E_SUB == 0
    hid = pl.pallas_call(
        _expert_up_kernel,
        grid_spec=pltpu.PrefetchScalarGridSpec(
            num_scalar_prefetch=2, grid=(f // tf, nblk),
            in_specs=[pl.BlockSpec((tm, d), lambda j, i, be, nv: (i, 0)),
                      pl.BlockSpec((1, d, tf), lambda j, i, be, nv: (be[i], 0, j)),
                      pl.BlockSpec((1, d, tf), lambda j, i, be, nv: (be[i], 0, j))],
            out_specs=pl.BlockSpec((tm, tf), lambda j, i, be, nv: (i, j)),
            scratch_shapes=[pltpu.VMEM((d, tf), BF16), pltpu.VMEM((d, tf), BF16)]),
        out_shape=jax.ShapeDtypeStruct((rows, f), BF16),
        compiler_params=_params("arbitrary", "arbitrary"),
        name="expert_up",
    )(blk_e, n_valid, xb, wg, wu)
    return pl.pallas_call(
        _expert_down_kernel,
        grid_spec=pltpu.PrefetchScalarGridSpec(
            num_scalar_prefetch=2, grid=(d // tn, nblk),
            in_specs=[pl.BlockSpec((tm, f), lambda j, i, be, nv: (i, 0)),
                      pl.BlockSpec((1, f, tn), lambda j, i, be, nv: (be[i], 0, j))],
            out_specs=pl.BlockSpec((tm, tn), lambda j, i, be, nv: (i, j)),
            scratch_shapes=[pltpu.VMEM((f, tn), BF16)]),
        out_shape=jax.ShapeDtypeStruct((rows, d), F32),
        compiler_params=_params("arbitrary", "arbitrary"),
        name="expert_down",
    )(blk_e, n_valid, hid, wd)


def _final_kernel(x_ref, moe_ref, gate_ref, g_ref, o_ref, *, cfg, tm, tile0):
    i = pl.program_id(0) + tile0
    for s in range(tm // CHUNK):
        seq = _seq_of_chunk(i * (tm // CHUNK) + s, cfg)
        rows = slice(s * CHUNK, (s + 1) * CHUNK)
        x = x_ref[rows, :] + gate_ref[pl.ds(seq, 1), :] * moe_ref[rows, :]
        o_ref[rows, :] = x * lax.rsqrt(jnp.mean(x * x, axis=-1, keepdims=True) + NORM_EPS) * g_ref[...]


def _final(x1, moe, gate, g, cfg, row0, nrows, tm=256):
    d = x1.shape[1]
    assert row0 % tm == 0 and nrows % tm == 0
    tile0 = row0 // tm
    full = lambda a: pl.BlockSpec(a.shape, lambda i: (0,) * a.ndim)
    return pl.pallas_call(
        functools.partial(_final_kernel, cfg=cfg, tm=tm, tile0=tile0),
        grid=(nrows // tm,),
        in_specs=[pl.BlockSpec((tm, d), lambda i: (i + tile0, 0)), pl.BlockSpec((tm, d), lambda i: (i + tile0, 0)),
                  full(gate), full(g)],
        out_specs=pl.BlockSpec((tm, d), lambda i: (i, 0)),
        out_shape=jax.ShapeDtypeStruct((nrows, d), F32),
        compiler_params=_params("parallel"),
        name="final_norm",
    )(x1, moe, gate, g)


def _dispatch(logits, n_experts, tm):
    n = logits.shape[0]
    glog = logits[:, :E_GROUPS]
    elog = logits[:, E_GROUPS:E_GROUPS + n_experts].reshape(n, E_GROUPS, E_PER_GROUP)
    gprob = jax.nn.softmax(glog, axis=-1)
    gsel = jnp.argmax(glog, axis=-1)
    elog_g = jnp.take_along_axis(elog, gsel[:, None, None], axis=1)[:, 0]
    eprob = jax.nn.softmax(elog_g, axis=-1)
    topv, topi = lax.top_k(eprob, TOP_K)
    pg = jnp.take_along_axis(gprob, gsel[:, None], axis=1)
    wts = topv / jnp.sum(topv, axis=-1, keepdims=True) * pg
    eid = (gsel[:, None] * E_PER_GROUP + topi).astype(jnp.int32)

    m = n * TOP_K
    e_flat = eid.reshape(m)
    order = jnp.argsort(e_flat)
    e_sorted = e_flat[order]
    counts = jnp.bincount(e_flat, length=n_experts).astype(jnp.int32)
    padded = (counts + tm - 1) // tm * tm
    start = jnp.cumsum(counts) - counts
    pend = jnp.cumsum(padded)
    pstart = pend - padded
    dest_sorted = pstart[e_sorted] + jnp.arange(m, dtype=jnp.int32) - start[e_sorted]
    n_blocks = -(-(m + n_experts * (tm - 1)) // tm)
    rows = n_blocks * tm
    tok_sorted = (order // TOP_K).astype(jnp.int32)
    row_tok = (jnp.arange(rows, dtype=jnp.int32) % n).at[dest_sorted].set(tok_sorted)
    slot_pos = jnp.zeros((m,), jnp.int32).at[order].set(dest_sorted).reshape(n, TOP_K)
    n_used = pend[-1] // tm
    blk = jnp.arange(n_blocks, dtype=jnp.int32)
    blk_e = jnp.minimum(jnp.searchsorted(pend, blk * tm, side='right'), n_experts - 1).astype(jnp.int32)
    n_valid = jnp.where(blk < n_used, jnp.clip(pstart[blk_e] + counts[blk_e] - blk * tm, 0, tm), 0).astype(jnp.int32)
    last_e = blk_e[jnp.maximum(n_used - 1, 0)]
    blk_e = jnp.where(blk < n_used, blk_e, last_e)
    return row_tok, slot_pos, wts, blk_e, n_valid


def _to_window(a, cfg):
    return jnp.concatenate([a, jnp.zeros(a.shape[:-1] + (cfg.win - cfg.r_cols,), a.dtype)], axis=-1)


def _pad_rows(a, rows):
    return jnp.concatenate([a, jnp.zeros((rows - a.shape[0],) + a.shape[1:], a.dtype)], axis=0)


def kernel(x_prompt, x_sample, state_conv, state_ssm, state_shift, state_wkv, c_prompt, c_sample, norm1_g, w_mod, b_mod, w_in, conv_w, conv_b, dt_bias, a_log, d_skip, m_norm_g, shift_mu, w0, w_up, a0, a_up, g_up, k_k, k_a, r_k, ln_x_g, ln_x_b, w_out, norm2_g, w_grp, b_grp, w_erouter, b_erouter, e_gate, e_up, e_down, final_norm_g):
    assert w_mod.shape[0] == 1, "single-layer trunk"
    bp, tp, d = x_prompt.shape
    bs, ts, _ = x_sample.shape
    mi = m_norm_g.shape[-1]
    rw = w0.shape[-1]
    lw, la, lg = w_up.shape[1], a_up.shape[1], g_up.shape[1]
    cfg = Cfg(d, bp, tp, bs, ts, mi, rw, lw, la, lg)
    nc = cfg.nc
    assert tp % CHUNK == 0 and ts % CHUNK == 0 and tp >= CONV_W - 1 and ts >= CONV_W - 1
    assert w_in.shape[-1] == cfg.o_rw + cfg.mh + cfg.r_cols and cfg.o_rw % LANES == 0
    assert state_conv.shape[-1] == cfg.cd and lw <= LANES and la <= LANES and cfg.mh <= LANES
    n = cfg.n_rows
    n_seq = cfg.n_seq
    mh, rh = cfg.mh, cfg.rh
    npair = rw // PAIR
    n_experts = e_gate.shape[1]

    x_p = x_prompt.reshape(bp * tp, d)
    x_s = x_sample.reshape(bs * ts, d)
    c_all = jnp.concatenate([c_prompt, c_sample], axis=0)

    mod = _modulation(c_all, w_mod[0], b_mod[0])
    shift1, scale1, gate1, shift2, scale2, gate2 = [mod[:, i * d:(i + 1) * d] for i in range(6)]

    wi = w_in[0]
    o2 = cfg.o_rw
    zc = lambda w: jnp.zeros((d, w), wi.dtype)
    w_proj = jnp.concatenate([wi[:, :o2], wi[:, o2 + mh:], zc(cfg.win - cfg.r_cols), wi[:, o2:o2 + mh], zc(LANES - mh)],
                             axis=1).astype(BF16)
    proj = _inproj(x_p, x_s, shift1, scale1, norm1_g, w_proj, cfg)

    zeros = lambda b, *s: jnp.zeros((b,) + s, F32)
    conv0 = jnp.concatenate([zeros(bp, CONV_W - 1, cfg.cd), state_conv[0]], axis=0)
    conv8 = jnp.concatenate([zeros(n_seq, 8 - (CONV_W - 1), cfg.cd), conv0], axis=1)
    nst = state_ssm.shape[-1]
    ssm0 = jnp.concatenate([zeros(bp, mi, nst), state_ssm[0].reshape(bs, mi, nst)], axis=0)
    lane_pad = lambda a: jnp.concatenate([a.reshape(1, -1), jnp.zeros((1, LANES - a.shape[-1]), F32)], axis=1)
    expand = ((jnp.arange(2 * LANES)[:, None] % LANES) == (jnp.arange(mi)[None, :] // HEAD)).astype(BF16)
    ym, ssm_new = _mamba(proj, conv8, ssm0, conv_w[0], conv_b[0].reshape(1, -1), lane_pad(dt_bias[0]),
                         lane_pad(a_log[0]), jnp.repeat(d_skip[0], HEAD).reshape(1, mi), m_norm_g[0].reshape(1, mi),
                         expand, cfg)

    sh0 = jnp.concatenate([zeros(bp, 1, state_shift.shape[-1]), state_shift[0]], axis=0)
    sh8 = jnp.concatenate([zeros(n_seq, 7, cfg.win), _to_window(sh0, cfg)], axis=1)
    to_pairs = lambda s: s.reshape(-1, npair, 2, HEAD, HEAD).transpose(0, 1, 3, 2, 4).reshape(-1, npair, HEAD, PAIR)
    from_pairs = lambda s: s.reshape(-1, npair, HEAD, 2, HEAD).transpose(0, 1, 3, 2, 4).reshape(-1, rh, HEAD, HEAD)
    wkv0 = jnp.concatenate([zeros(bp, npair, HEAD, PAIR), to_pairs(state_wkv[0])], axis=0)
    row = lambda a: a.reshape(1, -1)
    yr, wkv_new = _rwkv(proj, sh8, wkv0, _to_window(row(shift_mu[0]), cfg), row(w0[0]),
                        _pad_rows(w_up[0], LANES), row(a0[0]), _pad_rows(a_up[0], LANES), g_up[0],
                        row(k_k[0]), row(k_a[0]), row(r_k[0]), row(ln_x_g[0]), row(ln_x_b[0]), cfg)

    wo = w_out[0].astype(BF16)
    x1 = _outproj(ym, yr, wo[:mi], wo[mi:], x_p, x_s, gate1, cfg)
    wr = jnp.concatenate([w_grp[0], w_erouter[0], jnp.zeros((d, LANES - E_GROUPS - n_experts), F32)], axis=1)
    br = lane_pad(jnp.concatenate([b_grp[0], b_erouter[0]]))
    h2, logits = _router(x1, shift2, scale2, norm2_g, wr, br, cfg)

    tm_e = 2 * E_SUB
    row_tok, slot_pos, wts, blk_e, n_valid = _dispatch(logits, n_experts, tm_e)
    yb = _experts(h2[row_tok], blk_e, n_valid, e_gate[0], e_up[0], e_down[0], tm_e)
    moe = yb[slot_pos[:, 0]] * wts[:, 0:1] + yb[slot_pos[:, 1]] * wts[:, 1:2]

    np_ = bp * tp
    fg = final_norm_g.reshape(1, d)
    y_prompt = _final(x1, moe, gate2, fg, cfg, 0, np_).reshape(bp, tp, d)
    y_sample = _final(x1, moe, gate2, fg, cfg, np_, bs * ts).reshape(bs, ts, d)
    tails = lambda b, t, base: jnp.stack(
        [lax.slice(proj, (base + (i + 1) * t - (CONV_W - 1), 0), (base + (i + 1) * t, nc)) for i in range(b)])
    pp = tails(bp, tp, 0)
    ps = tails(bs, ts, np_)
    conv_of = lambda p: p[:, :, mi:mi + cfg.cd][None]
    shift_of = lambda p: p[:, -1:, cfg.o_rw:cfg.o_rw + cfg.r_cols][None]
    ssm_new = ssm_new.reshape(n_seq, mh, HEAD, nst)
    wkv_new = from_pairs(wkv_new)
    return (y_prompt, y_sample,
            conv_of(pp), ssm_new[:bp][None], shift_of(pp), wkv_new[:bp][None],
            conv_of(ps), ssm_new[bp:][None], shift_of(ps), wkv_new[bp:][None])
```

```python
import functools
from typing import NamedTuple

import jax
import jax.numpy as jnp
from jax import lax
from jax.experimental import pallas as pl
from jax.experimental.pallas import tpu as pltpu

F32 = jnp.float32
BF16 = jnp.bfloat16
HI = lax.Precision.HIGHEST

LANES = 128
CHUNK = 64
HEAD = 64
PAIR = 2 * HEAD
NORM_EPS = 1e-6
M_NORM_EPS = 1e-5
R_LN_EPS = 64e-5
CONV_W = 4
E_GROUPS = 4
E_PER_GROUP = 8
TOP_K = 2
VMEM_LIMIT = 56 * 1024 * 1024
E_SUB = 256
EXP_NEG_HALF = 0.6065306597126334
SA_GROUP = 8


class Cfg(NamedTuple):
    d: int
    bp: int
    tp: int
    bs: int
    ts: int
    mi: int
    rw: int
    lw: int
    la: int
    lg: int

    @property
    def cpp(self):
        return self.tp // CHUNK

    @property
    def cps(self):
        return self.ts // CHUNK

    @property
    def n_chunks(self):
        return self.bp * self.cpp + self.bs * self.cps

    @property
    def n_seq(self):
        return self.bp + self.bs

    @property
    def n_rows(self):
        return self.n_chunks * CHUNK

    @property
    def groups(self):
        return self.mi // 256

    @property
    def nb(self):
        return self.groups * LANES

    @property
    def cd(self):
        return self.mi + 2 * self.nb

    @property
    def mh(self):
        return self.mi // HEAD

    @property
    def rh(self):
        return self.rw // HEAD

    @property
    def o_rw(self):
        return self.mi + self.cd

    @property
    def r_cols(self):
        return 3 * self.rw + self.lw + self.la + self.lg

    @property
    def tail(self):
        return -(-(self.lw + self.la + self.lg) // LANES) * LANES

    @property
    def win(self):
        return 3 * self.rw + self.tail

    @property
    def o_dt(self):
        return self.o_rw + self.win

    @property
    def nc(self):
        return self.o_dt + LANES


def _seq_of_chunk(c, cfg):
    npc = cfg.bp * cfg.cpp
    return jnp.where(c < npc, c // cfg.cpp, cfg.bp + (c - npc) // cfg.cps)


def _is_first_chunk(c, cfg):
    npc = cfg.bp * cfg.cpp
    return jnp.where(c < npc, c % cfg.cpp == 0, (c - npc) % cfg.cps == 0)


def _silu(x):
    return x * jax.nn.sigmoid(x)


def _nt(a, b, **kw):
    return lax.dot_general(a, b, (((1,), (1,)), ((), ())), preferred_element_type=F32, **kw)


def _tn(a, b, **kw):
    return lax.dot_general(a, b, (((0,), (0,)), ((), ())), preferred_element_type=F32, **kw)


def _params(*sem):
    return pltpu.CompilerParams(dimension_semantics=sem, vmem_limit_bytes=VMEM_LIMIT)


def _mod_kernel(c_ref, w_ref, b_ref, o_ref):
    o_ref[...] = jnp.dot(_silu(c_ref[...]), w_ref[...], preferred_element_type=F32) + b_ref[...]


def _modulation(c_all, w_mod, b_mod):
    s, d = c_all.shape
    cols = w_mod.shape[1]
    tn = 512
    return pl.pallas_call(
        _mod_kernel,
        grid=(cols // tn,),
        in_specs=[pl.BlockSpec((s, d), lambda j: (0, 0)),
                  pl.BlockSpec((d, tn), lambda j: (0, j)),
                  pl.BlockSpec((1, tn), lambda j: (0, j))],
        out_specs=pl.BlockSpec((s, tn), lambda j: (0, j)),
        out_shape=jax.ShapeDtypeStruct((s, cols), F32),
        compiler_params=_params("parallel"),
        name="modulation",
    )(c_all, w_mod, b_mod.reshape(1, cols))


def _modulated_norm(x, g, scale, shift):
    y = x * lax.rsqrt(jnp.mean(x * x, axis=-1, keepdims=True) + NORM_EPS) * g
    return y * (1.0 + scale) + shift


def _prompt_tiles(cfg, tm):
    assert (cfg.bp * cfg.tp) % tm == 0 and (cfg.bs * cfg.ts) % tm == 0 and tm % CHUNK == 0
    return cfg.bp * cfg.tp // tm


def _row_specs(cfg, tm, width, col, buffers=2):
    npt = _prompt_tiles(cfg, tm)
    mode = dict(pipeline_mode=pl.Buffered(buffers)) if buffers != 2 else {}
    return [pl.BlockSpec((tm, width), lambda i, j: (jnp.minimum(i, npt - 1), jnp.where(i < npt, col(j), 0)), **mode),
            pl.BlockSpec((tm, width), lambda i, j: (jnp.maximum(i - npt, 0), jnp.where(i >= npt, col(j), 0)), **mode)]


def _inproj_kernel(xp_ref, xs_ref, shift_ref, scale_ref, g_ref, w_ref, o_ref, h_scr, *, cfg, tm):
    i = pl.program_id(0)
    first_col = pl.program_id(1) == 0
    npt = _prompt_tiles(cfg, tm)

    def norm_rows(x_ref):
        for s in range(tm // CHUNK):
            seq = _seq_of_chunk(i * (tm // CHUNK) + s, cfg)
            rows = slice(s * CHUNK, (s + 1) * CHUNK)
            h = _modulated_norm(x_ref[rows, :], g_ref[...], scale_ref[pl.ds(seq, 1), :], shift_ref[pl.ds(seq, 1), :])
            h_scr[rows, :] = h.astype(BF16)

    pl.when(jnp.logical_and(first_col, i < npt))(lambda: norm_rows(xp_ref))
    pl.when(jnp.logical_and(first_col, i >= npt))(lambda: norm_rows(xs_ref))
    o_ref[...] = jnp.dot(h_scr[...], w_ref[...], preferred_element_type=F32)


def _inproj(x_p, x_s, shift, scale, g, w_bf16, cfg, tm=512, tn=1024):
    d = x_p.shape[1]
    n = cfg.n_rows
    nc = w_bf16.shape[1]
    s = shift.shape[0]
    return pl.pallas_call(
        functools.partial(_inproj_kernel, cfg=cfg, tm=tm),
        grid=(n // tm, pl.cdiv(nc, tn)),
        in_specs=_row_specs(cfg, tm, d, lambda j: 0, buffers=1) + [
                  pl.BlockSpec((s, d), lambda i, j: (0, 0)),
                  pl.BlockSpec((s, d), lambda i, j: (0, 0)),
                  pl.BlockSpec((1, d), lambda i, j: (0, 0)),
                  pl.BlockSpec((d, tn), lambda i, j: (0, j))],
        out_specs=pl.BlockSpec((tm, tn), lambda i, j: (i, j)),
        out_shape=jax.ShapeDtypeStruct((n, nc), F32),
        scratch_shapes=[pltpu.VMEM((tm, d), BF16)],
        compiler_params=_params("parallel", "arbitrary"),
        name="inproj",
    )(x_p, x_s, shift, scale, g, w_bf16)


def _mamba_kernel(zx_ref, dt_ref, cst_ref, sst_ref, cw_ref, cb_ref, dtb_ref, alog_ref, dsk_ref, ng_ref, exp_ref,
                  ym_ref, ssm_ref, buf, u_scr, *, cfg):
    c = pl.program_id(0)
    mi, nb, cd = cfg.mi, cfg.nb, cfg.cd
    L = CHUNK

    @pl.when(_is_first_chunk(c, cfg))
    def _():
        buf[0:8, :] = cst_ref[0]
        ssm_ref[0] = sst_ref[0]

    xbc = zx_ref[:, mi:mi + cd]
    buf[8:8 + L, :] = xbc
    conv = (cb_ref[...] + buf[5:5 + L, :] * cw_ref[0:1, :] + buf[6:6 + L, :] * cw_ref[1:2, :]
            + buf[7:7 + L, :] * cw_ref[2:3, :] + xbc * cw_ref[3:4, :])
    buf[0:8, :] = buf[L:L + 8, :]
    u_scr[...] = _silu(conv)

    dt = jax.nn.softplus(dt_ref[...] + dtb_ref[...])
    da = dt * (-jnp.exp(alog_ref[...]))
    row = lax.broadcasted_iota(jnp.int32, (L, 256), 0)
    pos = lax.broadcasted_iota(jnp.int32, (L, 256), 1) % HEAD
    causal = row >= pos
    diag = row == pos
    tri2 = (lax.broadcasted_iota(jnp.int32, (L, 2 * L), 0)
            >= lax.broadcasted_iota(jnp.int32, (L, 2 * L), 1) % L).astype(BF16)
    lane = lax.broadcasted_iota(jnp.int32, (L, PAIR), 1)
    row16 = lax.broadcasted_iota(jnp.int32, (16, 256), 0)
    ones16 = jnp.ones((16, LANES), BF16)
    dt_hl = jnp.concatenate(_split_hi_lo(dt), axis=1)
    da_hl = jnp.concatenate(_split_hi_lo(da), axis=1)

    for g in range(cfg.groups):
        cs = slice(g * 256, (g + 1) * 256)
        xh = u_scr[:, cs]
        bg = u_scr[:, mi + g * LANES:mi + (g + 1) * LANES]
        cg = u_scr[:, mi + nb + g * LANES:mi + nb + (g + 1) * LANES]
        eg = exp_ref[:, cs]
        dte = jnp.dot(dt_hl, eg, preferred_element_type=F32)
        dae = jnp.dot(da_hl, eg, preferred_element_type=F32)
        cum = jnp.dot(tri2, jnp.concatenate(_split_hi_lo(dae), axis=0), preferred_element_type=F32)
        cum_row = jnp.sum(jnp.where(diag, cum, 0.0), axis=0, keepdims=True)
        last = cum[L - 1:L, :]
        decay = jnp.exp(jnp.where(causal, cum - cum_row, -jnp.inf))
        xdt = xh * dte
        sc2 = _nt(cg, jnp.concatenate([bg, bg], axis=0))
        ys = []
        for q in range(2):
            ps = slice(q * PAIR, (q + 1) * PAIR)
            xq = xdt[:, ps]
            rhs = jnp.concatenate([jnp.where(lane < HEAD, xq, 0.0), jnp.where(lane >= HEAD, xq, 0.0)], axis=0)
            ys.append(jnp.dot(decay[:, ps] * sc2, rhs, preferred_element_type=F32))
        y = jnp.concatenate(ys, axis=1)
        sg = ssm_ref[0, cs, :]
        y = y + _nt(cg, sg) * jnp.exp(cum)
        new = _tn(xdt * jnp.exp(last - cum), bg)
        e_last = jnp.exp(last)
        e_hi = e_last.astype(BF16).astype(F32)
        e_rows = jnp.where(row16 == 0, e_hi, jnp.where(row16 == 1, e_last - e_hi, 0.0)).astype(BF16)
        dcol = _tn(e_rows, ones16)
        ssm_ref[0, cs, :] = sg * dcol + new
        y = y + xh * dsk_ref[:, cs]
        y = y * _silu(zx_ref[:, cs])
        y = y * lax.rsqrt(jnp.mean(y * y, axis=-1, keepdims=True) + M_NORM_EPS)
        ym_ref[:, cs] = (y * ng_ref[:, cs]).astype(BF16)


def _mamba(proj, conv_state8, ssm_state, cw, cb, dtb, alog, dsk, ng, expand, cfg):
    n = proj.shape[0]
    mi, cd = cfg.mi, cfg.cd
    nst = ssm_state.shape[-1]
    seq = lambda c: _seq_of_chunk(c, cfg)
    full = lambda a: pl.BlockSpec(a.shape, lambda c: (0,) * a.ndim)
    return pl.pallas_call(
        functools.partial(_mamba_kernel, cfg=cfg),
        grid=(cfg.n_chunks,),
        in_specs=[pl.BlockSpec((CHUNK, mi + cd), lambda c: (c, 0)),
                  pl.BlockSpec((CHUNK, LANES), lambda c: (c, cfg.o_dt // LANES)),
                  pl.BlockSpec((1, 8, cd), lambda c: (seq(c), 0, 0)),
                  pl.BlockSpec((1, mi, nst), lambda c: (seq(c), 0, 0)),
                  full(cw), full(cb), full(dtb), full(alog), full(dsk), full(ng), full(expand)],
        out_specs=[pl.BlockSpec((CHUNK, mi), lambda c: (c, 0)),
                   pl.BlockSpec((1, mi, nst), lambda c: (seq(c), 0, 0))],
        out_shape=[jax.ShapeDtypeStruct((n, mi), BF16),
                   jax.ShapeDtypeStruct((cfg.n_seq, mi, nst), F32)],
        scratch_shapes=[pltpu.VMEM((CHUNK + 8, cd), F32), pltpu.VMEM((CHUNK, cd), F32)],
        compiler_params=_params("arbitrary"),
        name="mamba",
    )(proj, proj, conv_state8, ssm_state, cw, cb, dtb, alog, dsk, ng, expand)


def _seg_sum(x, ob):
    parts = [jnp.dot(x[:, p * PAIR:(p + 1) * PAIR], ob, preferred_element_type=F32) for p in range(x.shape[1] // PAIR)]
    return jnp.concatenate(parts, axis=1)


def _split_hi_lo(x):
    hi = x.astype(BF16)
    lo = (x - hi.astype(F32)).astype(BF16)
    return hi, lo


def _rwkv_kernel(rkv_ref, lora_ref, sh_ref, wkv_in_ref, mu_ref, w0_ref, wup_ref, a0_ref, aup_ref, gup_ref,
                 kk_ref, ka_ref, rk_ref, lng_ref, lnb_ref,
                 yr_ref, wkv_ref, pbuf, lbuf, w_s, a_s, b_s, k_s, y_s, vt_s, ut_s, y0_s, g_s, *, cfg):
    c = pl.program_id(0)
    rw = cfg.rw
    L = CHUNK
    npair = rw // PAIR

    @pl.when(_is_first_chunk(c, cfg))
    def _():
        pbuf[0:8, :] = sh_ref[0, :, 0:3 * rw]
        lbuf[0:8, :] = sh_ref[0, :, 3 * rw:]
        wkv_ref[0] = wkv_in_ref[0]

    pr = rkv_ref[...]
    pl_ = lora_ref[...]
    pbuf[8:8 + L, :] = pr
    lbuf[8:8 + L, :] = pl_
    xs = pr + (pbuf[7:7 + L, :] - pr) * mu_ref[:, 0:3 * rw]
    xl = pl_ + (lbuf[7:7 + L, :] - pl_) * mu_ref[:, 3 * rw:]
    pbuf[0:8, :] = pbuf[L:L + 8, :]
    lbuf[0:8, :] = lbuf[L:L + 8, :]

    r = xs[:, 0:rw]
    k = xs[:, rw:2 * rw]
    v = xs[:, 2 * rw:3 * rw]
    wd = xl[:, 0:LANES]
    ad = xl[:, cfg.lw:cfg.lw + LANES]
    gd = xl[:, cfg.lw + cfg.la:cfg.lw + cfg.la + cfg.lg]

    lane = lax.broadcasted_iota(jnp.int32, (PAIR, PAIR), 1)
    rowi = lax.broadcasted_iota(jnp.int32, (PAIR, PAIR), 0)
    ob = ((lane // HEAD) == (rowi // HEAD)).astype(F32)

    w_z = w0_ref[...] + jnp.dot(jnp.tanh(wd), wup_ref[...], preferred_element_type=F32)
    logw = -(jax.nn.sigmoid(w_z) * EXP_NEG_HALF)
    w_s[...] = jnp.exp(logw)
    a = jax.nn.sigmoid(a0_ref[...] + jnp.dot(ad, aup_ref[...], preferred_element_type=F32))
    kk = k * kk_ref[...]
    kk = kk * lax.rsqrt(jnp.maximum(_seg_sum(kk * kk, ob), 1e-24))
    kh = k * (1.0 + (a - 1.0) * ka_ref[...])
    a_s[...] = -kk
    b_s[...] = kk * a
    k_s[...] = kh

    tri2 = (lax.broadcasted_iota(jnp.int32, (L, 2 * L), 0)
            >= lax.broadcasted_iota(jnp.int32, (L, 2 * L), 1) % L).astype(BF16)
    cum = jnp.dot(tri2, jnp.concatenate(_split_hi_lo(logw), axis=0), preferred_element_type=F32)
    p_inv = jnp.exp(-cum)
    r_p = r * jnp.exp(cum)
    b_p = b_s[...] * p_inv
    k_p = kh * p_inv
    lo_lanes = lane < HEAD
    g_row = lax.broadcasted_iota(jnp.int32, (2 * PAIR, PAIR), 0)
    g_lane = lax.broadcasted_iota(jnp.int32, (2 * PAIR, PAIR), 1)
    g_keep = (g_row % HEAD) <= (g_lane % HEAD)

    def by_head(x):
        x2 = jnp.concatenate([x, x], axis=0)
        return jnp.where(lo_lanes == (rowi < HEAD), x2, 0.0).astype(BF16)

    for p in range(npair):
        ps = slice(p * PAIR, (p + 1) * PAIR)
        hs = slice(p * HEAD, (p + 1) * HEAD)
        r_bd = by_head(r_p[:, ps])
        y0_s[hs, :] = _nt(wkv_ref[0, p].astype(BF16), r_bd)
        gram = _nt(jnp.concatenate([by_head(b_p[:, ps]), by_head(k_p[:, ps])], axis=0), r_bd)
        g_s[p] = jnp.where(g_keep, gram, 0.0).astype(BF16)
        vt = v[:, ps].T
        vjt = jnp.concatenate([vt[0:HEAD], vt[HEAD:]], axis=1)
        hi = vjt.astype(BF16).astype(F32)
        vt_s[hs, :] = jnp.concatenate([hi, vjt - hi], axis=1).astype(BF16)

    ob_b = ob.astype(BF16)
    ob2 = jnp.concatenate([ob_b, ob_b], axis=0)
    lane_t = lax.broadcasted_iota(jnp.int32, (HEAD, PAIR), 1) % HEAD
    crow = lax.broadcasted_iota(jnp.int32, (2 * PAIR, 2 * PAIR), 0)
    clane = lax.broadcasted_iota(jnp.int32, (2 * PAIR, 2 * PAIR), 1)
    same_head = ((crow % PAIR) // HEAD) == ((clane % PAIR) // HEAD)
    c_dt = (crow % HEAD) - clane // PAIR

    def steps(blk, carry):
        t0 = pl.multiple_of(blk * 8, 8)
        rows = pl.ds(t0, 8)
        w8, a8, b8, k8 = w_s[rows, :], a_s[rows, :], b_s[rows, :], k_s[rows, :]
        s = [wkv_ref[0, p] for p in range(npair)]
        ut = [ut_s[p * HEAD:(p + 1) * HEAD, :] for p in range(npair)]
        vcol2 = None
        for i in range(8):
            row = lambda x8, p: x8[i:i + 1, p * PAIR:(p + 1) * PAIR]
            if i % 2 == 0:
                sel = (same_head & (c_dt == t0 + i)).astype(BF16)
                vcol2 = jnp.dot(vt_s[...], sel, preferred_element_type=F32)
            vcol = vcol2[:, (i % 2) * PAIR:(i % 2 + 1) * PAIR]
            for p0 in range(0, npair, SA_GROUP):
                group = range(p0, min(p0 + SA_GROUP, npair))
                lhs = []
                for p in group:
                    hi, lo = _split_hi_lo(s[p] * row(a8, p))
                    lhs.append(jnp.concatenate([hi, lo], axis=1))
                sa = jnp.dot(jnp.concatenate(lhs, axis=0), ob2, preferred_element_type=F32)
                for q, p in enumerate(group):
                    sa_p = sa[q * HEAD:(q + 1) * HEAD]
                    s[p] = s[p] * row(w8, p) + sa_p * row(b8, p) + vcol[p * HEAD:(p + 1) * HEAD] * row(k8, p)
                    ut[p] = jnp.where(lane_t == t0 + i, sa_p, ut[p])
        for p in range(npair):
            wkv_ref[0, p] = s[p]
            ut_s[p * HEAD:(p + 1) * HEAD, :] = ut[p]
        return carry

    ut_s[...] = jnp.zeros_like(ut_s)
    lax.fori_loop(0, L // 8, steps, 0)

    for p in range(npair):
        hs = slice(p * HEAD, (p + 1) * HEAD)
        lhs = jnp.concatenate([ut_s[hs, :].astype(BF16), vt_s[hs, 0:PAIR]], axis=1)
        yt = y0_s[hs, :] + jnp.dot(lhs, g_s[p], preferred_element_type=F32)
        ytt = yt.T
        y_s[:, p * PAIR:(p + 1) * PAIR] = jnp.concatenate([ytt[0:L], ytt[L:]], axis=1)

    y = y_s[...]
    mu = _seg_sum(y, ob) * (1.0 / HEAD)
    dlt = y - mu
    var = _seg_sum(dlt * dlt, ob) * (1.0 / HEAD)
    yn = dlt * lax.rsqrt(var + R_LN_EPS) * lng_ref[...] + lnb_ref[...]
    bonus = _seg_sum(r * k_s[...] * rk_ref[...], ob) * v
    g = jnp.dot(jax.nn.sigmoid(gd), gup_ref[...], preferred_element_type=F32)
    yr_ref[...] = ((yn + bonus) * g).astype(BF16)


def _rwkv(proj, shift8, wkv_pairs, mu, w0, wup, a0, aup, gup, kk, ka, rk, lng, lnb, cfg):
    n = proj.shape[0]
    rw, lp = cfg.rw, cfg.tail
    assert cfg.o_rw % (3 * rw) == 0 and (cfg.o_rw + 3 * rw) % lp == 0
    npair = rw // PAIR
    seq = lambda c: _seq_of_chunk(c, cfg)
    full = lambda a: pl.BlockSpec(a.shape, lambda c: (0,) * a.ndim)
    row_scr = pltpu.VMEM((CHUNK, rw), F32)
    return pl.pallas_call(
        functools.partial(_rwkv_kernel, cfg=cfg),
        grid=(cfg.n_chunks,),
        in_specs=[pl.BlockSpec((CHUNK, 3 * rw), lambda c: (c, cfg.o_rw // (3 * rw))),
                  pl.BlockSpec((CHUNK, lp), lambda c: (c, (cfg.o_rw + 3 * rw) // lp)),
                  pl.BlockSpec((1, 8, 3 * rw + lp), lambda c: (seq(c), 0, 0)),
                  pl.BlockSpec((1, npair, HEAD, PAIR), lambda c: (seq(c), 0, 0, 0)),
                  full(mu), full(w0), full(wup), full(a0), full(aup), full(gup),
                  full(kk), full(ka), full(rk), full(lng), full(lnb)],
        out_specs=[pl.BlockSpec((CHUNK, rw), lambda c: (c, 0)),
                   pl.BlockSpec((1, npair, HEAD, PAIR), lambda c: (seq(c), 0, 0, 0))],
        out_shape=[jax.ShapeDtypeStruct((n, rw), BF16),
                   jax.ShapeDtypeStruct((cfg.n_seq, npair, HEAD, PAIR), F32)],
        scratch_shapes=[pltpu.VMEM((CHUNK + 8, 3 * rw), F32), pltpu.VMEM((CHUNK + 8, lp), F32),
                        row_scr, row_scr, row_scr, row_scr, row_scr,
                        pltpu.VMEM((npair * HEAD, 2 * PAIR), BF16), pltpu.VMEM((npair * HEAD, PAIR), F32),
                        pltpu.VMEM((npair * HEAD, PAIR), F32), pltpu.VMEM((npair, 2 * PAIR, PAIR), BF16)],
        compiler_params=_params("arbitrary"),
        name="rwkv",
    )(proj, proj, shift8, wkv_pairs, mu, w0, wup, a0, aup, gup, kk, ka, rk, lng, lnb)


def _outproj_kernel(ym_ref, yr_ref, wa_ref, wb_ref, xp_ref, xs_ref, gate_ref, o_ref, *, cfg, tm):
    i = pl.program_id(0)
    acc = (jnp.dot(ym_ref[...], wa_ref[...], preferred_element_type=F32)
           + jnp.dot(yr_ref[...], wb_ref[...], preferred_element_type=F32))

    def residual(x_ref):
        for s in range(tm // CHUNK):
            seq = _seq_of_chunk(i * (tm // CHUNK) + s, cfg)
            rows = slice(s * CHUNK, (s + 1) * CHUNK)
            o_ref[rows, :] = x_ref[rows, :] + gate_ref[pl.ds(seq, 1), :] * acc[rows, :]

    npt = _prompt_tiles(cfg, tm)
    pl.when(i < npt)(lambda: residual(xp_ref))
    pl.when(i >= npt)(lambda: residual(xs_ref))


def _outproj(ym, yr, wa, wb, x_p, x_s, gate, cfg, tm=512, tn=1024):
    n, d = cfg.n_rows, x_p.shape[1]
    ka, kb = ym.shape[1], yr.shape[1]
    s = gate.shape[0]
    return pl.pallas_call(
        functools.partial(_outproj_kernel, cfg=cfg, tm=tm),
        grid=(n // tm, d // tn),
        in_specs=[pl.BlockSpec((tm, ka), lambda i, j: (i, 0)),
                  pl.BlockSpec((tm, kb), lambda i, j: (i, 0)),
                  pl.BlockSpec((ka, tn), lambda i, j: (0, j)),
                  pl.BlockSpec((kb, tn), lambda i, j: (0, j))]
                 + _row_specs(cfg, tm, tn, lambda j: j)
                 + [pl.BlockSpec((s, tn), lambda i, j: (0, j))],
        out_specs=pl.BlockSpec((tm, tn), lambda i, j: (i, j)),
        out_shape=jax.ShapeDtypeStruct((n, d), F32),
        compiler_params=_params("parallel", "parallel"),
        name="outproj",
    )(ym, yr, wa, wb, x_p, x_s, gate)


def _route(lg, n_experts):
    lane = lax.broadcasted_iota(jnp.int32, lg.shape, 1)
    first = lambda mask: jnp.min(jnp.where(mask, lane, LANES), axis=-1, keepdims=True)
    is_grp = lane < E_GROUPS
    gmax = jnp.max(jnp.where(is_grp, lg, -jnp.inf), axis=-1, keepdims=True)
    gsel = first(is_grp & (lg == gmax))
    pg = 1.0 / jnp.sum(jnp.where(is_grp, jnp.exp(lg - gmax), 0.0), axis=-1, keepdims=True)
    e_lane = lane - E_GROUPS
    in_grp = (e_lane >= 0) & (e_lane < n_experts) & (e_lane // E_PER_GROUP == gsel)
    emax = jnp.max(jnp.where(in_grp, lg, -jnp.inf), axis=-1, keepdims=True)
    p = jnp.where(in_grp, jnp.exp(lg - emax), 0.0)
    eprob = p / jnp.sum(p, axis=-1, keepdims=True)
    v1 = jnp.max(jnp.where(in_grp, eprob, -1.0), axis=-1, keepdims=True)
    i1 = first(in_grp & (eprob == v1))
    rest = in_grp & (lane != i1)
    v2 = jnp.max(jnp.where(rest, eprob, -1.0), axis=-1, keepdims=True)
    i2 = first(rest & (eprob == v2))
    tot = v1 + v2
    cols = [(i1 - E_GROUPS).astype(F32), (i2 - E_GROUPS).astype(F32), v1 / tot * pg, v2 / tot * pg]
    out = jnp.zeros(lg.shape, F32)
    for c, val in enumerate(cols):
        out = jnp.where(lane == c, val, out)
    return out


def _router_kernel(x_ref, shift_ref, scale_ref, g_ref, wr_ref, br_ref, h_ref, rt_ref, *, cfg, tm, n_experts):
    i = pl.program_id(0)
    for s in range(tm // CHUNK):
        seq = _seq_of_chunk(i * (tm // CHUNK) + s, cfg)
        rows = slice(s * CHUNK, (s + 1) * CHUNK)
        h = _modulated_norm(x_ref[rows, :], g_ref[...], scale_ref[pl.ds(seq, 1), :], shift_ref[pl.ds(seq, 1), :])
        h_ref[rows, :] = h.astype(BF16)
        logits = jnp.dot(h, wr_ref[...], precision=HI, preferred_element_type=F32) + br_ref[...]
        rt_ref[rows, :] = _route(logits, n_experts)


def _router(x1, shift, scale, g, wr, br, cfg, n_experts, tm=256):
    n, d = x1.shape
    assert E_GROUPS + n_experts <= LANES and n_experts == E_GROUPS * E_PER_GROUP
    full = lambda a: pl.BlockSpec(a.shape, lambda i: (0,) * a.ndim)
    return pl.pallas_call(
        functools.partial(_router_kernel, cfg=cfg, tm=tm, n_experts=n_experts),
        grid=(n // tm,),
        in_specs=[pl.BlockSpec((tm, d), lambda i: (i, 0)), full(shift), full(scale), full(g), full(wr), full(br)],
        out_specs=[pl.BlockSpec((tm, d), lambda i: (i, 0)), pl.BlockSpec((tm, LANES), lambda i: (i, 0))],
        out_shape=[jax.ShapeDtypeStruct((n, d), BF16), jax.ShapeDtypeStruct((n, LANES), F32)],
        compiler_params=_params("parallel"),
        name="router",
    )(x1, shift, scale, g, wr, br)


def _new_weights(be_ref):
    i = pl.program_id(1)
    return jnp.logical_or(i == 0, be_ref[i] != be_ref[jnp.maximum(i - 1, 0)])


def _for_valid_rows(nv_ref, out_ref, compute):
    nv = nv_ref[pl.program_id(1)]
    for s in range(out_ref.shape[0] // E_SUB):
        rows = slice(s * E_SUB, (s + 1) * E_SUB)

        @pl.when(nv > s * E_SUB)
        def _():
            out_ref[rows, :] = compute(rows)

        @pl.when(nv <= s * E_SUB)
        def _():
            out_ref[rows, :] = jnp.zeros((E_SUB, out_ref.shape[1]), out_ref.dtype)


def _expert_up_kernel(be_ref, nv_ref, x_ref, wg_ref, wu_ref, h_ref, wg_s, wu_s):
    @pl.when(_new_weights(be_ref))
    def _():
        wg_s[...] = wg_ref[0].astype(BF16)
        wu_s[...] = wu_ref[0].astype(BF16)

    def hidden(rows):
        x = x_ref[rows, :]
        gate = jnp.dot(x, wg_s[...], preferred_element_type=F32)
        up = jnp.dot(x, wu_s[...], preferred_element_type=F32)
        return (_silu(gate) * up).astype(BF16)

    _for_valid_rows(nv_ref, h_ref, hidden)


def _expert_down_kernel(be_ref, nv_ref, h_ref, wd_ref, y_ref, wd_s):
    @pl.when(_new_weights(be_ref))
    def _():
        wd_s[...] = wd_ref[0].astype(BF16)

    _for_valid_rows(nv_ref, y_ref, lambda rows: jnp.dot(h_ref[rows, :], wd_s[...], preferred_element_type=F32))


def _experts(xb, blk_e, n_valid, wg, wu, wd, tm, tf=512, tn=2048):
    rows, d = xb.shape
    f = wg.shape[2]
    nblk = rows // tm
    tf, tn = min(tf, f), min(tn, d)
    assert tm % E_SUB == 0 and f % tf == 0 and d % tn == 0
    hid = pl.pallas_call(
        _expert_up_kernel,
        grid_spec=pltpu.PrefetchScalarGridSpec(
            num_scalar_prefetch=2, grid=(f // tf, nblk),
            in_specs=[pl.BlockSpec((tm, d), lambda j, i, be, nv: (i, 0)),
                      pl.BlockSpec((1, d, tf), lambda j, i, be, nv: (be[i], 0, j)),
                      pl.BlockSpec((1, d, tf), lambda j, i, be, nv: (be[i], 0, j))],
            out_specs=pl.BlockSpec((tm, tf), lambda j, i, be, nv: (i, j)),
            scratch_shapes=[pltpu.VMEM((d, tf), BF16), pltpu.VMEM((d, tf), BF16)]),
        out_shape=jax.ShapeDtypeStruct((rows, f), BF16),
        compiler_params=_params("arbitrary", "arbitrary"),
        name="expert_up",
    )(blk_e, n_valid, xb, wg, wu)
    return pl.pallas_call(
        _expert_down_kernel,
        grid_spec=pltpu.PrefetchScalarGridSpec(
            num_scalar_prefetch=2, grid=(d // tn, nblk),
            in_specs=[pl.BlockSpec((tm, f), lambda j, i, be, nv: (i, 0)),
                      pl.BlockSpec((1, f, tn), lambda j, i, be, nv: (be[i], 0, j))],
            out_specs=pl.BlockSpec((tm, tn), lambda j, i, be, nv: (i, j)),
            scratch_shapes=[pltpu.VMEM((f, tn), BF16)]),
        out_shape=jax.ShapeDtypeStruct((rows, d), F32),
        compiler_params=_params("arbitrary", "arbitrary"),
        name="expert_down",
    )(blk_e, n_valid, hid, wd)


def _final_kernel(x_ref, moe_ref, gate_ref, g_ref, o_ref, *, cfg, tm, tile0):
    i = pl.program_id(0) + tile0
    for s in range(tm // CHUNK):
        seq = _seq_of_chunk(i * (tm // CHUNK) + s, cfg)
        rows = slice(s * CHUNK, (s + 1) * CHUNK)
        x = x_ref[rows, :] + gate_ref[pl.ds(seq, 1), :] * moe_ref[rows, :]
        o_ref[rows, :] = x * lax.rsqrt(jnp.mean(x * x, axis=-1, keepdims=True) + NORM_EPS) * g_ref[...]


def _final(x1, moe, gate, g, cfg, row0, nrows, tm=256):
    d = x1.shape[1]
    assert row0 % tm == 0 and nrows % tm == 0
    tile0 = row0 // tm
    full = lambda a: pl.BlockSpec(a.shape, lambda i: (0,) * a.ndim)
    return pl.pallas_call(
        functools.partial(_final_kernel, cfg=cfg, tm=tm, tile0=tile0),
        grid=(nrows // tm,),
        in_specs=[pl.BlockSpec((tm, d), lambda i: (i + tile0, 0)), pl.BlockSpec((tm, d), lambda i: (i + tile0, 0)),
                  full(gate), full(g)],
        out_specs=pl.BlockSpec((tm, d), lambda i: (i, 0)),
        out_shape=jax.ShapeDtypeStruct((nrows, d), F32),
        compiler_params=_params("parallel"),
        name="final_norm",
    )(x1, moe, gate, g)


def _dispatch(route, n_experts, tm):
    n = route.shape[0]
    eid = route[:, :TOP_K].astype(jnp.int32)
    wts = route[:, TOP_K:2 * TOP_K]

    m = n * TOP_K
    e_flat = eid.reshape(m)
    order = jnp.argsort(e_flat)
    e_sorted = e_flat[order]
    counts = jnp.bincount(e_flat, length=n_experts).astype(jnp.int32)
    padded = (counts + tm - 1) // tm * tm
    start = jnp.cumsum(counts) - counts
    pend = jnp.cumsum(padded)
    pstart = pend - padded
    dest_sorted = pstart[e_sorted] + jnp.arange(m, dtype=jnp.int32) - start[e_sorted]
    n_blocks = -(-(m + n_experts * (tm - 1)) // tm)
    rows = n_blocks * tm
    tok_sorted = (order // TOP_K).astype(jnp.int32)
    row_tok = (jnp.arange(rows, dtype=jnp.int32) % n).at[dest_sorted].set(tok_sorted)
    slot_pos = jnp.zeros((m,), jnp.int32).at[order].set(dest_sorted).reshape(n, TOP_K)
    n_used = pend[-1] // tm
    blk = jnp.arange(n_blocks, dtype=jnp.int32)
    blk_e = jnp.minimum(jnp.searchsorted(pend, blk * tm, side='right'), n_experts - 1).astype(jnp.int32)
    n_valid = jnp.where(blk < n_used, jnp.clip(pstart[blk_e] + counts[blk_e] - blk * tm, 0, tm), 0).astype(jnp.int32)
    last_e = blk_e[jnp.maximum(n_used - 1, 0)]
    blk_e = jnp.where(blk < n_used, blk_e, last_e)
    return row_tok, slot_pos, wts, blk_e, n_valid


def _to_window(a, cfg):
    return jnp.concatenate([a, jnp.zeros(a.shape[:-1] + (cfg.win - cfg.r_cols,), a.dtype)], axis=-1)


def _pad_rows(a, rows):
    return jnp.concatenate([a, jnp.zeros((rows - a.shape[0],) + a.shape[1:], a.dtype)], axis=0)


def kernel(x_prompt, x_sample, state_conv, state_ssm, state_shift, state_wkv, c_prompt, c_sample, norm1_g, w_mod, b_mod, w_in, conv_w, conv_b, dt_bias, a_log, d_skip, m_norm_g, shift_mu, w0, w_up, a0, a_up, g_up, k_k, k_a, r_k, ln_x_g, ln_x_b, w_out, norm2_g, w_grp, b_grp, w_erouter, b_erouter, e_gate, e_up, e_down, final_norm_g):
    assert w_mod.shape[0] == 1, "single-layer trunk"
    bp, tp, d = x_prompt.shape
    bs, ts, _ = x_sample.shape
    mi = m_norm_g.shape[-1]
    rw = w0.shape[-1]
    lw, la, lg = w_up.shape[1], a_up.shape[1], g_up.shape[1]
    cfg = Cfg(d, bp, tp, bs, ts, mi, rw, lw, la, lg)
    nc = cfg.nc
    assert tp % CHUNK == 0 and ts % CHUNK == 0 and tp >= CONV_W - 1 and ts >= CONV_W - 1
    assert w_in.shape[-1] == cfg.o_rw + cfg.mh + cfg.r_cols and cfg.o_rw % LANES == 0
    assert state_conv.shape[-1] == cfg.cd and lw <= LANES and la <= LANES and cfg.mh <= LANES
    n = cfg.n_rows
    n_seq = cfg.n_seq
    mh, rh = cfg.mh, cfg.rh
    npair = rw // PAIR
    n_experts = e_gate.shape[1]

    x_p = x_prompt.reshape(bp * tp, d)
    x_s = x_sample.reshape(bs * ts, d)
    c_all = jnp.concatenate([c_prompt, c_sample], axis=0)

    mod = _modulation(c_all, w_mod[0], b_mod[0])
    shift1, scale1, gate1, shift2, scale2, gate2 = [mod[:, i * d:(i + 1) * d] for i in range(6)]

    wi = w_in[0]
    o2 = cfg.o_rw
    zc = lambda w: jnp.zeros((d, w), wi.dtype)
    w_proj = jnp.concatenate([wi[:, :o2], wi[:, o2 + mh:], zc(cfg.win - cfg.r_cols), wi[:, o2:o2 + mh], zc(LANES - mh)],
                             axis=1).astype(BF16)
    proj = _inproj(x_p, x_s, shift1, scale1, norm1_g, w_proj, cfg)

    zeros = lambda b, *s: jnp.zeros((b,) + s, F32)
    conv0 = jnp.concatenate([zeros(bp, CONV_W - 1, cfg.cd), state_conv[0]], axis=0)
    conv8 = jnp.concatenate([zeros(n_seq, 8 - (CONV_W - 1), cfg.cd), conv0], axis=1)
    nst = state_ssm.shape[-1]
    ssm0 = jnp.concatenate([zeros(bp, mi, nst), state_ssm[0].reshape(bs, mi, nst)], axis=0)
    lane_pad = lambda a: jnp.concatenate([a.reshape(1, -1), jnp.zeros((1, LANES - a.shape[-1]), F32)], axis=1)
    expand = ((jnp.arange(2 * LANES)[:, None] % LANES) == (jnp.arange(mi)[None, :] // HEAD)).astype(BF16)
    ym, ssm_new = _mamba(proj, conv8, ssm0, conv_w[0], conv_b[0].reshape(1, -1), lane_pad(dt_bias[0]),
                         lane_pad(a_log[0]), jnp.repeat(d_skip[0], HEAD).reshape(1, mi), m_norm_g[0].reshape(1, mi),
                         expand, cfg)

    sh0 = jnp.concatenate([zeros(bp, 1, state_shift.shape[-1]), state_shift[0]], axis=0)
    sh8 = jnp.concatenate([zeros(n_seq, 7, cfg.win), _to_window(sh0, cfg)], axis=1)
    to_pairs = lambda s: s.reshape(-1, npair, 2, HEAD, HEAD).transpose(0, 1, 3, 2, 4).reshape(-1, npair, HEAD, PAIR)
    from_pairs = lambda s: s.reshape(-1, npair, HEAD, 2, HEAD).transpose(0, 1, 3, 2, 4).reshape(-1, rh, HEAD, HEAD)
    wkv0 = jnp.concatenate([zeros(bp, npair, HEAD, PAIR), to_pairs(state_wkv[0])], axis=0)
    row = lambda a: a.reshape(1, -1)
    yr, wkv_new = _rwkv(proj, sh8, wkv0, _to_window(row(shift_mu[0]), cfg), row(w0[0]),
                        _pad_rows(w_up[0], LANES), row(a0[0]), _pad_rows(a_up[0], LANES), g_up[0],
                        row(k_k[0]), row(k_a[0]), row(r_k[0]), row(ln_x_g[0]), row(ln_x_b[0]), cfg)

    wo = w_out[0].astype(BF16)
    x1 = _outproj(ym, yr, wo[:mi], wo[mi:], x_p, x_s, gate1, cfg)
    wr = jnp.concatenate([w_grp[0], w_erouter[0], jnp.zeros((d, LANES - E_GROUPS - n_experts), F32)], axis=1)
    br = lane_pad(jnp.concatenate([b_grp[0], b_erouter[0]]))
    h2, route = _router(x1, shift2, scale2, norm2_g, wr, br, cfg, n_experts)

    tm_e = 2 * E_SUB
    row_tok, slot_pos, wts, blk_e, n_valid = _dispatch(route, n_experts, tm_e)
    yb = _experts(h2[row_tok], blk_e, n_valid, e_gate[0], e_up[0], e_down[0], tm_e)
    moe = yb[slot_pos[:, 0]] * wts[:, 0:1] + yb[slot_pos[:, 1]] * wts[:, 1:2]

    np_ = bp * tp
    fg = final_norm_g.reshape(1, d)
    y_prompt = _final(x1, moe, gate2, fg, cfg, 0, np_).reshape(bp, tp, d)
    y_sample = _final(x1, moe, gate2, fg, cfg, np_, bs * ts).reshape(bs, ts, d)
    tails = lambda b, t, base: jnp.stack(
        [lax.slice(proj, (base + (i + 1) * t - (CONV_W - 1), 0), (base + (i + 1) * t, nc)) for i in range(b)])
    pp = tails(bp, tp, 0)
    ps = tails(bs, ts, np_)
    conv_of = lambda p: p[:, :, mi:mi + cfg.cd][None]
    shift_of = lambda p: p[:, -1:, cfg.o_rw:cfg.o_rw + cfg.r_cols][None]
    ssm_new = ssm_new.reshape(n_seq, mh, HEAD, nst)
    wkv_new = from_pairs(wkv_new)
    return (y_prompt, y_sample,
            conv_of(pp), ssm_new[:bp][None], shift_of(pp), wkv_new[:bp][None],
            conv_of(ps), ssm_new[bp:][None], shift_of(ps), wkv_new[bp:][None])
```

```python
import functools
from typing import NamedTuple

import jax
import jax.numpy as jnp
from jax import lax
from jax.experimental import pallas as pl
from jax.experimental.pallas import tpu as pltpu

F32 = jnp.float32
BF16 = jnp.bfloat16
HI = lax.Precision.HIGHEST

LANES = 128
CHUNK = 64
HEAD = 64
PAIR = 2 * HEAD
NORM_EPS = 1e-6
M_NORM_EPS = 1e-5
R_LN_EPS = 64e-5
CONV_W = 4
E_GROUPS = 4
E_PER_GROUP = 8
TOP_K = 2
VMEM_LIMIT = 56 * 1024 * 1024
E_SUB = 256
EXP_NEG_HALF = 0.6065306597126334
SA_GROUP = 8


class Cfg(NamedTuple):
    d: int
    bp: int
    tp: int
    bs: int
    ts: int
    mi: int
    rw: int
    lw: int
    la: int
    lg: int

    @property
    def cpp(self):
        return self.tp // CHUNK

    @property
    def cps(self):
        return self.ts // CHUNK

    @property
    def n_chunks(self):
        return self.bp * self.cpp + self.bs * self.cps

    @property
    def n_seq(self):
        return self.bp + self.bs

    @property
    def n_rows(self):
        return self.n_chunks * CHUNK

    @property
    def groups(self):
        return self.mi // 256

    @property
    def nb(self):
        return self.groups * LANES

    @property
    def cd(self):
        return self.mi + 2 * self.nb

    @property
    def mh(self):
        return self.mi // HEAD

    @property
    def rh(self):
        return self.rw // HEAD

    @property
    def o_rw(self):
        return self.mi + self.cd

    @property
    def r_cols(self):
        return 3 * self.rw + self.lw + self.la + self.lg

    @property
    def tail(self):
        return -(-(self.lw + self.la + self.lg) // LANES) * LANES

    @property
    def win(self):
        return 3 * self.rw + self.tail

    @property
    def nc(self):
        return self.o_rw + self.win


def _seq_of_chunk(c, cfg):
    npc = cfg.bp * cfg.cpp
    return jnp.where(c < npc, c // cfg.cpp, cfg.bp + (c - npc) // cfg.cps)


def _is_first_chunk(c, cfg):
    npc = cfg.bp * cfg.cpp
    return jnp.where(c < npc, c % cfg.cpp == 0, (c - npc) % cfg.cps == 0)


def _silu(x):
    return x * jax.nn.sigmoid(x)


def _nt(a, b, **kw):
    return lax.dot_general(a, b, (((1,), (1,)), ((), ())), preferred_element_type=F32, **kw)


def _tn(a, b, **kw):
    return lax.dot_general(a, b, (((0,), (0,)), ((), ())), preferred_element_type=F32, **kw)


def _params(*sem):
    return pltpu.CompilerParams(dimension_semantics=sem, vmem_limit_bytes=VMEM_LIMIT)


def _mod_kernel(c_ref, w_ref, b_ref, o_ref):
    o_ref[...] = jnp.dot(_silu(c_ref[...]), w_ref[...], preferred_element_type=F32) + b_ref[...]


def _modulation(c_all, w_mod, b_mod):
    s, d = c_all.shape
    cols = w_mod.shape[1]
    tn = 512
    return pl.pallas_call(
        _mod_kernel,
        grid=(cols // tn,),
        in_specs=[pl.BlockSpec((s, d), lambda j: (0, 0)),
                  pl.BlockSpec((d, tn), lambda j: (0, j)),
                  pl.BlockSpec((1, tn), lambda j: (0, j))],
        out_specs=pl.BlockSpec((s, tn), lambda j: (0, j)),
        out_shape=jax.ShapeDtypeStruct((s, cols), F32),
        compiler_params=_params("parallel"),
        name="modulation",
    )(c_all, w_mod, b_mod.reshape(1, cols))


def _modulated_norm(x, g, scale, shift):
    y = x * lax.rsqrt(jnp.mean(x * x, axis=-1, keepdims=True) + NORM_EPS) * g
    return y * (1.0 + scale) + shift


def _prompt_tiles(cfg, tm):
    assert (cfg.bp * cfg.tp) % tm == 0 and (cfg.bs * cfg.ts) % tm == 0 and tm % CHUNK == 0
    return cfg.bp * cfg.tp // tm


def _row_specs(cfg, tm, width, col=None):
    npt = _prompt_tiles(cfg, tm)
    c = (lambda *r: 0) if col is None else col
    return [pl.BlockSpec((tm, width), lambda i, *r: (jnp.minimum(i, npt - 1), jnp.where(i < npt, c(*r), 0))),
            pl.BlockSpec((tm, width), lambda i, *r: (jnp.maximum(i - npt, 0), jnp.where(i >= npt, c(*r), 0)))]


def _prenorm_kernel(xp_ref, xs_ref, shift_ref, scale_ref, g_ref, h_ref, *, cfg, tm):
    i = pl.program_id(0)

    def norm_rows(x_ref):
        for s in range(tm // CHUNK):
            seq = _seq_of_chunk(i * (tm // CHUNK) + s, cfg)
            rows = slice(s * CHUNK, (s + 1) * CHUNK)
            h = _modulated_norm(x_ref[rows, :], g_ref[...], scale_ref[pl.ds(seq, 1), :], shift_ref[pl.ds(seq, 1), :])
            h_ref[rows, :] = h.astype(BF16)

    npt = _prompt_tiles(cfg, tm)
    pl.when(i < npt)(lambda: norm_rows(xp_ref))
    pl.when(i >= npt)(lambda: norm_rows(xs_ref))


def _prenorm(x_p, x_s, shift, scale, g, cfg, tm=256):
    d = x_p.shape[1]
    full = lambda a: pl.BlockSpec(a.shape, lambda i: (0,) * a.ndim)
    return pl.pallas_call(
        functools.partial(_prenorm_kernel, cfg=cfg, tm=tm),
        grid=(cfg.n_rows // tm,),
        in_specs=_row_specs(cfg, tm, d) + [full(shift), full(scale), full(g)],
        out_specs=pl.BlockSpec((tm, d), lambda i: (i, 0)),
        out_shape=jax.ShapeDtypeStruct((cfg.n_rows, d), BF16),
        compiler_params=_params("parallel"),
        name="prenorm",
    )(x_p, x_s, shift, scale, g)


def _inproj_kernel(h_ref, w_ref, wn_ref, o_ref, wb_s, *, first_shifted, shift, tn):
    j = pl.program_id(0)
    new_tile = pl.program_id(1) == 0

    @pl.when(jnp.logical_and(new_tile, j < first_shifted))
    def _():
        wb_s[...] = w_ref[...].astype(BF16)

    @pl.when(jnp.logical_and(new_tile, j >= first_shifted))
    def _():
        w = jnp.concatenate([w_ref[...], wn_ref[...]], axis=1)
        wb_s[...] = w[:, shift:shift + tn].astype(BF16)

    o_ref[...] = jnp.dot(h_ref[...], wb_s[...], preferred_element_type=F32)


def _inproj(h, w, cfg, tm=1024, tn=512):
    n, d = h.shape
    nout = cfg.o_rw + cfg.win
    assert cfg.o_rw % tn == 0 and nout % tn == 0 and n % tm == 0 and cfg.mh <= LANES
    last_next = (w.shape[1] - 1) // LANES
    return pl.pallas_call(
        functools.partial(_inproj_kernel, first_shifted=cfg.o_rw // tn, shift=cfg.mh, tn=tn),
        grid=(nout // tn, n // tm),
        in_specs=[pl.BlockSpec((tm, d), lambda j, i: (i, 0)),
                  pl.BlockSpec((d, tn), lambda j, i: (0, j)),
                  pl.BlockSpec((d, LANES), lambda j, i: (0, jnp.minimum((j + 1) * (tn // LANES), last_next)))],
        out_specs=pl.BlockSpec((tm, tn), lambda j, i: (i, j)),
        out_shape=jax.ShapeDtypeStruct((n, nout), F32),
        scratch_shapes=[pltpu.VMEM((d, tn), BF16)],
        compiler_params=_params("arbitrary", "arbitrary"),
        name="inproj",
    )(h, w, w)


def _dtproj_kernel(h_ref, w_ref, o_ref):
    o_ref[...] = jnp.dot(h_ref[...], w_ref[...].astype(BF16), preferred_element_type=F32)


def _dtproj(h, w, cfg, tm=1024):
    n, d = h.shape
    return pl.pallas_call(
        _dtproj_kernel,
        grid=(n // tm,),
        in_specs=[pl.BlockSpec((tm, d), lambda i: (i, 0)), pl.BlockSpec((d, LANES), lambda i: (0, cfg.o_rw // LANES))],
        out_specs=pl.BlockSpec((tm, LANES), lambda i: (i, 0)),
        out_shape=jax.ShapeDtypeStruct((n, LANES), F32),
        compiler_params=_params("parallel"),
        name="dtproj",
    )(h, w)


def _mamba_kernel(zx_ref, dt_ref, cst_ref, sst_ref, cw_ref, cb_ref, dtb_ref, alog_ref, dsk_ref, ng_ref, exp_ref,
                  ym_ref, ssm_ref, buf, u_scr, *, cfg):
    c = pl.program_id(0)
    mi, nb, cd = cfg.mi, cfg.nb, cfg.cd
    L = CHUNK

    @pl.when(_is_first_chunk(c, cfg))
    def _():
        buf[0:8, :] = cst_ref[0]
        ssm_ref[0] = sst_ref[0]

    xbc = zx_ref[:, mi:mi + cd]
    buf[8:8 + L, :] = xbc
    conv = (cb_ref[...] + buf[5:5 + L, :] * cw_ref[0:1, :] + buf[6:6 + L, :] * cw_ref[1:2, :]
            + buf[7:7 + L, :] * cw_ref[2:3, :] + xbc * cw_ref[3:4, :])
    buf[0:8, :] = buf[L:L + 8, :]
    u_scr[...] = _silu(conv)

    dt = jax.nn.softplus(dt_ref[...] + dtb_ref[...])
    da = dt * (-jnp.exp(alog_ref[...]))
    row = lax.broadcasted_iota(jnp.int32, (L, 256), 0)
    pos = lax.broadcasted_iota(jnp.int32, (L, 256), 1) % HEAD
    causal = row >= pos
    diag = row == pos
    tri2 = (lax.broadcasted_iota(jnp.int32, (L, 2 * L), 0)
            >= lax.broadcasted_iota(jnp.int32, (L, 2 * L), 1) % L).astype(BF16)
    lane = lax.broadcasted_iota(jnp.int32, (L, PAIR), 1)
    row16 = lax.broadcasted_iota(jnp.int32, (16, 256), 0)
    ones16 = jnp.ones((16, LANES), BF16)
    dt_hl = jnp.concatenate(_split_hi_lo(dt), axis=1)
    da_hl = jnp.concatenate(_split_hi_lo(da), axis=1)

    for g in range(cfg.groups):
        cs = slice(g * 256, (g + 1) * 256)
        xh = u_scr[:, cs]
        bg = u_scr[:, mi + g * LANES:mi + (g + 1) * LANES]
        cg = u_scr[:, mi + nb + g * LANES:mi + nb + (g + 1) * LANES]
        eg = exp_ref[:, cs]
        dte = jnp.dot(dt_hl, eg, preferred_element_type=F32)
        dae = jnp.dot(da_hl, eg, preferred_element_type=F32)
        cum = jnp.dot(tri2, jnp.concatenate(_split_hi_lo(dae), axis=0), preferred_element_type=F32)
        cum_row = jnp.sum(jnp.where(diag, cum, 0.0), axis=0, keepdims=True)
        last = cum[L - 1:L, :]
        decay = jnp.exp(jnp.where(causal, cum - cum_row, -jnp.inf))
        xdt = xh * dte
        sc2 = _nt(cg, jnp.concatenate([bg, bg], axis=0))
        ys = []
        for q in range(2):
            ps = slice(q * PAIR, (q + 1) * PAIR)
            xq = xdt[:, ps]
            rhs = jnp.concatenate([jnp.where(lane < HEAD, xq, 0.0), jnp.where(lane >= HEAD, xq, 0.0)], axis=0)
            ys.append(jnp.dot(decay[:, ps] * sc2, rhs, preferred_element_type=F32))
        y = jnp.concatenate(ys, axis=1)
        sg = ssm_ref[0, cs, :]
        y = y + _nt(cg, sg) * jnp.exp(cum)
        new = _tn(xdt * jnp.exp(last - cum), bg)
        e_last = jnp.exp(last)
        e_hi = e_last.astype(BF16).astype(F32)
        e_rows = jnp.where(row16 == 0, e_hi, jnp.where(row16 == 1, e_last - e_hi, 0.0)).astype(BF16)
        dcol = _tn(e_rows, ones16)
        ssm_ref[0, cs, :] = sg * dcol + new
        y = y + xh * dsk_ref[:, cs]
        y = y * _silu(zx_ref[:, cs])
        y = y * lax.rsqrt(jnp.mean(y * y, axis=-1, keepdims=True) + M_NORM_EPS)
        ym_ref[:, cs] = (y * ng_ref[:, cs]).astype(BF16)


def _mamba(proj, dt_raw, conv_state8, ssm_state, cw, cb, dtb, alog, dsk, ng, expand, cfg):
    n = proj.shape[0]
    mi, cd = cfg.mi, cfg.cd
    nst = ssm_state.shape[-1]
    seq = lambda c: _seq_of_chunk(c, cfg)
    full = lambda a: pl.BlockSpec(a.shape, lambda c: (0,) * a.ndim)
    return pl.pallas_call(
        functools.partial(_mamba_kernel, cfg=cfg),
        grid=(cfg.n_chunks,),
        in_specs=[pl.BlockSpec((CHUNK, mi + cd), lambda c: (c, 0)),
                  pl.BlockSpec((CHUNK, LANES), lambda c: (c, 0)),
                  pl.BlockSpec((1, 8, cd), lambda c: (seq(c), 0, 0)),
                  pl.BlockSpec((1, mi, nst), lambda c: (seq(c), 0, 0)),
                  full(cw), full(cb), full(dtb), full(alog), full(dsk), full(ng), full(expand)],
        out_specs=[pl.BlockSpec((CHUNK, mi), lambda c: (c, 0)),
                   pl.BlockSpec((1, mi, nst), lambda c: (seq(c), 0, 0))],
        out_shape=[jax.ShapeDtypeStruct((n, mi), BF16),
                   jax.ShapeDtypeStruct((cfg.n_seq, mi, nst), F32)],
        scratch_shapes=[pltpu.VMEM((CHUNK + 8, cd), F32), pltpu.VMEM((CHUNK, cd), F32)],
        compiler_params=_params("arbitrary"),
        name="mamba",
    )(proj, dt_raw, conv_state8, ssm_state, cw, cb, dtb, alog, dsk, ng, expand)


def _seg_sum(x, ob):
    parts = [jnp.dot(x[:, p * PAIR:(p + 1) * PAIR], ob, preferred_element_type=F32) for p in range(x.shape[1] // PAIR)]
    return jnp.concatenate(parts, axis=1)


def _split_hi_lo(x):
    hi = x.astype(BF16)
    lo = (x - hi.astype(F32)).astype(BF16)
    return hi, lo


def _rwkv_kernel(rkv_ref, lora_ref, sh_ref, wkv_in_ref, mu_ref, w0_ref, wup_ref, a0_ref, aup_ref, gup_ref,
                 kk_ref, ka_ref, rk_ref, lng_ref, lnb_ref,
                 yr_ref, wkv_ref, pbuf, lbuf, w_s, a_s, b_s, k_s, y_s, vt_s, ut_s, y0_s, g_s, *, cfg):
    c = pl.program_id(0)
    rw = cfg.rw
    L = CHUNK
    npair = rw // PAIR

    @pl.when(_is_first_chunk(c, cfg))
    def _():
        pbuf[0:8, :] = sh_ref[0, :, 0:3 * rw]
        lbuf[0:8, :] = sh_ref[0, :, 3 * rw:]
        wkv_ref[0] = wkv_in_ref[0]

    pr = rkv_ref[...]
    pl_ = lora_ref[...]
    pbuf[8:8 + L, :] = pr
    lbuf[8:8 + L, :] = pl_
    xs = pr + (pbuf[7:7 + L, :] - pr) * mu_ref[:, 0:3 * rw]
    xl = pl_ + (lbuf[7:7 + L, :] - pl_) * mu_ref[:, 3 * rw:]
    pbuf[0:8, :] = pbuf[L:L + 8, :]
    lbuf[0:8, :] = lbuf[L:L + 8, :]

    r = xs[:, 0:rw]
    k = xs[:, rw:2 * rw]
    v = xs[:, 2 * rw:3 * rw]
    wd = xl[:, 0:LANES]
    ad = xl[:, cfg.lw:cfg.lw + LANES]
    gd = xl[:, cfg.lw + cfg.la:cfg.lw + cfg.la + cfg.lg]

    lane = lax.broadcasted_iota(jnp.int32, (PAIR, PAIR), 1)
    rowi = lax.broadcasted_iota(jnp.int32, (PAIR, PAIR), 0)
    ob = ((lane // HEAD) == (rowi // HEAD)).astype(F32)

    w_z = w0_ref[...] + jnp.dot(jnp.tanh(wd), wup_ref[...], preferred_element_type=F32)
    logw = -(jax.nn.sigmoid(w_z) * EXP_NEG_HALF)
    w_s[...] = jnp.exp(logw)
    a = jax.nn.sigmoid(a0_ref[...] + jnp.dot(ad, aup_ref[...], preferred_element_type=F32))
    kk = k * kk_ref[...]
    kk = kk * lax.rsqrt(jnp.maximum(_seg_sum(kk * kk, ob), 1e-24))
    kh = k * (1.0 + (a - 1.0) * ka_ref[...])
    a_s[...] = -kk
    b_s[...] = kk * a
    k_s[...] = kh

    tri2 = (lax.broadcasted_iota(jnp.int32, (L, 2 * L), 0)
            >= lax.broadcasted_iota(jnp.int32, (L, 2 * L), 1) % L).astype(BF16)
    cum = jnp.dot(tri2, jnp.concatenate(_split_hi_lo(logw), axis=0), preferred_element_type=F32)
    p_inv = jnp.exp(-cum)
    r_p = r * jnp.exp(cum)
    b_p = b_s[...] * p_inv
    k_p = kh * p_inv
    lo_lanes = lane < HEAD
    g_row = lax.broadcasted_iota(jnp.int32, (2 * PAIR, PAIR), 0)
    g_lane = lax.broadcasted_iota(jnp.int32, (2 * PAIR, PAIR), 1)
    g_keep = (g_row % HEAD) <= (g_lane % HEAD)

    def by_head(x):
        x2 = jnp.concatenate([x, x], axis=0)
        return jnp.where(lo_lanes == (rowi < HEAD), x2, 0.0).astype(BF16)

    for p in range(npair):
        ps = slice(p * PAIR, (p + 1) * PAIR)
        hs = slice(p * HEAD, (p + 1) * HEAD)
        r_bd = by_head(r_p[:, ps])
        y0_s[hs, :] = _nt(wkv_ref[0, p].astype(BF16), r_bd)
        gram = _nt(jnp.concatenate([by_head(b_p[:, ps]), by_head(k_p[:, ps])], axis=0), r_bd)
        g_s[p] = jnp.where(g_keep, gram, 0.0).astype(BF16)
        vt = v[:, ps].T
        vjt = jnp.concatenate([vt[0:HEAD], vt[HEAD:]], axis=1)
        hi = vjt.astype(BF16).astype(F32)
        vt_s[hs, :] = jnp.concatenate([hi, vjt - hi], axis=1).astype(BF16)

    ob_b = ob.astype(BF16)
    ob2 = jnp.concatenate([ob_b, ob_b], axis=0)
    lane_t = lax.broadcasted_iota(jnp.int32, (HEAD, PAIR), 1) % HEAD
    crow = lax.broadcasted_iota(jnp.int32, (2 * PAIR, 2 * PAIR), 0)
    clane = lax.broadcasted_iota(jnp.int32, (2 * PAIR, 2 * PAIR), 1)
    same_head = ((crow % PAIR) // HEAD) == ((clane % PAIR) // HEAD)
    c_dt = (crow % HEAD) - clane // PAIR

    def steps(blk, carry):
        t0 = pl.multiple_of(blk * 8, 8)
        rows = pl.ds(t0, 8)
        w8, a8, b8, k8 = w_s[rows, :], a_s[rows, :], b_s[rows, :], k_s[rows, :]
        s = [wkv_ref[0, p] for p in range(npair)]
        ut = [ut_s[p * HEAD:(p + 1) * HEAD, :] for p in range(npair)]
        vcol2 = None
        for i in range(8):
            row = lambda x8, p: x8[i:i + 1, p * PAIR:(p + 1) * PAIR]
            if i % 2 == 0:
                sel = (same_head & (c_dt == t0 + i)).astype(BF16)
                vcol2 = jnp.dot(vt_s[...], sel, preferred_element_type=F32)
            vcol = vcol2[:, (i % 2) * PAIR:(i % 2 + 1) * PAIR]
            for p0 in range(0, npair, SA_GROUP):
                group = range(p0, min(p0 + SA_GROUP, npair))
                lhs = []
                for p in group:
                    hi, lo = _split_hi_lo(s[p] * row(a8, p))
                    lhs.append(jnp.concatenate([hi, lo], axis=1))
                sa = jnp.dot(jnp.concatenate(lhs, axis=0), ob2, preferred_element_type=F32)
                for q, p in enumerate(group):
                    sa_p = sa[q * HEAD:(q + 1) * HEAD]
                    s[p] = s[p] * row(w8, p) + sa_p * row(b8, p) + vcol[p * HEAD:(p + 1) * HEAD] * row(k8, p)
                    ut[p] = jnp.where(lane_t == t0 + i, sa_p, ut[p])
        for p in range(npair):
            wkv_ref[0, p] = s[p]
            ut_s[p * HEAD:(p + 1) * HEAD, :] = ut[p]
        return carry

    ut_s[...] = jnp.zeros_like(ut_s)
    lax.fori_loop(0, L // 8, steps, 0)

    for p in range(npair):
        hs = slice(p * HEAD, (p + 1) * HEAD)
        lhs = jnp.concatenate([ut_s[hs, :].astype(BF16), vt_s[hs, 0:PAIR]], axis=1)
        yt = y0_s[hs, :] + jnp.dot(lhs, g_s[p], preferred_element_type=F32)
        ytt = yt.T
        y_s[:, p * PAIR:(p + 1) * PAIR] = jnp.concatenate([ytt[0:L], ytt[L:]], axis=1)

    y = y_s[...]
    mu = _seg_sum(y, ob) * (1.0 / HEAD)
    dlt = y - mu
    var = _seg_sum(dlt * dlt, ob) * (1.0 / HEAD)
    yn = dlt * lax.rsqrt(var + R_LN_EPS) * lng_ref[...] + lnb_ref[...]
    bonus = _seg_sum(r * k_s[...] * rk_ref[...], ob) * v
    g = jnp.dot(jax.nn.sigmoid(gd), gup_ref[...], preferred_element_type=F32)
    yr_ref[...] = ((yn + bonus) * g).astype(BF16)


def _rwkv(proj, shift8, wkv_pairs, mu, w0, wup, a0, aup, gup, kk, ka, rk, lng, lnb, cfg):
    n = proj.shape[0]
    rw, lp = cfg.rw, cfg.tail
    assert cfg.o_rw % (3 * rw) == 0 and (cfg.o_rw + 3 * rw) % lp == 0
    npair = rw // PAIR
    seq = lambda c: _seq_of_chunk(c, cfg)
    full = lambda a: pl.BlockSpec(a.shape, lambda c: (0,) * a.ndim)
    row_scr = pltpu.VMEM((CHUNK, rw), F32)
    return pl.pallas_call(
        functools.partial(_rwkv_kernel, cfg=cfg),
        grid=(cfg.n_chunks,),
        in_specs=[pl.BlockSpec((CHUNK, 3 * rw), lambda c: (c, cfg.o_rw // (3 * rw))),
                  pl.BlockSpec((CHUNK, lp), lambda c: (c, (cfg.o_rw + 3 * rw) // lp)),
                  pl.BlockSpec((1, 8, 3 * rw + lp), lambda c: (seq(c), 0, 0)),
                  pl.BlockSpec((1, npair, HEAD, PAIR), lambda c: (seq(c), 0, 0, 0)),
                  full(mu), full(w0), full(wup), full(a0), full(aup), full(gup),
                  full(kk), full(ka), full(rk), full(lng), full(lnb)],
        out_specs=[pl.BlockSpec((CHUNK, rw), lambda c: (c, 0)),
                   pl.BlockSpec((1, npair, HEAD, PAIR), lambda c: (seq(c), 0, 0, 0))],
        out_shape=[jax.ShapeDtypeStruct((n, rw), BF16),
                   jax.ShapeDtypeStruct((cfg.n_seq, npair, HEAD, PAIR), F32)],
        scratch_shapes=[pltpu.VMEM((CHUNK + 8, 3 * rw), F32), pltpu.VMEM((CHUNK + 8, lp), F32),
                        row_scr, row_scr, row_scr, row_scr, row_scr,
                        pltpu.VMEM((npair * HEAD, 2 * PAIR), BF16), pltpu.VMEM((npair * HEAD, PAIR), F32),
                        pltpu.VMEM((npair * HEAD, PAIR), F32), pltpu.VMEM((npair, 2 * PAIR, PAIR), BF16)],
        compiler_params=_params("arbitrary"),
        name="rwkv",
    )(proj, proj, shift8, wkv_pairs, mu, w0, wup, a0, aup, gup, kk, ka, rk, lng, lnb)


def _outproj_kernel(ym_ref, yr_ref, wa_ref, wb_ref, xp_ref, xs_ref, gate_ref, o_ref, *, cfg, tm):
    i = pl.program_id(0)
    acc = (jnp.dot(ym_ref[...], wa_ref[...], preferred_element_type=F32)
           + jnp.dot(yr_ref[...], wb_ref[...], preferred_element_type=F32))

    def residual(x_ref):
        for s in range(tm // CHUNK):
            seq = _seq_of_chunk(i * (tm // CHUNK) + s, cfg)
            rows = slice(s * CHUNK, (s + 1) * CHUNK)
            o_ref[rows, :] = x_ref[rows, :] + gate_ref[pl.ds(seq, 1), :] * acc[rows, :]

    npt = _prompt_tiles(cfg, tm)
    pl.when(i < npt)(lambda: residual(xp_ref))
    pl.when(i >= npt)(lambda: residual(xs_ref))


def _outproj(ym, yr, wa, wb, x_p, x_s, gate, cfg, tm=512, tn=1024):
    n, d = cfg.n_rows, x_p.shape[1]
    ka, kb = ym.shape[1], yr.shape[1]
    s = gate.shape[0]
    return pl.pallas_call(
        functools.partial(_outproj_kernel, cfg=cfg, tm=tm),
        grid=(n // tm, d // tn),
        in_specs=[pl.BlockSpec((tm, ka), lambda i, j: (i, 0)),
                  pl.BlockSpec((tm, kb), lambda i, j: (i, 0)),
                  pl.BlockSpec((ka, tn), lambda i, j: (0, j)),
                  pl.BlockSpec((kb, tn), lambda i, j: (0, j))]
                 + _row_specs(cfg, tm, tn, lambda j: j)
                 + [pl.BlockSpec((s, tn), lambda i, j: (0, j))],
        out_specs=pl.BlockSpec((tm, tn), lambda i, j: (i, j)),
        out_shape=jax.ShapeDtypeStruct((n, d), F32),
        compiler_params=_params("parallel", "parallel"),
        name="outproj",
    )(ym, yr, wa, wb, x_p, x_s, gate)


def _route(lg, n_experts):
    lane = lax.broadcasted_iota(jnp.int32, lg.shape, 1)
    first = lambda mask: jnp.min(jnp.where(mask, lane, LANES), axis=-1, keepdims=True)
    is_grp = lane < E_GROUPS
    gmax = jnp.max(jnp.where(is_grp, lg, -jnp.inf), axis=-1, keepdims=True)
    gsel = first(is_grp & (lg == gmax))
    pg = 1.0 / jnp.sum(jnp.where(is_grp, jnp.exp(lg - gmax), 0.0), axis=-1, keepdims=True)
    e_lane = lane - E_GROUPS
    in_grp = (e_lane >= 0) & (e_lane < n_experts) & (e_lane // E_PER_GROUP == gsel)
    emax = jnp.max(jnp.where(in_grp, lg, -jnp.inf), axis=-1, keepdims=True)
    p = jnp.where(in_grp, jnp.exp(lg - emax), 0.0)
    eprob = p / jnp.sum(p, axis=-1, keepdims=True)
    v1 = jnp.max(jnp.where(in_grp, eprob, -1.0), axis=-1, keepdims=True)
    i1 = first(in_grp & (eprob == v1))
    rest = in_grp & (lane != i1)
    v2 = jnp.max(jnp.where(rest, eprob, -1.0), axis=-1, keepdims=True)
    i2 = first(rest & (eprob == v2))
    tot = v1 + v2
    cols = [(i1 - E_GROUPS).astype(F32), (i2 - E_GROUPS).astype(F32), v1 / tot * pg, v2 / tot * pg]
    out = jnp.zeros(lg.shape, F32)
    for c, val in enumerate(cols):
        out = jnp.where(lane == c, val, out)
    return out


def _router_kernel(x_ref, shift_ref, scale_ref, g_ref, wh_ref, wl_ref, br_ref, h_ref, rt_ref, *, cfg, tm, n_experts):
    i = pl.program_id(0)
    for s in range(tm // CHUNK):
        seq = _seq_of_chunk(i * (tm // CHUNK) + s, cfg)
        rows = slice(s * CHUNK, (s + 1) * CHUNK)
        h = _modulated_norm(x_ref[rows, :], g_ref[...], scale_ref[pl.ds(seq, 1), :], shift_ref[pl.ds(seq, 1), :])
        h_hi, h_lo = _split_hi_lo(h)
        h_ref[rows, :] = h_hi
        dot = lambda a, b: jnp.dot(a, b[...], preferred_element_type=F32)
        logits = dot(h_hi, wh_ref) + (dot(h_hi, wl_ref) + dot(h_lo, wh_ref)) + br_ref[...]
        rt_ref[rows, :] = _route(logits, n_experts)


def _router(x1, shift, scale, g, wr, br, cfg, n_experts, tm=256):
    n, d = x1.shape
    assert E_GROUPS + n_experts <= LANES and n_experts == E_GROUPS * E_PER_GROUP
    full = lambda a: pl.BlockSpec(a.shape, lambda i: (0,) * a.ndim)
    w_hi, w_lo = _split_hi_lo(wr)
    return pl.pallas_call(
        functools.partial(_router_kernel, cfg=cfg, tm=tm, n_experts=n_experts),
        grid=(n // tm,),
        in_specs=[pl.BlockSpec((tm, d), lambda i: (i, 0)), full(shift), full(scale), full(g),
                  full(w_hi), full(w_lo), full(br)],
        out_specs=[pl.BlockSpec((tm, d), lambda i: (i, 0)), pl.BlockSpec((tm, LANES), lambda i: (i, 0))],
        out_shape=[jax.ShapeDtypeStruct((n, d), BF16), jax.ShapeDtypeStruct((n, LANES), F32)],
        compiler_params=_params("parallel"),
        name="router",
    )(x1, shift, scale, g, w_hi, w_lo, br)


def _new_weights(be_ref):
    i = pl.program_id(1)
    return jnp.logical_or(i == 0, be_ref[i] != be_ref[jnp.maximum(i - 1, 0)])


def _for_valid_rows(nv_ref, out_ref, compute):
    nv = nv_ref[pl.program_id(1)]
    for s in range(out_ref.shape[0] // E_SUB):
        rows = slice(s * E_SUB, (s + 1) * E_SUB)

        @pl.when(nv > s * E_SUB)
        def _():
            out_ref[rows, :] = compute(rows)

        @pl.when(nv <= s * E_SUB)
        def _():
            out_ref[rows, :] = jnp.zeros((E_SUB, out_ref.shape[1]), out_ref.dtype)


def _expert_up_kernel(be_ref, nv_ref, x_ref, wg_ref, wu_ref, h_ref, wg_s, wu_s):
    @pl.when(_new_weights(be_ref))
    def _():
        wg_s[...] = wg_ref[0].astype(BF16)
        wu_s[...] = wu_ref[0].astype(BF16)

    def hidden(rows):
        x = x_ref[rows, :]
        gate = jnp.dot(x, wg_s[...], preferred_element_type=F32)
        up = jnp.dot(x, wu_s[...], preferred_element_type=F32)
        return (_silu(gate) * up).astype(BF16)

    _for_valid_rows(nv_ref, h_ref, hidden)


def _expert_down_kernel(be_ref, nv_ref, h_ref, wd_ref, y_ref, wd_s):
    @pl.when(_new_weights(be_ref))
    def _():
        wd_s[...] = wd_ref[0].astype(BF16)

    _for_valid_rows(nv_ref, y_ref, lambda rows: jnp.dot(h_ref[rows, :], wd_s[...], preferred_element_type=F32))


def _experts(xb, blk_e, n_valid, wg, wu, wd, tm, tf=512, tn=2048):
    rows, d = xb.shape
    f = wg.shape[2]
    nblk = rows // tm
    tf, tn = min(tf, f), min(tn, d)
    assert tm % E_SUB == 0 and f % tf == 0 and d % tn == 0
    hid = pl.pallas_call(
        _expert_up_kernel,
        grid_spec=pltpu.PrefetchScalarGridSpec(
            num_scalar_prefetch=2, grid=(f // tf, nblk),
            in_specs=[pl.BlockSpec((tm, d), lambda j, i, be, nv: (i, 0)),
                      pl.BlockSpec((1, d, tf), lambda j, i, be, nv: (be[i], 0, j)),
                      pl.BlockSpec((1, d, tf), lambda j, i, be, nv: (be[i], 0, j))],
            out_specs=pl.BlockSpec((tm, tf), lambda j, i, be, nv: (i, j)),
            scratch_shapes=[pltpu.VMEM((d, tf), BF16), pltpu.VMEM((d, tf), BF16)]),
        out_shape=jax.ShapeDtypeStruct((rows, f), BF16),
        compiler_params=_params("arbitrary", "arbitrary"),
        name="expert_up",
    )(blk_e, n_valid, xb, wg, wu)
    return pl.pallas_call(
        _expert_down_kernel,
        grid_spec=pltpu.PrefetchScalarGridSpec(
            num_scalar_prefetch=2, grid=(d // tn, nblk),
            in_specs=[pl.BlockSpec((tm, f), lambda j, i, be, nv: (i, 0)),
                      pl.BlockSpec((1, f, tn), lambda j, i, be, nv: (be[i], 0, j))],
            out_specs=pl.BlockSpec((tm, tn), lambda j, i, be, nv: (i, j)),
            scratch_shapes=[pltpu.VMEM((f, tn), BF16)]),
        out_shape=jax.ShapeDtypeStruct((rows, d), F32),
        compiler_params=_params("arbitrary", "arbitrary"),
        name="expert_down",
    )(blk_e, n_valid, hid, wd)


def _final_kernel(x_ref, moe_ref, gate_ref, g_ref, o_ref, *, cfg, tm, tile0):
    i = pl.program_id(0) + tile0
    for s in range(tm // CHUNK):
        seq = _seq_of_chunk(i * (tm // CHUNK) + s, cfg)
        rows = slice(s * CHUNK, (s + 1) * CHUNK)
        x = x_ref[rows, :] + gate_ref[pl.ds(seq, 1), :] * moe_ref[rows, :]
        o_ref[rows, :] = x * lax.rsqrt(jnp.mean(x * x, axis=-1, keepdims=True) + NORM_EPS) * g_ref[...]


def _final(x1, moe, gate, g, cfg, row0, nrows, tm=256):
    d = x1.shape[1]
    assert row0 % tm == 0 and nrows % tm == 0
    tile0 = row0 // tm
    full = lambda a: pl.BlockSpec(a.shape, lambda i: (0,) * a.ndim)
    return pl.pallas_call(
        functools.partial(_final_kernel, cfg=cfg, tm=tm, tile0=tile0),
        grid=(nrows // tm,),
        in_specs=[pl.BlockSpec((tm, d), lambda i: (i + tile0, 0)), pl.BlockSpec((tm, d), lambda i: (i + tile0, 0)),
                  full(gate), full(g)],
        out_specs=pl.BlockSpec((tm, d), lambda i: (i, 0)),
        out_shape=jax.ShapeDtypeStruct((nrows, d), F32),
        compiler_params=_params("parallel"),
        name="final_norm",
    )(x1, moe, gate, g)


def _dispatch(route, n_experts, tm):
    n = route.shape[0]
    eid = route[:, :TOP_K].astype(jnp.int32)
    wts = route[:, TOP_K:2 * TOP_K]

    m = n * TOP_K
    e_flat = eid.reshape(m)
    order = jnp.argsort(e_flat)
    e_sorted = e_flat[order]
    counts = jnp.bincount(e_flat, length=n_experts).astype(jnp.int32)
    padded = (counts + tm - 1) // tm * tm
    start = jnp.cumsum(counts) - counts
    pend = jnp.cumsum(padded)
    pstart = pend - padded
    dest_sorted = pstart[e_sorted] + jnp.arange(m, dtype=jnp.int32) - start[e_sorted]
    n_blocks = -(-(m + n_experts * (tm - 1)) // tm)
    rows = n_blocks * tm
    tok_sorted = (order // TOP_K).astype(jnp.int32)
    row_tok = (jnp.arange(rows, dtype=jnp.int32) % n).at[dest_sorted].set(tok_sorted)
    slot_pos = jnp.zeros((m,), jnp.int32).at[order].set(dest_sorted).reshape(n, TOP_K)
    n_used = pend[-1] // tm
    blk = jnp.arange(n_blocks, dtype=jnp.int32)
    blk_e = jnp.minimum(jnp.searchsorted(pend, blk * tm, side='right'), n_experts - 1).astype(jnp.int32)
    n_valid = jnp.where(blk < n_used, jnp.clip(pstart[blk_e] + counts[blk_e] - blk * tm, 0, tm), 0).astype(jnp.int32)
    last_e = blk_e[jnp.maximum(n_used - 1, 0)]
    blk_e = jnp.where(blk < n_used, blk_e, last_e)
    return row_tok, slot_pos, wts, blk_e, n_valid


def _to_window(a, cfg):
    return jnp.concatenate([a, jnp.zeros(a.shape[:-1] + (cfg.win - cfg.r_cols,), a.dtype)], axis=-1)


def _pad_rows(a, rows):
    return jnp.concatenate([a, jnp.zeros((rows - a.shape[0],) + a.shape[1:], a.dtype)], axis=0)


def kernel(x_prompt, x_sample, state_conv, state_ssm, state_shift, state_wkv, c_prompt, c_sample, norm1_g, w_mod, b_mod, w_in, conv_w, conv_b, dt_bias, a_log, d_skip, m_norm_g, shift_mu, w0, w_up, a0, a_up, g_up, k_k, k_a, r_k, ln_x_g, ln_x_b, w_out, norm2_g, w_grp, b_grp, w_erouter, b_erouter, e_gate, e_up, e_down, final_norm_g):
    assert w_mod.shape[0] == 1, "single-layer trunk"
    bp, tp, d = x_prompt.shape
    bs, ts, _ = x_sample.shape
    mi = m_norm_g.shape[-1]
    rw = w0.shape[-1]
    lw, la, lg = w_up.shape[1], a_up.shape[1], g_up.shape[1]
    cfg = Cfg(d, bp, tp, bs, ts, mi, rw, lw, la, lg)
    nc = cfg.nc
    assert tp % CHUNK == 0 and ts % CHUNK == 0 and tp >= CONV_W - 1 and ts >= CONV_W - 1
    assert w_in.shape[-1] == cfg.o_rw + cfg.mh + cfg.r_cols and cfg.o_rw % LANES == 0
    assert state_conv.shape[-1] == cfg.cd and lw <= LANES and la <= LANES and cfg.mh <= LANES
    n = cfg.n_rows
    n_seq = cfg.n_seq
    mh, rh = cfg.mh, cfg.rh
    npair = rw // PAIR
    n_experts = e_gate.shape[1]

    x_p = x_prompt.reshape(bp * tp, d)
    x_s = x_sample.reshape(bs * ts, d)
    c_all = jnp.concatenate([c_prompt, c_sample], axis=0)

    mod = _modulation(c_all, w_mod[0], b_mod[0])
    shift1, scale1, gate1, shift2, scale2, gate2 = [mod[:, i * d:(i + 1) * d] for i in range(6)]

    h1 = _prenorm(x_p, x_s, shift1, scale1, norm1_g, cfg)
    proj = _inproj(h1, w_in[0], cfg)
    dt_raw = _dtproj(h1, w_in[0], cfg)

    zeros = lambda b, *s: jnp.zeros((b,) + s, F32)
    conv0 = jnp.concatenate([zeros(bp, CONV_W - 1, cfg.cd), state_conv[0]], axis=0)
    conv8 = jnp.concatenate([zeros(n_seq, 8 - (CONV_W - 1), cfg.cd), conv0], axis=1)
    nst = state_ssm.shape[-1]
    ssm0 = jnp.concatenate([zeros(bp, mi, nst), state_ssm[0].reshape(bs, mi, nst)], axis=0)
    lane_pad = lambda a: jnp.concatenate([a.reshape(1, -1), jnp.zeros((1, LANES - a.shape[-1]), F32)], axis=1)
    expand = ((jnp.arange(2 * LANES)[:, None] % LANES) == (jnp.arange(mi)[None, :] // HEAD)).astype(BF16)
    ym, ssm_new = _mamba(proj, dt_raw, conv8, ssm0, conv_w[0], conv_b[0].reshape(1, -1), lane_pad(dt_bias[0]),
                         lane_pad(a_log[0]), jnp.repeat(d_skip[0], HEAD).reshape(1, mi), m_norm_g[0].reshape(1, mi),
                         expand, cfg)

    sh0 = jnp.concatenate([zeros(bp, 1, state_shift.shape[-1]), state_shift[0]], axis=0)
    sh8 = jnp.concatenate([zeros(n_seq, 7, cfg.win), _to_window(sh0, cfg)], axis=1)
    to_pairs = lambda s: s.reshape(-1, npair, 2, HEAD, HEAD).transpose(0, 1, 3, 2, 4).reshape(-1, npair, HEAD, PAIR)
    from_pairs = lambda s: s.reshape(-1, npair, HEAD, 2, HEAD).transpose(0, 1, 3, 2, 4).reshape(-1, rh, HEAD, HEAD)
    wkv0 = jnp.concatenate([zeros(bp, npair, HEAD, PAIR), to_pairs(state_wkv[0])], axis=0)
    row = lambda a: a.reshape(1, -1)
    yr, wkv_new = _rwkv(proj, sh8, wkv0, _to_window(row(shift_mu[0]), cfg), row(w0[0]),
                        _pad_rows(w_up[0], LANES), row(a0[0]), _pad_rows(a_up[0], LANES), g_up[0],
                        row(k_k[0]), row(k_a[0]), row(r_k[0]), row(ln_x_g[0]), row(ln_x_b[0]), cfg)

    wo = w_out[0].astype(BF16)
    x1 = _outproj(ym, yr, wo[:mi], wo[mi:], x_p, x_s, gate1, cfg)
    wr = jnp.concatenate([w_grp[0], w_erouter[0], jnp.zeros((d, LANES - E_GROUPS - n_experts), F32)], axis=1)
    br = lane_pad(jnp.concatenate([b_grp[0], b_erouter[0]]))
    h2, route = _router(x1, shift2, scale2, norm2_g, wr, br, cfg, n_experts)

    tm_e = 2 * E_SUB
    row_tok, slot_pos, wts, blk_e, n_valid = _dispatch(route, n_experts, tm_e)
    yb = _experts(h2[row_tok], blk_e, n_valid, e_gate[0], e_up[0], e_down[0], tm_e)
    moe = yb[slot_pos[:, 0]] * wts[:, 0:1] + yb[slot_pos[:, 1]] * wts[:, 1:2]

    np_ = bp * tp
    fg = final_norm_g.reshape(1, d)
    y_prompt = _final(x1, moe, gate2, fg, cfg, 0, np_).reshape(bp, tp, d)
    y_sample = _final(x1, moe, gate2, fg, cfg, np_, bs * ts).reshape(bs, ts, d)
    tails = lambda b, t, base: jnp.stack(
        [lax.slice(proj, (base + (i + 1) * t - (CONV_W - 1), 0), (base + (i + 1) * t, nc)) for i in range(b)])
    pp = tails(bp, tp, 0)
    ps = tails(bs, ts, np_)
    conv_of = lambda p: p[:, :, mi:mi + cfg.cd][None]
    shift_of = lambda p: p[:, -1:, cfg.o_rw:cfg.o_rw + cfg.r_cols][None]
    ssm_new = ssm_new.reshape(n_seq, mh, HEAD, nst)
    wkv_new = from_pairs(wkv_new)
    return (y_prompt, y_sample,
            conv_of(pp), ssm_new[:bp][None], shift_of(pp), wkv_new[:bp][None],
            conv_of(ps), ssm_new[bp:][None], shift_of(ps), wkv_new[bp:][None])
```

```python
import functools
from typing import NamedTuple

import jax
import jax.numpy as jnp
from jax import lax
from jax.experimental import pallas as pl
from jax.experimental.pallas import tpu as pltpu

F32 = jnp.float32
BF16 = jnp.bfloat16
HI = lax.Precision.HIGHEST

LANES = 128
CHUNK = 64
HEAD = 64
PAIR = 2 * HEAD
NORM_EPS = 1e-6
M_NORM_EPS = 1e-5
R_LN_EPS = 64e-5
CONV_W = 4
E_GROUPS = 4
E_PER_GROUP = 8
TOP_K = 2
VMEM_LIMIT = 56 * 1024 * 1024
E_SUB = 256
EXP_NEG_HALF = 0.6065306597126334
SA_GROUP = 8


class Cfg(NamedTuple):
    d: int
    bp: int
    tp: int
    bs: int
    ts: int
    mi: int
    rw: int
    lw: int
    la: int
    lg: int

    @property
    def cpp(self):
        return self.tp // CHUNK

    @property
    def cps(self):
        return self.ts // CHUNK

    @property
    def n_chunks(self):
        return self.bp * self.cpp + self.bs * self.cps

    @property
    def n_seq(self):
        return self.bp + self.bs

    @property
    def n_rows(self):
        return self.n_chunks * CHUNK

    @property
    def groups(self):
        return self.mi // 256

    @property
    def nb(self):
        return self.groups * LANES

    @property
    def cd(self):
        return self.mi + 2 * self.nb

    @property
    def mh(self):
        return self.mi // HEAD

    @property
    def rh(self):
        return self.rw // HEAD

    @property
    def o_rw(self):
        return self.mi + self.cd

    @property
    def r_cols(self):
        return 3 * self.rw + self.lw + self.la + self.lg

    @property
    def tail(self):
        return -(-(self.lw + self.la + self.lg) // LANES) * LANES

    @property
    def win(self):
        return 3 * self.rw + self.tail

    @property
    def nc(self):
        return self.o_rw + self.win


def _seq_of_chunk(c, cfg):
    npc = cfg.bp * cfg.cpp
    return jnp.where(c < npc, c // cfg.cpp, cfg.bp + (c - npc) // cfg.cps)


def _is_first_chunk(c, cfg):
    npc = cfg.bp * cfg.cpp
    return jnp.where(c < npc, c % cfg.cpp == 0, (c - npc) % cfg.cps == 0)


def _silu(x):
    return x * jax.nn.sigmoid(x)


def _nt(a, b, **kw):
    return lax.dot_general(a, b, (((1,), (1,)), ((), ())), preferred_element_type=F32, **kw)


def _tn(a, b, **kw):
    return lax.dot_general(a, b, (((0,), (0,)), ((), ())), preferred_element_type=F32, **kw)


def _params(*sem):
    return pltpu.CompilerParams(dimension_semantics=sem, vmem_limit_bytes=VMEM_LIMIT)


def _mod_kernel(c_ref, w_ref, b_ref, o_ref):
    o_ref[...] = jnp.dot(_silu(c_ref[...]), w_ref[...], preferred_element_type=F32) + b_ref[...]


def _modulation(c_all, w_mod, b_mod):
    s, d = c_all.shape
    cols = w_mod.shape[1]
    tn = 512
    return pl.pallas_call(
        _mod_kernel,
        grid=(cols // tn,),
        in_specs=[pl.BlockSpec((s, d), lambda j: (0, 0)),
                  pl.BlockSpec((d, tn), lambda j: (0, j)),
                  pl.BlockSpec((1, tn), lambda j: (0, j))],
        out_specs=pl.BlockSpec((s, tn), lambda j: (0, j)),
        out_shape=jax.ShapeDtypeStruct((s, cols), F32),
        compiler_params=_params("parallel"),
        name="modulation",
    )(c_all, w_mod, b_mod.reshape(1, cols))


def _modulated_norm(x, g, scale, shift):
    y = x * lax.rsqrt(jnp.mean(x * x, axis=-1, keepdims=True) + NORM_EPS) * g
    return y * (1.0 + scale) + shift


def _prompt_tiles(cfg, tm):
    assert (cfg.bp * cfg.tp) % tm == 0 and (cfg.bs * cfg.ts) % tm == 0 and tm % CHUNK == 0
    return cfg.bp * cfg.tp // tm


def _row_specs(cfg, tm, width, col=None):
    npt = _prompt_tiles(cfg, tm)
    c = (lambda *r: 0) if col is None else col
    return [pl.BlockSpec((tm, width), lambda i, *r: (jnp.minimum(i, npt - 1), jnp.where(i < npt, c(*r), 0))),
            pl.BlockSpec((tm, width), lambda i, *r: (jnp.maximum(i - npt, 0), jnp.where(i >= npt, c(*r), 0)))]


def _prenorm_kernel(xp_ref, xs_ref, shift_ref, scale_ref, g_ref, h_ref, *, cfg, tm):
    i = pl.program_id(0)

    def norm_rows(x_ref):
        for s in range(tm // CHUNK):
            seq = _seq_of_chunk(i * (tm // CHUNK) + s, cfg)
            rows = slice(s * CHUNK, (s + 1) * CHUNK)
            h = _modulated_norm(x_ref[rows, :], g_ref[...], scale_ref[pl.ds(seq, 1), :], shift_ref[pl.ds(seq, 1), :])
            h_ref[rows, :] = h.astype(BF16)

    npt = _prompt_tiles(cfg, tm)
    pl.when(i < npt)(lambda: norm_rows(xp_ref))
    pl.when(i >= npt)(lambda: norm_rows(xs_ref))


def _prenorm(x_p, x_s, shift, scale, g, cfg, tm=256):
    d = x_p.shape[1]
    full = lambda a: pl.BlockSpec(a.shape, lambda i: (0,) * a.ndim)
    return pl.pallas_call(
        functools.partial(_prenorm_kernel, cfg=cfg, tm=tm),
        grid=(cfg.n_rows // tm,),
        in_specs=_row_specs(cfg, tm, d) + [full(shift), full(scale), full(g)],
        out_specs=pl.BlockSpec((tm, d), lambda i: (i, 0)),
        out_shape=jax.ShapeDtypeStruct((cfg.n_rows, d), BF16),
        compiler_params=_params("parallel"),
        name="prenorm",
    )(x_p, x_s, shift, scale, g)


def _inproj_kernel(h_ref, w_ref, wn_ref, o_ref, wb_s, *, first_shifted, shift, tn):
    j = pl.program_id(0)
    new_tile = pl.program_id(1) == 0

    @pl.when(jnp.logical_and(new_tile, j < first_shifted))
    def _():
        wb_s[...] = w_ref[...].astype(BF16)

    @pl.when(jnp.logical_and(new_tile, j >= first_shifted))
    def _():
        w = jnp.concatenate([w_ref[...], wn_ref[...]], axis=0)
        wb_s[...] = w[shift:shift + tn, :].astype(BF16)

    o_ref[...] = _nt(h_ref[...], wb_s[...])


def _inproj(h, wt, cfg, tm=1024, tn=512):
    n, d = h.shape
    nout = cfg.o_rw + cfg.win
    assert cfg.o_rw % tn == 0 and nout % tn == 0 and n % tm == 0 and cfg.mh <= LANES and cfg.mh % 8 == 0
    last_next = (wt.shape[0] - 1) // LANES
    return pl.pallas_call(
        functools.partial(_inproj_kernel, first_shifted=cfg.o_rw // tn, shift=cfg.mh, tn=tn),
        grid=(nout // tn, n // tm),
        in_specs=[pl.BlockSpec((tm, d), lambda j, i: (i, 0)),
                  pl.BlockSpec((tn, d), lambda j, i: (j, 0)),
                  pl.BlockSpec((LANES, d), lambda j, i: (jnp.minimum((j + 1) * (tn // LANES), last_next), 0))],
        out_specs=pl.BlockSpec((tm, tn), lambda j, i: (i, j)),
        out_shape=jax.ShapeDtypeStruct((n, nout), F32),
        scratch_shapes=[pltpu.VMEM((tn, d), BF16)],
        compiler_params=_params("arbitrary", "arbitrary"),
        name="inproj",
    )(h, wt, wt)


def _dtproj_kernel(h_ref, w_ref, o_ref):
    o_ref[...] = _nt(h_ref[...], w_ref[...].astype(BF16))


def _dtproj(h, wt, cfg, tm=1024):
    n, d = h.shape
    return pl.pallas_call(
        _dtproj_kernel,
        grid=(n // tm,),
        in_specs=[pl.BlockSpec((tm, d), lambda i: (i, 0)), pl.BlockSpec((LANES, d), lambda i: (cfg.o_rw // LANES, 0))],
        out_specs=pl.BlockSpec((tm, LANES), lambda i: (i, 0)),
        out_shape=jax.ShapeDtypeStruct((n, LANES), F32),
        compiler_params=_params("parallel"),
        name="dtproj",
    )(h, wt)


def _mamba_kernel(zx_ref, dt_ref, cst_ref, sst_ref, cw_ref, cb_ref, dtb_ref, alog_ref, dsk_ref, ng_ref, exp_ref,
                  ym_ref, ssm_ref, buf, u_scr, *, cfg):
    c = pl.program_id(0)
    mi, nb, cd = cfg.mi, cfg.nb, cfg.cd
    L = CHUNK

    @pl.when(_is_first_chunk(c, cfg))
    def _():
        buf[0:8, :] = cst_ref[0]
        ssm_ref[0] = sst_ref[0]

    xbc = zx_ref[:, mi:mi + cd]
    buf[8:8 + L, :] = xbc
    conv = (cb_ref[...] + buf[5:5 + L, :] * cw_ref[0:1, :] + buf[6:6 + L, :] * cw_ref[1:2, :]
            + buf[7:7 + L, :] * cw_ref[2:3, :] + xbc * cw_ref[3:4, :])
    buf[0:8, :] = buf[L:L + 8, :]
    u_scr[...] = _silu(conv)

    dt = jax.nn.softplus(dt_ref[...] + dtb_ref[...])
    da = dt * (-jnp.exp(alog_ref[...]))
    row = lax.broadcasted_iota(jnp.int32, (L, 256), 0)
    pos = lax.broadcasted_iota(jnp.int32, (L, 256), 1) % HEAD
    causal = row >= pos
    diag = row == pos
    tri2 = (lax.broadcasted_iota(jnp.int32, (L, 2 * L), 0)
            >= lax.broadcasted_iota(jnp.int32, (L, 2 * L), 1) % L).astype(BF16)
    lane = lax.broadcasted_iota(jnp.int32, (L, PAIR), 1)
    row16 = lax.broadcasted_iota(jnp.int32, (16, 256), 0)
    ones16 = jnp.ones((16, LANES), BF16)
    dt_hl = jnp.concatenate(_split_hi_lo(dt), axis=1)
    da_hl = jnp.concatenate(_split_hi_lo(da), axis=1)

    for g in range(cfg.groups):
        cs = slice(g * 256, (g + 1) * 256)
        xh = u_scr[:, cs]
        bg = u_scr[:, mi + g * LANES:mi + (g + 1) * LANES]
        cg = u_scr[:, mi + nb + g * LANES:mi + nb + (g + 1) * LANES]
        eg = exp_ref[:, cs]
        dte = jnp.dot(dt_hl, eg, preferred_element_type=F32)
        dae = jnp.dot(da_hl, eg, preferred_element_type=F32)
        cum = jnp.dot(tri2, jnp.concatenate(_split_hi_lo(dae), axis=0), preferred_element_type=F32)
        cum_row = jnp.sum(jnp.where(diag, cum, 0.0), axis=0, keepdims=True)
        last = cum[L - 1:L, :]
        decay = jnp.exp(jnp.where(causal, cum - cum_row, -jnp.inf))
        xdt = xh * dte
        sc2 = _nt(cg, jnp.concatenate([bg, bg], axis=0))
        ys = []
        for q in range(2):
            ps = slice(q * PAIR, (q + 1) * PAIR)
            xq = xdt[:, ps]
            rhs = jnp.concatenate([jnp.where(lane < HEAD, xq, 0.0), jnp.where(lane >= HEAD, xq, 0.0)], axis=0)
            ys.append(jnp.dot(decay[:, ps] * sc2, rhs, preferred_element_type=F32))
        y = jnp.concatenate(ys, axis=1)
        sg = ssm_ref[0, cs, :]
        y = y + _nt(cg, sg) * jnp.exp(cum)
        new = _tn(xdt * jnp.exp(last - cum), bg)
        e_last = jnp.exp(last)
        e_hi = e_last.astype(BF16).astype(F32)
        e_rows = jnp.where(row16 == 0, e_hi, jnp.where(row16 == 1, e_last - e_hi, 0.0)).astype(BF16)
        dcol = _tn(e_rows, ones16)
        ssm_ref[0, cs, :] = sg * dcol + new
        y = y + xh * dsk_ref[:, cs]
        y = y * _silu(zx_ref[:, cs])
        y = y * lax.rsqrt(jnp.mean(y * y, axis=-1, keepdims=True) + M_NORM_EPS)
        ym_ref[:, cs] = (y * ng_ref[:, cs]).astype(BF16)


def _mamba(proj, dt_raw, conv_state8, ssm_state, cw, cb, dtb, alog, dsk, ng, expand, cfg):
    n = proj.shape[0]
    mi, cd = cfg.mi, cfg.cd
    nst = ssm_state.shape[-1]
    seq = lambda c: _seq_of_chunk(c, cfg)
    full = lambda a: pl.BlockSpec(a.shape, lambda c: (0,) * a.ndim)
    return pl.pallas_call(
        functools.partial(_mamba_kernel, cfg=cfg),
        grid=(cfg.n_chunks,),
        in_specs=[pl.BlockSpec((CHUNK, mi + cd), lambda c: (c, 0)),
                  pl.BlockSpec((CHUNK, LANES), lambda c: (c, 0)),
                  pl.BlockSpec((1, 8, cd), lambda c: (seq(c), 0, 0)),
                  pl.BlockSpec((1, mi, nst), lambda c: (seq(c), 0, 0)),
                  full(cw), full(cb), full(dtb), full(alog), full(dsk), full(ng), full(expand)],
        out_specs=[pl.BlockSpec((CHUNK, mi), lambda c: (c, 0)),
                   pl.BlockSpec((1, mi, nst), lambda c: (seq(c), 0, 0))],
        out_shape=[jax.ShapeDtypeStruct((n, mi), BF16),
                   jax.ShapeDtypeStruct((cfg.n_seq, mi, nst), F32)],
        scratch_shapes=[pltpu.VMEM((CHUNK + 8, cd), F32), pltpu.VMEM((CHUNK, cd), F32)],
        compiler_params=_params("arbitrary"),
        name="mamba",
    )(proj, dt_raw, conv_state8, ssm_state, cw, cb, dtb, alog, dsk, ng, expand)


def _seg_sum(x, ob):
    parts = [jnp.dot(x[:, p * PAIR:(p + 1) * PAIR], ob, preferred_element_type=F32) for p in range(x.shape[1] // PAIR)]
    return jnp.concatenate(parts, axis=1)


def _split_hi_lo(x):
    hi = x.astype(BF16)
    lo = (x - hi.astype(F32)).astype(BF16)
    return hi, lo


def _rwkv_kernel(rkv_ref, lora_ref, sh_ref, wkv_in_ref, mu_ref, w0_ref, wup_ref, a0_ref, aup_ref, gup_ref,
                 kk_ref, ka_ref, rk_ref, lng_ref, lnb_ref,
                 yr_ref, wkv_ref, pbuf, lbuf, w_s, a_s, b_s, k_s, y_s, vt_s, ut_s, y0_s, g_s, *, cfg):
    c = pl.program_id(0)
    rw = cfg.rw
    L = CHUNK
    npair = rw // PAIR

    @pl.when(_is_first_chunk(c, cfg))
    def _():
        pbuf[0:8, :] = sh_ref[0, :, 0:3 * rw]
        lbuf[0:8, :] = sh_ref[0, :, 3 * rw:]
        wkv_ref[0] = wkv_in_ref[0]

    pr = rkv_ref[...]
    pl_ = lora_ref[...]
    pbuf[8:8 + L, :] = pr
    lbuf[8:8 + L, :] = pl_
    xs = pr + (pbuf[7:7 + L, :] - pr) * mu_ref[:, 0:3 * rw]
    xl = pl_ + (lbuf[7:7 + L, :] - pl_) * mu_ref[:, 3 * rw:]
    pbuf[0:8, :] = pbuf[L:L + 8, :]
    lbuf[0:8, :] = lbuf[L:L + 8, :]

    r = xs[:, 0:rw]
    k = xs[:, rw:2 * rw]
    v = xs[:, 2 * rw:3 * rw]
    wd = xl[:, 0:LANES]
    ad = xl[:, cfg.lw:cfg.lw + LANES]
    gd = xl[:, cfg.lw + cfg.la:cfg.lw + cfg.la + cfg.lg]

    lane = lax.broadcasted_iota(jnp.int32, (PAIR, PAIR), 1)
    rowi = lax.broadcasted_iota(jnp.int32, (PAIR, PAIR), 0)
    ob = ((lane // HEAD) == (rowi // HEAD)).astype(F32)

    w_z = w0_ref[...] + jnp.dot(jnp.tanh(wd), wup_ref[...], preferred_element_type=F32)
    logw = -(jax.nn.sigmoid(w_z) * EXP_NEG_HALF)
    w_s[...] = jnp.exp(logw)
    a = jax.nn.sigmoid(a0_ref[...] + jnp.dot(ad, aup_ref[...], preferred_element_type=F32))
    kk = k * kk_ref[...]
    kk = kk * lax.rsqrt(jnp.maximum(_seg_sum(kk * kk, ob), 1e-24))
    kh = k * (1.0 + (a - 1.0) * ka_ref[...])
    a_s[...] = -kk
    b_s[...] = kk * a
    k_s[...] = kh

    tri2 = (lax.broadcasted_iota(jnp.int32, (L, 2 * L), 0)
            >= lax.broadcasted_iota(jnp.int32, (L, 2 * L), 1) % L).astype(BF16)
    cum = jnp.dot(tri2, jnp.concatenate(_split_hi_lo(logw), axis=0), preferred_element_type=F32)
    p_inv = jnp.exp(-cum)
    r_p = r * jnp.exp(cum)
    b_p = b_s[...] * p_inv
    k_p = kh * p_inv
    lo_lanes = lane < HEAD
    g_row = lax.broadcasted_iota(jnp.int32, (2 * PAIR, PAIR), 0)
    g_lane = lax.broadcasted_iota(jnp.int32, (2 * PAIR, PAIR), 1)
    g_keep = (g_row % HEAD) <= (g_lane % HEAD)

    def by_head(x):
        x2 = jnp.concatenate([x, x], axis=0)
        return jnp.where(lo_lanes == (rowi < HEAD), x2, 0.0).astype(BF16)

    for p in range(npair):
        ps = slice(p * PAIR, (p + 1) * PAIR)
        hs = slice(p * HEAD, (p + 1) * HEAD)
        r_bd = by_head(r_p[:, ps])
        y0_s[hs, :] = _nt(wkv_ref[0, p].astype(BF16), r_bd)
        gram = _nt(jnp.concatenate([by_head(b_p[:, ps]), by_head(k_p[:, ps])], axis=0), r_bd)
        g_s[p] = jnp.where(g_keep, gram, 0.0).astype(BF16)
        vt = v[:, ps].T
        vjt = jnp.concatenate([vt[0:HEAD], vt[HEAD:]], axis=1)
        hi = vjt.astype(BF16).astype(F32)
        vt_s[hs, :] = jnp.concatenate([hi, vjt - hi], axis=1).astype(BF16)

    ob_b = ob.astype(BF16)
    ob2 = jnp.concatenate([ob_b, ob_b], axis=0)
    lane_t = lax.broadcasted_iota(jnp.int32, (HEAD, PAIR), 1) % HEAD
    crow = lax.broadcasted_iota(jnp.int32, (2 * PAIR, 2 * PAIR), 0)
    clane = lax.broadcasted_iota(jnp.int32, (2 * PAIR, 2 * PAIR), 1)
    same_head = ((crow % PAIR) // HEAD) == ((clane % PAIR) // HEAD)
    c_dt = (crow % HEAD) - clane // PAIR

    def steps(blk, carry):
        t0 = pl.multiple_of(blk * 8, 8)
        rows = pl.ds(t0, 8)
        w8, a8, b8, k8 = w_s[rows, :], a_s[rows, :], b_s[rows, :], k_s[rows, :]
        s = [wkv_ref[0, p] for p in range(npair)]
        ut = [ut_s[p * HEAD:(p + 1) * HEAD, :] for p in range(npair)]
        vcol2 = None
        for i in range(8):
            row = lambda x8, p: x8[i:i + 1, p * PAIR:(p + 1) * PAIR]
            if i % 2 == 0:
                sel = (same_head & (c_dt == t0 + i)).astype(BF16)
                vcol2 = jnp.dot(vt_s[...], sel, preferred_element_type=F32)
            vcol = vcol2[:, (i % 2) * PAIR:(i % 2 + 1) * PAIR]
            for p0 in range(0, npair, SA_GROUP):
                group = range(p0, min(p0 + SA_GROUP, npair))
                lhs = []
                for p in group:
                    hi, lo = _split_hi_lo(s[p] * row(a8, p))
                    lhs.append(jnp.concatenate([hi, lo], axis=1))
                sa = jnp.dot(jnp.concatenate(lhs, axis=0), ob2, preferred_element_type=F32)
                for q, p in enumerate(group):
                    sa_p = sa[q * HEAD:(q + 1) * HEAD]
                    s[p] = s[p] * row(w8, p) + sa_p * row(b8, p) + vcol[p * HEAD:(p + 1) * HEAD] * row(k8, p)
                    ut[p] = jnp.where(lane_t == t0 + i, sa_p, ut[p])
        for p in range(npair):
            wkv_ref[0, p] = s[p]
            ut_s[p * HEAD:(p + 1) * HEAD, :] = ut[p]
        return carry

    ut_s[...] = jnp.zeros_like(ut_s)
    lax.fori_loop(0, L // 8, steps, 0)

    for p in range(npair):
        hs = slice(p * HEAD, (p + 1) * HEAD)
        lhs = jnp.concatenate([ut_s[hs, :].astype(BF16), vt_s[hs, 0:PAIR]], axis=1)
        yt = y0_s[hs, :] + jnp.dot(lhs, g_s[p], preferred_element_type=F32)
        ytt = yt.T
        y_s[:, p * PAIR:(p + 1) * PAIR] = jnp.concatenate([ytt[0:L], ytt[L:]], axis=1)

    y = y_s[...]
    mu = _seg_sum(y, ob) * (1.0 / HEAD)
    dlt = y - mu
    var = _seg_sum(dlt * dlt, ob) * (1.0 / HEAD)
    yn = dlt * lax.rsqrt(var + R_LN_EPS) * lng_ref[...] + lnb_ref[...]
    bonus = _seg_sum(r * k_s[...] * rk_ref[...], ob) * v
    g = jnp.dot(jax.nn.sigmoid(gd), gup_ref[...], preferred_element_type=F32)
    yr_ref[...] = ((yn + bonus) * g).astype(BF16)


def _rwkv(proj, shift8, wkv_pairs, mu, w0, wup, a0, aup, gup, kk, ka, rk, lng, lnb, cfg):
    n = proj.shape[0]
    rw, lp = cfg.rw, cfg.tail
    assert cfg.o_rw % (3 * rw) == 0 and (cfg.o_rw + 3 * rw) % lp == 0
    npair = rw // PAIR
    seq = lambda c: _seq_of_chunk(c, cfg)
    full = lambda a: pl.BlockSpec(a.shape, lambda c: (0,) * a.ndim)
    row_scr = pltpu.VMEM((CHUNK, rw), F32)
    return pl.pallas_call(
        functools.partial(_rwkv_kernel, cfg=cfg),
        grid=(cfg.n_chunks,),
        in_specs=[pl.BlockSpec((CHUNK, 3 * rw), lambda c: (c, cfg.o_rw // (3 * rw))),
                  pl.BlockSpec((CHUNK, lp), lambda c: (c, (cfg.o_rw + 3 * rw) // lp)),
                  pl.BlockSpec((1, 8, 3 * rw + lp), lambda c: (seq(c), 0, 0)),
                  pl.BlockSpec((1, npair, HEAD, PAIR), lambda c: (seq(c), 0, 0, 0)),
                  full(mu), full(w0), full(wup), full(a0), full(aup), full(gup),
                  full(kk), full(ka), full(rk), full(lng), full(lnb)],
        out_specs=[pl.BlockSpec((CHUNK, rw), lambda c: (c, 0)),
                   pl.BlockSpec((1, npair, HEAD, PAIR), lambda c: (seq(c), 0, 0, 0))],
        out_shape=[jax.ShapeDtypeStruct((n, rw), BF16),
                   jax.ShapeDtypeStruct((cfg.n_seq, npair, HEAD, PAIR), F32)],
        scratch_shapes=[pltpu.VMEM((CHUNK + 8, 3 * rw), F32), pltpu.VMEM((CHUNK + 8, lp), F32),
                        row_scr, row_scr, row_scr, row_scr, row_scr,
                        pltpu.VMEM((npair * HEAD, 2 * PAIR), BF16), pltpu.VMEM((npair * HEAD, PAIR), F32),
                        pltpu.VMEM((npair * HEAD, PAIR), F32), pltpu.VMEM((npair, 2 * PAIR, PAIR), BF16)],
        compiler_params=_params("arbitrary"),
        name="rwkv",
    )(proj, proj, shift8, wkv_pairs, mu, w0, wup, a0, aup, gup, kk, ka, rk, lng, lnb)


def _outproj_kernel(ym_ref, yr_ref, wa_ref, wb_ref, xp_ref, xs_ref, gate_ref, o_ref, *, cfg, tm):
    i = pl.program_id(0)
    acc = (jnp.dot(ym_ref[...], wa_ref[...], preferred_element_type=F32)
           + jnp.dot(yr_ref[...], wb_ref[...], preferred_element_type=F32))

    def residual(x_ref):
        for s in range(tm // CHUNK):
            seq = _seq_of_chunk(i * (tm // CHUNK) + s, cfg)
            rows = slice(s * CHUNK, (s + 1) * CHUNK)
            o_ref[rows, :] = x_ref[rows, :] + gate_ref[pl.ds(seq, 1), :] * acc[rows, :]

    npt = _prompt_tiles(cfg, tm)
    pl.when(i < npt)(lambda: residual(xp_ref))
    pl.when(i >= npt)(lambda: residual(xs_ref))


def _outproj(ym, yr, wa, wb, x_p, x_s, gate, cfg, tm=512, tn=1024):
    n, d = cfg.n_rows, x_p.shape[1]
    ka, kb = ym.shape[1], yr.shape[1]
    s = gate.shape[0]
    return pl.pallas_call(
        functools.partial(_outproj_kernel, cfg=cfg, tm=tm),
        grid=(n // tm, d // tn),
        in_specs=[pl.BlockSpec((tm, ka), lambda i, j: (i, 0)),
                  pl.BlockSpec((tm, kb), lambda i, j: (i, 0)),
                  pl.BlockSpec((ka, tn), lambda i, j: (0, j)),
                  pl.BlockSpec((kb, tn), lambda i, j: (0, j))]
                 + _row_specs(cfg, tm, tn, lambda j: j)
                 + [pl.BlockSpec((s, tn), lambda i, j: (0, j))],
        out_specs=pl.BlockSpec((tm, tn), lambda i, j: (i, j)),
        out_shape=jax.ShapeDtypeStruct((n, d), F32),
        compiler_params=_params("parallel", "parallel"),
        name="outproj",
    )(ym, yr, wa, wb, x_p, x_s, gate)


def _route(lg, n_experts):
    lane = lax.broadcasted_iota(jnp.int32, lg.shape, 1)
    first = lambda mask: jnp.min(jnp.where(mask, lane, LANES), axis=-1, keepdims=True)
    is_grp = lane < E_GROUPS
    gmax = jnp.max(jnp.where(is_grp, lg, -jnp.inf), axis=-1, keepdims=True)
    gsel = first(is_grp & (lg == gmax))
    pg = 1.0 / jnp.sum(jnp.where(is_grp, jnp.exp(lg - gmax), 0.0), axis=-1, keepdims=True)
    e_lane = lane - E_GROUPS
    in_grp = (e_lane >= 0) & (e_lane < n_experts) & (e_lane // E_PER_GROUP == gsel)
    emax = jnp.max(jnp.where(in_grp, lg, -jnp.inf), axis=-1, keepdims=True)
    p = jnp.where(in_grp, jnp.exp(lg - emax), 0.0)
    eprob = p / jnp.sum(p, axis=-1, keepdims=True)
    v1 = jnp.max(jnp.where(in_grp, eprob, -1.0), axis=-1, keepdims=True)
    i1 = first(in_grp & (eprob == v1))
    rest = in_grp & (lane != i1)
    v2 = jnp.max(jnp.where(rest, eprob, -1.0), axis=-1, keepdims=True)
    i2 = first(rest & (eprob == v2))
    tot = v1 + v2
    cols = [(i1 - E_GROUPS).astype(F32), (i2 - E_GROUPS).astype(F32), v1 / tot * pg, v2 / tot * pg]
    out = jnp.zeros(lg.shape, F32)
    for c, val in enumerate(cols):
        out = jnp.where(lane == c, val, out)
    return out


def _router_kernel(x_ref, shift_ref, scale_ref, g_ref, wh_ref, wl_ref, br_ref, h_ref, rt_ref, *, cfg, tm, n_experts):
    i = pl.program_id(0)
    for s in range(tm // CHUNK):
        seq = _seq_of_chunk(i * (tm // CHUNK) + s, cfg)
        rows = slice(s * CHUNK, (s + 1) * CHUNK)
        h = _modulated_norm(x_ref[rows, :], g_ref[...], scale_ref[pl.ds(seq, 1), :], shift_ref[pl.ds(seq, 1), :])
        h_hi, h_lo = _split_hi_lo(h)
        h_ref[rows, :] = h_hi
        dot = lambda a, b: jnp.dot(a, b[...], preferred_element_type=F32)
        logits = dot(h_hi, wh_ref) + (dot(h_hi, wl_ref) + dot(h_lo, wh_ref)) + br_ref[...]
        rt_ref[rows, :] = _route(logits, n_experts)


def _router(x1, shift, scale, g, wr, br, cfg, n_experts, tm=256):
    n, d = x1.shape
    assert E_GROUPS + n_experts <= LANES and n_experts == E_GROUPS * E_PER_GROUP
    full = lambda a: pl.BlockSpec(a.shape, lambda i: (0,) * a.ndim)
    w_hi, w_lo = _split_hi_lo(wr)
    return pl.pallas_call(
        functools.partial(_router_kernel, cfg=cfg, tm=tm, n_experts=n_experts),
        grid=(n // tm,),
        in_specs=[pl.BlockSpec((tm, d), lambda i: (i, 0)), full(shift), full(scale), full(g),
                  full(w_hi), full(w_lo), full(br)],
        out_specs=[pl.BlockSpec((tm, d), lambda i: (i, 0)), pl.BlockSpec((tm, LANES), lambda i: (i, 0))],
        out_shape=[jax.ShapeDtypeStruct((n, d), BF16), jax.ShapeDtypeStruct((n, LANES), F32)],
        compiler_params=_params("parallel"),
        name="router",
    )(x1, shift, scale, g, w_hi, w_lo, br)


def _new_weights(be_ref):
    i = pl.program_id(1)
    return jnp.logical_or(i == 0, be_ref[i] != be_ref[jnp.maximum(i - 1, 0)])


def _for_valid_rows(nv_ref, out_ref, compute):
    nv = nv_ref[pl.program_id(1)]
    for s in range(out_ref.shape[0] // E_SUB):
        rows = slice(s * E_SUB, (s + 1) * E_SUB)

        @pl.when(nv > s * E_SUB)
        def _():
            out_ref[rows, :] = compute(rows)

        @pl.when(nv <= s * E_SUB)
        def _():
            out_ref[rows, :] = jnp.zeros((E_SUB, out_ref.shape[1]), out_ref.dtype)


def _expert_up_kernel(be_ref, nv_ref, x_ref, wg_ref, wu_ref, h_ref, wg_s, wu_s):
    @pl.when(_new_weights(be_ref))
    def _():
        wg_s[...] = wg_ref[0].astype(BF16)
        wu_s[...] = wu_ref[0].astype(BF16)

    def hidden(rows):
        x = x_ref[rows, :]
        gate = jnp.dot(x, wg_s[...], preferred_element_type=F32)
        up = jnp.dot(x, wu_s[...], preferred_element_type=F32)
        return (_silu(gate) * up).astype(BF16)

    _for_valid_rows(nv_ref, h_ref, hidden)


def _expert_down_kernel(be_ref, nv_ref, h_ref, wd_ref, y_ref, wd_s):
    @pl.when(_new_weights(be_ref))
    def _():
        wd_s[...] = wd_ref[0].astype(BF16)

    _for_valid_rows(nv_ref, y_ref, lambda rows: jnp.dot(h_ref[rows, :], wd_s[...], preferred_element_type=F32))


def _experts(xb, blk_e, n_valid, wg, wu, wd, tm, tf=512, tn=2048):
    rows, d = xb.shape
    f = wg.shape[2]
    nblk = rows // tm
    tf, tn = min(tf, f), min(tn, d)
    assert tm % E_SUB == 0 and f % tf == 0 and d % tn == 0
    hid = pl.pallas_call(
        _expert_up_kernel,
        grid_spec=pltpu.PrefetchScalarGridSpec(
            num_scalar_prefetch=2, grid=(f // tf, nblk),
            in_specs=[pl.BlockSpec((tm, d), lambda j, i, be, nv: (i, 0)),
                      pl.BlockSpec((1, d, tf), lambda j, i, be, nv: (be[i], 0, j)),
                      pl.BlockSpec((1, d, tf), lambda j, i, be, nv: (be[i], 0, j))],
            out_specs=pl.BlockSpec((tm, tf), lambda j, i, be, nv: (i, j)),
            scratch_shapes=[pltpu.VMEM((d, tf), BF16), pltpu.VMEM((d, tf), BF16)]),
        out_shape=jax.ShapeDtypeStruct((rows, f), BF16),
        compiler_params=_params("arbitrary", "arbitrary"),
        name="expert_up",
    )(blk_e, n_valid, xb, wg, wu)
    return pl.pallas_call(
        _expert_down_kernel,
        grid_spec=pltpu.PrefetchScalarGridSpec(
            num_scalar_prefetch=2, grid=(d // tn, nblk),
            in_specs=[pl.BlockSpec((tm, f), lambda j, i, be, nv: (i, 0)),
                      pl.BlockSpec((1, f, tn), lambda j, i, be, nv: (be[i], 0, j))],
            out_specs=pl.BlockSpec((tm, tn), lambda j, i, be, nv: (i, j)),
            scratch_shapes=[pltpu.VMEM((f, tn), BF16)]),
        out_shape=jax.ShapeDtypeStruct((rows, d), F32),
        compiler_params=_params("arbitrary", "arbitrary"),
        name="expert_down",
    )(blk_e, n_valid, hid, wd)


def _final_kernel(x_ref, moe_ref, gate_ref, g_ref, o_ref, *, cfg, tm, tile0):
    i = pl.program_id(0) + tile0
    for s in range(tm // CHUNK):
        seq = _seq_of_chunk(i * (tm // CHUNK) + s, cfg)
        rows = slice(s * CHUNK, (s + 1) * CHUNK)
        x = x_ref[rows, :] + gate_ref[pl.ds(seq, 1), :] * moe_ref[rows, :]
        o_ref[rows, :] = x * lax.rsqrt(jnp.mean(x * x, axis=-1, keepdims=True) + NORM_EPS) * g_ref[...]


def _final(x1, moe, gate, g, cfg, row0, nrows, tm=256):
    d = x1.shape[1]
    assert row0 % tm == 0 and nrows % tm == 0
    tile0 = row0 // tm
    full = lambda a: pl.BlockSpec(a.shape, lambda i: (0,) * a.ndim)
    return pl.pallas_call(
        functools.partial(_final_kernel, cfg=cfg, tm=tm, tile0=tile0),
        grid=(nrows // tm,),
        in_specs=[pl.BlockSpec((tm, d), lambda i: (i + tile0, 0)), pl.BlockSpec((tm, d), lambda i: (i + tile0, 0)),
                  full(gate), full(g)],
        out_specs=pl.BlockSpec((tm, d), lambda i: (i, 0)),
        out_shape=jax.ShapeDtypeStruct((nrows, d), F32),
        compiler_params=_params("parallel"),
        name="final_norm",
    )(x1, moe, gate, g)


def _dispatch(route, n_experts, tm):
    n = route.shape[0]
    eid = route[:, :TOP_K].astype(jnp.int32)
    wts = route[:, TOP_K:2 * TOP_K]

    m = n * TOP_K
    e_flat = eid.reshape(m)
    order = jnp.argsort(e_flat)
    e_sorted = e_flat[order]
    counts = jnp.bincount(e_flat, length=n_experts).astype(jnp.int32)
    padded = (counts + tm - 1) // tm * tm
    start = jnp.cumsum(counts) - counts
    pend = jnp.cumsum(padded)
    pstart = pend - padded
    dest_sorted = pstart[e_sorted] + jnp.arange(m, dtype=jnp.int32) - start[e_sorted]
    n_blocks = -(-(m + n_experts * (tm - 1)) // tm)
    rows = n_blocks * tm
    tok_sorted = (order // TOP_K).astype(jnp.int32)
    row_tok = (jnp.arange(rows, dtype=jnp.int32) % n).at[dest_sorted].set(tok_sorted)
    slot_pos = jnp.zeros((m,), jnp.int32).at[order].set(dest_sorted).reshape(n, TOP_K)
    n_used = pend[-1] // tm
    blk = jnp.arange(n_blocks, dtype=jnp.int32)
    blk_e = jnp.minimum(jnp.searchsorted(pend, blk * tm, side='right'), n_experts - 1).astype(jnp.int32)
    n_valid = jnp.where(blk < n_used, jnp.clip(pstart[blk_e] + counts[blk_e] - blk * tm, 0, tm), 0).astype(jnp.int32)
    last_e = blk_e[jnp.maximum(n_used - 1, 0)]
    blk_e = jnp.where(blk < n_used, blk_e, last_e)
    return row_tok, slot_pos, wts, blk_e, n_valid


def _to_window(a, cfg):
    return jnp.concatenate([a, jnp.zeros(a.shape[:-1] + (cfg.win - cfg.r_cols,), a.dtype)], axis=-1)


def _pad_rows(a, rows):
    return jnp.concatenate([a, jnp.zeros((rows - a.shape[0],) + a.shape[1:], a.dtype)], axis=0)


def kernel(x_prompt, x_sample, state_conv, state_ssm, state_shift, state_wkv, c_prompt, c_sample, norm1_g, w_mod, b_mod, w_in, conv_w, conv_b, dt_bias, a_log, d_skip, m_norm_g, shift_mu, w0, w_up, a0, a_up, g_up, k_k, k_a, r_k, ln_x_g, ln_x_b, w_out, norm2_g, w_grp, b_grp, w_erouter, b_erouter, e_gate, e_up, e_down, final_norm_g):
    assert w_mod.shape[0] == 1, "single-layer trunk"
    bp, tp, d = x_prompt.shape
    bs, ts, _ = x_sample.shape
    mi = m_norm_g.shape[-1]
    rw = w0.shape[-1]
    lw, la, lg = w_up.shape[1], a_up.shape[1], g_up.shape[1]
    cfg = Cfg(d, bp, tp, bs, ts, mi, rw, lw, la, lg)
    nc = cfg.nc
    assert tp % CHUNK == 0 and ts % CHUNK == 0 and tp >= CONV_W - 1 and ts >= CONV_W - 1
    assert w_in.shape[-1] == cfg.o_rw + cfg.mh + cfg.r_cols and cfg.o_rw % LANES == 0
    assert state_conv.shape[-1] == cfg.cd and lw <= LANES and la <= LANES and cfg.mh <= LANES
    n = cfg.n_rows
    n_seq = cfg.n_seq
    mh, rh = cfg.mh, cfg.rh
    npair = rw // PAIR
    n_experts = e_gate.shape[1]

    x_p = x_prompt.reshape(bp * tp, d)
    x_s = x_sample.reshape(bs * ts, d)
    c_all = jnp.concatenate([c_prompt, c_sample], axis=0)

    mod = _modulation(c_all, w_mod[0], b_mod[0])
    shift1, scale1, gate1, shift2, scale2, gate2 = [mod[:, i * d:(i + 1) * d] for i in range(6)]

    h1 = _prenorm(x_p, x_s, shift1, scale1, norm1_g, cfg)
    w_in_t = w_in[0].T
    proj = _inproj(h1, w_in_t, cfg)
    dt_raw = _dtproj(h1, w_in_t, cfg)

    zeros = lambda b, *s: jnp.zeros((b,) + s, F32)
    conv0 = jnp.concatenate([zeros(bp, CONV_W - 1, cfg.cd), state_conv[0]], axis=0)
    conv8 = jnp.concatenate([zeros(n_seq, 8 - (CONV_W - 1), cfg.cd), conv0], axis=1)
    nst = state_ssm.shape[-1]
    ssm0 = jnp.concatenate([zeros(bp, mi, nst), state_ssm[0].reshape(bs, mi, nst)], axis=0)
    lane_pad = lambda a: jnp.concatenate([a.reshape(1, -1), jnp.zeros((1, LANES - a.shape[-1]), F32)], axis=1)
    expand = ((jnp.arange(2 * LANES)[:, None] % LANES) == (jnp.arange(mi)[None, :] // HEAD)).astype(BF16)
    ym, ssm_new = _mamba(proj, dt_raw, conv8, ssm0, conv_w[0], conv_b[0].reshape(1, -1), lane_pad(dt_bias[0]),
                         lane_pad(a_log[0]), jnp.repeat(d_skip[0], HEAD).reshape(1, mi), m_norm_g[0].reshape(1, mi),
                         expand, cfg)

    sh0 = jnp.concatenate([zeros(bp, 1, state_shift.shape[-1]), state_shift[0]], axis=0)
    sh8 = jnp.concatenate([zeros(n_seq, 7, cfg.win), _to_window(sh0, cfg)], axis=1)
    to_pairs = lambda s: s.reshape(-1, npair, 2, HEAD, HEAD).transpose(0, 1, 3, 2, 4).reshape(-1, npair, HEAD, PAIR)
    from_pairs = lambda s: s.reshape(-1, npair, HEAD, 2, HEAD).transpose(0, 1, 3, 2, 4).reshape(-1, rh, HEAD, HEAD)
    wkv0 = jnp.concatenate([zeros(bp, npair, HEAD, PAIR), to_pairs(state_wkv[0])], axis=0)
    row = lambda a: a.reshape(1, -1)
    yr, wkv_new = _rwkv(proj, sh8, wkv0, _to_window(row(shift_mu[0]), cfg), row(w0[0]),
                        _pad_rows(w_up[0], LANES), row(a0[0]), _pad_rows(a_up[0], LANES), g_up[0],
                        row(k_k[0]), row(k_a[0]), row(r_k[0]), row(ln_x_g[0]), row(ln_x_b[0]), cfg)

    wo = w_out[0].astype(BF16)
    x1 = _outproj(ym, yr, wo[:mi], wo[mi:], x_p, x_s, gate1, cfg)
    wr = jnp.concatenate([w_grp[0], w_erouter[0], jnp.zeros((d, LANES - E_GROUPS - n_experts), F32)], axis=1)
    br = lane_pad(jnp.concatenate([b_grp[0], b_erouter[0]]))
    h2, route = _router(x1, shift2, scale2, norm2_g, wr, br, cfg, n_experts)

    tm_e = 2 * E_SUB
    row_tok, slot_pos, wts, blk_e, n_valid = _dispatch(route, n_experts, tm_e)
    yb = _experts(h2[row_tok], blk_e, n_valid, e_gate[0], e_up[0], e_down[0], tm_e)
    moe = yb[slot_pos[:, 0]] * wts[:, 0:1] + yb[slot_pos[:, 1]] * wts[:, 1:2]

    np_ = bp * tp
    fg = final_norm_g.reshape(1, d)
    y_prompt = _final(x1, moe, gate2, fg, cfg, 0, np_).reshape(bp, tp, d)
    y_sample = _final(x1, moe, gate2, fg, cfg, np_, bs * ts).reshape(bs, ts, d)
    tails = lambda b, t, base: jnp.stack(
        [lax.slice(proj, (base + (i + 1) * t - (CONV_W - 1), 0), (base + (i + 1) * t, nc)) for i in range(b)])
    pp = tails(bp, tp, 0)
    ps = tails(bs, ts, np_)
    conv_of = lambda p: p[:, :, mi:mi + cfg.cd][None]
    shift_of = lambda p: p[:, -1:, cfg.o_rw:cfg.o_rw + cfg.r_cols][None]
    ssm_new = ssm_new.reshape(n_seq, mh, HEAD, nst)
    wkv_new = from_pairs(wkv_new)
    return (y_prompt, y_sample,
            conv_of(pp), ssm_new[:bp][None], shift_of(pp), wkv_new[:bp][None],
            conv_of(ps), ssm_new[bp:][None], shift_of(ps), wkv_new[bp:][None])
```

```python
import functools
from typing import NamedTuple

import jax
import jax.numpy as jnp
from jax import lax
from jax.experimental import pallas as pl
from jax.experimental.pallas import tpu as pltpu

F32 = jnp.float32
BF16 = jnp.bfloat16
HI = lax.Precision.HIGHEST

LANES = 128
CHUNK = 64
HEAD = 64
PAIR = 2 * HEAD
NORM_EPS = 1e-6
M_NORM_EPS = 1e-5
R_LN_EPS = 64e-5
CONV_W = 4
E_GROUPS = 4
E_PER_GROUP = 8
TOP_K = 2
VMEM_LIMIT = 56 * 1024 * 1024
E_SUB = 256
EXP_NEG_HALF = 0.6065306597126334
SA_GROUP = 16


class Cfg(NamedTuple):
    d: int
    bp: int
    tp: int
    bs: int
    ts: int
    mi: int
    rw: int
    lw: int
    la: int
    lg: int

    @property
    def cpp(self):
        return self.tp // CHUNK

    @property
    def cps(self):
        return self.ts // CHUNK

    @property
    def n_chunks(self):
        return self.bp * self.cpp + self.bs * self.cps

    @property
    def n_seq(self):
        return self.bp + self.bs

    @property
    def n_rows(self):
        return self.n_chunks * CHUNK

    @property
    def groups(self):
        return self.mi // 256

    @property
    def nb(self):
        return self.groups * LANES

    @property
    def cd(self):
        return self.mi + 2 * self.nb

    @property
    def mh(self):
        return self.mi // HEAD

    @property
    def rh(self):
        return self.rw // HEAD

    @property
    def o_rw(self):
        return self.mi + self.cd

    @property
    def r_cols(self):
        return 3 * self.rw + self.lw + self.la + self.lg

    @property
    def tail(self):
        return -(-(self.lw + self.la + self.lg) // LANES) * LANES

    @property
    def win(self):
        return 3 * self.rw + self.tail

    @property
    def nc(self):
        return self.o_rw + self.win


def _seq_of_chunk(c, cfg):
    npc = cfg.bp * cfg.cpp
    return jnp.where(c < npc, c // cfg.cpp, cfg.bp + (c - npc) // cfg.cps)


def _is_first_chunk(c, cfg):
    npc = cfg.bp * cfg.cpp
    return jnp.where(c < npc, c % cfg.cpp == 0, (c - npc) % cfg.cps == 0)


def _silu(x):
    return x * jax.nn.sigmoid(x)


def _nt(a, b, **kw):
    return lax.dot_general(a, b, (((1,), (1,)), ((), ())), preferred_element_type=F32, **kw)


def _tn(a, b, **kw):
    return lax.dot_general(a, b, (((0,), (0,)), ((), ())), preferred_element_type=F32, **kw)


def _params(*sem):
    return pltpu.CompilerParams(dimension_semantics=sem, vmem_limit_bytes=VMEM_LIMIT)


def _mod_kernel(c_ref, w_ref, b_ref, o_ref):
    o_ref[...] = jnp.dot(_silu(c_ref[...]), w_ref[...], preferred_element_type=F32) + b_ref[...]


def _modulation(c_all, w_mod, b_mod):
    s, d = c_all.shape
    cols = w_mod.shape[1]
    tn = 512
    return pl.pallas_call(
        _mod_kernel,
        grid=(cols // tn,),
        in_specs=[pl.BlockSpec((s, d), lambda j: (0, 0)),
                  pl.BlockSpec((d, tn), lambda j: (0, j)),
                  pl.BlockSpec((1, tn), lambda j: (0, j))],
        out_specs=pl.BlockSpec((s, tn), lambda j: (0, j)),
        out_shape=jax.ShapeDtypeStruct((s, cols), F32),
        compiler_params=_params("parallel"),
        name="modulation",
    )(c_all, w_mod, b_mod.reshape(1, cols))


def _modulated_norm(x, g, scale, shift):
    y = x * lax.rsqrt(jnp.mean(x * x, axis=-1, keepdims=True) + NORM_EPS) * g
    return y * (1.0 + scale) + shift


def _prompt_tiles(cfg, tm):
    assert (cfg.bp * cfg.tp) % tm == 0 and (cfg.bs * cfg.ts) % tm == 0 and tm % CHUNK == 0
    return cfg.bp * cfg.tp // tm


def _row_specs(cfg, tm, width, col=None):
    npt = _prompt_tiles(cfg, tm)
    c = (lambda *r: 0) if col is None else col
    return [pl.BlockSpec((tm, width), lambda i, *r: (jnp.minimum(i, npt - 1), jnp.where(i < npt, c(*r), 0))),
            pl.BlockSpec((tm, width), lambda i, *r: (jnp.maximum(i - npt, 0), jnp.where(i >= npt, c(*r), 0)))]


def _prenorm_kernel(xp_ref, xs_ref, shift_ref, scale_ref, g_ref, h_ref, *, cfg, tm):
    i = pl.program_id(0)

    def norm_rows(x_ref):
        for s in range(tm // CHUNK):
            seq = _seq_of_chunk(i * (tm // CHUNK) + s, cfg)
            rows = slice(s * CHUNK, (s + 1) * CHUNK)
            h = _modulated_norm(x_ref[rows, :], g_ref[...], scale_ref[pl.ds(seq, 1), :], shift_ref[pl.ds(seq, 1), :])
            h_ref[rows, :] = h.astype(BF16)

    npt = _prompt_tiles(cfg, tm)
    pl.when(i < npt)(lambda: norm_rows(xp_ref))
    pl.when(i >= npt)(lambda: norm_rows(xs_ref))


def _prenorm(x_p, x_s, shift, scale, g, cfg, tm=256):
    d = x_p.shape[1]
    full = lambda a: pl.BlockSpec(a.shape, lambda i: (0,) * a.ndim)
    return pl.pallas_call(
        functools.partial(_prenorm_kernel, cfg=cfg, tm=tm),
        grid=(cfg.n_rows // tm,),
        in_specs=_row_specs(cfg, tm, d) + [full(shift), full(scale), full(g)],
        out_specs=pl.BlockSpec((tm, d), lambda i: (i, 0)),
        out_shape=jax.ShapeDtypeStruct((cfg.n_rows, d), BF16),
        compiler_params=_params("parallel"),
        name="prenorm",
    )(x_p, x_s, shift, scale, g)


def _inproj_kernel(h_ref, w_ref, wn_ref, o_ref, wb_s, *, first_shifted, shift, tn):
    j = pl.program_id(0)
    new_tile = pl.program_id(1) == 0

    @pl.when(jnp.logical_and(new_tile, j < first_shifted))
    def _():
        wb_s[...] = w_ref[...].astype(BF16)

    @pl.when(jnp.logical_and(new_tile, j >= first_shifted))
    def _():
        w = jnp.concatenate([w_ref[...], wn_ref[...]], axis=0)
        wb_s[...] = w[shift:shift + tn, :].astype(BF16)

    o_ref[...] = _nt(h_ref[...], wb_s[...])


def _inproj(h, wt, cfg, tm=1024, tn=512):
    n, d = h.shape
    nout = cfg.o_rw + cfg.win
    assert cfg.o_rw % tn == 0 and nout % tn == 0 and n % tm == 0 and cfg.mh <= LANES and cfg.mh % 8 == 0
    last_next = (wt.shape[0] - 1) // LANES
    return pl.pallas_call(
        functools.partial(_inproj_kernel, first_shifted=cfg.o_rw // tn, shift=cfg.mh, tn=tn),
        grid=(nout // tn, n // tm),
        in_specs=[pl.BlockSpec((tm, d), lambda j, i: (i, 0)),
                  pl.BlockSpec((tn, d), lambda j, i: (j, 0)),
                  pl.BlockSpec((LANES, d), lambda j, i: (jnp.minimum((j + 1) * (tn // LANES), last_next), 0))],
        out_specs=pl.BlockSpec((tm, tn), lambda j, i: (i, j)),
        out_shape=jax.ShapeDtypeStruct((n, nout), F32),
        scratch_shapes=[pltpu.VMEM((tn, d), BF16)],
        compiler_params=_params("arbitrary", "arbitrary"),
        name="inproj",
    )(h, wt, wt)


def _dtproj_kernel(h_ref, w_ref, o_ref):
    o_ref[...] = _nt(h_ref[...], w_ref[...].astype(BF16))


def _dtproj(h, wt, cfg, tm=1024):
    n, d = h.shape
    return pl.pallas_call(
        _dtproj_kernel,
        grid=(n // tm,),
        in_specs=[pl.BlockSpec((tm, d), lambda i: (i, 0)), pl.BlockSpec((LANES, d), lambda i: (cfg.o_rw // LANES, 0))],
        out_specs=pl.BlockSpec((tm, LANES), lambda i: (i, 0)),
        out_shape=jax.ShapeDtypeStruct((n, LANES), F32),
        compiler_params=_params("parallel"),
        name="dtproj",
    )(h, wt)


def _mamba_kernel(zx_ref, dt_ref, cst_ref, sst_ref, cw_ref, cb_ref, dtb_ref, alog_ref, dsk_ref, ng_ref, exp_ref,
                  ym_ref, ssm_ref, buf, u_scr, *, cfg):
    c = pl.program_id(0)
    mi, nb, cd = cfg.mi, cfg.nb, cfg.cd
    L = CHUNK

    @pl.when(_is_first_chunk(c, cfg))
    def _():
        buf[0:8, :] = cst_ref[0]
        ssm_ref[0] = sst_ref[0]

    xbc = zx_ref[:, mi:mi + cd]
    buf[8:8 + L, :] = xbc
    conv = (cb_ref[...] + buf[5:5 + L, :] * cw_ref[0:1, :] + buf[6:6 + L, :] * cw_ref[1:2, :]
            + buf[7:7 + L, :] * cw_ref[2:3, :] + xbc * cw_ref[3:4, :])
    buf[0:8, :] = buf[L:L + 8, :]
    u_scr[...] = _silu(conv)

    dt = jax.nn.softplus(dt_ref[...] + dtb_ref[...])
    da = dt * (-jnp.exp(alog_ref[...]))
    row = lax.broadcasted_iota(jnp.int32, (L, 256), 0)
    pos = lax.broadcasted_iota(jnp.int32, (L, 256), 1) % HEAD
    causal = row >= pos
    diag = row == pos
    tri2 = (lax.broadcasted_iota(jnp.int32, (L, 2 * L), 0)
            >= lax.broadcasted_iota(jnp.int32, (L, 2 * L), 1) % L).astype(BF16)
    lane = lax.broadcasted_iota(jnp.int32, (L, PAIR), 1)
    row16 = lax.broadcasted_iota(jnp.int32, (16, 256), 0)
    ones16 = jnp.ones((16, LANES), BF16)
    dt_hl = jnp.concatenate(_split_hi_lo(dt), axis=1)
    da_hl = jnp.concatenate(_split_hi_lo(da), axis=1)

    for g in range(cfg.groups):
        cs = slice(g * 256, (g + 1) * 256)
        xh = u_scr[:, cs]
        bg = u_scr[:, mi + g * LANES:mi + (g + 1) * LANES]
        cg = u_scr[:, mi + nb + g * LANES:mi + nb + (g + 1) * LANES]
        eg = exp_ref[:, cs]
        dte = jnp.dot(dt_hl, eg, preferred_element_type=F32)
        dae = jnp.dot(da_hl, eg, preferred_element_type=F32)
        cum = jnp.dot(tri2, jnp.concatenate(_split_hi_lo(dae), axis=0), preferred_element_type=F32)
        cum_row = jnp.sum(jnp.where(diag, cum, 0.0), axis=0, keepdims=True)
        last = cum[L - 1:L, :]
        decay = jnp.exp(jnp.where(causal, cum - cum_row, -jnp.inf))
        xdt = xh * dte
        sc2 = _nt(cg, jnp.concatenate([bg, bg], axis=0))
        ys = []
        for q in range(2):
            ps = slice(q * PAIR, (q + 1) * PAIR)
            xq = xdt[:, ps]
            rhs = jnp.concatenate([jnp.where(lane < HEAD, xq, 0.0), jnp.where(lane >= HEAD, xq, 0.0)], axis=0)
            ys.append(jnp.dot(decay[:, ps] * sc2, rhs, preferred_element_type=F32))
        y = jnp.concatenate(ys, axis=1)
        sg = ssm_ref[0, cs, :]
        y = y + _nt(cg, sg) * jnp.exp(cum)
        new = _tn(xdt * jnp.exp(last - cum), bg)
        e_last = jnp.exp(last)
        e_hi = e_last.astype(BF16).astype(F32)
        e_rows = jnp.where(row16 == 0, e_hi, jnp.where(row16 == 1, e_last - e_hi, 0.0)).astype(BF16)
        dcol = _tn(e_rows, ones16)
        ssm_ref[0, cs, :] = sg * dcol + new
        y = y + xh * dsk_ref[:, cs]
        y = y * _silu(zx_ref[:, cs])
        y = y * lax.rsqrt(jnp.mean(y * y, axis=-1, keepdims=True) + M_NORM_EPS)
        ym_ref[:, cs] = (y * ng_ref[:, cs]).astype(BF16)


def _mamba(proj, dt_raw, conv_state8, ssm_state, cw, cb, dtb, alog, dsk, ng, expand, cfg):
    n = proj.shape[0]
    mi, cd = cfg.mi, cfg.cd
    nst = ssm_state.shape[-1]
    seq = lambda c: _seq_of_chunk(c, cfg)
    full = lambda a: pl.BlockSpec(a.shape, lambda c: (0,) * a.ndim)
    return pl.pallas_call(
        functools.partial(_mamba_kernel, cfg=cfg),
        grid=(cfg.n_chunks,),
        in_specs=[pl.BlockSpec((CHUNK, mi + cd), lambda c: (c, 0)),
                  pl.BlockSpec((CHUNK, LANES), lambda c: (c, 0)),
                  pl.BlockSpec((1, 8, cd), lambda c: (seq(c), 0, 0)),
                  pl.BlockSpec((1, mi, nst), lambda c: (seq(c), 0, 0)),
                  full(cw), full(cb), full(dtb), full(alog), full(dsk), full(ng), full(expand)],
        out_specs=[pl.BlockSpec((CHUNK, mi), lambda c: (c, 0)),
                   pl.BlockSpec((1, mi, nst), lambda c: (seq(c), 0, 0))],
        out_shape=[jax.ShapeDtypeStruct((n, mi), BF16),
                   jax.ShapeDtypeStruct((cfg.n_seq, mi, nst), F32)],
        scratch_shapes=[pltpu.VMEM((CHUNK + 8, cd), F32), pltpu.VMEM((CHUNK, cd), F32)],
        compiler_params=_params("arbitrary"),
        name="mamba",
    )(proj, dt_raw, conv_state8, ssm_state, cw, cb, dtb, alog, dsk, ng, expand)


def _seg_sum(x, ob):
    w = ob.shape[0]
    xb = x.astype(BF16)
    parts = [jnp.dot(xb[:, p * w:(p + 1) * w], ob, preferred_element_type=F32) for p in range(x.shape[1] // w)]
    return jnp.concatenate(parts, axis=1)


def _split_hi_lo(x):
    hi = x.astype(BF16)
    lo = (x - hi.astype(F32)).astype(BF16)
    return hi, lo


def _rwkv_kernel(rkv_ref, lora_ref, sh_ref, wkv_in_ref, mu_ref, w0_ref, wup_ref, a0_ref, aup_ref, gup_ref,
                 kk_ref, ka_ref, rk_ref, lng_ref, lnb_ref,
                 yr_ref, wkv_ref, pbuf, lbuf, w_s, a_s, b_s, k_s, y_s, vt_s, ut_s, y0_s, g_s, *, cfg):
    c = pl.program_id(0)
    rw = cfg.rw
    L = CHUNK
    npair = rw // PAIR

    @pl.when(_is_first_chunk(c, cfg))
    def _():
        pbuf[0:8, :] = sh_ref[0, :, 0:3 * rw]
        lbuf[0:8, :] = sh_ref[0, :, 3 * rw:]
        wkv_ref[0] = wkv_in_ref[0]

    pr = rkv_ref[...]
    pl_ = lora_ref[...]
    pbuf[8:8 + L, :] = pr
    lbuf[8:8 + L, :] = pl_
    xs = pr + (pbuf[7:7 + L, :] - pr) * mu_ref[:, 0:3 * rw]
    xl = pl_ + (lbuf[7:7 + L, :] - pl_) * mu_ref[:, 3 * rw:]
    pbuf[0:8, :] = pbuf[L:L + 8, :]
    lbuf[0:8, :] = lbuf[L:L + 8, :]

    r = xs[:, 0:rw]
    k = xs[:, rw:2 * rw]
    v = xs[:, 2 * rw:3 * rw]
    wd = xl[:, 0:LANES]
    ad = xl[:, cfg.lw:cfg.lw + LANES]
    gd = xl[:, cfg.lw + cfg.la:cfg.lw + cfg.la + cfg.lg]

    lane = lax.broadcasted_iota(jnp.int32, (PAIR, PAIR), 1)
    rowi = lax.broadcasted_iota(jnp.int32, (PAIR, PAIR), 0)
    o_row = lax.broadcasted_iota(jnp.int32, (2 * PAIR, 2 * PAIR), 0)
    o_lane = lax.broadcasted_iota(jnp.int32, (2 * PAIR, 2 * PAIR), 1)
    ob = ((o_row // HEAD) == (o_lane // HEAD)).astype(BF16)

    w_z = w0_ref[...] + jnp.dot(jnp.tanh(wd), wup_ref[...], preferred_element_type=F32)
    logw = -(jax.nn.sigmoid(w_z) * EXP_NEG_HALF)
    w_s[...] = jnp.exp(logw)
    a = jax.nn.sigmoid(a0_ref[...] + jnp.dot(ad, aup_ref[...], preferred_element_type=F32))
    kk = k * kk_ref[...]
    kk = kk * lax.rsqrt(jnp.maximum(_seg_sum(kk * kk, ob), 1e-24))
    kh = k * (1.0 + (a - 1.0) * ka_ref[...])
    a_s[...] = -kk
    b_s[...] = kk * a
    k_s[...] = kh

    tri2 = (lax.broadcasted_iota(jnp.int32, (L, 2 * L), 0)
            >= lax.broadcasted_iota(jnp.int32, (L, 2 * L), 1) % L).astype(BF16)
    cum = jnp.dot(tri2, jnp.concatenate(_split_hi_lo(logw), axis=0), preferred_element_type=F32)
    p_inv = jnp.exp(-cum)
    r_p = r * jnp.exp(cum)
    b_p = b_s[...] * p_inv
    k_p = kh * p_inv
    lo_lanes = lane < HEAD
    g_row = lax.broadcasted_iota(jnp.int32, (2 * PAIR, PAIR), 0)
    g_lane = lax.broadcasted_iota(jnp.int32, (2 * PAIR, PAIR), 1)
    g_keep = (g_row % HEAD) <= (g_lane % HEAD)

    def by_head(x):
        x2 = jnp.concatenate([x, x], axis=0)
        return jnp.where(lo_lanes == (rowi < HEAD), x2, 0.0).astype(BF16)

    for p in range(npair):
        ps = slice(p * PAIR, (p + 1) * PAIR)
        hs = slice(p * HEAD, (p + 1) * HEAD)
        r_bd = by_head(r_p[:, ps])
        y0_s[hs, :] = _nt(wkv_ref[0, p].astype(BF16), r_bd)
        gram = _nt(jnp.concatenate([by_head(b_p[:, ps]), by_head(k_p[:, ps])], axis=0), r_bd)
        g_s[p] = jnp.where(g_keep, gram, 0.0).astype(BF16)
        vt = v[:, ps].T
        vjt = jnp.concatenate([vt[0:HEAD], vt[HEAD:]], axis=1)
        hi = vjt.astype(BF16).astype(F32)
        vt_s[hs, :] = jnp.concatenate([hi, vjt - hi], axis=1).astype(BF16)

    lane_t = lax.broadcasted_iota(jnp.int32, (HEAD, PAIR), 1) % HEAD
    crow = lax.broadcasted_iota(jnp.int32, (2 * PAIR, 2 * PAIR), 0)
    clane = lax.broadcasted_iota(jnp.int32, (2 * PAIR, 2 * PAIR), 1)
    same_head = ((crow % PAIR) // HEAD) == ((clane % PAIR) // HEAD)
    c_dt = (crow % HEAD) - clane // PAIR

    def steps(blk, carry):
        t0 = pl.multiple_of(blk * 8, 8)
        rows = pl.ds(t0, 8)
        w8, a8, b8, k8 = w_s[rows, :], a_s[rows, :], b_s[rows, :], k_s[rows, :]
        s = [wkv_ref[0, p] for p in range(npair)]
        vcol2 = None
        for i in range(8):
            row = lambda x8, p: x8[i:i + 1, p * PAIR:(p + 1) * PAIR]
            this_step = lane_t == t0 + i
            if i % 2 == 0:
                sel = (same_head & (c_dt == t0 + i)).astype(BF16)
                vcol2 = jnp.dot(vt_s[...], sel, preferred_element_type=F32)
            vcol = vcol2[:, (i % 2) * PAIR:(i % 2 + 1) * PAIR]
            for p0 in range(0, npair, SA_GROUP):
                group = range(p0, min(p0 + SA_GROUP, npair))
                lhs = []
                for p in group:
                    prod = (s[p] * row(a8, p)).astype(BF16)
                    lhs.append(jnp.concatenate([prod[0:HEAD // 2], prod[HEAD // 2:]], axis=1))
                sa = jnp.dot(jnp.concatenate(lhs, axis=0), ob, preferred_element_type=F32)
                for q, p in enumerate(group):
                    half = sa[q * (HEAD // 2):(q + 1) * (HEAD // 2)]
                    sa_p = jnp.concatenate([half[:, 0:PAIR], half[:, PAIR:]], axis=0)
                    s[p] = s[p] * row(w8, p) + sa_p * row(b8, p) + vcol[p * HEAD:(p + 1) * HEAD] * row(k8, p)
                    pltpu.store(ut_s.at[p * HEAD:(p + 1) * HEAD, :], sa_p, mask=this_step)
        for p in range(npair):
            wkv_ref[0, p] = s[p]
        return carry

    ut_s[...] = jnp.zeros_like(ut_s)
    lax.fori_loop(0, L // 8, steps, 0)

    for p in range(npair):
        hs = slice(p * HEAD, (p + 1) * HEAD)
        lhs = jnp.concatenate([ut_s[hs, :].astype(BF16), vt_s[hs, 0:PAIR]], axis=1)
        yt = y0_s[hs, :] + jnp.dot(lhs, g_s[p], preferred_element_type=F32)
        ytt = yt.T
        y_s[:, p * PAIR:(p + 1) * PAIR] = jnp.concatenate([ytt[0:L], ytt[L:]], axis=1)

    y = y_s[...]
    mu = _seg_sum(y, ob) * (1.0 / HEAD)
    dlt = y - mu
    var = _seg_sum(dlt * dlt, ob) * (1.0 / HEAD)
    yn = dlt * lax.rsqrt(var + R_LN_EPS) * lng_ref[...] + lnb_ref[...]
    bonus = _seg_sum(r * k_s[...] * rk_ref[...], ob) * v
    g = jnp.dot(jax.nn.sigmoid(gd), gup_ref[...], preferred_element_type=F32)
    yr_ref[...] = ((yn + bonus) * g).astype(BF16)


def _rwkv(proj, shift8, wkv_pairs, mu, w0, wup, a0, aup, gup, kk, ka, rk, lng, lnb, cfg):
    n = proj.shape[0]
    rw, lp = cfg.rw, cfg.tail
    assert cfg.o_rw % (3 * rw) == 0 and (cfg.o_rw + 3 * rw) % lp == 0
    npair = rw // PAIR
    seq = lambda c: _seq_of_chunk(c, cfg)
    full = lambda a: pl.BlockSpec(a.shape, lambda c: (0,) * a.ndim)
    row_scr = pltpu.VMEM((CHUNK, rw), F32)
    return pl.pallas_call(
        functools.partial(_rwkv_kernel, cfg=cfg),
        grid=(cfg.n_chunks,),
        in_specs=[pl.BlockSpec((CHUNK, 3 * rw), lambda c: (c, cfg.o_rw // (3 * rw))),
                  pl.BlockSpec((CHUNK, lp), lambda c: (c, (cfg.o_rw + 3 * rw) // lp)),
                  pl.BlockSpec((1, 8, 3 * rw + lp), lambda c: (seq(c), 0, 0)),
                  pl.BlockSpec((1, npair, HEAD, PAIR), lambda c: (seq(c), 0, 0, 0)),
                  full(mu), full(w0), full(wup), full(a0), full(aup), full(gup),
                  full(kk), full(ka), full(rk), full(lng), full(lnb)],
        out_specs=[pl.BlockSpec((CHUNK, rw), lambda c: (c, 0)),
                   pl.BlockSpec((1, npair, HEAD, PAIR), lambda c: (seq(c), 0, 0, 0))],
        out_shape=[jax.ShapeDtypeStruct((n, rw), BF16),
                   jax.ShapeDtypeStruct((cfg.n_seq, npair, HEAD, PAIR), F32)],
        scratch_shapes=[pltpu.VMEM((CHUNK + 8, 3 * rw), F32), pltpu.VMEM((CHUNK + 8, lp), F32),
                        row_scr, row_scr, row_scr, row_scr, row_scr,
                        pltpu.VMEM((npair * HEAD, 2 * PAIR), BF16), pltpu.VMEM((npair * HEAD, PAIR), F32),
                        pltpu.VMEM((npair * HEAD, PAIR), F32), pltpu.VMEM((npair, 2 * PAIR, PAIR), BF16)],
        compiler_params=_params("arbitrary"),
        name="rwkv",
    )(proj, proj, shift8, wkv_pairs, mu, w0, wup, a0, aup, gup, kk, ka, rk, lng, lnb)


def _outproj_kernel(ym_ref, yr_ref, wa_ref, wb_ref, xp_ref, xs_ref, gate_ref, o_ref, *, cfg, tm):
    i = pl.program_id(0)
    acc = (jnp.dot(ym_ref[...], wa_ref[...], preferred_element_type=F32)
           + jnp.dot(yr_ref[...], wb_ref[...], preferred_element_type=F32))

    def residual(x_ref):
        for s in range(tm // CHUNK):
            seq = _seq_of_chunk(i * (tm // CHUNK) + s, cfg)
            rows = slice(s * CHUNK, (s + 1) * CHUNK)
            o_ref[rows, :] = x_ref[rows, :] + gate_ref[pl.ds(seq, 1), :] * acc[rows, :]

    npt = _prompt_tiles(cfg, tm)
    pl.when(i < npt)(lambda: residual(xp_ref))
    pl.when(i >= npt)(lambda: residual(xs_ref))


def _outproj(ym, yr, wa, wb, x_p, x_s, gate, cfg, tm=512, tn=1024):
    n, d = cfg.n_rows, x_p.shape[1]
    ka, kb = ym.shape[1], yr.shape[1]
    s = gate.shape[0]
    return pl.pallas_call(
        functools.partial(_outproj_kernel, cfg=cfg, tm=tm),
        grid=(n // tm, d // tn),
        in_specs=[pl.BlockSpec((tm, ka), lambda i, j: (i, 0)),
                  pl.BlockSpec((tm, kb), lambda i, j: (i, 0)),
                  pl.BlockSpec((ka, tn), lambda i, j: (0, j)),
                  pl.BlockSpec((kb, tn), lambda i, j: (0, j))]
                 + _row_specs(cfg, tm, tn, lambda j: j)
                 + [pl.BlockSpec((s, tn), lambda i, j: (0, j))],
        out_specs=pl.BlockSpec((tm, tn), lambda i, j: (i, j)),
        out_shape=jax.ShapeDtypeStruct((n, d), F32),
        compiler_params=_params("parallel", "parallel"),
        name="outproj",
    )(ym, yr, wa, wb, x_p, x_s, gate)


def _route(lg, n_experts):
    lane = lax.broadcasted_iota(jnp.int32, lg.shape, 1)
    first = lambda mask: jnp.min(jnp.where(mask, lane, LANES), axis=-1, keepdims=True)
    is_grp = lane < E_GROUPS
    gmax = jnp.max(jnp.where(is_grp, lg, -jnp.inf), axis=-1, keepdims=True)
    gsel = first(is_grp & (lg == gmax))
    pg = 1.0 / jnp.sum(jnp.where(is_grp, jnp.exp(lg - gmax), 0.0), axis=-1, keepdims=True)
    e_lane = lane - E_GROUPS
    in_grp = (e_lane >= 0) & (e_lane < n_experts) & (e_lane // E_PER_GROUP == gsel)
    emax = jnp.max(jnp.where(in_grp, lg, -jnp.inf), axis=-1, keepdims=True)
    p = jnp.where(in_grp, jnp.exp(lg - emax), 0.0)
    eprob = p / jnp.sum(p, axis=-1, keepdims=True)
    v1 = jnp.max(jnp.where(in_grp, eprob, -1.0), axis=-1, keepdims=True)
    i1 = first(in_grp & (eprob == v1))
    rest = in_grp & (lane != i1)
    v2 = jnp.max(jnp.where(rest, eprob, -1.0), axis=-1, keepdims=True)
    i2 = first(rest & (eprob == v2))
    tot = v1 + v2
    cols = [(i1 - E_GROUPS).astype(F32), (i2 - E_GROUPS).astype(F32), v1 / tot * pg, v2 / tot * pg]
    out = jnp.zeros(lg.shape, F32)
    for c, val in enumerate(cols):
        out = jnp.where(lane == c, val, out)
    return out


def _router_kernel(x_ref, shift_ref, scale_ref, g_ref, wh_ref, wl_ref, br_ref, h_ref, rt_ref, *, cfg, tm, n_experts):
    i = pl.program_id(0)
    for s in range(tm // CHUNK):
        seq = _seq_of_chunk(i * (tm // CHUNK) + s, cfg)
        rows = slice(s * CHUNK, (s + 1) * CHUNK)
        h = _modulated_norm(x_ref[rows, :], g_ref[...], scale_ref[pl.ds(seq, 1), :], shift_ref[pl.ds(seq, 1), :])
        h_hi, h_lo = _split_hi_lo(h)
        h_ref[rows, :] = h_hi
        dot = lambda a, b: jnp.dot(a, b[...], preferred_element_type=F32)
        logits = dot(h_hi, wh_ref) + (dot(h_hi, wl_ref) + dot(h_lo, wh_ref)) + br_ref[...]
        rt_ref[rows, :] = _route(logits, n_experts)


def _router(x1, shift, scale, g, wr, br, cfg, n_experts, tm=256):
    n, d = x1.shape
    assert E_GROUPS + n_experts <= LANES and n_experts == E_GROUPS * E_PER_GROUP
    full = lambda a: pl.BlockSpec(a.shape, lambda i: (0,) * a.ndim)
    w_hi, w_lo = _split_hi_lo(wr)
    return pl.pallas_call(
        functools.partial(_router_kernel, cfg=cfg, tm=tm, n_experts=n_experts),
        grid=(n // tm,),
        in_specs=[pl.BlockSpec((tm, d), lambda i: (i, 0)), full(shift), full(scale), full(g),
                  full(w_hi), full(w_lo), full(br)],
        out_specs=[pl.BlockSpec((tm, d), lambda i: (i, 0)), pl.BlockSpec((tm, LANES), lambda i: (i, 0))],
        out_shape=[jax.ShapeDtypeStruct((n, d), BF16), jax.ShapeDtypeStruct((n, LANES), F32)],
        compiler_params=_params("parallel"),
        name="router",
    )(x1, shift, scale, g, w_hi, w_lo, br)


def _new_weights(be_ref):
    i = pl.program_id(1)
    return jnp.logical_or(i == 0, be_ref[i] != be_ref[jnp.maximum(i - 1, 0)])


def _for_valid_rows(nv_ref, out_ref, compute):
    nv = nv_ref[pl.program_id(1)]
    for s in range(out_ref.shape[0] // E_SUB):
        rows = slice(s * E_SUB, (s + 1) * E_SUB)

        @pl.when(nv > s * E_SUB)
        def _():
            out_ref[rows, :] = compute(rows)

        @pl.when(nv <= s * E_SUB)
        def _():
            out_ref[rows, :] = jnp.zeros((E_SUB, out_ref.shape[1]), out_ref.dtype)


def _expert_up_kernel(be_ref, nv_ref, x_ref, wg_ref, wu_ref, h_ref, wg_s, wu_s):
    @pl.when(_new_weights(be_ref))
    def _():
        wg_s[...] = wg_ref[0].astype(BF16)
        wu_s[...] = wu_ref[0].astype(BF16)

    def hidden(rows):
        x = x_ref[rows, :]
        gate = jnp.dot(x, wg_s[...], preferred_element_type=F32)
        up = jnp.dot(x, wu_s[...], preferred_element_type=F32)
        return (_silu(gate) * up).astype(BF16)

    _for_valid_rows(nv_ref, h_ref, hidden)


def _expert_down_kernel(be_ref, nv_ref, h_ref, wd_ref, y_ref, wd_s):
    @pl.when(_new_weights(be_ref))
    def _():
        wd_s[...] = wd_ref[0].astype(BF16)

    _for_valid_rows(nv_ref, y_ref, lambda rows: jnp.dot(h_ref[rows, :], wd_s[...], preferred_element_type=F32))


def _experts(xb, blk_e, n_valid, wg, wu, wd, tm, tf=512, tn=2048):
    rows, d = xb.shape
    f = wg.shape[2]
    nblk = rows // tm
    tf, tn = min(tf, f), min(tn, d)
    assert tm % E_SUB == 0 and f % tf == 0 and d % tn == 0
    hid = pl.pallas_call(
        _expert_up_kernel,
        grid_spec=pltpu.PrefetchScalarGridSpec(
            num_scalar_prefetch=2, grid=(f // tf, nblk),
            in_specs=[pl.BlockSpec((tm, d), lambda j, i, be, nv: (i, 0)),
                      pl.BlockSpec((1, d, tf), lambda j, i, be, nv: (be[i], 0, j)),
                      pl.BlockSpec((1, d, tf), lambda j, i, be, nv: (be[i], 0, j))],
            out_specs=pl.BlockSpec((tm, tf), lambda j, i, be, nv: (i, j)),
            scratch_shapes=[pltpu.VMEM((d, tf), BF16), pltpu.VMEM((d, tf), BF16)]),
        out_shape=jax.ShapeDtypeStruct((rows, f), BF16),
        compiler_params=_params("arbitrary", "arbitrary"),
        name="expert_up",
    )(blk_e, n_valid, xb, wg, wu)
    return pl.pallas_call(
        _expert_down_kernel,
        grid_spec=pltpu.PrefetchScalarGridSpec(
            num_scalar_prefetch=2, grid=(d // tn, nblk),
            in_specs=[pl.BlockSpec((tm, f), lambda j, i, be, nv: (i, 0)),
                      pl.BlockSpec((1, f, tn), lambda j, i, be, nv: (be[i], 0, j))],
            out_specs=pl.BlockSpec((tm, tn), lambda j, i, be, nv: (i, j)),
            scratch_shapes=[pltpu.VMEM((f, tn), BF16)]),
        out_shape=jax.ShapeDtypeStruct((rows, d), F32),
        compiler_params=_params("arbitrary", "arbitrary"),
        name="expert_down",
    )(blk_e, n_valid, hid, wd)


def _final_kernel(x_ref, moe_ref, gate_ref, g_ref, o_ref, *, cfg, tm, tile0):
    i = pl.program_id(0) + tile0
    for s in range(tm // CHUNK):
        seq = _seq_of_chunk(i * (tm // CHUNK) + s, cfg)
        rows = slice(s * CHUNK, (s + 1) * CHUNK)
        x = x_ref[rows, :] + gate_ref[pl.ds(seq, 1), :] * moe_ref[rows, :]
        o_ref[rows, :] = x * lax.rsqrt(jnp.mean(x * x, axis=-1, keepdims=True) + NORM_EPS) * g_ref[...]


def _final(x1, moe, gate, g, cfg, row0, nrows, tm=256):
    d = x1.shape[1]
    assert row0 % tm == 0 and nrows % tm == 0
    tile0 = row0 // tm
    full = lambda a: pl.BlockSpec(a.shape, lambda i: (0,) * a.ndim)
    return pl.pallas_call(
        functools.partial(_final_kernel, cfg=cfg, tm=tm, tile0=tile0),
        grid=(nrows // tm,),
        in_specs=[pl.BlockSpec((tm, d), lambda i: (i + tile0, 0)), pl.BlockSpec((tm, d), lambda i: (i + tile0, 0)),
                  full(gate), full(g)],
        out_specs=pl.BlockSpec((tm, d), lambda i: (i, 0)),
        out_shape=jax.ShapeDtypeStruct((nrows, d), F32),
        compiler_params=_params("parallel"),
        name="final_norm",
    )(x1, moe, gate, g)


def _dispatch(route, n_experts, tm):
    n = route.shape[0]
    eid = route[:, :TOP_K].astype(jnp.int32)
    wts = route[:, TOP_K:2 * TOP_K]

    m = n * TOP_K
    e_flat = eid.reshape(m)
    order = jnp.argsort(e_flat)
    e_sorted = e_flat[order]
    counts = jnp.bincount(e_flat, length=n_experts).astype(jnp.int32)
    padded = (counts + tm - 1) // tm * tm
    start = jnp.cumsum(counts) - counts
    pend = jnp.cumsum(padded)
    pstart = pend - padded
    dest_sorted = pstart[e_sorted] + jnp.arange(m, dtype=jnp.int32) - start[e_sorted]
    n_blocks = -(-(m + n_experts * (tm - 1)) // tm)
    rows = n_blocks * tm
    tok_sorted = (order // TOP_K).astype(jnp.int32)
    row_tok = (jnp.arange(rows, dtype=jnp.int32) % n).at[dest_sorted].set(tok_sorted)
    slot_pos = jnp.zeros((m,), jnp.int32).at[order].set(dest_sorted).reshape(n, TOP_K)
    n_used = pend[-1] // tm
    blk = jnp.arange(n_blocks, dtype=jnp.int32)
    blk_e = jnp.minimum(jnp.searchsorted(pend, blk * tm, side='right'), n_experts - 1).astype(jnp.int32)
    n_valid = jnp.where(blk < n_used, jnp.clip(pstart[blk_e] + counts[blk_e] - blk * tm, 0, tm), 0).astype(jnp.int32)
    last_e = blk_e[jnp.maximum(n_used - 1, 0)]
    blk_e = jnp.where(blk < n_used, blk_e, last_e)
    return row_tok, slot_pos, wts, blk_e, n_valid


def _to_window(a, cfg):
    return jnp.concatenate([a, jnp.zeros(a.shape[:-1] + (cfg.win - cfg.r_cols,), a.dtype)], axis=-1)


def _pad_rows(a, rows):
    return jnp.concatenate([a, jnp.zeros((rows - a.shape[0],) + a.shape[1:], a.dtype)], axis=0)


def kernel(x_prompt, x_sample, state_conv, state_ssm, state_shift, state_wkv, c_prompt, c_sample, norm1_g, w_mod, b_mod, w_in, conv_w, conv_b, dt_bias, a_log, d_skip, m_norm_g, shift_mu, w0, w_up, a0, a_up, g_up, k_k, k_a, r_k, ln_x_g, ln_x_b, w_out, norm2_g, w_grp, b_grp, w_erouter, b_erouter, e_gate, e_up, e_down, final_norm_g):
    assert w_mod.shape[0] == 1, "single-layer trunk"
    bp, tp, d = x_prompt.shape
    bs, ts, _ = x_sample.shape
    mi = m_norm_g.shape[-1]
    rw = w0.shape[-1]
    lw, la, lg = w_up.shape[1], a_up.shape[1], g_up.shape[1]
    cfg = Cfg(d, bp, tp, bs, ts, mi, rw, lw, la, lg)
    nc = cfg.nc
    assert tp % CHUNK == 0 and ts % CHUNK == 0 and tp >= CONV_W - 1 and ts >= CONV_W - 1
    assert w_in.shape[-1] == cfg.o_rw + cfg.mh + cfg.r_cols and cfg.o_rw % LANES == 0
    assert state_conv.shape[-1] == cfg.cd and lw <= LANES and la <= LANES and cfg.mh <= LANES
    n = cfg.n_rows
    n_seq = cfg.n_seq
    mh, rh = cfg.mh, cfg.rh
    npair = rw // PAIR
    n_experts = e_gate.shape[1]

    x_p = x_prompt.reshape(bp * tp, d)
    x_s = x_sample.reshape(bs * ts, d)
    c_all = jnp.concatenate([c_prompt, c_sample], axis=0)

    mod = _modulation(c_all, w_mod[0], b_mod[0])
    shift1, scale1, gate1, shift2, scale2, gate2 = [mod[:, i * d:(i + 1) * d] for i in range(6)]

    h1 = _prenorm(x_p, x_s, shift1, scale1, norm1_g, cfg)
    w_in_t = w_in[0].T
    proj = _inproj(h1, w_in_t, cfg)
    dt_raw = _dtproj(h1, w_in_t, cfg)

    zeros = lambda b, *s: jnp.zeros((b,) + s, F32)
    conv0 = jnp.concatenate([zeros(bp, CONV_W - 1, cfg.cd), state_conv[0]], axis=0)
    conv8 = jnp.concatenate([zeros(n_seq, 8 - (CONV_W - 1), cfg.cd), conv0], axis=1)
    nst = state_ssm.shape[-1]
    ssm0 = jnp.concatenate([zeros(bp, mi, nst), state_ssm[0].reshape(bs, mi, nst)], axis=0)
    lane_pad = lambda a: jnp.concatenate([a.reshape(1, -1), jnp.zeros((1, LANES - a.shape[-1]), F32)], axis=1)
    expand = ((jnp.arange(2 * LANES)[:, None] % LANES) == (jnp.arange(mi)[None, :] // HEAD)).astype(BF16)
    ym, ssm_new = _mamba(proj, dt_raw, conv8, ssm0, conv_w[0], conv_b[0].reshape(1, -1), lane_pad(dt_bias[0]),
                         lane_pad(a_log[0]), jnp.repeat(d_skip[0], HEAD).reshape(1, mi), m_norm_g[0].reshape(1, mi),
                         expand, cfg)

    sh0 = jnp.concatenate([zeros(bp, 1, state_shift.shape[-1]), state_shift[0]], axis=0)
    sh8 = jnp.concatenate([zeros(n_seq, 7, cfg.win), _to_window(sh0, cfg)], axis=1)
    to_pairs = lambda s: s.reshape(-1, npair, 2, HEAD, HEAD).transpose(0, 1, 3, 2, 4).reshape(-1, npair, HEAD, PAIR)
    from_pairs = lambda s: s.reshape(-1, npair, HEAD, 2, HEAD).transpose(0, 1, 3, 2, 4).reshape(-1, rh, HEAD, HEAD)
    wkv0 = jnp.concatenate([zeros(bp, npair, HEAD, PAIR), to_pairs(state_wkv[0])], axis=0)
    row = lambda a: a.reshape(1, -1)
    yr, wkv_new = _rwkv(proj, sh8, wkv0, _to_window(row(shift_mu[0]), cfg), row(w0[0]),
                        _pad_rows(w_up[0], LANES), row(a0[0]), _pad_rows(a_up[0], LANES), g_up[0],
                        row(k_k[0]), row(k_a[0]), row(r_k[0]), row(ln_x_g[0]), row(ln_x_b[0]), cfg)

    wo = w_out[0].astype(BF16)
    x1 = _outproj(ym, yr, wo[:mi], wo[mi:], x_p, x_s, gate1, cfg)
    wr = jnp.concatenate([w_grp[0], w_erouter[0], jnp.zeros((d, LANES - E_GROUPS - n_experts), F32)], axis=1)
    br = lane_pad(jnp.concatenate([b_grp[0], b_erouter[0]]))
    h2, route = _router(x1, shift2, scale2, norm2_g, wr, br, cfg, n_experts)

    tm_e = 2 * E_SUB
    row_tok, slot_pos, wts, blk_e, n_valid = _dispatch(route, n_experts, tm_e)
    yb = _experts(h2[row_tok], blk_e, n_valid, e_gate[0], e_up[0], e_down[0], tm_e)
    moe = yb[slot_pos[:, 0]] * wts[:, 0:1] + yb[slot_pos[:, 1]] * wts[:, 1:2]

    np_ = bp * tp
    fg = final_norm_g.reshape(1, d)
    y_prompt = _final(x1, moe, gate2, fg, cfg, 0, np_).reshape(bp, tp, d)
    y_sample = _final(x1, moe, gate2, fg, cfg, np_, bs * ts).reshape(bs, ts, d)
    tails = lambda b, t, base: jnp.stack(
        [lax.slice(proj, (base + (i + 1) * t - (CONV_W - 1), 0), (base + (i + 1) * t, nc)) for i in range(b)])
    pp = tails(bp, tp, 0)
    ps = tails(bs, ts, np_)
    conv_of = lambda p: p[:, :, mi:mi + cfg.cd][None]
    shift_of = lambda p: p[:, -1:, cfg.o_rw:cfg.o_rw + cfg.r_cols][None]
    ssm_new = ssm_new.reshape(n_seq, mh, HEAD, nst)
    wkv_new = from_pairs(wkv_new)
    return (y_prompt, y_sample,
            conv_of(pp), ssm_new[:bp][None], shift_of(pp), wkv_new[:bp][None],
            conv_of(ps), ssm_new[bp:][None], shift_of(ps), wkv_new[bp:][None])
```

```python
import functools
from typing import NamedTuple

import jax
import jax.numpy as jnp
from jax import lax
from jax.experimental import pallas as pl
from jax.experimental.pallas import tpu as pltpu

F32 = jnp.float32
BF16 = jnp.bfloat16
HI = lax.Precision.HIGHEST

LANES = 128
CHUNK = 64
HEAD = 64
PAIR = 2 * HEAD
NORM_EPS = 1e-6
M_NORM_EPS = 1e-5
R_LN_EPS = 64e-5
CONV_W = 4
E_GROUPS = 4
E_PER_GROUP = 8
TOP_K = 2
VMEM_LIMIT = 56 * 1024 * 1024
E_SUB = 256
EXP_NEG_HALF = 0.6065306597126334
SA_GROUP = 16


class Cfg(NamedTuple):
    d: int
    bp: int
    tp: int
    bs: int
    ts: int
    mi: int
    rw: int
    lw: int
    la: int
    lg: int

    @property
    def cpp(self):
        return self.tp // CHUNK

    @property
    def cps(self):
        return self.ts // CHUNK

    @property
    def n_chunks(self):
        return self.bp * self.cpp + self.bs * self.cps

    @property
    def n_seq(self):
        return self.bp + self.bs

    @property
    def n_rows(self):
        return self.n_chunks * CHUNK

    @property
    def groups(self):
        return self.mi // 256

    @property
    def nb(self):
        return self.groups * LANES

    @property
    def cd(self):
        return self.mi + 2 * self.nb

    @property
    def mh(self):
        return self.mi // HEAD

    @property
    def rh(self):
        return self.rw // HEAD

    @property
    def o_rw(self):
        return self.mi + self.cd

    @property
    def r_cols(self):
        return 3 * self.rw + self.lw + self.la + self.lg

    @property
    def tail(self):
        return -(-(self.lw + self.la + self.lg) // LANES) * LANES

    @property
    def win(self):
        return 3 * self.rw + self.tail

    @property
    def nc(self):
        return self.o_rw + self.win


def _seq_of_chunk(c, cfg):
    npc = cfg.bp * cfg.cpp
    return jnp.where(c < npc, c // cfg.cpp, cfg.bp + (c - npc) // cfg.cps)


def _is_first_chunk(c, cfg):
    npc = cfg.bp * cfg.cpp
    return jnp.where(c < npc, c % cfg.cpp == 0, (c - npc) % cfg.cps == 0)


def _silu(x):
    return x * jax.nn.sigmoid(x)


def _nt(a, b, **kw):
    return lax.dot_general(a, b, (((1,), (1,)), ((), ())), preferred_element_type=F32, **kw)


def _tn(a, b, **kw):
    return lax.dot_general(a, b, (((0,), (0,)), ((), ())), preferred_element_type=F32, **kw)


def _params(*sem):
    return pltpu.CompilerParams(dimension_semantics=sem, vmem_limit_bytes=VMEM_LIMIT)


def _mod_kernel(c_ref, w_ref, b_ref, o_ref):
    o_ref[...] = jnp.dot(_silu(c_ref[...]), w_ref[...], preferred_element_type=F32) + b_ref[...]


def _modulation(c_all, w_mod, b_mod):
    s, d = c_all.shape
    cols = w_mod.shape[1]
    tn = 512
    return pl.pallas_call(
        _mod_kernel,
        grid=(cols // tn,),
        in_specs=[pl.BlockSpec((s, d), lambda j: (0, 0)),
                  pl.BlockSpec((d, tn), lambda j: (0, j)),
                  pl.BlockSpec((1, tn), lambda j: (0, j))],
        out_specs=pl.BlockSpec((s, tn), lambda j: (0, j)),
        out_shape=jax.ShapeDtypeStruct((s, cols), F32),
        compiler_params=_params("parallel"),
        name="modulation",
    )(c_all, w_mod, b_mod.reshape(1, cols))


def _modulated_norm(x, g, scale, shift):
    y = x * lax.rsqrt(jnp.mean(x * x, axis=-1, keepdims=True) + NORM_EPS) * g
    return y * (1.0 + scale) + shift


def _prompt_tiles(cfg, tm):
    assert (cfg.bp * cfg.tp) % tm == 0 and (cfg.bs * cfg.ts) % tm == 0 and tm % CHUNK == 0
    return cfg.bp * cfg.tp // tm


def _row_specs(cfg, tm, width, col=None):
    npt = _prompt_tiles(cfg, tm)
    c = (lambda *r: 0) if col is None else col
    return [pl.BlockSpec((tm, width), lambda i, *r: (jnp.minimum(i, npt - 1), jnp.where(i < npt, c(*r), 0))),
            pl.BlockSpec((tm, width), lambda i, *r: (jnp.maximum(i - npt, 0), jnp.where(i >= npt, c(*r), 0)))]


def _prenorm_kernel(xp_ref, xs_ref, shift_ref, scale_ref, g_ref, h_ref, *, cfg, tm):
    i = pl.program_id(0)

    def norm_rows(x_ref):
        for s in range(tm // CHUNK):
            seq = _seq_of_chunk(i * (tm // CHUNK) + s, cfg)
            rows = slice(s * CHUNK, (s + 1) * CHUNK)
            h = _modulated_norm(x_ref[rows, :], g_ref[...], scale_ref[pl.ds(seq, 1), :], shift_ref[pl.ds(seq, 1), :])
            h_ref[rows, :] = h.astype(BF16)

    npt = _prompt_tiles(cfg, tm)
    pl.when(i < npt)(lambda: norm_rows(xp_ref))
    pl.when(i >= npt)(lambda: norm_rows(xs_ref))


def _prenorm(x_p, x_s, shift, scale, g, cfg, tm=256):
    d = x_p.shape[1]
    full = lambda a: pl.BlockSpec(a.shape, lambda i: (0,) * a.ndim)
    return pl.pallas_call(
        functools.partial(_prenorm_kernel, cfg=cfg, tm=tm),
        grid=(cfg.n_rows // tm,),
        in_specs=_row_specs(cfg, tm, d) + [full(shift), full(scale), full(g)],
        out_specs=pl.BlockSpec((tm, d), lambda i: (i, 0)),
        out_shape=jax.ShapeDtypeStruct((cfg.n_rows, d), BF16),
        compiler_params=_params("parallel"),
        name="prenorm",
    )(x_p, x_s, shift, scale, g)


def _inproj_kernel(h_ref, w_ref, wn_ref, o_ref, wb_s, *, first_shifted, shift, tn, n_features):
    j = pl.program_id(0)
    new_tile = pl.program_id(1) == 0

    @pl.when(jnp.logical_and(new_tile, j < first_shifted))
    def _():
        wb_s[...] = w_ref[...].astype(BF16)

    @pl.when(jnp.logical_and(new_tile, j >= first_shifted))
    def _():
        w = jnp.concatenate([w_ref[...], wn_ref[...]], axis=0)[shift:shift + tn, :]
        feature = j * tn + shift + lax.broadcasted_iota(jnp.int32, w.shape, 0)
        wb_s[...] = jnp.where(feature < n_features, w, 0.0).astype(BF16)

    o_ref[...] = _nt(h_ref[...], wb_s[...])


def _inproj(h, wt, cfg, tm=1024, tn=512):
    n, d = h.shape
    nout = cfg.o_rw + cfg.win
    assert cfg.o_rw % tn == 0 and nout % tn == 0 and n % tm == 0 and cfg.mh <= LANES and cfg.mh % 8 == 0
    last_next = (wt.shape[0] - 1) // LANES
    return pl.pallas_call(
        functools.partial(_inproj_kernel, first_shifted=cfg.o_rw // tn, shift=cfg.mh, tn=tn, n_features=wt.shape[0]),
        grid=(nout // tn, n // tm),
        in_specs=[pl.BlockSpec((tm, d), lambda j, i: (i, 0)),
                  pl.BlockSpec((tn, d), lambda j, i: (j, 0)),
                  pl.BlockSpec((LANES, d), lambda j, i: (jnp.minimum((j + 1) * (tn // LANES), last_next), 0))],
        out_specs=pl.BlockSpec((tm, tn), lambda j, i: (i, j)),
        out_shape=jax.ShapeDtypeStruct((n, nout), F32),
        scratch_shapes=[pltpu.VMEM((tn, d), BF16)],
        compiler_params=_params("arbitrary", "arbitrary"),
        name="inproj",
    )(h, wt, wt)


def _dtproj_kernel(h_ref, w_ref, o_ref):
    o_ref[...] = _nt(h_ref[...], w_ref[...].astype(BF16))


def _dtproj(h, wt, cfg, tm=1024):
    n, d = h.shape
    return pl.pallas_call(
        _dtproj_kernel,
        grid=(n // tm,),
        in_specs=[pl.BlockSpec((tm, d), lambda i: (i, 0)), pl.BlockSpec((LANES, d), lambda i: (cfg.o_rw // LANES, 0))],
        out_specs=pl.BlockSpec((tm, LANES), lambda i: (i, 0)),
        out_shape=jax.ShapeDtypeStruct((n, LANES), F32),
        compiler_params=_params("parallel"),
        name="dtproj",
    )(h, wt)


def _mamba_kernel(zx_ref, dt_ref, cst_ref, sst_ref, cw_ref, cb_ref, dtb_ref, alog_ref, dsk_ref, ng_ref, exp_ref,
                  ym_ref, ssm_ref, buf, u_scr, *, cfg):
    c = pl.program_id(0)
    mi, nb, cd = cfg.mi, cfg.nb, cfg.cd
    L = CHUNK

    @pl.when(_is_first_chunk(c, cfg))
    def _():
        buf[0:8, :] = cst_ref[0]
        ssm_ref[0] = sst_ref[0]

    xbc = zx_ref[:, mi:mi + cd]
    buf[8:8 + L, :] = xbc
    conv = (cb_ref[...] + buf[5:5 + L, :] * cw_ref[0:1, :] + buf[6:6 + L, :] * cw_ref[1:2, :]
            + buf[7:7 + L, :] * cw_ref[2:3, :] + xbc * cw_ref[3:4, :])
    buf[0:8, :] = buf[L:L + 8, :]
    u_scr[...] = _silu(conv)

    dt = jax.nn.softplus(dt_ref[...] + dtb_ref[...])
    da = dt * (-jnp.exp(alog_ref[...]))
    row = lax.broadcasted_iota(jnp.int32, (L, 256), 0)
    pos = lax.broadcasted_iota(jnp.int32, (L, 256), 1) % HEAD
    causal = row >= pos
    diag = row == pos
    tri2 = (lax.broadcasted_iota(jnp.int32, (L, 2 * L), 0)
            >= lax.broadcasted_iota(jnp.int32, (L, 2 * L), 1) % L).astype(BF16)
    lane = lax.broadcasted_iota(jnp.int32, (L, PAIR), 1)
    row16 = lax.broadcasted_iota(jnp.int32, (16, 256), 0)
    ones16 = jnp.ones((16, LANES), BF16)
    dt_hl = jnp.concatenate(_split_hi_lo(dt), axis=1)
    da_hl = jnp.concatenate(_split_hi_lo(da), axis=1)

    for g in range(cfg.groups):
        cs = slice(g * 256, (g + 1) * 256)
        xh = u_scr[:, cs]
        bg = u_scr[:, mi + g * LANES:mi + (g + 1) * LANES]
        cg = u_scr[:, mi + nb + g * LANES:mi + nb + (g + 1) * LANES]
        eg = exp_ref[:, cs]
        dte = jnp.dot(dt_hl, eg, preferred_element_type=F32)
        dae = jnp.dot(da_hl, eg, preferred_element_type=F32)
        cum = jnp.dot(tri2, jnp.concatenate(_split_hi_lo(dae), axis=0), preferred_element_type=F32)
        cum_row = jnp.sum(jnp.where(diag, cum, 0.0), axis=0, keepdims=True)
        last = cum[L - 1:L, :]
        decay = jnp.exp(jnp.where(causal, cum - cum_row, -jnp.inf))
        xdt = xh * dte
        sc2 = _nt(cg, jnp.concatenate([bg, bg], axis=0))
        ys = []
        for q in range(2):
            ps = slice(q * PAIR, (q + 1) * PAIR)
            xq = xdt[:, ps]
            rhs = jnp.concatenate([jnp.where(lane < HEAD, xq, 0.0), jnp.where(lane >= HEAD, xq, 0.0)], axis=0)
            ys.append(jnp.dot(decay[:, ps] * sc2, rhs, preferred_element_type=F32))
        y = jnp.concatenate(ys, axis=1)
        sg = ssm_ref[0, cs, :]
        y = y + _nt(cg, sg) * jnp.exp(cum)
        new = _tn(xdt * jnp.exp(last - cum), bg)
        e_last = jnp.exp(last)
        e_hi = e_last.astype(BF16).astype(F32)
        e_rows = jnp.where(row16 == 0, e_hi, jnp.where(row16 == 1, e_last - e_hi, 0.0)).astype(BF16)
        dcol = _tn(e_rows, ones16)
        ssm_ref[0, cs, :] = sg * dcol + new
        y = y + xh * dsk_ref[:, cs]
        y = y * _silu(zx_ref[:, cs])
        y = y * lax.rsqrt(jnp.mean(y * y, axis=-1, keepdims=True) + M_NORM_EPS)
        ym_ref[:, cs] = (y * ng_ref[:, cs]).astype(BF16)


def _mamba(proj, dt_raw, conv_state8, ssm_state, cw, cb, dtb, alog, dsk, ng, expand, cfg):
    n = proj.shape[0]
    mi, cd = cfg.mi, cfg.cd
    nst = ssm_state.shape[-1]
    seq = lambda c: _seq_of_chunk(c, cfg)
    full = lambda a: pl.BlockSpec(a.shape, lambda c: (0,) * a.ndim)
    return pl.pallas_call(
        functools.partial(_mamba_kernel, cfg=cfg),
        grid=(cfg.n_chunks,),
        in_specs=[pl.BlockSpec((CHUNK, mi + cd), lambda c: (c, 0)),
                  pl.BlockSpec((CHUNK, LANES), lambda c: (c, 0)),
                  pl.BlockSpec((1, 8, cd), lambda c: (seq(c), 0, 0)),
                  pl.BlockSpec((1, mi, nst), lambda c: (seq(c), 0, 0)),
                  full(cw), full(cb), full(dtb), full(alog), full(dsk), full(ng), full(expand)],
        out_specs=[pl.BlockSpec((CHUNK, mi), lambda c: (c, 0)),
                   pl.BlockSpec((1, mi, nst), lambda c: (seq(c), 0, 0))],
        out_shape=[jax.ShapeDtypeStruct((n, mi), BF16),
                   jax.ShapeDtypeStruct((cfg.n_seq, mi, nst), F32)],
        scratch_shapes=[pltpu.VMEM((CHUNK + 8, cd), F32), pltpu.VMEM((CHUNK, cd), F32)],
        compiler_params=_params("arbitrary"),
        name="mamba",
    )(proj, dt_raw, conv_state8, ssm_state, cw, cb, dtb, alog, dsk, ng, expand)


def _seg_sum(x, ob):
    w = ob.shape[0]
    xb = x.astype(BF16)
    parts = [jnp.dot(xb[:, p * w:(p + 1) * w], ob, preferred_element_type=F32) for p in range(x.shape[1] // w)]
    return jnp.concatenate(parts, axis=1)


def _split_hi_lo(x):
    hi = x.astype(BF16)
    lo = (x - hi.astype(F32)).astype(BF16)
    return hi, lo


def _rwkv_kernel(rkv_ref, lora_ref, sh_ref, wkv_in_ref, mu_ref, w0_ref, wup_ref, a0_ref, aup_ref, gup_ref,
                 kk_ref, ka_ref, rk_ref, lng_ref, lnb_ref,
                 yr_ref, wkv_ref, pbuf, lbuf, w_s, a_s, b_s, k_s, y_s, vt_s, ut_s, y0_s, g_s, *, cfg):
    c = pl.program_id(0)
    rw = cfg.rw
    L = CHUNK
    npair = rw // PAIR

    @pl.when(_is_first_chunk(c, cfg))
    def _():
        pbuf[0:8, :] = sh_ref[0, :, 0:3 * rw]
        lbuf[0:8, :] = sh_ref[0, :, 3 * rw:]
        wkv_ref[0] = wkv_in_ref[0]

    pr = rkv_ref[...]
    pl_ = lora_ref[...]
    pbuf[8:8 + L, :] = pr
    lbuf[8:8 + L, :] = pl_
    xs = pr + (pbuf[7:7 + L, :] - pr) * mu_ref[:, 0:3 * rw]
    xl = pl_ + (lbuf[7:7 + L, :] - pl_) * mu_ref[:, 3 * rw:]
    pbuf[0:8, :] = pbuf[L:L + 8, :]
    lbuf[0:8, :] = lbuf[L:L + 8, :]

    r = xs[:, 0:rw]
    k = xs[:, rw:2 * rw]
    v = xs[:, 2 * rw:3 * rw]
    wd = xl[:, 0:LANES]
    ad = xl[:, cfg.lw:cfg.lw + LANES]
    gd = xl[:, cfg.lw + cfg.la:cfg.lw + cfg.la + cfg.lg]

    lane = lax.broadcasted_iota(jnp.int32, (PAIR, PAIR), 1)
    rowi = lax.broadcasted_iota(jnp.int32, (PAIR, PAIR), 0)
    o_row = lax.broadcasted_iota(jnp.int32, (2 * PAIR, 2 * PAIR), 0)
    o_lane = lax.broadcasted_iota(jnp.int32, (2 * PAIR, 2 * PAIR), 1)
    ob = ((o_row // HEAD) == (o_lane // HEAD)).astype(BF16)

    w_z = w0_ref[...] + jnp.dot(jnp.tanh(wd), wup_ref[...], preferred_element_type=F32)
    logw = -(jax.nn.sigmoid(w_z) * EXP_NEG_HALF)
    w_s[...] = jnp.exp(logw)
    a = jax.nn.sigmoid(a0_ref[...] + jnp.dot(ad, aup_ref[...], preferred_element_type=F32))
    kk = k * kk_ref[...]
    kk = kk * lax.rsqrt(jnp.maximum(_seg_sum(kk * kk, ob), 1e-24))
    kh = k * (1.0 + (a - 1.0) * ka_ref[...])
    a_s[...] = -kk
    b_s[...] = kk * a
    k_s[...] = kh

    tri2 = (lax.broadcasted_iota(jnp.int32, (L, 2 * L), 0)
            >= lax.broadcasted_iota(jnp.int32, (L, 2 * L), 1) % L).astype(BF16)
    cum = jnp.dot(tri2, jnp.concatenate(_split_hi_lo(logw), axis=0), preferred_element_type=F32)
    p_inv = jnp.exp(-cum)
    r_p = r * jnp.exp(cum)
    b_p = b_s[...] * p_inv
    k_p = kh * p_inv
    lo_lanes = lane < HEAD
    g_row = lax.broadcasted_iota(jnp.int32, (2 * PAIR, PAIR), 0)
    g_lane = lax.broadcasted_iota(jnp.int32, (2 * PAIR, PAIR), 1)
    g_keep = (g_row % HEAD) <= (g_lane % HEAD)

    def by_head(x):
        x2 = jnp.concatenate([x, x], axis=0)
        return jnp.where(lo_lanes == (rowi < HEAD), x2, 0.0).astype(BF16)

    for p in range(npair):
        ps = slice(p * PAIR, (p + 1) * PAIR)
        hs = slice(p * HEAD, (p + 1) * HEAD)
        r_bd = by_head(r_p[:, ps])
        y0_s[hs, :] = _nt(wkv_ref[0, p].astype(BF16), r_bd)
        gram = _nt(jnp.concatenate([by_head(b_p[:, ps]), by_head(k_p[:, ps])], axis=0), r_bd)
        g_s[p] = jnp.where(g_keep, gram, 0.0).astype(BF16)
        vt = v[:, ps].T
        vjt = jnp.concatenate([vt[0:HEAD], vt[HEAD:]], axis=1)
        hi = vjt.astype(BF16).astype(F32)
        vt_s[hs, :] = jnp.concatenate([hi, vjt - hi], axis=1).astype(BF16)

    lane_t = lax.broadcasted_iota(jnp.int32, (HEAD, PAIR), 1) % HEAD
    crow = lax.broadcasted_iota(jnp.int32, (2 * PAIR, 2 * PAIR), 0)
    clane = lax.broadcasted_iota(jnp.int32, (2 * PAIR, 2 * PAIR), 1)
    same_head = ((crow % PAIR) // HEAD) == ((clane % PAIR) // HEAD)
    c_dt = (crow % HEAD) - clane // PAIR

    def steps(blk, carry):
        t0 = pl.multiple_of(blk * 8, 8)
        rows = pl.ds(t0, 8)
        w8, a8, b8, k8 = w_s[rows, :], a_s[rows, :], b_s[rows, :], k_s[rows, :]
        s = [wkv_ref[0, p] for p in range(npair)]
        vcol2 = None
        for i in range(8):
            row = lambda x8, p: x8[i:i + 1, p * PAIR:(p + 1) * PAIR]
            this_step = lane_t == t0 + i
            if i % 2 == 0:
                sel = (same_head & (c_dt == t0 + i)).astype(BF16)
                vcol2 = jnp.dot(vt_s[...], sel, preferred_element_type=F32)
            vcol = vcol2[:, (i % 2) * PAIR:(i % 2 + 1) * PAIR]
            for p0 in range(0, npair, SA_GROUP):
                group = range(p0, min(p0 + SA_GROUP, npair))
                lhs = []
                for p in group:
                    prod = (s[p] * row(a8, p)).astype(BF16)
                    lhs.append(jnp.concatenate([prod[0:HEAD // 2], prod[HEAD // 2:]], axis=1))
                sa = jnp.dot(jnp.concatenate(lhs, axis=0), ob, preferred_element_type=F32)
                for q, p in enumerate(group):
                    half = sa[q * (HEAD // 2):(q + 1) * (HEAD // 2)]
                    sa_p = jnp.concatenate([half[:, 0:PAIR], half[:, PAIR:]], axis=0)
                    s[p] = s[p] * row(w8, p) + sa_p * row(b8, p) + vcol[p * HEAD:(p + 1) * HEAD] * row(k8, p)
                    pltpu.store(ut_s.at[p * HEAD:(p + 1) * HEAD, :], sa_p, mask=this_step)
        for p in range(npair):
            wkv_ref[0, p] = s[p]
        return carry

    ut_s[...] = jnp.zeros_like(ut_s)
    lax.fori_loop(0, L // 8, steps, 0)

    for p in range(npair):
        hs = slice(p * HEAD, (p + 1) * HEAD)
        lhs = jnp.concatenate([ut_s[hs, :].astype(BF16), vt_s[hs, 0:PAIR]], axis=1)
        yt = y0_s[hs, :] + jnp.dot(lhs, g_s[p], preferred_element_type=F32)
        ytt = yt.T
        y_s[:, p * PAIR:(p + 1) * PAIR] = jnp.concatenate([ytt[0:L], ytt[L:]], axis=1)

    y = y_s[...]
    mu = _seg_sum(y, ob) * (1.0 / HEAD)
    dlt = y - mu
    var = _seg_sum(dlt * dlt, ob) * (1.0 / HEAD)
    yn = dlt * lax.rsqrt(var + R_LN_EPS) * lng_ref[...] + lnb_ref[...]
    bonus = _seg_sum(r * k_s[...] * rk_ref[...], ob) * v
    g = jnp.dot(jax.nn.sigmoid(gd), gup_ref[...], preferred_element_type=F32)
    yr_ref[...] = ((yn + bonus) * g).astype(BF16)


def _rwkv(proj, shift8, wkv_pairs, mu, w0, wup, a0, aup, gup, kk, ka, rk, lng, lnb, cfg):
    n = proj.shape[0]
    rw, lp = cfg.rw, cfg.tail
    assert cfg.o_rw % (3 * rw) == 0 and (cfg.o_rw + 3 * rw) % lp == 0
    npair = rw // PAIR
    seq = lambda c: _seq_of_chunk(c, cfg)
    full = lambda a: pl.BlockSpec(a.shape, lambda c: (0,) * a.ndim)
    row_scr = pltpu.VMEM((CHUNK, rw), F32)
    return pl.pallas_call(
        functools.partial(_rwkv_kernel, cfg=cfg),
        grid=(cfg.n_chunks,),
        in_specs=[pl.BlockSpec((CHUNK, 3 * rw), lambda c: (c, cfg.o_rw // (3 * rw))),
                  pl.BlockSpec((CHUNK, lp), lambda c: (c, (cfg.o_rw + 3 * rw) // lp)),
                  pl.BlockSpec((1, 8, 3 * rw + lp), lambda c: (seq(c), 0, 0)),
                  pl.BlockSpec((1, npair, HEAD, PAIR), lambda c: (seq(c), 0, 0, 0)),
                  full(mu), full(w0), full(wup), full(a0), full(aup), full(gup),
                  full(kk), full(ka), full(rk), full(lng), full(lnb)],
        out_specs=[pl.BlockSpec((CHUNK, rw), lambda c: (c, 0)),
                   pl.BlockSpec((1, npair, HEAD, PAIR), lambda c: (seq(c), 0, 0, 0))],
        out_shape=[jax.ShapeDtypeStruct((n, rw), BF16),
                   jax.ShapeDtypeStruct((cfg.n_seq, npair, HEAD, PAIR), F32)],
        scratch_shapes=[pltpu.VMEM((CHUNK + 8, 3 * rw), F32), pltpu.VMEM((CHUNK + 8, lp), F32),
                        row_scr, row_scr, row_scr, row_scr, row_scr,
                        pltpu.VMEM((npair * HEAD, 2 * PAIR), BF16), pltpu.VMEM((npair * HEAD, PAIR), F32),
                        pltpu.VMEM((npair * HEAD, PAIR), F32), pltpu.VMEM((npair, 2 * PAIR, PAIR), BF16)],
        compiler_params=_params("arbitrary"),
        name="rwkv",
    )(proj, proj, shift8, wkv_pairs, mu, w0, wup, a0, aup, gup, kk, ka, rk, lng, lnb)


def _outproj_kernel(ym_ref, yr_ref, wa_ref, wb_ref, xp_ref, xs_ref, gate_ref, o_ref, *, cfg, tm):
    i = pl.program_id(0)
    acc = (jnp.dot(ym_ref[...], wa_ref[...], preferred_element_type=F32)
           + jnp.dot(yr_ref[...], wb_ref[...], preferred_element_type=F32))

    def residual(x_ref):
        for s in range(tm // CHUNK):
            seq = _seq_of_chunk(i * (tm // CHUNK) + s, cfg)
            rows = slice(s * CHUNK, (s + 1) * CHUNK)
            o_ref[rows, :] = x_ref[rows, :] + gate_ref[pl.ds(seq, 1), :] * acc[rows, :]

    npt = _prompt_tiles(cfg, tm)
    pl.when(i < npt)(lambda: residual(xp_ref))
    pl.when(i >= npt)(lambda: residual(xs_ref))


def _outproj(ym, yr, wa, wb, x_p, x_s, gate, cfg, tm=512, tn=1024):
    n, d = cfg.n_rows, x_p.shape[1]
    ka, kb = ym.shape[1], yr.shape[1]
    s = gate.shape[0]
    return pl.pallas_call(
        functools.partial(_outproj_kernel, cfg=cfg, tm=tm),
        grid=(n // tm, d // tn),
        in_specs=[pl.BlockSpec((tm, ka), lambda i, j: (i, 0)),
                  pl.BlockSpec((tm, kb), lambda i, j: (i, 0)),
                  pl.BlockSpec((ka, tn), lambda i, j: (0, j)),
                  pl.BlockSpec((kb, tn), lambda i, j: (0, j))]
                 + _row_specs(cfg, tm, tn, lambda j: j)
                 + [pl.BlockSpec((s, tn), lambda i, j: (0, j))],
        out_specs=pl.BlockSpec((tm, tn), lambda i, j: (i, j)),
        out_shape=jax.ShapeDtypeStruct((n, d), F32),
        compiler_params=_params("parallel", "parallel"),
        name="outproj",
    )(ym, yr, wa, wb, x_p, x_s, gate)


def _route(lg, n_experts):
    lane = lax.broadcasted_iota(jnp.int32, lg.shape, 1)
    first = lambda mask: jnp.min(jnp.where(mask, lane, LANES), axis=-1, keepdims=True)
    is_grp = lane < E_GROUPS
    gmax = jnp.max(jnp.where(is_grp, lg, -jnp.inf), axis=-1, keepdims=True)
    gsel = first(is_grp & (lg == gmax))
    pg = 1.0 / jnp.sum(jnp.where(is_grp, jnp.exp(lg - gmax), 0.0), axis=-1, keepdims=True)
    e_lane = lane - E_GROUPS
    in_grp = (e_lane >= 0) & (e_lane < n_experts) & (e_lane // E_PER_GROUP == gsel)
    emax = jnp.max(jnp.where(in_grp, lg, -jnp.inf), axis=-1, keepdims=True)
    p = jnp.where(in_grp, jnp.exp(lg - emax), 0.0)
    eprob = p / jnp.sum(p, axis=-1, keepdims=True)
    v1 = jnp.max(jnp.where(in_grp, eprob, -1.0), axis=-1, keepdims=True)
    i1 = first(in_grp & (eprob == v1))
    rest = in_grp & (lane != i1)
    v2 = jnp.max(jnp.where(rest, eprob, -1.0), axis=-1, keepdims=True)
    i2 = first(rest & (eprob == v2))
    tot = v1 + v2
    cols = [(i1 - E_GROUPS).astype(F32), (i2 - E_GROUPS).astype(F32), v1 / tot * pg, v2 / tot * pg]
    out = jnp.zeros(lg.shape, F32)
    for c, val in enumerate(cols):
        out = jnp.where(lane == c, val, out)
    return out


def _router_kernel(x_ref, shift_ref, scale_ref, g_ref, wh_ref, wl_ref, br_ref, h_ref, rt_ref, *, cfg, tm, n_experts):
    i = pl.program_id(0)
    for s in range(tm // CHUNK):
        seq = _seq_of_chunk(i * (tm // CHUNK) + s, cfg)
        rows = slice(s * CHUNK, (s + 1) * CHUNK)
        h = _modulated_norm(x_ref[rows, :], g_ref[...], scale_ref[pl.ds(seq, 1), :], shift_ref[pl.ds(seq, 1), :])
        h_hi, h_lo = _split_hi_lo(h)
        h_ref[rows, :] = h_hi
        dot = lambda a, b: jnp.dot(a, b[...], preferred_element_type=F32)
        logits = dot(h_hi, wh_ref) + (dot(h_hi, wl_ref) + dot(h_lo, wh_ref)) + br_ref[...]
        rt_ref[rows, :] = _route(logits, n_experts)


def _router(x1, shift, scale, g, wr, br, cfg, n_experts, tm=256):
    n, d = x1.shape
    assert E_GROUPS + n_experts <= LANES and n_experts == E_GROUPS * E_PER_GROUP
    full = lambda a: pl.BlockSpec(a.shape, lambda i: (0,) * a.ndim)
    w_hi, w_lo = _split_hi_lo(wr)
    return pl.pallas_call(
        functools.partial(_router_kernel, cfg=cfg, tm=tm, n_experts=n_experts),
        grid=(n // tm,),
        in_specs=[pl.BlockSpec((tm, d), lambda i: (i, 0)), full(shift), full(scale), full(g),
                  full(w_hi), full(w_lo), full(br)],
        out_specs=[pl.BlockSpec((tm, d), lambda i: (i, 0)), pl.BlockSpec((tm, LANES), lambda i: (i, 0))],
        out_shape=[jax.ShapeDtypeStruct((n, d), BF16), jax.ShapeDtypeStruct((n, LANES), F32)],
        compiler_params=_params("parallel"),
        name="router",
    )(x1, shift, scale, g, w_hi, w_lo, br)


def _new_weights(be_ref):
    i = pl.program_id(1)
    return jnp.logical_or(i == 0, be_ref[i] != be_ref[jnp.maximum(i - 1, 0)])


def _for_valid_rows(nv_ref, out_ref, compute):
    nv = nv_ref[pl.program_id(1)]
    for s in range(out_ref.shape[0] // E_SUB):
        rows = slice(s * E_SUB, (s + 1) * E_SUB)

        @pl.when(nv > s * E_SUB)
        def _():
            out_ref[rows, :] = compute(rows)

        @pl.when(nv <= s * E_SUB)
        def _():
            out_ref[rows, :] = jnp.zeros((E_SUB, out_ref.shape[1]), out_ref.dtype)


def _expert_up_kernel(be_ref, nv_ref, x_ref, wg_ref, wu_ref, h_ref, wg_s, wu_s):
    @pl.when(_new_weights(be_ref))
    def _():
        wg_s[...] = wg_ref[0].astype(BF16)
        wu_s[...] = wu_ref[0].astype(BF16)

    def hidden(rows):
        x = x_ref[rows, :]
        gate = jnp.dot(x, wg_s[...], preferred_element_type=F32)
        up = jnp.dot(x, wu_s[...], preferred_element_type=F32)
        return (_silu(gate) * up).astype(BF16)

    _for_valid_rows(nv_ref, h_ref, hidden)


def _expert_down_kernel(be_ref, nv_ref, h_ref, wd_ref, y_ref, wd_s):
    @pl.when(_new_weights(be_ref))
    def _():
        wd_s[...] = wd_ref[0].astype(BF16)

    _for_valid_rows(nv_ref, y_ref, lambda rows: jnp.dot(h_ref[rows, :], wd_s[...], preferred_element_type=F32))


def _experts(xb, blk_e, n_valid, wg, wu, wd, tm, tf=512, tn=2048):
    rows, d = xb.shape
    f = wg.shape[2]
    nblk = rows // tm
    tf, tn = min(tf, f), min(tn, d)
    assert tm % E_SUB == 0 and f % tf == 0 and d % tn == 0
    hid = pl.pallas_call(
        _expert_up_kernel,
        grid_spec=pltpu.PrefetchScalarGridSpec(
            num_scalar_prefetch=2, grid=(f // tf, nblk),
            in_specs=[pl.BlockSpec((tm, d), lambda j, i, be, nv: (i, 0)),
                      pl.BlockSpec((1, d, tf), lambda j, i, be, nv: (be[i], 0, j)),
                      pl.BlockSpec((1, d, tf), lambda j, i, be, nv: (be[i], 0, j))],
            out_specs=pl.BlockSpec((tm, tf), lambda j, i, be, nv: (i, j)),
            scratch_shapes=[pltpu.VMEM((d, tf), BF16), pltpu.VMEM((d, tf), BF16)]),
        out_shape=jax.ShapeDtypeStruct((rows, f), BF16),
        compiler_params=_params("arbitrary", "arbitrary"),
        name="expert_up",
    )(blk_e, n_valid, xb, wg, wu)
    return pl.pallas_call(
        _expert_down_kernel,
        grid_spec=pltpu.PrefetchScalarGridSpec(
            num_scalar_prefetch=2, grid=(d // tn, nblk),
            in_specs=[pl.BlockSpec((tm, f), lambda j, i, be, nv: (i, 0)),
                      pl.BlockSpec((1, f, tn), lambda j, i, be, nv: (be[i], 0, j))],
            out_specs=pl.BlockSpec((tm, tn), lambda j, i, be, nv: (i, j)),
            scratch_shapes=[pltpu.VMEM((f, tn), BF16)]),
        out_shape=jax.ShapeDtypeStruct((rows, d), F32),
        compiler_params=_params("arbitrary", "arbitrary"),
        name="expert_down",
    )(blk_e, n_valid, hid, wd)


def _final_kernel(pos_ref, x_ref, rt_ref, gate_ref, g_ref, y_hbm, o_ref, ybuf, sem, *, cfg, tm, tile0, ntiles):
    i = pl.program_id(0)

    def row_copy(pos, slot, k, r):
        return pltpu.make_async_copy(y_hbm.at[pl.ds(pos, 1), :], ybuf.at[slot, k, pl.ds(r, 1), :], sem.at[slot])

    def fetch(tile, slot):
        base = (tile + tile0) * (tm * TOP_K)

        def body(r, c):
            for k in range(TOP_K):
                row_copy(pos_ref[base + r * TOP_K + k], slot, k, r).start()
            return c

        lax.fori_loop(0, tm, body, 0, unroll=8)

    def wait_all(slot):
        def body(r, c):
            for k in range(TOP_K):
                row_copy(0, slot, k, r).wait()
            return c

        lax.fori_loop(0, tm, body, 0, unroll=8)

    pl.when(i == 0)(lambda: fetch(0, 0))
    pl.when(i + 1 < ntiles)(lambda: fetch(i + 1, (i + 1) % 2))
    slot = i % 2
    wait_all(slot)

    for s in range(tm // CHUNK):
        seq = _seq_of_chunk((i + tile0) * (tm // CHUNK) + s, cfg)
        rows = slice(s * CHUNK, (s + 1) * CHUNK)
        rt = rt_ref[rows, :]
        moe = ybuf[slot, 0, rows, :] * rt[:, TOP_K:TOP_K + 1]
        for k in range(1, TOP_K):
            moe = moe + ybuf[slot, k, rows, :] * rt[:, TOP_K + k:TOP_K + k + 1]
        x = x_ref[rows, :] + gate_ref[pl.ds(seq, 1), :] * moe
        o_ref[rows, :] = x * lax.rsqrt(jnp.mean(x * x, axis=-1, keepdims=True) + NORM_EPS) * g_ref[...]


def _final(x1, yb, slot_pos, route, gate, g, cfg, row0, nrows, tm=256):
    d = x1.shape[1]
    assert row0 % tm == 0 and nrows % tm == 0
    tile0 = row0 // tm
    ntiles = nrows // tm
    full = lambda a: pl.BlockSpec(a.shape, lambda i, pos: (0,) * a.ndim)
    return pl.pallas_call(
        functools.partial(_final_kernel, cfg=cfg, tm=tm, tile0=tile0, ntiles=ntiles),
        grid_spec=pltpu.PrefetchScalarGridSpec(
            num_scalar_prefetch=1, grid=(ntiles,),
            in_specs=[pl.BlockSpec((tm, d), lambda i, pos: (i + tile0, 0)),
                      pl.BlockSpec((tm, LANES), lambda i, pos: (i + tile0, 0)),
                      full(gate), full(g), pl.BlockSpec(memory_space=pl.ANY)],
            out_specs=pl.BlockSpec((tm, d), lambda i, pos: (i, 0)),
            scratch_shapes=[pltpu.VMEM((2, TOP_K, tm, d), F32), pltpu.SemaphoreType.DMA((2,))]),
        out_shape=jax.ShapeDtypeStruct((nrows, d), F32),
        compiler_params=_params("arbitrary"),
        name="final_norm",
    )(slot_pos.reshape(-1), x1, route, gate, g, yb)


def _dispatch(route, n_experts, tm):
    n = route.shape[0]
    eid = route[:, :TOP_K].astype(jnp.int32)

    m = n * TOP_K
    e_flat = eid.reshape(m)
    order = jnp.argsort(e_flat).astype(jnp.int32)
    rank = jnp.argsort(order).astype(jnp.int32)
    e_sorted = e_flat[order]
    experts = jnp.arange(n_experts, dtype=jnp.int32)
    start = jnp.searchsorted(e_sorted, experts, side='left').astype(jnp.int32)
    counts = jnp.searchsorted(e_sorted, experts, side='right').astype(jnp.int32) - start
    padded = (counts + tm - 1) // tm * tm
    pend = jnp.cumsum(padded)
    pstart = pend - padded
    dest_sorted = pstart[e_sorted] + jnp.arange(m, dtype=jnp.int32) - start[e_sorted]
    slot_pos = dest_sorted[rank].reshape(n, TOP_K)
    n_blocks = -(-(m + n_experts * (tm - 1)) // tm)
    rows = n_blocks * tm
    r = jnp.arange(rows, dtype=jnp.int32)
    e_r = jnp.minimum(jnp.searchsorted(pend, r, side='right'), n_experts - 1)
    idx = r - pstart[e_r]
    row_tok = jnp.where(idx < counts[e_r], order[jnp.clip(start[e_r] + idx, 0, m - 1)] // TOP_K, r % n)
    n_used = pend[-1] // tm
    blk = jnp.arange(n_blocks, dtype=jnp.int32)
    blk_e = jnp.minimum(jnp.searchsorted(pend, blk * tm, side='right'), n_experts - 1).astype(jnp.int32)
    n_valid = jnp.where(blk < n_used, jnp.clip(pstart[blk_e] + counts[blk_e] - blk * tm, 0, tm), 0).astype(jnp.int32)
    last_e = blk_e[jnp.maximum(n_used - 1, 0)]
    blk_e = jnp.where(blk < n_used, blk_e, last_e)
    return row_tok, slot_pos, blk_e, n_valid


def _to_window(a, cfg):
    return jnp.concatenate([a, jnp.zeros(a.shape[:-1] + (cfg.win - cfg.r_cols,), a.dtype)], axis=-1)


def _pad_rows(a, rows):
    return jnp.concatenate([a, jnp.zeros((rows - a.shape[0],) + a.shape[1:], a.dtype)], axis=0)


def kernel(x_prompt, x_sample, state_conv, state_ssm, state_shift, state_wkv, c_prompt, c_sample, norm1_g, w_mod, b_mod, w_in, conv_w, conv_b, dt_bias, a_log, d_skip, m_norm_g, shift_mu, w0, w_up, a0, a_up, g_up, k_k, k_a, r_k, ln_x_g, ln_x_b, w_out, norm2_g, w_grp, b_grp, w_erouter, b_erouter, e_gate, e_up, e_down, final_norm_g):
    assert w_mod.shape[0] == 1, "single-layer trunk"
    bp, tp, d = x_prompt.shape
    bs, ts, _ = x_sample.shape
    mi = m_norm_g.shape[-1]
    rw = w0.shape[-1]
    lw, la, lg = w_up.shape[1], a_up.shape[1], g_up.shape[1]
    cfg = Cfg(d, bp, tp, bs, ts, mi, rw, lw, la, lg)
    nc = cfg.nc
    assert tp % CHUNK == 0 and ts % CHUNK == 0 and tp >= CONV_W - 1 and ts >= CONV_W - 1
    assert w_in.shape[-1] == cfg.o_rw + cfg.mh + cfg.r_cols and cfg.o_rw % LANES == 0
    assert state_conv.shape[-1] == cfg.cd and lw <= LANES and la <= LANES and cfg.mh <= LANES
    n = cfg.n_rows
    n_seq = cfg.n_seq
    mh, rh = cfg.mh, cfg.rh
    npair = rw // PAIR
    n_experts = e_gate.shape[1]

    x_p = x_prompt.reshape(bp * tp, d)
    x_s = x_sample.reshape(bs * ts, d)
    c_all = jnp.concatenate([c_prompt, c_sample], axis=0)

    mod = _modulation(c_all, w_mod[0], b_mod[0])
    shift1, scale1, gate1, shift2, scale2, gate2 = [mod[:, i * d:(i + 1) * d] for i in range(6)]

    h1 = _prenorm(x_p, x_s, shift1, scale1, norm1_g, cfg)
    w_in_t = w_in[0].T
    proj = _inproj(h1, w_in_t, cfg)
    dt_raw = _dtproj(h1, w_in_t, cfg)

    zeros = lambda b, *s: jnp.zeros((b,) + s, F32)
    conv0 = jnp.concatenate([zeros(bp, CONV_W - 1, cfg.cd), state_conv[0]], axis=0)
    conv8 = jnp.concatenate([zeros(n_seq, 8 - (CONV_W - 1), cfg.cd), conv0], axis=1)
    nst = state_ssm.shape[-1]
    ssm0 = jnp.concatenate([zeros(bp, mi, nst), state_ssm[0].reshape(bs, mi, nst)], axis=0)
    lane_pad = lambda a: jnp.concatenate([a.reshape(1, -1), jnp.zeros((1, LANES - a.shape[-1]), F32)], axis=1)
    expand = ((jnp.arange(2 * LANES)[:, None] % LANES) == (jnp.arange(mi)[None, :] // HEAD)).astype(BF16)
    ym, ssm_new = _mamba(proj, dt_raw, conv8, ssm0, conv_w[0], conv_b[0].reshape(1, -1), lane_pad(dt_bias[0]),
                         lane_pad(a_log[0]), jnp.repeat(d_skip[0], HEAD).reshape(1, mi), m_norm_g[0].reshape(1, mi),
                         expand, cfg)

    sh0 = jnp.concatenate([zeros(bp, 1, state_shift.shape[-1]), state_shift[0]], axis=0)
    sh8 = jnp.concatenate([zeros(n_seq, 7, cfg.win), _to_window(sh0, cfg)], axis=1)
    to_pairs = lambda s: s.reshape(-1, npair, 2, HEAD, HEAD).transpose(0, 1, 3, 2, 4).reshape(-1, npair, HEAD, PAIR)
    from_pairs = lambda s: s.reshape(-1, npair, HEAD, 2, HEAD).transpose(0, 1, 3, 2, 4).reshape(-1, rh, HEAD, HEAD)
    wkv0 = jnp.concatenate([zeros(bp, npair, HEAD, PAIR), to_pairs(state_wkv[0])], axis=0)
    row = lambda a: a.reshape(1, -1)
    yr, wkv_new = _rwkv(proj, sh8, wkv0, _to_window(row(shift_mu[0]), cfg), row(w0[0]),
                        _pad_rows(w_up[0], LANES), row(a0[0]), _pad_rows(a_up[0], LANES), g_up[0],
                        row(k_k[0]), row(k_a[0]), row(r_k[0]), row(ln_x_g[0]), row(ln_x_b[0]), cfg)

    wo = w_out[0].astype(BF16)
    x1 = _outproj(ym, yr, wo[:mi], wo[mi:], x_p, x_s, gate1, cfg)
    wr = jnp.concatenate([w_grp[0], w_erouter[0], jnp.zeros((d, LANES - E_GROUPS - n_experts), F32)], axis=1)
    br = lane_pad(jnp.concatenate([b_grp[0], b_erouter[0]]))
    h2, route = _router(x1, shift2, scale2, norm2_g, wr, br, cfg, n_experts)

    tm_e = 2 * E_SUB
    row_tok, slot_pos, blk_e, n_valid = _dispatch(route, n_experts, tm_e)
    yb = _experts(h2[row_tok], blk_e, n_valid, e_gate[0], e_up[0], e_down[0], tm_e)

    np_ = bp * tp
    fg = final_norm_g.reshape(1, d)
    y_prompt = _final(x1, yb, slot_pos, route, gate2, fg, cfg, 0, np_).reshape(bp, tp, d)
    y_sample = _final(x1, yb, slot_pos, route, gate2, fg, cfg, np_, bs * ts).reshape(bs, ts, d)
    tails = lambda b, t, base: jnp.stack(
        [lax.slice(proj, (base + (i + 1) * t - (CONV_W - 1), 0), (base + (i + 1) * t, nc)) for i in range(b)])
    pp = tails(bp, tp, 0)
    ps = tails(bs, ts, np_)
    conv_of = lambda p: p[:, :, mi:mi + cfg.cd][None]
    shift_of = lambda p: p[:, -1:, cfg.o_rw:cfg.o_rw + cfg.r_cols][None]
    ssm_new = ssm_new.reshape(n_seq, mh, HEAD, nst)
    wkv_new = from_pairs(wkv_new)
    return (y_prompt, y_sample,
            conv_of(pp), ssm_new[:bp][None], shift_of(pp), wkv_new[:bp][None],
            conv_of(ps), ssm_new[bp:][None], shift_of(ps), wkv_new[bp:][None])
```

```python
import functools
from typing import NamedTuple

import jax
import jax.numpy as jnp
from jax import lax
from jax.experimental import pallas as pl
from jax.experimental.pallas import tpu as pltpu

F32 = jnp.float32
BF16 = jnp.bfloat16
HI = lax.Precision.HIGHEST

LANES = 128
CHUNK = 64
HEAD = 64
PAIR = 2 * HEAD
NORM_EPS = 1e-6
M_NORM_EPS = 1e-5
R_LN_EPS = 64e-5
CONV_W = 4
E_GROUPS = 4
E_PER_GROUP = 8
TOP_K = 2
VMEM_LIMIT = 56 * 1024 * 1024
E_SUB = 256
EXP_NEG_HALF = 0.6065306597126334
SA_GROUP = 16


class Cfg(NamedTuple):
    d: int
    bp: int
    tp: int
    bs: int
    ts: int
    mi: int
    rw: int
    lw: int
    la: int
    lg: int

    @property
    def cpp(self):
        return self.tp // CHUNK

    @property
    def cps(self):
        return self.ts // CHUNK

    @property
    def n_chunks(self):
        return self.bp * self.cpp + self.bs * self.cps

    @property
    def n_seq(self):
        return self.bp + self.bs

    @property
    def n_rows(self):
        return self.n_chunks * CHUNK

    @property
    def groups(self):
        return self.mi // 256

    @property
    def nb(self):
        return self.groups * LANES

    @property
    def cd(self):
        return self.mi + 2 * self.nb

    @property
    def mh(self):
        return self.mi // HEAD

    @property
    def rh(self):
        return self.rw // HEAD

    @property
    def o_rw(self):
        return self.mi + self.cd

    @property
    def r_cols(self):
        return 3 * self.rw + self.lw + self.la + self.lg

    @property
    def tail(self):
        return -(-(self.lw + self.la + self.lg) // LANES) * LANES

    @property
    def win(self):
        return 3 * self.rw + self.tail

    @property
    def nc(self):
        return self.o_rw + self.win


def _seq_of_chunk(c, cfg):
    npc = cfg.bp * cfg.cpp
    return jnp.where(c < npc, c // cfg.cpp, cfg.bp + (c - npc) // cfg.cps)


def _is_first_chunk(c, cfg):
    npc = cfg.bp * cfg.cpp
    return jnp.where(c < npc, c % cfg.cpp == 0, (c - npc) % cfg.cps == 0)


def _silu(x):
    return x * jax.nn.sigmoid(x)


def _nt(a, b, **kw):
    return lax.dot_general(a, b, (((1,), (1,)), ((), ())), preferred_element_type=F32, **kw)


def _tn(a, b, **kw):
    return lax.dot_general(a, b, (((0,), (0,)), ((), ())), preferred_element_type=F32, **kw)


def _params(*sem):
    return pltpu.CompilerParams(dimension_semantics=sem, vmem_limit_bytes=VMEM_LIMIT)


def _mod_kernel(c_ref, w_ref, b_ref, o_ref):
    o_ref[...] = jnp.dot(_silu(c_ref[...]), w_ref[...], preferred_element_type=F32) + b_ref[...]


def _modulation(c_all, w_mod, b_mod):
    s, d = c_all.shape
    cols = w_mod.shape[1]
    tn = 512
    return pl.pallas_call(
        _mod_kernel,
        grid=(cols // tn,),
        in_specs=[pl.BlockSpec((s, d), lambda j: (0, 0)),
                  pl.BlockSpec((d, tn), lambda j: (0, j)),
                  pl.BlockSpec((1, tn), lambda j: (0, j))],
        out_specs=pl.BlockSpec((s, tn), lambda j: (0, j)),
        out_shape=jax.ShapeDtypeStruct((s, cols), F32),
        compiler_params=_params("parallel"),
        name="modulation",
    )(c_all, w_mod, b_mod.reshape(1, cols))


def _modulated_norm(x, g, scale, shift):
    y = x * lax.rsqrt(jnp.mean(x * x, axis=-1, keepdims=True) + NORM_EPS) * g
    return y * (1.0 + scale) + shift


def _prompt_tiles(cfg, tm):
    assert (cfg.bp * cfg.tp) % tm == 0 and (cfg.bs * cfg.ts) % tm == 0 and tm % CHUNK == 0
    return cfg.bp * cfg.tp // tm


def _row_specs(cfg, tm, width, col=None):
    npt = _prompt_tiles(cfg, tm)
    c = (lambda *r: 0) if col is None else col
    return [pl.BlockSpec((tm, width), lambda i, *r: (jnp.minimum(i, npt - 1), jnp.where(i < npt, c(*r), 0))),
            pl.BlockSpec((tm, width), lambda i, *r: (jnp.maximum(i - npt, 0), jnp.where(i >= npt, c(*r), 0)))]


def _prenorm_kernel(xp_ref, xs_ref, shift_ref, scale_ref, g_ref, h_ref, *, cfg, tm):
    i = pl.program_id(0)

    def norm_rows(x_ref):
        for s in range(tm // CHUNK):
            seq = _seq_of_chunk(i * (tm // CHUNK) + s, cfg)
            rows = slice(s * CHUNK, (s + 1) * CHUNK)
            h = _modulated_norm(x_ref[rows, :], g_ref[...], scale_ref[pl.ds(seq, 1), :], shift_ref[pl.ds(seq, 1), :])
            h_ref[rows, :] = h.astype(BF16)

    npt = _prompt_tiles(cfg, tm)
    pl.when(i < npt)(lambda: norm_rows(xp_ref))
    pl.when(i >= npt)(lambda: norm_rows(xs_ref))


def _prenorm(x_p, x_s, shift, scale, g, cfg, tm=256):
    d = x_p.shape[1]
    full = lambda a: pl.BlockSpec(a.shape, lambda i: (0,) * a.ndim)
    return pl.pallas_call(
        functools.partial(_prenorm_kernel, cfg=cfg, tm=tm),
        grid=(cfg.n_rows // tm,),
        in_specs=_row_specs(cfg, tm, d) + [full(shift), full(scale), full(g)],
        out_specs=pl.BlockSpec((tm, d), lambda i: (i, 0)),
        out_shape=jax.ShapeDtypeStruct((cfg.n_rows, d), BF16),
        compiler_params=_params("parallel"),
        name="prenorm",
    )(x_p, x_s, shift, scale, g)


def _inproj_kernel(h_ref, w_ref, wn_ref, o_ref, wb_s, *, first_shifted, shift, tn, n_features):
    j = pl.program_id(0)
    new_tile = pl.program_id(1) == 0

    @pl.when(jnp.logical_and(new_tile, j < first_shifted))
    def _():
        wb_s[...] = w_ref[...].astype(BF16)

    @pl.when(jnp.logical_and(new_tile, j >= first_shifted))
    def _():
        w = jnp.concatenate([w_ref[...], wn_ref[...]], axis=0)[shift:shift + tn, :]
        feature = j * tn + shift + lax.broadcasted_iota(jnp.int32, w.shape, 0)
        wb_s[...] = jnp.where(feature < n_features, w, 0.0).astype(BF16)

    o_ref[...] = _nt(h_ref[...], wb_s[...])


def _inproj(h, wt, cfg, tm=1024, tn=512):
    n, d = h.shape
    nout = cfg.o_rw + cfg.win
    assert cfg.o_rw % tn == 0 and nout % tn == 0 and n % tm == 0 and cfg.mh <= LANES and cfg.mh % 8 == 0
    last_next = (wt.shape[0] - 1) // LANES
    return pl.pallas_call(
        functools.partial(_inproj_kernel, first_shifted=cfg.o_rw // tn, shift=cfg.mh, tn=tn, n_features=wt.shape[0]),
        grid=(nout // tn, n // tm),
        in_specs=[pl.BlockSpec((tm, d), lambda j, i: (i, 0)),
                  pl.BlockSpec((tn, d), lambda j, i: (j, 0)),
                  pl.BlockSpec((LANES, d), lambda j, i: (jnp.minimum((j + 1) * (tn // LANES), last_next), 0))],
        out_specs=pl.BlockSpec((tm, tn), lambda j, i: (i, j)),
        out_shape=jax.ShapeDtypeStruct((n, nout), F32),
        scratch_shapes=[pltpu.VMEM((tn, d), BF16)],
        compiler_params=_params("arbitrary", "arbitrary"),
        name="inproj",
    )(h, wt, wt)


def _dtproj_kernel(h_ref, w_ref, o_ref):
    o_ref[...] = _nt(h_ref[...], w_ref[...].astype(BF16))


def _dtproj(h, wt, cfg, tm=1024):
    n, d = h.shape
    return pl.pallas_call(
        _dtproj_kernel,
        grid=(n // tm,),
        in_specs=[pl.BlockSpec((tm, d), lambda i: (i, 0)), pl.BlockSpec((LANES, d), lambda i: (cfg.o_rw // LANES, 0))],
        out_specs=pl.BlockSpec((tm, LANES), lambda i: (i, 0)),
        out_shape=jax.ShapeDtypeStruct((n, LANES), F32),
        compiler_params=_params("parallel"),
        name="dtproj",
    )(h, wt)


def _mamba_kernel(zx_ref, dt_ref, cst_ref, sst_ref, cw_ref, cb_ref, dtb_ref, alog_ref, dsk_ref, ng_ref, exp_ref,
                  ym_ref, ssm_ref, buf, u_scr, *, cfg):
    c = pl.program_id(0)
    mi, nb, cd = cfg.mi, cfg.nb, cfg.cd
    L = CHUNK

    @pl.when(_is_first_chunk(c, cfg))
    def _():
        buf[0:8, :] = cst_ref[0]
        ssm_ref[0] = sst_ref[0]

    xbc = zx_ref[:, mi:mi + cd]
    buf[8:8 + L, :] = xbc
    conv = (cb_ref[...] + buf[5:5 + L, :] * cw_ref[0:1, :] + buf[6:6 + L, :] * cw_ref[1:2, :]
            + buf[7:7 + L, :] * cw_ref[2:3, :] + xbc * cw_ref[3:4, :])
    buf[0:8, :] = buf[L:L + 8, :]
    u_scr[...] = _silu(conv)

    dt = jax.nn.softplus(dt_ref[...] + dtb_ref[...])
    da = dt * (-jnp.exp(alog_ref[...]))
    row = lax.broadcasted_iota(jnp.int32, (L, 256), 0)
    pos = lax.broadcasted_iota(jnp.int32, (L, 256), 1) % HEAD
    causal = row >= pos
    diag = row == pos
    tri2 = (lax.broadcasted_iota(jnp.int32, (L, 2 * L), 0)
            >= lax.broadcasted_iota(jnp.int32, (L, 2 * L), 1) % L).astype(BF16)
    lane = lax.broadcasted_iota(jnp.int32, (L, PAIR), 1)
    row16 = lax.broadcasted_iota(jnp.int32, (16, 256), 0)
    ones16 = jnp.ones((16, LANES), BF16)
    dt_hl = jnp.concatenate(_split_hi_lo(dt), axis=1)
    da_hl = jnp.concatenate(_split_hi_lo(da), axis=1)

    for g in range(cfg.groups):
        cs = slice(g * 256, (g + 1) * 256)
        xh = u_scr[:, cs]
        bg = u_scr[:, mi + g * LANES:mi + (g + 1) * LANES]
        cg = u_scr[:, mi + nb + g * LANES:mi + nb + (g + 1) * LANES]
        eg = exp_ref[:, cs]
        dte = jnp.dot(dt_hl, eg, preferred_element_type=F32)
        dae = jnp.dot(da_hl, eg, preferred_element_type=F32)
        cum = jnp.dot(tri2, jnp.concatenate(_split_hi_lo(dae), axis=0), preferred_element_type=F32)
        cum_row = jnp.sum(jnp.where(diag, cum, 0.0), axis=0, keepdims=True)
        last = cum[L - 1:L, :]
        decay = jnp.exp(jnp.where(causal, cum - cum_row, -jnp.inf))
        xdt = xh * dte
        sc2 = _nt(cg, jnp.concatenate([bg, bg], axis=0))
        ys = []
        for q in range(2):
            ps = slice(q * PAIR, (q + 1) * PAIR)
            xq = xdt[:, ps]
            rhs = jnp.concatenate([jnp.where(lane < HEAD, xq, 0.0), jnp.where(lane >= HEAD, xq, 0.0)], axis=0)
            ys.append(jnp.dot(decay[:, ps] * sc2, rhs, preferred_element_type=F32))
        y = jnp.concatenate(ys, axis=1)
        sg = ssm_ref[0, cs, :]
        y = y + _nt(cg, sg) * jnp.exp(cum)
        new = _tn(xdt * jnp.exp(last - cum), bg)
        e_last = jnp.exp(last)
        e_hi = e_last.astype(BF16).astype(F32)
        e_rows = jnp.where(row16 == 0, e_hi, jnp.where(row16 == 1, e_last - e_hi, 0.0)).astype(BF16)
        dcol = _tn(e_rows, ones16)
        ssm_ref[0, cs, :] = sg * dcol + new
        y = y + xh * dsk_ref[:, cs]
        y = y * _silu(zx_ref[:, cs])
        y = y * lax.rsqrt(jnp.mean(y * y, axis=-1, keepdims=True) + M_NORM_EPS)
        ym_ref[:, cs] = (y * ng_ref[:, cs]).astype(BF16)


def _mamba(proj, dt_raw, conv_state8, ssm_state, cw, cb, dtb, alog, dsk, ng, expand, cfg):
    n = proj.shape[0]
    mi, cd = cfg.mi, cfg.cd
    nst = ssm_state.shape[-1]
    seq = lambda c: _seq_of_chunk(c, cfg)
    full = lambda a: pl.BlockSpec(a.shape, lambda c: (0,) * a.ndim)
    return pl.pallas_call(
        functools.partial(_mamba_kernel, cfg=cfg),
        grid=(cfg.n_chunks,),
        in_specs=[pl.BlockSpec((CHUNK, mi + cd), lambda c: (c, 0)),
                  pl.BlockSpec((CHUNK, LANES), lambda c: (c, 0)),
                  pl.BlockSpec((1, 8, cd), lambda c: (seq(c), 0, 0)),
                  pl.BlockSpec((1, mi, nst), lambda c: (seq(c), 0, 0)),
                  full(cw), full(cb), full(dtb), full(alog), full(dsk), full(ng), full(expand)],
        out_specs=[pl.BlockSpec((CHUNK, mi), lambda c: (c, 0)),
                   pl.BlockSpec((1, mi, nst), lambda c: (seq(c), 0, 0))],
        out_shape=[jax.ShapeDtypeStruct((n, mi), BF16),
                   jax.ShapeDtypeStruct((cfg.n_seq, mi, nst), F32)],
        scratch_shapes=[pltpu.VMEM((CHUNK + 8, cd), F32), pltpu.VMEM((CHUNK, cd), F32)],
        compiler_params=_params("arbitrary"),
        name="mamba",
    )(proj, dt_raw, conv_state8, ssm_state, cw, cb, dtb, alog, dsk, ng, expand)


def _seg_sum(x, ob):
    w = ob.shape[0]
    xb = x.astype(BF16)
    parts = [jnp.dot(xb[:, p * w:(p + 1) * w], ob, preferred_element_type=F32) for p in range(x.shape[1] // w)]
    return jnp.concatenate(parts, axis=1)


def _split_hi_lo(x):
    hi = x.astype(BF16)
    lo = (x - hi.astype(F32)).astype(BF16)
    return hi, lo


def _rwkv_kernel(rkv_ref, lora_ref, sh_ref, wkv_in_ref, mu_ref, w0_ref, wup_ref, a0_ref, aup_ref, gup_ref,
                 kk_ref, ka_ref, rk_ref, lng_ref, lnb_ref,
                 yr_ref, wkv_ref, pbuf, lbuf, w_s, a_s, b_s, k_s, y_s, vt_s, ut_s, y0_s, g_s, *, cfg):
    c = pl.program_id(0)
    rw = cfg.rw
    L = CHUNK
    npair = rw // PAIR

    @pl.when(_is_first_chunk(c, cfg))
    def _():
        pbuf[0:8, :] = sh_ref[0, :, 0:3 * rw]
        lbuf[0:8, :] = sh_ref[0, :, 3 * rw:]
        wkv_ref[0] = wkv_in_ref[0]

    pr = rkv_ref[...]
    pl_ = lora_ref[...]
    pbuf[8:8 + L, :] = pr
    lbuf[8:8 + L, :] = pl_
    xs = pr + (pbuf[7:7 + L, :] - pr) * mu_ref[:, 0:3 * rw]
    xl = pl_ + (lbuf[7:7 + L, :] - pl_) * mu_ref[:, 3 * rw:]
    pbuf[0:8, :] = pbuf[L:L + 8, :]
    lbuf[0:8, :] = lbuf[L:L + 8, :]

    r = xs[:, 0:rw]
    k = xs[:, rw:2 * rw]
    v = xs[:, 2 * rw:3 * rw]
    wd = xl[:, 0:LANES]
    ad = xl[:, cfg.lw:cfg.lw + LANES]
    gd = xl[:, cfg.lw + cfg.la:cfg.lw + cfg.la + cfg.lg]

    lane = lax.broadcasted_iota(jnp.int32, (PAIR, PAIR), 1)
    rowi = lax.broadcasted_iota(jnp.int32, (PAIR, PAIR), 0)
    o_row = lax.broadcasted_iota(jnp.int32, (2 * PAIR, 2 * PAIR), 0)
    o_lane = lax.broadcasted_iota(jnp.int32, (2 * PAIR, 2 * PAIR), 1)
    ob = ((o_row // HEAD) == (o_lane // HEAD)).astype(BF16)

    w_z = w0_ref[...] + jnp.dot(jnp.tanh(wd), wup_ref[...], preferred_element_type=F32)
    logw = -(jax.nn.sigmoid(w_z) * EXP_NEG_HALF)
    w_s[...] = jnp.exp(logw)
    a = jax.nn.sigmoid(a0_ref[...] + jnp.dot(ad, aup_ref[...], preferred_element_type=F32))
    kk = k * kk_ref[...]
    kk = kk * lax.rsqrt(jnp.maximum(_seg_sum(kk * kk, ob), 1e-24))
    kh = k * (1.0 + (a - 1.0) * ka_ref[...])
    a_s[...] = -kk
    b_s[...] = kk * a
    k_s[...] = kh

    tri2 = (lax.broadcasted_iota(jnp.int32, (L, 2 * L), 0)
            >= lax.broadcasted_iota(jnp.int32, (L, 2 * L), 1) % L).astype(BF16)
    cum = jnp.dot(tri2, jnp.concatenate(_split_hi_lo(logw), axis=0), preferred_element_type=F32)
    p_inv = jnp.exp(-cum)
    r_p = r * jnp.exp(cum)
    b_p = b_s[...] * p_inv
    k_p = kh * p_inv
    lo_lanes = lane < HEAD
    g_row = lax.broadcasted_iota(jnp.int32, (2 * PAIR, PAIR), 0)
    g_lane = lax.broadcasted_iota(jnp.int32, (2 * PAIR, PAIR), 1)
    g_keep = (g_row % HEAD) <= (g_lane % HEAD)

    def by_head(x):
        x2 = jnp.concatenate([x, x], axis=0)
        return jnp.where(lo_lanes == (rowi < HEAD), x2, 0.0).astype(BF16)

    for p in range(npair):
        ps = slice(p * PAIR, (p + 1) * PAIR)
        hs = slice(p * HEAD, (p + 1) * HEAD)
        r_bd = by_head(r_p[:, ps])
        y0_s[hs, :] = _nt(wkv_ref[0, p].astype(BF16), r_bd)
        gram = _nt(jnp.concatenate([by_head(b_p[:, ps]), by_head(k_p[:, ps])], axis=0), r_bd)
        g_s[p] = jnp.where(g_keep, gram, 0.0).astype(BF16)
        vt = v[:, ps].T
        vjt = jnp.concatenate([vt[0:HEAD], vt[HEAD:]], axis=1)
        hi = vjt.astype(BF16).astype(F32)
        vt_s[hs, :] = jnp.concatenate([hi, vjt - hi], axis=1).astype(BF16)

    lane_t = lax.broadcasted_iota(jnp.int32, (HEAD, PAIR), 1) % HEAD
    crow = lax.broadcasted_iota(jnp.int32, (2 * PAIR, 2 * PAIR), 0)
    clane = lax.broadcasted_iota(jnp.int32, (2 * PAIR, 2 * PAIR), 1)
    same_head = ((crow % PAIR) // HEAD) == ((clane % PAIR) // HEAD)
    c_dt = (crow % HEAD) - clane // PAIR

    def steps(blk, carry):
        t0 = pl.multiple_of(blk * 8, 8)
        rows = pl.ds(t0, 8)
        w8, a8, b8, k8 = w_s[rows, :], a_s[rows, :], b_s[rows, :], k_s[rows, :]
        s = [wkv_ref[0, p] for p in range(npair)]
        vcol2 = None
        for i in range(8):
            row = lambda x8, p: x8[i:i + 1, p * PAIR:(p + 1) * PAIR]
            this_step = lane_t == t0 + i
            if i % 2 == 0:
                sel = (same_head & (c_dt == t0 + i)).astype(BF16)
                vcol2 = jnp.dot(vt_s[...], sel, preferred_element_type=F32)
            vcol = vcol2[:, (i % 2) * PAIR:(i % 2 + 1) * PAIR]
            for p0 in range(0, npair, SA_GROUP):
                group = range(p0, min(p0 + SA_GROUP, npair))
                lhs = []
                for p in group:
                    prod = (s[p] * row(a8, p)).astype(BF16)
                    lhs.append(jnp.concatenate([prod[0:HEAD // 2], prod[HEAD // 2:]], axis=1))
                sa = jnp.dot(jnp.concatenate(lhs, axis=0), ob, preferred_element_type=F32)
                for q, p in enumerate(group):
                    half = sa[q * (HEAD // 2):(q + 1) * (HEAD // 2)]
                    sa_p = jnp.concatenate([half[:, 0:PAIR], half[:, PAIR:]], axis=0)
                    s[p] = s[p] * row(w8, p) + sa_p * row(b8, p) + vcol[p * HEAD:(p + 1) * HEAD] * row(k8, p)
                    pltpu.store(ut_s.at[p * HEAD:(p + 1) * HEAD, :], sa_p, mask=this_step)
        for p in range(npair):
            wkv_ref[0, p] = s[p]
        return carry

    ut_s[...] = jnp.zeros_like(ut_s)
    lax.fori_loop(0, L // 8, steps, 0)

    for p in range(npair):
        hs = slice(p * HEAD, (p + 1) * HEAD)
        lhs = jnp.concatenate([ut_s[hs, :].astype(BF16), vt_s[hs, 0:PAIR]], axis=1)
        yt = y0_s[hs, :] + jnp.dot(lhs, g_s[p], preferred_element_type=F32)
        ytt = yt.T
        y_s[:, p * PAIR:(p + 1) * PAIR] = jnp.concatenate([ytt[0:L], ytt[L:]], axis=1)

    y = y_s[...]
    mu = _seg_sum(y, ob) * (1.0 / HEAD)
    dlt = y - mu
    var = _seg_sum(dlt * dlt, ob) * (1.0 / HEAD)
    yn = dlt * lax.rsqrt(var + R_LN_EPS) * lng_ref[...] + lnb_ref[...]
    bonus = _seg_sum(r * k_s[...] * rk_ref[...], ob) * v
    g = jnp.dot(jax.nn.sigmoid(gd), gup_ref[...], preferred_element_type=F32)
    yr_ref[...] = ((yn + bonus) * g).astype(BF16)


def _rwkv(proj, shift8, wkv_pairs, mu, w0, wup, a0, aup, gup, kk, ka, rk, lng, lnb, cfg):
    n = proj.shape[0]
    rw, lp = cfg.rw, cfg.tail
    assert cfg.o_rw % (3 * rw) == 0 and (cfg.o_rw + 3 * rw) % lp == 0
    npair = rw // PAIR
    seq = lambda c: _seq_of_chunk(c, cfg)
    full = lambda a: pl.BlockSpec(a.shape, lambda c: (0,) * a.ndim)
    row_scr = pltpu.VMEM((CHUNK, rw), F32)
    return pl.pallas_call(
        functools.partial(_rwkv_kernel, cfg=cfg),
        grid=(cfg.n_chunks,),
        in_specs=[pl.BlockSpec((CHUNK, 3 * rw), lambda c: (c, cfg.o_rw // (3 * rw))),
                  pl.BlockSpec((CHUNK, lp), lambda c: (c, (cfg.o_rw + 3 * rw) // lp)),
                  pl.BlockSpec((1, 8, 3 * rw + lp), lambda c: (seq(c), 0, 0)),
                  pl.BlockSpec((1, npair, HEAD, PAIR), lambda c: (seq(c), 0, 0, 0)),
                  full(mu), full(w0), full(wup), full(a0), full(aup), full(gup),
                  full(kk), full(ka), full(rk), full(lng), full(lnb)],
        out_specs=[pl.BlockSpec((CHUNK, rw), lambda c: (c, 0)),
                   pl.BlockSpec((1, npair, HEAD, PAIR), lambda c: (seq(c), 0, 0, 0))],
        out_shape=[jax.ShapeDtypeStruct((n, rw), BF16),
                   jax.ShapeDtypeStruct((cfg.n_seq, npair, HEAD, PAIR), F32)],
        scratch_shapes=[pltpu.VMEM((CHUNK + 8, 3 * rw), F32), pltpu.VMEM((CHUNK + 8, lp), F32),
                        row_scr, row_scr, row_scr, row_scr, row_scr,
                        pltpu.VMEM((npair * HEAD, 2 * PAIR), BF16), pltpu.VMEM((npair * HEAD, PAIR), F32),
                        pltpu.VMEM((npair * HEAD, PAIR), F32), pltpu.VMEM((npair, 2 * PAIR, PAIR), BF16)],
        compiler_params=_params("arbitrary"),
        name="rwkv",
    )(proj, proj, shift8, wkv_pairs, mu, w0, wup, a0, aup, gup, kk, ka, rk, lng, lnb)


def _outproj_kernel(ym_ref, yr_ref, wa_ref, wb_ref, xp_ref, xs_ref, gate_ref, o_ref, *, cfg, tm):
    i = pl.program_id(0)
    acc = (jnp.dot(ym_ref[...], wa_ref[...], preferred_element_type=F32)
           + jnp.dot(yr_ref[...], wb_ref[...], preferred_element_type=F32))

    def residual(x_ref):
        for s in range(tm // CHUNK):
            seq = _seq_of_chunk(i * (tm // CHUNK) + s, cfg)
            rows = slice(s * CHUNK, (s + 1) * CHUNK)
            o_ref[rows, :] = x_ref[rows, :] + gate_ref[pl.ds(seq, 1), :] * acc[rows, :]

    npt = _prompt_tiles(cfg, tm)
    pl.when(i < npt)(lambda: residual(xp_ref))
    pl.when(i >= npt)(lambda: residual(xs_ref))


def _outproj(ym, yr, wa, wb, x_p, x_s, gate, cfg, tm=512, tn=1024):
    n, d = cfg.n_rows, x_p.shape[1]
    ka, kb = ym.shape[1], yr.shape[1]
    s = gate.shape[0]
    return pl.pallas_call(
        functools.partial(_outproj_kernel, cfg=cfg, tm=tm),
        grid=(n // tm, d // tn),
        in_specs=[pl.BlockSpec((tm, ka), lambda i, j: (i, 0)),
                  pl.BlockSpec((tm, kb), lambda i, j: (i, 0)),
                  pl.BlockSpec((ka, tn), lambda i, j: (0, j)),
                  pl.BlockSpec((kb, tn), lambda i, j: (0, j))]
                 + _row_specs(cfg, tm, tn, lambda j: j)
                 + [pl.BlockSpec((s, tn), lambda i, j: (0, j))],
        out_specs=pl.BlockSpec((tm, tn), lambda i, j: (i, j)),
        out_shape=jax.ShapeDtypeStruct((n, d), F32),
        compiler_params=_params("parallel", "parallel"),
        name="outproj",
    )(ym, yr, wa, wb, x_p, x_s, gate)


def _route(lg, n_experts):
    lane = lax.broadcasted_iota(jnp.int32, lg.shape, 1)
    first = lambda mask: jnp.min(jnp.where(mask, lane, LANES), axis=-1, keepdims=True)
    is_grp = lane < E_GROUPS
    gmax = jnp.max(jnp.where(is_grp, lg, -jnp.inf), axis=-1, keepdims=True)
    gsel = first(is_grp & (lg == gmax))
    pg = 1.0 / jnp.sum(jnp.where(is_grp, jnp.exp(lg - gmax), 0.0), axis=-1, keepdims=True)
    e_lane = lane - E_GROUPS
    in_grp = (e_lane >= 0) & (e_lane < n_experts) & (e_lane // E_PER_GROUP == gsel)
    emax = jnp.max(jnp.where(in_grp, lg, -jnp.inf), axis=-1, keepdims=True)
    p = jnp.where(in_grp, jnp.exp(lg - emax), 0.0)
    eprob = p / jnp.sum(p, axis=-1, keepdims=True)
    v1 = jnp.max(jnp.where(in_grp, eprob, -1.0), axis=-1, keepdims=True)
    i1 = first(in_grp & (eprob == v1))
    rest = in_grp & (lane != i1)
    v2 = jnp.max(jnp.where(rest, eprob, -1.0), axis=-1, keepdims=True)
    i2 = first(rest & (eprob == v2))
    tot = v1 + v2
    cols = [(i1 - E_GROUPS).astype(F32), (i2 - E_GROUPS).astype(F32), v1 / tot * pg, v2 / tot * pg]
    out = jnp.zeros(lg.shape, F32)
    for c, val in enumerate(cols):
        out = jnp.where(lane == c, val, out)
    return out


def _router_kernel(x_ref, shift_ref, scale_ref, g_ref, wh_ref, wl_ref, br_ref, h_ref, rt_ref, *, cfg, tm, n_experts):
    i = pl.program_id(0)
    for s in range(tm // CHUNK):
        seq = _seq_of_chunk(i * (tm // CHUNK) + s, cfg)
        rows = slice(s * CHUNK, (s + 1) * CHUNK)
        h = _modulated_norm(x_ref[rows, :], g_ref[...], scale_ref[pl.ds(seq, 1), :], shift_ref[pl.ds(seq, 1), :])
        h_hi, h_lo = _split_hi_lo(h)
        h_ref[rows, :] = h_hi
        dot = lambda a, b: jnp.dot(a, b[...], preferred_element_type=F32)
        logits = dot(h_hi, wh_ref) + (dot(h_hi, wl_ref) + dot(h_lo, wh_ref)) + br_ref[...]
        rt_ref[rows, :] = _route(logits, n_experts)


def _router(x1, shift, scale, g, wr, br, cfg, n_experts, tm=256):
    n, d = x1.shape
    assert E_GROUPS + n_experts <= LANES and n_experts == E_GROUPS * E_PER_GROUP
    full = lambda a: pl.BlockSpec(a.shape, lambda i: (0,) * a.ndim)
    w_hi, w_lo = _split_hi_lo(wr)
    return pl.pallas_call(
        functools.partial(_router_kernel, cfg=cfg, tm=tm, n_experts=n_experts),
        grid=(n // tm,),
        in_specs=[pl.BlockSpec((tm, d), lambda i: (i, 0)), full(shift), full(scale), full(g),
                  full(w_hi), full(w_lo), full(br)],
        out_specs=[pl.BlockSpec((tm, d), lambda i: (i, 0)), pl.BlockSpec((tm, LANES), lambda i: (i, 0))],
        out_shape=[jax.ShapeDtypeStruct((n, d), BF16), jax.ShapeDtypeStruct((n, LANES), F32)],
        compiler_params=_params("parallel"),
        name="router",
    )(x1, shift, scale, g, w_hi, w_lo, br)


def _new_weights(be_ref):
    i = pl.program_id(1)
    return jnp.logical_or(i == 0, be_ref[i] != be_ref[jnp.maximum(i - 1, 0)])


def _for_valid_rows(nv_ref, out_ref, compute):
    nv = nv_ref[pl.program_id(1)]
    for s in range(out_ref.shape[0] // E_SUB):
        rows = slice(s * E_SUB, (s + 1) * E_SUB)

        @pl.when(nv > s * E_SUB)
        def _():
            out_ref[rows, :] = compute(rows)

        @pl.when(nv <= s * E_SUB)
        def _():
            out_ref[rows, :] = jnp.zeros((E_SUB, out_ref.shape[1]), out_ref.dtype)


def _expert_up_kernel(be_ref, nv_ref, x_ref, wg_ref, wu_ref, h_ref, wg_s, wu_s):
    @pl.when(_new_weights(be_ref))
    def _():
        wg_s[...] = wg_ref[0].astype(BF16)
        wu_s[...] = wu_ref[0].astype(BF16)

    def hidden(rows):
        x = x_ref[rows, :]
        gate = jnp.dot(x, wg_s[...], preferred_element_type=F32)
        up = jnp.dot(x, wu_s[...], preferred_element_type=F32)
        return (_silu(gate) * up).astype(BF16)

    _for_valid_rows(nv_ref, h_ref, hidden)


def _expert_down_kernel(be_ref, nv_ref, h_ref, wd_ref, y_ref, wd_s):
    @pl.when(_new_weights(be_ref))
    def _():
        wd_s[...] = wd_ref[0].astype(BF16)

    _for_valid_rows(nv_ref, y_ref, lambda rows: jnp.dot(h_ref[rows, :], wd_s[...], preferred_element_type=F32))


def _experts(xb, blk_e, n_valid, wg, wu, wd, tm, tf=512, tn=2048):
    rows, d = xb.shape
    f = wg.shape[2]
    nblk = rows // tm
    tf, tn = min(tf, f), min(tn, d)
    assert tm % E_SUB == 0 and f % tf == 0 and d % tn == 0
    hid = pl.pallas_call(
        _expert_up_kernel,
        grid_spec=pltpu.PrefetchScalarGridSpec(
            num_scalar_prefetch=2, grid=(f // tf, nblk),
            in_specs=[pl.BlockSpec((tm, d), lambda j, i, be, nv: (i, 0)),
                      pl.BlockSpec((1, d, tf), lambda j, i, be, nv: (be[i], 0, j)),
                      pl.BlockSpec((1, d, tf), lambda j, i, be, nv: (be[i], 0, j))],
            out_specs=pl.BlockSpec((tm, tf), lambda j, i, be, nv: (i, j)),
            scratch_shapes=[pltpu.VMEM((d, tf), BF16), pltpu.VMEM((d, tf), BF16)]),
        out_shape=jax.ShapeDtypeStruct((rows, f), BF16),
        compiler_params=_params("arbitrary", "arbitrary"),
        name="expert_up",
    )(blk_e, n_valid, xb, wg, wu)
    return pl.pallas_call(
        _expert_down_kernel,
        grid_spec=pltpu.PrefetchScalarGridSpec(
            num_scalar_prefetch=2, grid=(d // tn, nblk),
            in_specs=[pl.BlockSpec((tm, f), lambda j, i, be, nv: (i, 0)),
                      pl.BlockSpec((1, f, tn), lambda j, i, be, nv: (be[i], 0, j))],
            out_specs=pl.BlockSpec((tm, tn), lambda j, i, be, nv: (i, j)),
            scratch_shapes=[pltpu.VMEM((f, tn), BF16)]),
        out_shape=jax.ShapeDtypeStruct((rows, d), F32),
        compiler_params=_params("arbitrary", "arbitrary"),
        name="expert_down",
    )(blk_e, n_valid, hid, wd)


def _final_kernel(pos_ref, x_ref, rt_ref, gate_ref, g_ref, y_hbm, o_ref, ybuf, sem, *, cfg, tm, tile0, ntiles):
    i = pl.program_id(0)

    def row_copy(pos, slot, k, r):
        return pltpu.make_async_copy(y_hbm.at[pl.ds(pos, 1), :], ybuf.at[slot, k, pl.ds(r, 1), :], sem.at[slot])

    def fetch(tile, slot):
        base = (tile + tile0) * (tm * TOP_K)

        def body(r, c):
            for k in range(TOP_K):
                row_copy(pos_ref[base + r * TOP_K + k], slot, k, r).start()
            return c

        lax.fori_loop(0, tm, body, 0, unroll=8)

    def wait_all(slot):
        def body(r, c):
            for k in range(TOP_K):
                row_copy(0, slot, k, r).wait()
            return c

        lax.fori_loop(0, tm, body, 0, unroll=8)

    pl.when(i == 0)(lambda: fetch(0, 0))
    pl.when(i + 1 < ntiles)(lambda: fetch(i + 1, (i + 1) % 2))
    slot = i % 2
    wait_all(slot)

    for s in range(tm // CHUNK):
        seq = _seq_of_chunk((i + tile0) * (tm // CHUNK) + s, cfg)
        rows = slice(s * CHUNK, (s + 1) * CHUNK)
        rt = rt_ref[rows, :]
        moe = ybuf[slot, 0, rows, :] * rt[:, TOP_K:TOP_K + 1]
        for k in range(1, TOP_K):
            moe = moe + ybuf[slot, k, rows, :] * rt[:, TOP_K + k:TOP_K + k + 1]
        x = x_ref[rows, :] + gate_ref[pl.ds(seq, 1), :] * moe
        o_ref[rows, :] = x * lax.rsqrt(jnp.mean(x * x, axis=-1, keepdims=True) + NORM_EPS) * g_ref[...]


def _final(x1, yb, slot_pos, route, gate, g, cfg, row0, nrows, tm=256):
    d = x1.shape[1]
    assert row0 % tm == 0 and nrows % tm == 0
    tile0 = row0 // tm
    ntiles = nrows // tm
    full = lambda a: pl.BlockSpec(a.shape, lambda i, pos: (0,) * a.ndim)
    return pl.pallas_call(
        functools.partial(_final_kernel, cfg=cfg, tm=tm, tile0=tile0, ntiles=ntiles),
        grid_spec=pltpu.PrefetchScalarGridSpec(
            num_scalar_prefetch=1, grid=(ntiles,),
            in_specs=[pl.BlockSpec((tm, d), lambda i, pos: (i + tile0, 0)),
                      pl.BlockSpec((tm, LANES), lambda i, pos: (i + tile0, 0)),
                      full(gate), full(g), pl.BlockSpec(memory_space=pl.ANY)],
            out_specs=pl.BlockSpec((tm, d), lambda i, pos: (i, 0)),
            scratch_shapes=[pltpu.VMEM((2, TOP_K, tm, d), F32), pltpu.SemaphoreType.DMA((2,))]),
        out_shape=jax.ShapeDtypeStruct((nrows, d), F32),
        compiler_params=_params("arbitrary"),
        name="final_norm",
    )(slot_pos.reshape(-1), x1, route, gate, g, yb)


def _dispatch(route, n_experts, tm):
    n = route.shape[0]
    eid = route[:, :TOP_K].astype(jnp.int32)

    m = n * TOP_K
    e_flat = eid.reshape(m)
    order = jnp.argsort(e_flat).astype(jnp.int32)
    rank = jnp.argsort(order).astype(jnp.int32)
    e_sorted = e_flat[order]
    experts = jnp.arange(n_experts, dtype=jnp.int32)
    counts = jnp.sum(e_flat[:, None] == experts[None, :], axis=0, dtype=jnp.int32)
    start = jnp.cumsum(counts) - counts
    padded = (counts + tm - 1) // tm * tm
    pend = jnp.cumsum(padded)
    pstart = pend - padded
    dest_sorted = pstart[e_sorted] + jnp.arange(m, dtype=jnp.int32) - start[e_sorted]
    slot_pos = dest_sorted[rank].reshape(n, TOP_K)
    n_blocks = -(-(m + n_experts * (tm - 1)) // tm)
    rows = n_blocks * tm
    r = jnp.arange(rows, dtype=jnp.int32)
    owner = lambda row: jnp.minimum(jnp.sum(row[:, None] >= pend[None, :], axis=1, dtype=jnp.int32), n_experts - 1)
    e_r = owner(r)
    idx = r - pstart[e_r]
    row_tok = jnp.where(idx < counts[e_r], order[jnp.clip(start[e_r] + idx, 0, m - 1)] // TOP_K, r % n)
    n_used = pend[-1] // tm
    blk = jnp.arange(n_blocks, dtype=jnp.int32)
    blk_e = owner(blk * tm)
    n_valid = jnp.where(blk < n_used, jnp.clip(pstart[blk_e] + counts[blk_e] - blk * tm, 0, tm), 0).astype(jnp.int32)
    last_e = blk_e[jnp.maximum(n_used - 1, 0)]
    blk_e = jnp.where(blk < n_used, blk_e, last_e)
    return row_tok, slot_pos, blk_e, n_valid


def _to_window(a, cfg):
    return jnp.concatenate([a, jnp.zeros(a.shape[:-1] + (cfg.win - cfg.r_cols,), a.dtype)], axis=-1)


def _pad_rows(a, rows):
    return jnp.concatenate([a, jnp.zeros((rows - a.shape[0],) + a.shape[1:], a.dtype)], axis=0)


def kernel(x_prompt, x_sample, state_conv, state_ssm, state_shift, state_wkv, c_prompt, c_sample, norm1_g, w_mod, b_mod, w_in, conv_w, conv_b, dt_bias, a_log, d_skip, m_norm_g, shift_mu, w0, w_up, a0, a_up, g_up, k_k, k_a, r_k, ln_x_g, ln_x_b, w_out, norm2_g, w_grp, b_grp, w_erouter, b_erouter, e_gate, e_up, e_down, final_norm_g):
    assert w_mod.shape[0] == 1, "single-layer trunk"
    bp, tp, d = x_prompt.shape
    bs, ts, _ = x_sample.shape
    mi = m_norm_g.shape[-1]
    rw = w0.shape[-1]
    lw, la, lg = w_up.shape[1], a_up.shape[1], g_up.shape[1]
    cfg = Cfg(d, bp, tp, bs, ts, mi, rw, lw, la, lg)
    nc = cfg.nc
    assert tp % CHUNK == 0 and ts % CHUNK == 0 and tp >= CONV_W - 1 and ts >= CONV_W - 1
    assert w_in.shape[-1] == cfg.o_rw + cfg.mh + cfg.r_cols and cfg.o_rw % LANES == 0
    assert state_conv.shape[-1] == cfg.cd and lw <= LANES and la <= LANES and cfg.mh <= LANES
    n = cfg.n_rows
    n_seq = cfg.n_seq
    mh, rh = cfg.mh, cfg.rh
    npair = rw // PAIR
    n_experts = e_gate.shape[1]

    x_p = x_prompt.reshape(bp * tp, d)
    x_s = x_sample.reshape(bs * ts, d)
    c_all = jnp.concatenate([c_prompt, c_sample], axis=0)

    mod = _modulation(c_all, w_mod[0], b_mod[0])
    shift1, scale1, gate1, shift2, scale2, gate2 = [mod[:, i * d:(i + 1) * d] for i in range(6)]

    h1 = _prenorm(x_p, x_s, shift1, scale1, norm1_g, cfg)
    w_in_t = w_in[0].T
    proj = _inproj(h1, w_in_t, cfg)
    dt_raw = _dtproj(h1, w_in_t, cfg)

    zeros = lambda b, *s: jnp.zeros((b,) + s, F32)
    conv0 = jnp.concatenate([zeros(bp, CONV_W - 1, cfg.cd), state_conv[0]], axis=0)
    conv8 = jnp.concatenate([zeros(n_seq, 8 - (CONV_W - 1), cfg.cd), conv0], axis=1)
    nst = state_ssm.shape[-1]
    ssm0 = jnp.concatenate([zeros(bp, mi, nst), state_ssm[0].reshape(bs, mi, nst)], axis=0)
    lane_pad = lambda a: jnp.concatenate([a.reshape(1, -1), jnp.zeros((1, LANES - a.shape[-1]), F32)], axis=1)
    expand = ((jnp.arange(2 * LANES)[:, None] % LANES) == (jnp.arange(mi)[None, :] // HEAD)).astype(BF16)
    ym, ssm_new = _mamba(proj, dt_raw, conv8, ssm0, conv_w[0], conv_b[0].reshape(1, -1), lane_pad(dt_bias[0]),
                         lane_pad(a_log[0]), jnp.repeat(d_skip[0], HEAD).reshape(1, mi), m_norm_g[0].reshape(1, mi),
                         expand, cfg)

    sh0 = jnp.concatenate([zeros(bp, 1, state_shift.shape[-1]), state_shift[0]], axis=0)
    sh8 = jnp.concatenate([zeros(n_seq, 7, cfg.win), _to_window(sh0, cfg)], axis=1)
    to_pairs = lambda s: s.reshape(-1, npair, 2, HEAD, HEAD).transpose(0, 1, 3, 2, 4).reshape(-1, npair, HEAD, PAIR)
    from_pairs = lambda s: s.reshape(-1, npair, HEAD, 2, HEAD).transpose(0, 1, 3, 2, 4).reshape(-1, rh, HEAD, HEAD)
    wkv0 = jnp.concatenate([zeros(bp, npair, HEAD, PAIR), to_pairs(state_wkv[0])], axis=0)
    row = lambda a: a.reshape(1, -1)
    yr, wkv_new = _rwkv(proj, sh8, wkv0, _to_window(row(shift_mu[0]), cfg), row(w0[0]),
                        _pad_rows(w_up[0], LANES), row(a0[0]), _pad_rows(a_up[0], LANES), g_up[0],
                        row(k_k[0]), row(k_a[0]), row(r_k[0]), row(ln_x_g[0]), row(ln_x_b[0]), cfg)

    wo = w_out[0].astype(BF16)
    x1 = _outproj(ym, yr, wo[:mi], wo[mi:], x_p, x_s, gate1, cfg)
    wr = jnp.concatenate([w_grp[0], w_erouter[0], jnp.zeros((d, LANES - E_GROUPS - n_experts), F32)], axis=1)
    br = lane_pad(jnp.concatenate([b_grp[0], b_erouter[0]]))
    h2, route = _router(x1, shift2, scale2, norm2_g, wr, br, cfg, n_experts)

    tm_e = 2 * E_SUB
    row_tok, slot_pos, blk_e, n_valid = _dispatch(route, n_experts, tm_e)
    yb = _experts(h2[row_tok], blk_e, n_valid, e_gate[0], e_up[0], e_down[0], tm_e)

    np_ = bp * tp
    fg = final_norm_g.reshape(1, d)
    y_prompt = _final(x1, yb, slot_pos, route, gate2, fg, cfg, 0, np_).reshape(bp, tp, d)
    y_sample = _final(x1, yb, slot_pos, route, gate2, fg, cfg, np_, bs * ts).reshape(bs, ts, d)
    tails = lambda b, t, base: jnp.stack(
        [lax.slice(proj, (base + (i + 1) * t - (CONV_W - 1), 0), (base + (i + 1) * t, nc)) for i in range(b)])
    pp = tails(bp, tp, 0)
    ps = tails(bs, ts, np_)
    conv_of = lambda p: p[:, :, mi:mi + cfg.cd][None]
    shift_of = lambda p: p[:, -1:, cfg.o_rw:cfg.o_rw + cfg.r_cols][None]
    ssm_new = ssm_new.reshape(n_seq, mh, HEAD, nst)
    wkv_new = from_pairs(wkv_new)
    return (y_prompt, y_sample,
            conv_of(pp), ssm_new[:bp][None], shift_of(pp), wkv_new[:bp][None],
            conv_of(ps), ssm_new[bp:][None], shift_of(ps), wkv_new[bp:][None])
```

```python
import functools
from typing import NamedTuple

import jax
import jax.numpy as jnp
from jax import lax
from jax.experimental import pallas as pl
from jax.experimental.pallas import tpu as pltpu

F32 = jnp.float32
BF16 = jnp.bfloat16
HI = lax.Precision.HIGHEST

LANES = 128
CHUNK = 64
HEAD = 64
PAIR = 2 * HEAD
NORM_EPS = 1e-6
M_NORM_EPS = 1e-5
R_LN_EPS = 64e-5
CONV_W = 4
E_GROUPS = 4
E_PER_GROUP = 8
TOP_K = 2
VMEM_LIMIT = 56 * 1024 * 1024
E_SUB = 256
EXP_NEG_HALF = 0.6065306597126334
SA_GROUP = 16


class Cfg(NamedTuple):
    d: int
    bp: int
    tp: int
    bs: int
    ts: int
    mi: int
    rw: int
    lw: int
    la: int
    lg: int

    @property
    def cpp(self):
        return self.tp // CHUNK

    @property
    def cps(self):
        return self.ts // CHUNK

    @property
    def n_chunks(self):
        return self.bp * self.cpp + self.bs * self.cps

    @property
    def n_seq(self):
        return self.bp + self.bs

    @property
    def n_rows(self):
        return self.n_chunks * CHUNK

    @property
    def groups(self):
        return self.mi // 256

    @property
    def nb(self):
        return self.groups * LANES

    @property
    def cd(self):
        return self.mi + 2 * self.nb

    @property
    def mh(self):
        return self.mi // HEAD

    @property
    def rh(self):
        return self.rw // HEAD

    @property
    def o_rw(self):
        return self.mi + self.cd

    @property
    def r_cols(self):
        return 3 * self.rw + self.lw + self.la + self.lg

    @property
    def tail(self):
        return -(-(self.lw + self.la + self.lg) // LANES) * LANES

    @property
    def win(self):
        return 3 * self.rw + self.tail

    @property
    def nc(self):
        return self.o_rw + self.win


def _seq_of_chunk(c, cfg):
    npc = cfg.bp * cfg.cpp
    return jnp.where(c < npc, c // cfg.cpp, cfg.bp + (c - npc) // cfg.cps)


def _is_first_chunk(c, cfg):
    npc = cfg.bp * cfg.cpp
    return jnp.where(c < npc, c % cfg.cpp == 0, (c - npc) % cfg.cps == 0)


def _silu(x):
    return x * jax.nn.sigmoid(x)


def _nt(a, b, **kw):
    return lax.dot_general(a, b, (((1,), (1,)), ((), ())), preferred_element_type=F32, **kw)


def _tn(a, b, **kw):
    return lax.dot_general(a, b, (((0,), (0,)), ((), ())), preferred_element_type=F32, **kw)


def _params(*sem):
    return pltpu.CompilerParams(dimension_semantics=sem, vmem_limit_bytes=VMEM_LIMIT)


def _mod_kernel(c_ref, w_ref, b_ref, o_ref):
    o_ref[...] = jnp.dot(_silu(c_ref[...]), w_ref[...], preferred_element_type=F32) + b_ref[...]


def _modulation(c_all, w_mod, b_mod):
    s, d = c_all.shape
    cols = w_mod.shape[1]
    tn = 512
    return pl.pallas_call(
        _mod_kernel,
        grid=(cols // tn,),
        in_specs=[pl.BlockSpec((s, d), lambda j: (0, 0)),
                  pl.BlockSpec((d, tn), lambda j: (0, j)),
                  pl.BlockSpec((1, tn), lambda j: (0, j))],
        out_specs=pl.BlockSpec((s, tn), lambda j: (0, j)),
        out_shape=jax.ShapeDtypeStruct((s, cols), F32),
        compiler_params=_params("parallel"),
        name="modulation",
    )(c_all, w_mod, b_mod.reshape(1, cols))


def _modulated_norm(x, g, scale, shift):
    y = x * lax.rsqrt(jnp.mean(x * x, axis=-1, keepdims=True) + NORM_EPS) * g
    return y * (1.0 + scale) + shift


def _prompt_tiles(cfg, tm):
    assert (cfg.bp * cfg.tp) % tm == 0 and (cfg.bs * cfg.ts) % tm == 0 and tm % CHUNK == 0
    return cfg.bp * cfg.tp // tm


def _row_specs(cfg, tm, width, col=None):
    npt = _prompt_tiles(cfg, tm)
    c = (lambda *r: 0) if col is None else col
    return [pl.BlockSpec((tm, width), lambda i, *r: (jnp.minimum(i, npt - 1), jnp.where(i < npt, c(*r), 0))),
            pl.BlockSpec((tm, width), lambda i, *r: (jnp.maximum(i - npt, 0), jnp.where(i >= npt, c(*r), 0)))]


def _prenorm_kernel(xp_ref, xs_ref, shift_ref, scale_ref, g_ref, h_ref, *, cfg, tm):
    i = pl.program_id(0)

    def norm_rows(x_ref):
        for s in range(tm // CHUNK):
            seq = _seq_of_chunk(i * (tm // CHUNK) + s, cfg)
            rows = slice(s * CHUNK, (s + 1) * CHUNK)
            h = _modulated_norm(x_ref[rows, :], g_ref[...], scale_ref[pl.ds(seq, 1), :], shift_ref[pl.ds(seq, 1), :])
            h_ref[rows, :] = h.astype(BF16)

    npt = _prompt_tiles(cfg, tm)
    pl.when(i < npt)(lambda: norm_rows(xp_ref))
    pl.when(i >= npt)(lambda: norm_rows(xs_ref))


def _prenorm(x_p, x_s, shift, scale, g, cfg, tm=256):
    d = x_p.shape[1]
    full = lambda a: pl.BlockSpec(a.shape, lambda i: (0,) * a.ndim)
    return pl.pallas_call(
        functools.partial(_prenorm_kernel, cfg=cfg, tm=tm),
        grid=(cfg.n_rows // tm,),
        in_specs=_row_specs(cfg, tm, d) + [full(shift), full(scale), full(g)],
        out_specs=pl.BlockSpec((tm, d), lambda i: (i, 0)),
        out_shape=jax.ShapeDtypeStruct((cfg.n_rows, d), BF16),
        compiler_params=_params("parallel"),
        name="prenorm",
    )(x_p, x_s, shift, scale, g)


def _inproj_kernel(h_ref, w_ref, wn_ref, o_ref, wb_s, *, first_shifted, shift, tn, n_features):
    j = pl.program_id(0)
    new_tile = pl.program_id(1) == 0

    @pl.when(jnp.logical_and(new_tile, j < first_shifted))
    def _():
        wb_s[...] = w_ref[...].astype(BF16)

    @pl.when(jnp.logical_and(new_tile, j >= first_shifted))
    def _():
        w = jnp.concatenate([w_ref[...], wn_ref[...]], axis=0)[shift:shift + tn, :]
        feature = j * tn + shift + lax.broadcasted_iota(jnp.int32, w.shape, 0)
        wb_s[...] = jnp.where(feature < n_features, w, 0.0).astype(BF16)

    o_ref[...] = _nt(h_ref[...], wb_s[...])


def _inproj(h, wt, cfg, tm=1024, tn=512):
    n, d = h.shape
    nout = cfg.o_rw + cfg.win
    assert cfg.o_rw % tn == 0 and nout % tn == 0 and n % tm == 0 and cfg.mh <= LANES and cfg.mh % 8 == 0
    last_next = (wt.shape[0] - 1) // LANES
    return pl.pallas_call(
        functools.partial(_inproj_kernel, first_shifted=cfg.o_rw // tn, shift=cfg.mh, tn=tn, n_features=wt.shape[0]),
        grid=(nout // tn, n // tm),
        in_specs=[pl.BlockSpec((tm, d), lambda j, i: (i, 0)),
                  pl.BlockSpec((tn, d), lambda j, i: (j, 0)),
                  pl.BlockSpec((LANES, d), lambda j, i: (jnp.minimum((j + 1) * (tn // LANES), last_next), 0))],
        out_specs=pl.BlockSpec((tm, tn), lambda j, i: (i, j)),
        out_shape=jax.ShapeDtypeStruct((n, nout), F32),
        scratch_shapes=[pltpu.VMEM((tn, d), BF16)],
        compiler_params=_params("arbitrary", "arbitrary"),
        name="inproj",
    )(h, wt, wt)


def _dtproj_kernel(h_ref, w_ref, o_ref):
    o_ref[...] = _nt(h_ref[...], w_ref[...].astype(BF16))


def _dtproj(h, wt, cfg, tm=1024):
    n, d = h.shape
    return pl.pallas_call(
        _dtproj_kernel,
        grid=(n // tm,),
        in_specs=[pl.BlockSpec((tm, d), lambda i: (i, 0)), pl.BlockSpec((LANES, d), lambda i: (cfg.o_rw // LANES, 0))],
        out_specs=pl.BlockSpec((tm, LANES), lambda i: (i, 0)),
        out_shape=jax.ShapeDtypeStruct((n, LANES), F32),
        compiler_params=_params("parallel"),
        name="dtproj",
    )(h, wt)


def _mamba_kernel(zx_ref, dt_ref, cst_ref, sst_ref, cw_ref, cb_ref, dtb_ref, alog_ref, dsk_ref, ng_ref, exp_ref,
                  ym_ref, ssm_ref, buf, u_scr, *, cfg):
    c = pl.program_id(0)
    mi, nb, cd = cfg.mi, cfg.nb, cfg.cd
    L = CHUNK

    @pl.when(_is_first_chunk(c, cfg))
    def _():
        buf[0:8, :] = cst_ref[0]
        ssm_ref[0] = sst_ref[0]

    xbc = zx_ref[:, mi:mi + cd]
    buf[8:8 + L, :] = xbc
    conv = (cb_ref[...] + buf[5:5 + L, :] * cw_ref[0:1, :] + buf[6:6 + L, :] * cw_ref[1:2, :]
            + buf[7:7 + L, :] * cw_ref[2:3, :] + xbc * cw_ref[3:4, :])
    buf[0:8, :] = buf[L:L + 8, :]
    u_scr[...] = _silu(conv)

    dt = jax.nn.softplus(dt_ref[...] + dtb_ref[...])
    da = dt * (-jnp.exp(alog_ref[...]))
    row = lax.broadcasted_iota(jnp.int32, (L, mi), 0)
    pos = lax.broadcasted_iota(jnp.int32, (L, mi), 1) % HEAD
    tri2 = (lax.broadcasted_iota(jnp.int32, (L, 2 * L), 0)
            >= lax.broadcasted_iota(jnp.int32, (L, 2 * L), 1) % L).astype(BF16)
    lane = lax.broadcasted_iota(jnp.int32, (L, PAIR), 1)
    row16 = lax.broadcasted_iota(jnp.int32, (16, mi), 0)
    ones16 = jnp.ones((16, LANES), BF16)

    dt_hl = jnp.concatenate(_split_hi_lo(dt), axis=1)
    da_hl = jnp.concatenate(_split_hi_lo(da), axis=1)
    dte = jnp.dot(dt_hl, exp_ref[...], preferred_element_type=F32)
    dae = jnp.dot(da_hl, exp_ref[...], preferred_element_type=F32)
    cum = jnp.dot(tri2, jnp.concatenate(_split_hi_lo(dae), axis=0), preferred_element_type=F32)
    cum_row = jnp.sum(jnp.where(row == pos, cum, 0.0), axis=0, keepdims=True)
    last = cum[L - 1:L, :]
    decay = jnp.exp(jnp.where(row >= pos, cum - cum_row, -jnp.inf))
    xdt = u_scr[:, 0:mi] * dte
    x_end = xdt * jnp.exp(last - cum)
    e_cum = jnp.exp(cum)
    e_last = jnp.exp(last)
    e_hi = e_last.astype(BF16).astype(F32)
    e_rows = jnp.where(row16 == 0, e_hi, jnp.where(row16 == 1, e_last - e_hi, 0.0)).astype(BF16)

    groups = range(cfg.groups)
    gs = lambda g: slice(g * 256, (g + 1) * 256)
    b_of = lambda g: u_scr[:, mi + g * LANES:mi + (g + 1) * LANES]
    c_of = lambda g: u_scr[:, mi + nb + g * LANES:mi + nb + (g + 1) * LANES]
    sc2 = [_nt(c_of(g), jnp.concatenate([b_of(g), b_of(g)], axis=0)) for g in groups]
    y_diag = []
    for g in groups:
        ys = []
        for q in range(2):
            ps = slice(g * 256 + q * PAIR, g * 256 + (q + 1) * PAIR)
            xq = xdt[:, ps]
            rhs = jnp.concatenate([jnp.where(lane < HEAD, xq, 0.0), jnp.where(lane >= HEAD, xq, 0.0)], axis=0)
            ys.append(jnp.dot(decay[:, ps] * sc2[g], rhs, preferred_element_type=F32))
        y_diag.append(jnp.concatenate(ys, axis=1))
    y_off = [_nt(c_of(g), ssm_ref[0, gs(g), :]) for g in groups]
    new = [_tn(x_end[:, gs(g)], b_of(g)) for g in groups]
    dcol = [_tn(e_rows[:, gs(g)], ones16) for g in groups]
    for g in groups:
        cs = gs(g)
        ssm_ref[0, cs, :] = ssm_ref[0, cs, :] * dcol[g] + new[g]
        y = y_diag[g] + y_off[g] * e_cum[:, cs] + u_scr[:, cs] * dsk_ref[:, cs]
        y = y * _silu(zx_ref[:, cs])
        y = y * lax.rsqrt(jnp.mean(y * y, axis=-1, keepdims=True) + M_NORM_EPS)
        ym_ref[:, cs] = (y * ng_ref[:, cs]).astype(BF16)


def _mamba(proj, dt_raw, conv_state8, ssm_state, cw, cb, dtb, alog, dsk, ng, expand, cfg):
    n = proj.shape[0]
    mi, cd = cfg.mi, cfg.cd
    nst = ssm_state.shape[-1]
    seq = lambda c: _seq_of_chunk(c, cfg)
    full = lambda a: pl.BlockSpec(a.shape, lambda c: (0,) * a.ndim)
    return pl.pallas_call(
        functools.partial(_mamba_kernel, cfg=cfg),
        grid=(cfg.n_chunks,),
        in_specs=[pl.BlockSpec((CHUNK, mi + cd), lambda c: (c, 0)),
                  pl.BlockSpec((CHUNK, LANES), lambda c: (c, 0)),
                  pl.BlockSpec((1, 8, cd), lambda c: (seq(c), 0, 0)),
                  pl.BlockSpec((1, mi, nst), lambda c: (seq(c), 0, 0)),
                  full(cw), full(cb), full(dtb), full(alog), full(dsk), full(ng), full(expand)],
        out_specs=[pl.BlockSpec((CHUNK, mi), lambda c: (c, 0)),
                   pl.BlockSpec((1, mi, nst), lambda c: (seq(c), 0, 0))],
        out_shape=[jax.ShapeDtypeStruct((n, mi), BF16),
                   jax.ShapeDtypeStruct((cfg.n_seq, mi, nst), F32)],
        scratch_shapes=[pltpu.VMEM((CHUNK + 8, cd), F32), pltpu.VMEM((CHUNK, cd), F32)],
        compiler_params=_params("arbitrary"),
        name="mamba",
    )(proj, dt_raw, conv_state8, ssm_state, cw, cb, dtb, alog, dsk, ng, expand)


def _seg_sum(x, ob):
    w = ob.shape[0]
    xb = x.astype(BF16)
    parts = [jnp.dot(xb[:, p * w:(p + 1) * w], ob, preferred_element_type=F32) for p in range(x.shape[1] // w)]
    return jnp.concatenate(parts, axis=1)


def _split_hi_lo(x):
    hi = x.astype(BF16)
    lo = (x - hi.astype(F32)).astype(BF16)
    return hi, lo


def _rwkv_kernel(rkv_ref, lora_ref, sh_ref, wkv_in_ref, mu_ref, w0_ref, wup_ref, a0_ref, aup_ref, gup_ref,
                 kk_ref, ka_ref, rk_ref, lng_ref, lnb_ref,
                 yr_ref, wkv_ref, pbuf, lbuf, w_s, a_s, b_s, k_s, y_s, vt_s, ut_s, y0_s, g_s, *, cfg):
    c = pl.program_id(0)
    rw = cfg.rw
    L = CHUNK
    npair = rw // PAIR

    @pl.when(_is_first_chunk(c, cfg))
    def _():
        pbuf[0:8, :] = sh_ref[0, :, 0:3 * rw]
        lbuf[0:8, :] = sh_ref[0, :, 3 * rw:]
        wkv_ref[0] = wkv_in_ref[0]

    pr = rkv_ref[...]
    pl_ = lora_ref[...]
    pbuf[8:8 + L, :] = pr
    lbuf[8:8 + L, :] = pl_
    xs = pr + (pbuf[7:7 + L, :] - pr) * mu_ref[:, 0:3 * rw]
    xl = pl_ + (lbuf[7:7 + L, :] - pl_) * mu_ref[:, 3 * rw:]
    pbuf[0:8, :] = pbuf[L:L + 8, :]
    lbuf[0:8, :] = lbuf[L:L + 8, :]

    r = xs[:, 0:rw]
    k = xs[:, rw:2 * rw]
    v = xs[:, 2 * rw:3 * rw]
    wd = xl[:, 0:LANES]
    ad = xl[:, cfg.lw:cfg.lw + LANES]
    gd = xl[:, cfg.lw + cfg.la:cfg.lw + cfg.la + cfg.lg]

    lane = lax.broadcasted_iota(jnp.int32, (PAIR, PAIR), 1)
    rowi = lax.broadcasted_iota(jnp.int32, (PAIR, PAIR), 0)
    o_row = lax.broadcasted_iota(jnp.int32, (2 * PAIR, 2 * PAIR), 0)
    o_lane = lax.broadcasted_iota(jnp.int32, (2 * PAIR, 2 * PAIR), 1)
    ob = ((o_row // HEAD) == (o_lane // HEAD)).astype(BF16)

    w_z = w0_ref[...] + jnp.dot(jnp.tanh(wd), wup_ref[...], preferred_element_type=F32)
    logw = -(jax.nn.sigmoid(w_z) * EXP_NEG_HALF)
    w_s[...] = jnp.exp(logw)
    a = jax.nn.sigmoid(a0_ref[...] + jnp.dot(ad, aup_ref[...], preferred_element_type=F32))
    kk = k * kk_ref[...]
    kk = kk * lax.rsqrt(jnp.maximum(_seg_sum(kk * kk, ob), 1e-24))
    kh = k * (1.0 + (a - 1.0) * ka_ref[...])
    a_s[...] = -kk
    b_s[...] = kk * a
    k_s[...] = kh

    tri2 = (lax.broadcasted_iota(jnp.int32, (L, 2 * L), 0)
            >= lax.broadcasted_iota(jnp.int32, (L, 2 * L), 1) % L).astype(BF16)
    cum = jnp.dot(tri2, jnp.concatenate(_split_hi_lo(logw), axis=0), preferred_element_type=F32)
    p_inv = jnp.exp(-cum)
    r_p = r * jnp.exp(cum)
    b_p = b_s[...] * p_inv
    k_p = kh * p_inv
    lo_lanes = lane < HEAD
    g_row = lax.broadcasted_iota(jnp.int32, (2 * PAIR, PAIR), 0)
    g_lane = lax.broadcasted_iota(jnp.int32, (2 * PAIR, PAIR), 1)
    g_keep = (g_row % HEAD) <= (g_lane % HEAD)

    def by_head(x):
        x2 = jnp.concatenate([x, x], axis=0)
        return jnp.where(lo_lanes == (rowi < HEAD), x2, 0.0).astype(BF16)

    pairs = range(npair)
    lanes_of = lambda p: slice(p * PAIR, (p + 1) * PAIR)
    rows_of = lambda p: slice(p * HEAD, (p + 1) * HEAD)
    r_bd = [by_head(r_p[:, lanes_of(p)]) for p in pairs]
    y0 = [_nt(wkv_ref[0, p].astype(BF16), r_bd[p]) for p in pairs]
    gram = [_nt(jnp.concatenate([by_head(b_p[:, lanes_of(p)]), by_head(k_p[:, lanes_of(p)])], axis=0), r_bd[p])
            for p in pairs]
    vts = [v[:, lanes_of(p)].T for p in pairs]
    for p in pairs:
        y0_s[rows_of(p), :] = y0[p]
        g_s[p] = jnp.where(g_keep, gram[p], 0.0).astype(BF16)
        vjt = jnp.concatenate([vts[p][0:HEAD], vts[p][HEAD:]], axis=1)
        hi = vjt.astype(BF16).astype(F32)
        vt_s[rows_of(p), :] = jnp.concatenate([hi, vjt - hi], axis=1).astype(BF16)

    lane_t = lax.broadcasted_iota(jnp.int32, (HEAD, PAIR), 1) % HEAD
    crow = lax.broadcasted_iota(jnp.int32, (2 * PAIR, 2 * PAIR), 0)
    clane = lax.broadcasted_iota(jnp.int32, (2 * PAIR, 2 * PAIR), 1)
    same_head = ((crow % PAIR) // HEAD) == ((clane % PAIR) // HEAD)
    c_dt = (crow % HEAD) - clane // PAIR

    def steps(blk, carry):
        t0 = pl.multiple_of(blk * 8, 8)
        rows = pl.ds(t0, 8)
        w8, a8, b8, k8 = w_s[rows, :], a_s[rows, :], b_s[rows, :], k_s[rows, :]
        s = [wkv_ref[0, p] for p in range(npair)]
        vcol2 = None
        for i in range(8):
            row = lambda x8, p: x8[i:i + 1, p * PAIR:(p + 1) * PAIR]
            this_step = lane_t == t0 + i
            if i % 2 == 0:
                sel = (same_head & (c_dt == t0 + i)).astype(BF16)
                vcol2 = jnp.dot(vt_s[...], sel, preferred_element_type=F32)
            vcol = vcol2[:, (i % 2) * PAIR:(i % 2 + 1) * PAIR]
            for p0 in range(0, npair, SA_GROUP):
                group = range(p0, min(p0 + SA_GROUP, npair))
                lhs = []
                for p in group:
                    prod = (s[p] * row(a8, p)).astype(BF16)
                    lhs.append(jnp.concatenate([prod[0:HEAD // 2], prod[HEAD // 2:]], axis=1))
                sa = jnp.dot(jnp.concatenate(lhs, axis=0), ob, preferred_element_type=F32)
                for q, p in enumerate(group):
                    half = sa[q * (HEAD // 2):(q + 1) * (HEAD // 2)]
                    sa_p = jnp.concatenate([half[:, 0:PAIR], half[:, PAIR:]], axis=0)
                    s[p] = s[p] * row(w8, p) + sa_p * row(b8, p) + vcol[p * HEAD:(p + 1) * HEAD] * row(k8, p)
                    pltpu.store(ut_s.at[p * HEAD:(p + 1) * HEAD, :], sa_p, mask=this_step)
        for p in range(npair):
            wkv_ref[0, p] = s[p]
        return carry

    ut_s[...] = jnp.zeros_like(ut_s)
    lax.fori_loop(0, L // 8, steps, 0)

    yts = [jnp.dot(jnp.concatenate([ut_s[rows_of(p), :].astype(BF16), vt_s[rows_of(p), 0:PAIR]], axis=1), g_s[p],
                   preferred_element_type=F32) for p in pairs]
    ytt = [(y0_s[rows_of(p), :] + yts[p]).T for p in pairs]
    for p in pairs:
        y_s[:, lanes_of(p)] = jnp.concatenate([ytt[p][0:L], ytt[p][L:]], axis=1)

    y = y_s[...]
    mu = _seg_sum(y, ob) * (1.0 / HEAD)
    dlt = y - mu
    var = _seg_sum(dlt * dlt, ob) * (1.0 / HEAD)
    yn = dlt * lax.rsqrt(var + R_LN_EPS) * lng_ref[...] + lnb_ref[...]
    bonus = _seg_sum(r * k_s[...] * rk_ref[...], ob) * v
    g = jnp.dot(jax.nn.sigmoid(gd), gup_ref[...], preferred_element_type=F32)
    yr_ref[...] = ((yn + bonus) * g).astype(BF16)


def _rwkv(proj, shift8, wkv_pairs, mu, w0, wup, a0, aup, gup, kk, ka, rk, lng, lnb, cfg):
    n = proj.shape[0]
    rw, lp = cfg.rw, cfg.tail
    assert cfg.o_rw % (3 * rw) == 0 and (cfg.o_rw + 3 * rw) % lp == 0
    npair = rw // PAIR
    seq = lambda c: _seq_of_chunk(c, cfg)
    full = lambda a: pl.BlockSpec(a.shape, lambda c: (0,) * a.ndim)
    row_scr = pltpu.VMEM((CHUNK, rw), F32)
    return pl.pallas_call(
        functools.partial(_rwkv_kernel, cfg=cfg),
        grid=(cfg.n_chunks,),
        in_specs=[pl.BlockSpec((CHUNK, 3 * rw), lambda c: (c, cfg.o_rw // (3 * rw))),
                  pl.BlockSpec((CHUNK, lp), lambda c: (c, (cfg.o_rw + 3 * rw) // lp)),
                  pl.BlockSpec((1, 8, 3 * rw + lp), lambda c: (seq(c), 0, 0)),
                  pl.BlockSpec((1, npair, HEAD, PAIR), lambda c: (seq(c), 0, 0, 0)),
                  full(mu), full(w0), full(wup), full(a0), full(aup), full(gup),
                  full(kk), full(ka), full(rk), full(lng), full(lnb)],
        out_specs=[pl.BlockSpec((CHUNK, rw), lambda c: (c, 0)),
                   pl.BlockSpec((1, npair, HEAD, PAIR), lambda c: (seq(c), 0, 0, 0))],
        out_shape=[jax.ShapeDtypeStruct((n, rw), BF16),
                   jax.ShapeDtypeStruct((cfg.n_seq, npair, HEAD, PAIR), F32)],
        scratch_shapes=[pltpu.VMEM((CHUNK + 8, 3 * rw), F32), pltpu.VMEM((CHUNK + 8, lp), F32),
                        row_scr, row_scr, row_scr, row_scr, row_scr,
                        pltpu.VMEM((npair * HEAD, 2 * PAIR), BF16), pltpu.VMEM((npair * HEAD, PAIR), F32),
                        pltpu.VMEM((npair * HEAD, PAIR), F32), pltpu.VMEM((npair, 2 * PAIR, PAIR), BF16)],
        compiler_params=_params("arbitrary"),
        name="rwkv",
    )(proj, proj, shift8, wkv_pairs, mu, w0, wup, a0, aup, gup, kk, ka, rk, lng, lnb)


def _outproj_kernel(ym_ref, yr_ref, wa_ref, wb_ref, xp_ref, xs_ref, gate_ref, o_ref, *, cfg, tm):
    i = pl.program_id(0)
    acc = (jnp.dot(ym_ref[...], wa_ref[...], preferred_element_type=F32)
           + jnp.dot(yr_ref[...], wb_ref[...], preferred_element_type=F32))

    def residual(x_ref):
        for s in range(tm // CHUNK):
            seq = _seq_of_chunk(i * (tm // CHUNK) + s, cfg)
            rows = slice(s * CHUNK, (s + 1) * CHUNK)
            o_ref[rows, :] = x_ref[rows, :] + gate_ref[pl.ds(seq, 1), :] * acc[rows, :]

    npt = _prompt_tiles(cfg, tm)
    pl.when(i < npt)(lambda: residual(xp_ref))
    pl.when(i >= npt)(lambda: residual(xs_ref))


def _outproj(ym, yr, wa, wb, x_p, x_s, gate, cfg, tm=512, tn=1024):
    n, d = cfg.n_rows, x_p.shape[1]
    ka, kb = ym.shape[1], yr.shape[1]
    s = gate.shape[0]
    return pl.pallas_call(
        functools.partial(_outproj_kernel, cfg=cfg, tm=tm),
        grid=(n // tm, d // tn),
        in_specs=[pl.BlockSpec((tm, ka), lambda i, j: (i, 0)),
                  pl.BlockSpec((tm, kb), lambda i, j: (i, 0)),
                  pl.BlockSpec((ka, tn), lambda i, j: (0, j)),
                  pl.BlockSpec((kb, tn), lambda i, j: (0, j))]
                 + _row_specs(cfg, tm, tn, lambda j: j)
                 + [pl.BlockSpec((s, tn), lambda i, j: (0, j))],
        out_specs=pl.BlockSpec((tm, tn), lambda i, j: (i, j)),
        out_shape=jax.ShapeDtypeStruct((n, d), F32),
        compiler_params=_params("parallel", "parallel"),
        name="outproj",
    )(ym, yr, wa, wb, x_p, x_s, gate)


def _route(lg, n_experts):
    lane = lax.broadcasted_iota(jnp.int32, lg.shape, 1)
    first = lambda mask: jnp.min(jnp.where(mask, lane, LANES), axis=-1, keepdims=True)
    is_grp = lane < E_GROUPS
    gmax = jnp.max(jnp.where(is_grp, lg, -jnp.inf), axis=-1, keepdims=True)
    gsel = first(is_grp & (lg == gmax))
    pg = 1.0 / jnp.sum(jnp.where(is_grp, jnp.exp(lg - gmax), 0.0), axis=-1, keepdims=True)
    e_lane = lane - E_GROUPS
    in_grp = (e_lane >= 0) & (e_lane < n_experts) & (e_lane // E_PER_GROUP == gsel)
    emax = jnp.max(jnp.where(in_grp, lg, -jnp.inf), axis=-1, keepdims=True)
    p = jnp.where(in_grp, jnp.exp(lg - emax), 0.0)
    eprob = p / jnp.sum(p, axis=-1, keepdims=True)
    v1 = jnp.max(jnp.where(in_grp, eprob, -1.0), axis=-1, keepdims=True)
    i1 = first(in_grp & (eprob == v1))
    rest = in_grp & (lane != i1)
    v2 = jnp.max(jnp.where(rest, eprob, -1.0), axis=-1, keepdims=True)
    i2 = first(rest & (eprob == v2))
    tot = v1 + v2
    cols = [(i1 - E_GROUPS).astype(F32), (i2 - E_GROUPS).astype(F32), v1 / tot * pg, v2 / tot * pg]
    out = jnp.zeros(lg.shape, F32)
    for c, val in enumerate(cols):
        out = jnp.where(lane == c, val, out)
    return out


def _router_kernel(x_ref, shift_ref, scale_ref, g_ref, wh_ref, wl_ref, br_ref, h_ref, rt_ref, *, cfg, tm, n_experts):
    i = pl.program_id(0)
    for s in range(tm // CHUNK):
        seq = _seq_of_chunk(i * (tm // CHUNK) + s, cfg)
        rows = slice(s * CHUNK, (s + 1) * CHUNK)
        h = _modulated_norm(x_ref[rows, :], g_ref[...], scale_ref[pl.ds(seq, 1), :], shift_ref[pl.ds(seq, 1), :])
        h_hi, h_lo = _split_hi_lo(h)
        h_ref[rows, :] = h_hi
        dot = lambda a, b: jnp.dot(a, b[...], preferred_element_type=F32)
        logits = dot(h_hi, wh_ref) + (dot(h_hi, wl_ref) + dot(h_lo, wh_ref)) + br_ref[...]
        rt_ref[rows, :] = _route(logits, n_experts)


def _router(x1, shift, scale, g, wr, br, cfg, n_experts, tm=256):
    n, d = x1.shape
    assert E_GROUPS + n_experts <= LANES and n_experts == E_GROUPS * E_PER_GROUP
    full = lambda a: pl.BlockSpec(a.shape, lambda i: (0,) * a.ndim)
    w_hi, w_lo = _split_hi_lo(wr)
    return pl.pallas_call(
        functools.partial(_router_kernel, cfg=cfg, tm=tm, n_experts=n_experts),
        grid=(n // tm,),
        in_specs=[pl.BlockSpec((tm, d), lambda i: (i, 0)), full(shift), full(scale), full(g),
                  full(w_hi), full(w_lo), full(br)],
        out_specs=[pl.BlockSpec((tm, d), lambda i: (i, 0)), pl.BlockSpec((tm, LANES), lambda i: (i, 0))],
        out_shape=[jax.ShapeDtypeStruct((n, d), BF16), jax.ShapeDtypeStruct((n, LANES), F32)],
        compiler_params=_params("parallel"),
        name="router",
    )(x1, shift, scale, g, w_hi, w_lo, br)


def _new_weights(be_ref):
    i = pl.program_id(1)
    return jnp.logical_or(i == 0, be_ref[i] != be_ref[jnp.maximum(i - 1, 0)])


def _for_valid_rows(nv_ref, out_ref, compute):
    nv = nv_ref[pl.program_id(1)]
    for s in range(out_ref.shape[0] // E_SUB):
        rows = slice(s * E_SUB, (s + 1) * E_SUB)

        @pl.when(nv > s * E_SUB)
        def _():
            out_ref[rows, :] = compute(rows)

        @pl.when(nv <= s * E_SUB)
        def _():
            out_ref[rows, :] = jnp.zeros((E_SUB, out_ref.shape[1]), out_ref.dtype)


def _expert_up_kernel(be_ref, nv_ref, x_ref, wg_ref, wu_ref, h_ref, wg_s, wu_s):
    @pl.when(_new_weights(be_ref))
    def _():
        wg_s[...] = wg_ref[0].astype(BF16)
        wu_s[...] = wu_ref[0].astype(BF16)

    def hidden(rows):
        x = x_ref[rows, :]
        gate = jnp.dot(x, wg_s[...], preferred_element_type=F32)
        up = jnp.dot(x, wu_s[...], preferred_element_type=F32)
        return (_silu(gate) * up).astype(BF16)

    _for_valid_rows(nv_ref, h_ref, hidden)


def _expert_down_kernel(be_ref, nv_ref, h_ref, wd_ref, y_ref, wd_s):
    @pl.when(_new_weights(be_ref))
    def _():
        wd_s[...] = wd_ref[0].astype(BF16)

    _for_valid_rows(nv_ref, y_ref, lambda rows: jnp.dot(h_ref[rows, :], wd_s[...], preferred_element_type=F32))


def _experts(xb, blk_e, n_valid, wg, wu, wd, tm, tf=512, tn=2048):
    rows, d = xb.shape
    f = wg.shape[2]
    nblk = rows // tm
    tf, tn = min(tf, f), min(tn, d)
    assert tm % E_SUB == 0 and f % tf == 0 and d % tn == 0
    hid = pl.pallas_call(
        _expert_up_kernel,
        grid_spec=pltpu.PrefetchScalarGridSpec(
            num_scalar_prefetch=2, grid=(f // tf, nblk),
            in_specs=[pl.BlockSpec((tm, d), lambda j, i, be, nv: (i, 0)),
                      pl.BlockSpec((1, d, tf), lambda j, i, be, nv: (be[i], 0, j)),
                      pl.BlockSpec((1, d, tf), lambda j, i, be, nv: (be[i], 0, j))],
            out_specs=pl.BlockSpec((tm, tf), lambda j, i, be, nv: (i, j)),
            scratch_shapes=[pltpu.VMEM((d, tf), BF16), pltpu.VMEM((d, tf), BF16)]),
        out_shape=jax.ShapeDtypeStruct((rows, f), BF16),
        compiler_params=_params("arbitrary", "arbitrary"),
        name="expert_up",
    )(blk_e, n_valid, xb, wg, wu)
    return pl.pallas_call(
        _expert_down_kernel,
        grid_spec=pltpu.PrefetchScalarGridSpec(
            num_scalar_prefetch=2, grid=(d // tn, nblk),
            in_specs=[pl.BlockSpec((tm, f), lambda j, i, be, nv: (i, 0)),
                      pl.BlockSpec((1, f, tn), lambda j, i, be, nv: (be[i], 0, j))],
            out_specs=pl.BlockSpec((tm, tn), lambda j, i, be, nv: (i, j)),
            scratch_shapes=[pltpu.VMEM((f, tn), BF16)]),
        out_shape=jax.ShapeDtypeStruct((rows, d), F32),
        compiler_params=_params("arbitrary", "arbitrary"),
        name="expert_down",
    )(blk_e, n_valid, hid, wd)


def _final_kernel(pos_ref, x_ref, rt_ref, gate_ref, g_ref, y_hbm, o_ref, ybuf, sem, *, cfg, tm, tile0, ntiles):
    i = pl.program_id(0)

    def row_copy(pos, slot, k, r):
        return pltpu.make_async_copy(y_hbm.at[pl.ds(pos, 1), :], ybuf.at[slot, k, pl.ds(r, 1), :], sem.at[slot])

    def fetch(tile, slot):
        base = (tile + tile0) * (tm * TOP_K)

        def body(r, c):
            for k in range(TOP_K):
                row_copy(pos_ref[base + r * TOP_K + k], slot, k, r).start()
            return c

        lax.fori_loop(0, tm, body, 0, unroll=8)

    def wait_all(slot):
        def body(r, c):
            for k in range(TOP_K):
                row_copy(0, slot, k, r).wait()
            return c

        lax.fori_loop(0, tm, body, 0, unroll=8)

    pl.when(i == 0)(lambda: fetch(0, 0))
    pl.when(i + 1 < ntiles)(lambda: fetch(i + 1, (i + 1) % 2))
    slot = i % 2
    wait_all(slot)

    for s in range(tm // CHUNK):
        seq = _seq_of_chunk((i + tile0) * (tm // CHUNK) + s, cfg)
        rows = slice(s * CHUNK, (s + 1) * CHUNK)
        rt = rt_ref[rows, :]
        moe = ybuf[slot, 0, rows, :] * rt[:, TOP_K:TOP_K + 1]
        for k in range(1, TOP_K):
            moe = moe + ybuf[slot, k, rows, :] * rt[:, TOP_K + k:TOP_K + k + 1]
        x = x_ref[rows, :] + gate_ref[pl.ds(seq, 1), :] * moe
        o_ref[rows, :] = x * lax.rsqrt(jnp.mean(x * x, axis=-1, keepdims=True) + NORM_EPS) * g_ref[...]


def _final(x1, yb, slot_pos, route, gate, g, cfg, row0, nrows, tm=256):
    d = x1.shape[1]
    assert row0 % tm == 0 and nrows % tm == 0
    tile0 = row0 // tm
    ntiles = nrows // tm
    full = lambda a: pl.BlockSpec(a.shape, lambda i, pos: (0,) * a.ndim)
    return pl.pallas_call(
        functools.partial(_final_kernel, cfg=cfg, tm=tm, tile0=tile0, ntiles=ntiles),
        grid_spec=pltpu.PrefetchScalarGridSpec(
            num_scalar_prefetch=1, grid=(ntiles,),
            in_specs=[pl.BlockSpec((tm, d), lambda i, pos: (i + tile0, 0)),
                      pl.BlockSpec((tm, LANES), lambda i, pos: (i + tile0, 0)),
                      full(gate), full(g), pl.BlockSpec(memory_space=pl.ANY)],
            out_specs=pl.BlockSpec((tm, d), lambda i, pos: (i, 0)),
            scratch_shapes=[pltpu.VMEM((2, TOP_K, tm, d), F32), pltpu.SemaphoreType.DMA((2,))]),
        out_shape=jax.ShapeDtypeStruct((nrows, d), F32),
        compiler_params=_params("arbitrary"),
        name="final_norm",
    )(slot_pos.reshape(-1), x1, route, gate, g, yb)


def _dispatch(route, n_experts, tm):
    n = route.shape[0]
    eid = route[:, :TOP_K].astype(jnp.int32)

    m = n * TOP_K
    e_flat = eid.reshape(m)
    order = jnp.argsort(e_flat).astype(jnp.int32)
    rank = jnp.argsort(order).astype(jnp.int32)
    e_sorted = e_flat[order]
    experts = jnp.arange(n_experts, dtype=jnp.int32)
    counts = jnp.sum(e_flat[:, None] == experts[None, :], axis=0, dtype=jnp.int32)
    start = jnp.cumsum(counts) - counts
    padded = (counts + tm - 1) // tm * tm
    pend = jnp.cumsum(padded)
    pstart = pend - padded
    dest_sorted = pstart[e_sorted] + jnp.arange(m, dtype=jnp.int32) - start[e_sorted]
    slot_pos = dest_sorted[rank].reshape(n, TOP_K)
    n_blocks = -(-(m + n_experts * (tm - 1)) // tm)
    rows = n_blocks * tm
    r = jnp.arange(rows, dtype=jnp.int32)
    owner = lambda row: jnp.minimum(jnp.sum(row[:, None] >= pend[None, :], axis=1, dtype=jnp.int32), n_experts - 1)
    e_r = owner(r)
    idx = r - pstart[e_r]
    row_tok = jnp.where(idx < counts[e_r], order[jnp.clip(start[e_r] + idx, 0, m - 1)] // TOP_K, r % n)
    n_used = pend[-1] // tm
    blk = jnp.arange(n_blocks, dtype=jnp.int32)
    blk_e = owner(blk * tm)
    n_valid = jnp.where(blk < n_used, jnp.clip(pstart[blk_e] + counts[blk_e] - blk * tm, 0, tm), 0).astype(jnp.int32)
    last_e = blk_e[jnp.maximum(n_used - 1, 0)]
    blk_e = jnp.where(blk < n_used, blk_e, last_e)
    return row_tok, slot_pos, blk_e, n_valid


def _to_window(a, cfg):
    return jnp.concatenate([a, jnp.zeros(a.shape[:-1] + (cfg.win - cfg.r_cols,), a.dtype)], axis=-1)


def _pad_rows(a, rows):
    return jnp.concatenate([a, jnp.zeros((rows - a.shape[0],) + a.shape[1:], a.dtype)], axis=0)


def kernel(x_prompt, x_sample, state_conv, state_ssm, state_shift, state_wkv, c_prompt, c_sample, norm1_g, w_mod, b_mod, w_in, conv_w, conv_b, dt_bias, a_log, d_skip, m_norm_g, shift_mu, w0, w_up, a0, a_up, g_up, k_k, k_a, r_k, ln_x_g, ln_x_b, w_out, norm2_g, w_grp, b_grp, w_erouter, b_erouter, e_gate, e_up, e_down, final_norm_g):
    assert w_mod.shape[0] == 1, "single-layer trunk"
    bp, tp, d = x_prompt.shape
    bs, ts, _ = x_sample.shape
    mi = m_norm_g.shape[-1]
    rw = w0.shape[-1]
    lw, la, lg = w_up.shape[1], a_up.shape[1], g_up.shape[1]
    cfg = Cfg(d, bp, tp, bs, ts, mi, rw, lw, la, lg)
    nc = cfg.nc
    assert tp % CHUNK == 0 and ts % CHUNK == 0 and tp >= CONV_W - 1 and ts >= CONV_W - 1
    assert w_in.shape[-1] == cfg.o_rw + cfg.mh + cfg.r_cols and cfg.o_rw % LANES == 0
    assert state_conv.shape[-1] == cfg.cd and lw <= LANES and la <= LANES and cfg.mh <= LANES
    n = cfg.n_rows
    n_seq = cfg.n_seq
    mh, rh = cfg.mh, cfg.rh
    npair = rw // PAIR
    n_experts = e_gate.shape[1]

    x_p = x_prompt.reshape(bp * tp, d)
    x_s = x_sample.reshape(bs * ts, d)
    c_all = jnp.concatenate([c_prompt, c_sample], axis=0)

    mod = _modulation(c_all, w_mod[0], b_mod[0])
    shift1, scale1, gate1, shift2, scale2, gate2 = [mod[:, i * d:(i + 1) * d] for i in range(6)]

    h1 = _prenorm(x_p, x_s, shift1, scale1, norm1_g, cfg)
    w_in_t = w_in[0].T
    proj = _inproj(h1, w_in_t, cfg)
    dt_raw = _dtproj(h1, w_in_t, cfg)

    zeros = lambda b, *s: jnp.zeros((b,) + s, F32)
    conv0 = jnp.concatenate([zeros(bp, CONV_W - 1, cfg.cd), state_conv[0]], axis=0)
    conv8 = jnp.concatenate([zeros(n_seq, 8 - (CONV_W - 1), cfg.cd), conv0], axis=1)
    nst = state_ssm.shape[-1]
    ssm0 = jnp.concatenate([zeros(bp, mi, nst), state_ssm[0].reshape(bs, mi, nst)], axis=0)
    lane_pad = lambda a: jnp.concatenate([a.reshape(1, -1), jnp.zeros((1, LANES - a.shape[-1]), F32)], axis=1)
    expand = ((jnp.arange(2 * LANES)[:, None] % LANES) == (jnp.arange(mi)[None, :] // HEAD)).astype(BF16)
    ym, ssm_new = _mamba(proj, dt_raw, conv8, ssm0, conv_w[0], conv_b[0].reshape(1, -1), lane_pad(dt_bias[0]),
                         lane_pad(a_log[0]), jnp.repeat(d_skip[0], HEAD).reshape(1, mi), m_norm_g[0].reshape(1, mi),
                         expand, cfg)

    sh0 = jnp.concatenate([zeros(bp, 1, state_shift.shape[-1]), state_shift[0]], axis=0)
    sh8 = jnp.concatenate([zeros(n_seq, 7, cfg.win), _to_window(sh0, cfg)], axis=1)
    to_pairs = lambda s: s.reshape(-1, npair, 2, HEAD, HEAD).transpose(0, 1, 3, 2, 4).reshape(-1, npair, HEAD, PAIR)
    from_pairs = lambda s: s.reshape(-1, npair, HEAD, 2, HEAD).transpose(0, 1, 3, 2, 4).reshape(-1, rh, HEAD, HEAD)
    wkv0 = jnp.concatenate([zeros(bp, npair, HEAD, PAIR), to_pairs(state_wkv[0])], axis=0)
    row = lambda a: a.reshape(1, -1)
    yr, wkv_new = _rwkv(proj, sh8, wkv0, _to_window(row(shift_mu[0]), cfg), row(w0[0]),
                        _pad_rows(w_up[0], LANES), row(a0[0]), _pad_rows(a_up[0], LANES), g_up[0],
                        row(k_k[0]), row(k_a[0]), row(r_k[0]), row(ln_x_g[0]), row(ln_x_b[0]), cfg)

    wo = w_out[0].astype(BF16)
    x1 = _outproj(ym, yr, wo[:mi], wo[mi:], x_p, x_s, gate1, cfg)
    wr = jnp.concatenate([w_grp[0], w_erouter[0], jnp.zeros((d, LANES - E_GROUPS - n_experts), F32)], axis=1)
    br = lane_pad(jnp.concatenate([b_grp[0], b_erouter[0]]))
    h2, route = _router(x1, shift2, scale2, norm2_g, wr, br, cfg, n_experts)

    tm_e = 2 * E_SUB
    row_tok, slot_pos, blk_e, n_valid = _dispatch(route, n_experts, tm_e)
    yb = _experts(h2[row_tok], blk_e, n_valid, e_gate[0], e_up[0], e_down[0], tm_e)

    np_ = bp * tp
    fg = final_norm_g.reshape(1, d)
    y_prompt = _final(x1, yb, slot_pos, route, gate2, fg, cfg, 0, np_).reshape(bp, tp, d)
    y_sample = _final(x1, yb, slot_pos, route, gate2, fg, cfg, np_, bs * ts).reshape(bs, ts, d)
    tails = lambda b, t, base: jnp.stack(
        [lax.slice(proj, (base + (i + 1) * t - (CONV_W - 1), 0), (base + (i + 1) * t, nc)) for i in range(b)])
    pp = tails(bp, tp, 0)
    ps = tails(bs, ts, np_)
    conv_of = lambda p: p[:, :, mi:mi + cfg.cd][None]
    shift_of = lambda p: p[:, -1:, cfg.o_rw:cfg.o_rw + cfg.r_cols][None]
    ssm_new = ssm_new.reshape(n_seq, mh, HEAD, nst)
    wkv_new = from_pairs(wkv_new)
    return (y_prompt, y_sample,
            conv_of(pp), ssm_new[:bp][None], shift_of(pp), wkv_new[:bp][None],
            conv_of(ps), ssm_new[bp:][None], shift_of(ps), wkv_new[bp:][None])
```

```python
import functools
from typing import NamedTuple

import jax
import jax.numpy as jnp
from jax import lax
from jax.experimental import pallas as pl
from jax.experimental.pallas import tpu as pltpu

F32 = jnp.float32
BF16 = jnp.bfloat16
HI = lax.Precision.HIGHEST

LANES = 128
CHUNK = 64
HEAD = 64
PAIR = 2 * HEAD
NORM_EPS = 1e-6
M_NORM_EPS = 1e-5
R_LN_EPS = 64e-5
CONV_W = 4
E_GROUPS = 4
E_PER_GROUP = 8
TOP_K = 2
VMEM_LIMIT = 56 * 1024 * 1024
E_SUB = 256
EXP_NEG_HALF = 0.6065306597126334
SA_GROUP = 16
SCAN_STEPS = 32


class Cfg(NamedTuple):
    d: int
    bp: int
    tp: int
    bs: int
    ts: int
    mi: int
    rw: int
    lw: int
    la: int
    lg: int

    @property
    def cpp(self):
        return self.tp // CHUNK

    @property
    def cps(self):
        return self.ts // CHUNK

    @property
    def n_chunks(self):
        return self.bp * self.cpp + self.bs * self.cps

    @property
    def n_seq(self):
        return self.bp + self.bs

    @property
    def n_rows(self):
        return self.n_chunks * CHUNK

    @property
    def groups(self):
        return self.mi // 256

    @property
    def nb(self):
        return self.groups * LANES

    @property
    def cd(self):
        return self.mi + 2 * self.nb

    @property
    def mh(self):
        return self.mi // HEAD

    @property
    def rh(self):
        return self.rw // HEAD

    @property
    def o_rw(self):
        return self.mi + self.cd

    @property
    def r_cols(self):
        return 3 * self.rw + self.lw + self.la + self.lg

    @property
    def tail(self):
        return -(-(self.lw + self.la + self.lg) // LANES) * LANES

    @property
    def win(self):
        return 3 * self.rw + self.tail

    @property
    def nc(self):
        return self.o_rw + self.win


def _seq_of_chunk(c, cfg):
    npc = cfg.bp * cfg.cpp
    return jnp.where(c < npc, c // cfg.cpp, cfg.bp + (c - npc) // cfg.cps)


def _is_first_chunk(c, cfg):
    npc = cfg.bp * cfg.cpp
    return jnp.where(c < npc, c % cfg.cpp == 0, (c - npc) % cfg.cps == 0)


def _silu(x):
    return x * jax.nn.sigmoid(x)


def _nt(a, b, **kw):
    return lax.dot_general(a, b, (((1,), (1,)), ((), ())), preferred_element_type=F32, **kw)


def _tn(a, b, **kw):
    return lax.dot_general(a, b, (((0,), (0,)), ((), ())), preferred_element_type=F32, **kw)


def _params(*sem):
    return pltpu.CompilerParams(dimension_semantics=sem, vmem_limit_bytes=VMEM_LIMIT)


def _mod_kernel(c_ref, w_ref, b_ref, o_ref):
    o_ref[...] = jnp.dot(_silu(c_ref[...]), w_ref[...], preferred_element_type=F32) + b_ref[...]


def _modulation(c_all, w_mod, b_mod):
    s, d = c_all.shape
    cols = w_mod.shape[1]
    tn = 512
    return pl.pallas_call(
        _mod_kernel,
        grid=(cols // tn,),
        in_specs=[pl.BlockSpec((s, d), lambda j: (0, 0)),
                  pl.BlockSpec((d, tn), lambda j: (0, j)),
                  pl.BlockSpec((1, tn), lambda j: (0, j))],
        out_specs=pl.BlockSpec((s, tn), lambda j: (0, j)),
        out_shape=jax.ShapeDtypeStruct((s, cols), F32),
        compiler_params=_params("parallel"),
        name="modulation",
    )(c_all, w_mod, b_mod.reshape(1, cols))


def _modulated_norm(x, g, scale, shift):
    y = x * lax.rsqrt(jnp.mean(x * x, axis=-1, keepdims=True) + NORM_EPS) * g
    return y * (1.0 + scale) + shift


def _prompt_tiles(cfg, tm):
    assert (cfg.bp * cfg.tp) % tm == 0 and (cfg.bs * cfg.ts) % tm == 0 and tm % CHUNK == 0
    return cfg.bp * cfg.tp // tm


def _row_specs(cfg, tm, width, col=None):
    npt = _prompt_tiles(cfg, tm)
    c = (lambda *r: 0) if col is None else col
    return [pl.BlockSpec((tm, width), lambda i, *r: (jnp.minimum(i, npt - 1), jnp.where(i < npt, c(*r), 0))),
            pl.BlockSpec((tm, width), lambda i, *r: (jnp.maximum(i - npt, 0), jnp.where(i >= npt, c(*r), 0)))]


def _prenorm_kernel(xp_ref, xs_ref, shift_ref, scale_ref, g_ref, h_ref, *, cfg, tm):
    i = pl.program_id(0)

    def norm_rows(x_ref):
        for s in range(tm // CHUNK):
            seq = _seq_of_chunk(i * (tm // CHUNK) + s, cfg)
            rows = slice(s * CHUNK, (s + 1) * CHUNK)
            h = _modulated_norm(x_ref[rows, :], g_ref[...], scale_ref[pl.ds(seq, 1), :], shift_ref[pl.ds(seq, 1), :])
            h_ref[rows, :] = h.astype(BF16)

    npt = _prompt_tiles(cfg, tm)
    pl.when(i < npt)(lambda: norm_rows(xp_ref))
    pl.when(i >= npt)(lambda: norm_rows(xs_ref))


def _prenorm(x_p, x_s, shift, scale, g, cfg, tm=256):
    d = x_p.shape[1]
    full = lambda a: pl.BlockSpec(a.shape, lambda i: (0,) * a.ndim)
    return pl.pallas_call(
        functools.partial(_prenorm_kernel, cfg=cfg, tm=tm),
        grid=(cfg.n_rows // tm,),
        in_specs=_row_specs(cfg, tm, d) + [full(shift), full(scale), full(g)],
        out_specs=pl.BlockSpec((tm, d), lambda i: (i, 0)),
        out_shape=jax.ShapeDtypeStruct((cfg.n_rows, d), BF16),
        compiler_params=_params("parallel"),
        name="prenorm",
    )(x_p, x_s, shift, scale, g)


def _inproj_kernel(h_ref, w_ref, wn_ref, o_ref, wb_s, *, first_shifted, shift, tn, n_features):
    j = pl.program_id(0)
    new_tile = pl.program_id(1) == 0

    @pl.when(jnp.logical_and(new_tile, j < first_shifted))
    def _():
        wb_s[...] = w_ref[...].astype(BF16)

    @pl.when(jnp.logical_and(new_tile, j >= first_shifted))
    def _():
        w = jnp.concatenate([w_ref[...], wn_ref[...]], axis=0)[shift:shift + tn, :]
        feature = j * tn + shift + lax.broadcasted_iota(jnp.int32, w.shape, 0)
        wb_s[...] = jnp.where(feature < n_features, w, 0.0).astype(BF16)

    o_ref[...] = _nt(h_ref[...], wb_s[...])


def _inproj(h, wt, cfg, tm=1024, tn=512):
    n, d = h.shape
    nout = cfg.o_rw + cfg.win
    assert cfg.o_rw % tn == 0 and nout % tn == 0 and n % tm == 0 and cfg.mh <= LANES and cfg.mh % 8 == 0
    last_next = (wt.shape[0] - 1) // LANES
    return pl.pallas_call(
        functools.partial(_inproj_kernel, first_shifted=cfg.o_rw // tn, shift=cfg.mh, tn=tn, n_features=wt.shape[0]),
        grid=(nout // tn, n // tm),
        in_specs=[pl.BlockSpec((tm, d), lambda j, i: (i, 0)),
                  pl.BlockSpec((tn, d), lambda j, i: (j, 0)),
                  pl.BlockSpec((LANES, d), lambda j, i: (jnp.minimum((j + 1) * (tn // LANES), last_next), 0))],
        out_specs=pl.BlockSpec((tm, tn), lambda j, i: (i, j)),
        out_shape=jax.ShapeDtypeStruct((n, nout), F32),
        scratch_shapes=[pltpu.VMEM((tn, d), BF16)],
        compiler_params=_params("arbitrary", "arbitrary"),
        name="inproj",
    )(h, wt, wt)


def _dtproj_kernel(h_ref, w_ref, o_ref):
    o_ref[...] = _nt(h_ref[...], w_ref[...].astype(BF16))


def _dtproj(h, wt, cfg, tm=1024):
    n, d = h.shape
    return pl.pallas_call(
        _dtproj_kernel,
        grid=(n // tm,),
        in_specs=[pl.BlockSpec((tm, d), lambda i: (i, 0)), pl.BlockSpec((LANES, d), lambda i: (cfg.o_rw // LANES, 0))],
        out_specs=pl.BlockSpec((tm, LANES), lambda i: (i, 0)),
        out_shape=jax.ShapeDtypeStruct((n, LANES), F32),
        compiler_params=_params("parallel"),
        name="dtproj",
    )(h, wt)


def _mamba_kernel(zx_ref, dt_ref, cst_ref, sst_ref, cw_ref, cb_ref, dtb_ref, alog_ref, dsk_ref, ng_ref, exp_ref,
                  ym_ref, ssm_ref, buf, u_scr, *, cfg):
    c = pl.program_id(0)
    mi, nb, cd = cfg.mi, cfg.nb, cfg.cd
    L = CHUNK

    @pl.when(_is_first_chunk(c, cfg))
    def _():
        buf[0:8, :] = cst_ref[0]
        ssm_ref[0] = sst_ref[0]

    xbc = zx_ref[:, mi:mi + cd]
    buf[8:8 + L, :] = xbc
    conv = (cb_ref[...] + buf[5:5 + L, :] * cw_ref[0:1, :] + buf[6:6 + L, :] * cw_ref[1:2, :]
            + buf[7:7 + L, :] * cw_ref[2:3, :] + xbc * cw_ref[3:4, :])
    buf[0:8, :] = buf[L:L + 8, :]
    u_scr[...] = _silu(conv)

    dt = jax.nn.softplus(dt_ref[...] + dtb_ref[...])
    da = dt * (-jnp.exp(alog_ref[...]))
    row = lax.broadcasted_iota(jnp.int32, (L, mi), 0)
    pos = lax.broadcasted_iota(jnp.int32, (L, mi), 1) % HEAD
    tri2 = (lax.broadcasted_iota(jnp.int32, (L, 2 * L), 0)
            >= lax.broadcasted_iota(jnp.int32, (L, 2 * L), 1) % L).astype(BF16)
    lane = lax.broadcasted_iota(jnp.int32, (L, PAIR), 1)
    row16 = lax.broadcasted_iota(jnp.int32, (16, mi), 0)
    ones16 = jnp.ones((16, LANES), BF16)

    dt_hl = jnp.concatenate(_split_hi_lo(dt), axis=1)
    da_hl = jnp.concatenate(_split_hi_lo(da), axis=1)
    dte = jnp.dot(dt_hl, exp_ref[...], preferred_element_type=F32)
    dae = jnp.dot(da_hl, exp_ref[...], preferred_element_type=F32)
    cum = jnp.dot(tri2, jnp.concatenate(_split_hi_lo(dae), axis=0), preferred_element_type=F32)
    cum_row = jnp.sum(jnp.where(row == pos, cum, 0.0), axis=0, keepdims=True)
    last = cum[L - 1:L, :]
    decay = jnp.exp(jnp.where(row >= pos, cum - cum_row, -jnp.inf))
    xdt = u_scr[:, 0:mi] * dte
    x_end = xdt * jnp.exp(last - cum)
    e_cum = jnp.exp(cum)
    e_last = jnp.exp(last)
    e_hi = e_last.astype(BF16).astype(F32)
    e_rows = jnp.where(row16 == 0, e_hi, jnp.where(row16 == 1, e_last - e_hi, 0.0)).astype(BF16)

    groups = range(cfg.groups)
    gs = lambda g: slice(g * 256, (g + 1) * 256)
    b_of = lambda g: u_scr[:, mi + g * LANES:mi + (g + 1) * LANES]
    c_of = lambda g: u_scr[:, mi + nb + g * LANES:mi + nb + (g + 1) * LANES]
    sc2 = [_nt(c_of(g), jnp.concatenate([b_of(g), b_of(g)], axis=0)) for g in groups]
    y_diag = []
    for g in groups:
        ys = []
        for q in range(2):
            ps = slice(g * 256 + q * PAIR, g * 256 + (q + 1) * PAIR)
            xq = xdt[:, ps]
            rhs = jnp.concatenate([jnp.where(lane < HEAD, xq, 0.0), jnp.where(lane >= HEAD, xq, 0.0)], axis=0)
            ys.append(jnp.dot(decay[:, ps] * sc2[g], rhs, preferred_element_type=F32))
        y_diag.append(jnp.concatenate(ys, axis=1))
    y_off = [_nt(c_of(g), ssm_ref[0, gs(g), :]) for g in groups]
    new = [_tn(x_end[:, gs(g)], b_of(g)) for g in groups]
    dcol = [_tn(e_rows[:, gs(g)], ones16) for g in groups]
    for g in groups:
        cs = gs(g)
        ssm_ref[0, cs, :] = ssm_ref[0, cs, :] * dcol[g] + new[g]
        y = y_diag[g] + y_off[g] * e_cum[:, cs] + u_scr[:, cs] * dsk_ref[:, cs]
        y = y * _silu(zx_ref[:, cs])
        y = y * lax.rsqrt(jnp.mean(y * y, axis=-1, keepdims=True) + M_NORM_EPS)
        ym_ref[:, cs] = (y * ng_ref[:, cs]).astype(BF16)


def _mamba(proj, dt_raw, conv_state8, ssm_state, cw, cb, dtb, alog, dsk, ng, expand, cfg):
    n = proj.shape[0]
    mi, cd = cfg.mi, cfg.cd
    nst = ssm_state.shape[-1]
    seq = lambda c: _seq_of_chunk(c, cfg)
    full = lambda a: pl.BlockSpec(a.shape, lambda c: (0,) * a.ndim)
    return pl.pallas_call(
        functools.partial(_mamba_kernel, cfg=cfg),
        grid=(cfg.n_chunks,),
        in_specs=[pl.BlockSpec((CHUNK, mi + cd), lambda c: (c, 0)),
                  pl.BlockSpec((CHUNK, LANES), lambda c: (c, 0)),
                  pl.BlockSpec((1, 8, cd), lambda c: (seq(c), 0, 0)),
                  pl.BlockSpec((1, mi, nst), lambda c: (seq(c), 0, 0)),
                  full(cw), full(cb), full(dtb), full(alog), full(dsk), full(ng), full(expand)],
        out_specs=[pl.BlockSpec((CHUNK, mi), lambda c: (c, 0)),
                   pl.BlockSpec((1, mi, nst), lambda c: (seq(c), 0, 0))],
        out_shape=[jax.ShapeDtypeStruct((n, mi), BF16),
                   jax.ShapeDtypeStruct((cfg.n_seq, mi, nst), F32)],
        scratch_shapes=[pltpu.VMEM((CHUNK + 8, cd), F32), pltpu.VMEM((CHUNK, cd), F32)],
        compiler_params=_params("arbitrary"),
        name="mamba",
    )(proj, dt_raw, conv_state8, ssm_state, cw, cb, dtb, alog, dsk, ng, expand)


def _seg_sum(x, ob):
    w = ob.shape[0]
    xb = x.astype(BF16)
    parts = [jnp.dot(xb[:, p * w:(p + 1) * w], ob, preferred_element_type=F32) for p in range(x.shape[1] // w)]
    return jnp.concatenate(parts, axis=1)


def _split_hi_lo(x):
    hi = x.astype(BF16)
    lo = (x - hi.astype(F32)).astype(BF16)
    return hi, lo


def _rwkv_kernel(rkv_ref, lora_ref, sh_ref, wkv_in_ref, mu_ref, w0_ref, wup_ref, a0_ref, aup_ref, gup_ref,
                 kk_ref, ka_ref, rk_ref, lng_ref, lnb_ref,
                 yr_ref, wkv_ref, pbuf, lbuf, w_s, a_s, b_s, k_s, y_s, vt_s, ut_s, y0_s, g_s, *, cfg):
    c = pl.program_id(0)
    rw = cfg.rw
    L = CHUNK
    npair = rw // PAIR

    @pl.when(_is_first_chunk(c, cfg))
    def _():
        pbuf[0:8, :] = sh_ref[0, :, 0:3 * rw]
        lbuf[0:8, :] = sh_ref[0, :, 3 * rw:]
        wkv_ref[0] = wkv_in_ref[0]

    pr = rkv_ref[...]
    pl_ = lora_ref[...]
    pbuf[8:8 + L, :] = pr
    lbuf[8:8 + L, :] = pl_
    xs = pr + (pbuf[7:7 + L, :] - pr) * mu_ref[:, 0:3 * rw]
    xl = pl_ + (lbuf[7:7 + L, :] - pl_) * mu_ref[:, 3 * rw:]
    pbuf[0:8, :] = pbuf[L:L + 8, :]
    lbuf[0:8, :] = lbuf[L:L + 8, :]

    r = xs[:, 0:rw]
    k = xs[:, rw:2 * rw]
    v = xs[:, 2 * rw:3 * rw]
    wd = xl[:, 0:LANES]
    ad = xl[:, cfg.lw:cfg.lw + LANES]
    gd = xl[:, cfg.lw + cfg.la:cfg.lw + cfg.la + cfg.lg]

    lane = lax.broadcasted_iota(jnp.int32, (PAIR, PAIR), 1)
    rowi = lax.broadcasted_iota(jnp.int32, (PAIR, PAIR), 0)
    o_row = lax.broadcasted_iota(jnp.int32, (2 * PAIR, 2 * PAIR), 0)
    o_lane = lax.broadcasted_iota(jnp.int32, (2 * PAIR, 2 * PAIR), 1)
    ob = ((o_row // HEAD) == (o_lane // HEAD)).astype(BF16)

    w_z = w0_ref[...] + jnp.dot(jnp.tanh(wd), wup_ref[...], preferred_element_type=F32)
    logw = -(jax.nn.sigmoid(w_z) * EXP_NEG_HALF)
    w_s[...] = jnp.exp(logw)
    a = jax.nn.sigmoid(a0_ref[...] + jnp.dot(ad, aup_ref[...], preferred_element_type=F32))
    kk = k * kk_ref[...]
    kk = kk * lax.rsqrt(jnp.maximum(_seg_sum(kk * kk, ob), 1e-24))
    kh = k * (1.0 + (a - 1.0) * ka_ref[...])
    a_s[...] = -kk
    b_s[...] = kk * a
    k_s[...] = kh

    tri2 = (lax.broadcasted_iota(jnp.int32, (L, 2 * L), 0)
            >= lax.broadcasted_iota(jnp.int32, (L, 2 * L), 1) % L).astype(BF16)
    cum = jnp.dot(tri2, jnp.concatenate(_split_hi_lo(logw), axis=0), preferred_element_type=F32)
    p_inv = jnp.exp(-cum)
    r_p = r * jnp.exp(cum)
    b_p = b_s[...] * p_inv
    k_p = kh * p_inv
    lo_lanes = lane < HEAD
    g_row = lax.broadcasted_iota(jnp.int32, (2 * PAIR, PAIR), 0)
    g_lane = lax.broadcasted_iota(jnp.int32, (2 * PAIR, PAIR), 1)
    g_keep = (g_row % HEAD) <= (g_lane % HEAD)

    def by_head(x):
        x2 = jnp.concatenate([x, x], axis=0)
        return jnp.where(lo_lanes == (rowi < HEAD), x2, 0.0).astype(BF16)

    pairs = range(npair)
    lanes_of = lambda p: slice(p * PAIR, (p + 1) * PAIR)
    rows_of = lambda p: slice(p * HEAD, (p + 1) * HEAD)
    r_bd = [by_head(r_p[:, lanes_of(p)]) for p in pairs]
    y0 = [_nt(wkv_ref[0, p].astype(BF16), r_bd[p]) for p in pairs]
    gram = [_nt(jnp.concatenate([by_head(b_p[:, lanes_of(p)]), by_head(k_p[:, lanes_of(p)])], axis=0), r_bd[p])
            for p in pairs]
    vts = [v[:, lanes_of(p)].T for p in pairs]
    for p in pairs:
        y0_s[rows_of(p), :] = y0[p]
        g_s[p] = jnp.where(g_keep, gram[p], 0.0).astype(BF16)
        vjt = jnp.concatenate([vts[p][0:HEAD], vts[p][HEAD:]], axis=1)
        hi = vjt.astype(BF16).astype(F32)
        vt_s[rows_of(p), :] = jnp.concatenate([hi, vjt - hi], axis=1).astype(BF16)

    lane_t = lax.broadcasted_iota(jnp.int32, (HEAD, PAIR), 1) % HEAD
    crow = lax.broadcasted_iota(jnp.int32, (2 * PAIR, 2 * PAIR), 0)
    clane = lax.broadcasted_iota(jnp.int32, (2 * PAIR, 2 * PAIR), 1)
    same_head = ((crow % PAIR) // HEAD) == ((clane % PAIR) // HEAD)
    c_dt = (crow % HEAD) - clane // PAIR

    def steps(blk, carry):
        t0 = pl.multiple_of(blk * SCAN_STEPS, SCAN_STEPS)
        rows = pl.ds(t0, SCAN_STEPS)
        w8, a8, b8, k8 = w_s[rows, :], a_s[rows, :], b_s[rows, :], k_s[rows, :]
        s = [wkv_ref[0, p] for p in range(npair)]
        vcol2 = None
        for i in range(SCAN_STEPS):
            row = lambda x8, p: x8[i:i + 1, p * PAIR:(p + 1) * PAIR]
            this_step = lane_t == t0 + i
            if i % 2 == 0:
                sel = (same_head & (c_dt == t0 + i)).astype(BF16)
                vcol2 = jnp.dot(vt_s[...], sel, preferred_element_type=F32)
            vcol = vcol2[:, (i % 2) * PAIR:(i % 2 + 1) * PAIR]
            for p0 in range(0, npair, SA_GROUP):
                group = range(p0, min(p0 + SA_GROUP, npair))
                lhs = []
                for p in group:
                    prod = (s[p] * row(a8, p)).astype(BF16)
                    lhs.append(jnp.concatenate([prod[0:HEAD // 2], prod[HEAD // 2:]], axis=1))
                sa = jnp.dot(jnp.concatenate(lhs, axis=0), ob, preferred_element_type=F32)
                for q, p in enumerate(group):
                    half = sa[q * (HEAD // 2):(q + 1) * (HEAD // 2)]
                    sa_p = jnp.concatenate([half[:, 0:PAIR], half[:, PAIR:]], axis=0)
                    s[p] = s[p] * row(w8, p) + sa_p * row(b8, p) + vcol[p * HEAD:(p + 1) * HEAD] * row(k8, p)
                    pltpu.store(ut_s.at[p * HEAD:(p + 1) * HEAD, :], sa_p, mask=this_step)
        for p in range(npair):
            wkv_ref[0, p] = s[p]
        return carry

    ut_s[...] = jnp.zeros_like(ut_s)
    lax.fori_loop(0, L // SCAN_STEPS, steps, 0)

    yts = [jnp.dot(jnp.concatenate([ut_s[rows_of(p), :].astype(BF16), vt_s[rows_of(p), 0:PAIR]], axis=1), g_s[p],
                   preferred_element_type=F32) for p in pairs]
    ytt = [(y0_s[rows_of(p), :] + yts[p]).T for p in pairs]
    for p in pairs:
        y_s[:, lanes_of(p)] = jnp.concatenate([ytt[p][0:L], ytt[p][L:]], axis=1)

    y = y_s[...]
    mu = _seg_sum(y, ob) * (1.0 / HEAD)
    dlt = y - mu
    var = _seg_sum(dlt * dlt, ob) * (1.0 / HEAD)
    yn = dlt * lax.rsqrt(var + R_LN_EPS) * lng_ref[...] + lnb_ref[...]
    bonus = _seg_sum(r * k_s[...] * rk_ref[...], ob) * v
    g = jnp.dot(jax.nn.sigmoid(gd), gup_ref[...], preferred_element_type=F32)
    yr_ref[...] = ((yn + bonus) * g).astype(BF16)


def _rwkv(proj, shift8, wkv_pairs, mu, w0, wup, a0, aup, gup, kk, ka, rk, lng, lnb, cfg):
    n = proj.shape[0]
    rw, lp = cfg.rw, cfg.tail
    assert cfg.o_rw % (3 * rw) == 0 and (cfg.o_rw + 3 * rw) % lp == 0
    npair = rw // PAIR
    seq = lambda c: _seq_of_chunk(c, cfg)
    full = lambda a: pl.BlockSpec(a.shape, lambda c: (0,) * a.ndim)
    row_scr = pltpu.VMEM((CHUNK, rw), F32)
    return pl.pallas_call(
        functools.partial(_rwkv_kernel, cfg=cfg),
        grid=(cfg.n_chunks,),
        in_specs=[pl.BlockSpec((CHUNK, 3 * rw), lambda c: (c, cfg.o_rw // (3 * rw))),
                  pl.BlockSpec((CHUNK, lp), lambda c: (c, (cfg.o_rw + 3 * rw) // lp)),
                  pl.BlockSpec((1, 8, 3 * rw + lp), lambda c: (seq(c), 0, 0)),
                  pl.BlockSpec((1, npair, HEAD, PAIR), lambda c: (seq(c), 0, 0, 0)),
                  full(mu), full(w0), full(wup), full(a0), full(aup), full(gup),
                  full(kk), full(ka), full(rk), full(lng), full(lnb)],
        out_specs=[pl.BlockSpec((CHUNK, rw), lambda c: (c, 0)),
                   pl.BlockSpec((1, npair, HEAD, PAIR), lambda c: (seq(c), 0, 0, 0))],
        out_shape=[jax.ShapeDtypeStruct((n, rw), BF16),
                   jax.ShapeDtypeStruct((cfg.n_seq, npair, HEAD, PAIR), F32)],
        scratch_shapes=[pltpu.VMEM((CHUNK + 8, 3 * rw), F32), pltpu.VMEM((CHUNK + 8, lp), F32),
                        row_scr, row_scr, row_scr, row_scr, row_scr,
                        pltpu.VMEM((npair * HEAD, 2 * PAIR), BF16), pltpu.VMEM((npair * HEAD, PAIR), F32),
                        pltpu.VMEM((npair * HEAD, PAIR), F32), pltpu.VMEM((npair, 2 * PAIR, PAIR), BF16)],
        compiler_params=_params("arbitrary"),
        name="rwkv",
    )(proj, proj, shift8, wkv_pairs, mu, w0, wup, a0, aup, gup, kk, ka, rk, lng, lnb)


def _outproj_kernel(ym_ref, yr_ref, wa_ref, wb_ref, xp_ref, xs_ref, gate_ref, o_ref, *, cfg, tm):
    i = pl.program_id(0)
    acc = (jnp.dot(ym_ref[...], wa_ref[...], preferred_element_type=F32)
           + jnp.dot(yr_ref[...], wb_ref[...], preferred_element_type=F32))

    def residual(x_ref):
        for s in range(tm // CHUNK):
            seq = _seq_of_chunk(i * (tm // CHUNK) + s, cfg)
            rows = slice(s * CHUNK, (s + 1) * CHUNK)
            o_ref[rows, :] = x_ref[rows, :] + gate_ref[pl.ds(seq, 1), :] * acc[rows, :]

    npt = _prompt_tiles(cfg, tm)
    pl.when(i < npt)(lambda: residual(xp_ref))
    pl.when(i >= npt)(lambda: residual(xs_ref))


def _outproj(ym, yr, wa, wb, x_p, x_s, gate, cfg, tm=512, tn=1024):
    n, d = cfg.n_rows, x_p.shape[1]
    ka, kb = ym.shape[1], yr.shape[1]
    s = gate.shape[0]
    return pl.pallas_call(
        functools.partial(_outproj_kernel, cfg=cfg, tm=tm),
        grid=(n // tm, d // tn),
        in_specs=[pl.BlockSpec((tm, ka), lambda i, j: (i, 0)),
                  pl.BlockSpec((tm, kb), lambda i, j: (i, 0)),
                  pl.BlockSpec((ka, tn), lambda i, j: (0, j)),
                  pl.BlockSpec((kb, tn), lambda i, j: (0, j))]
                 + _row_specs(cfg, tm, tn, lambda j: j)
                 + [pl.BlockSpec((s, tn), lambda i, j: (0, j))],
        out_specs=pl.BlockSpec((tm, tn), lambda i, j: (i, j)),
        out_shape=jax.ShapeDtypeStruct((n, d), F32),
        compiler_params=_params("parallel", "parallel"),
        name="outproj",
    )(ym, yr, wa, wb, x_p, x_s, gate)


def _route(lg, n_experts):
    lane = lax.broadcasted_iota(jnp.int32, lg.shape, 1)
    first = lambda mask: jnp.min(jnp.where(mask, lane, LANES), axis=-1, keepdims=True)
    is_grp = lane < E_GROUPS
    gmax = jnp.max(jnp.where(is_grp, lg, -jnp.inf), axis=-1, keepdims=True)
    gsel = first(is_grp & (lg == gmax))
    pg = 1.0 / jnp.sum(jnp.where(is_grp, jnp.exp(lg - gmax), 0.0), axis=-1, keepdims=True)
    e_lane = lane - E_GROUPS
    in_grp = (e_lane >= 0) & (e_lane < n_experts) & (e_lane // E_PER_GROUP == gsel)
    emax = jnp.max(jnp.where(in_grp, lg, -jnp.inf), axis=-1, keepdims=True)
    p = jnp.where(in_grp, jnp.exp(lg - emax), 0.0)
    eprob = p / jnp.sum(p, axis=-1, keepdims=True)
    v1 = jnp.max(jnp.where(in_grp, eprob, -1.0), axis=-1, keepdims=True)
    i1 = first(in_grp & (eprob == v1))
    rest = in_grp & (lane != i1)
    v2 = jnp.max(jnp.where(rest, eprob, -1.0), axis=-1, keepdims=True)
    i2 = first(rest & (eprob == v2))
    tot = v1 + v2
    cols = [(i1 - E_GROUPS).astype(F32), (i2 - E_GROUPS).astype(F32), v1 / tot * pg, v2 / tot * pg]
    out = jnp.zeros(lg.shape, F32)
    for c, val in enumerate(cols):
        out = jnp.where(lane == c, val, out)
    return out


def _router_kernel(x_ref, shift_ref, scale_ref, g_ref, wh_ref, wl_ref, br_ref, h_ref, rt_ref, *, cfg, tm, n_experts):
    i = pl.program_id(0)
    for s in range(tm // CHUNK):
        seq = _seq_of_chunk(i * (tm // CHUNK) + s, cfg)
        rows = slice(s * CHUNK, (s + 1) * CHUNK)
        h = _modulated_norm(x_ref[rows, :], g_ref[...], scale_ref[pl.ds(seq, 1), :], shift_ref[pl.ds(seq, 1), :])
        h_hi, h_lo = _split_hi_lo(h)
        h_ref[rows, :] = h_hi
        dot = lambda a, b: jnp.dot(a, b[...], preferred_element_type=F32)
        logits = dot(h_hi, wh_ref) + (dot(h_hi, wl_ref) + dot(h_lo, wh_ref)) + br_ref[...]
        rt_ref[rows, :] = _route(logits, n_experts)


def _router(x1, shift, scale, g, wr, br, cfg, n_experts, tm=256):
    n, d = x1.shape
    assert E_GROUPS + n_experts <= LANES and n_experts == E_GROUPS * E_PER_GROUP
    full = lambda a: pl.BlockSpec(a.shape, lambda i: (0,) * a.ndim)
    w_hi, w_lo = _split_hi_lo(wr)
    return pl.pallas_call(
        functools.partial(_router_kernel, cfg=cfg, tm=tm, n_experts=n_experts),
        grid=(n // tm,),
        in_specs=[pl.BlockSpec((tm, d), lambda i: (i, 0)), full(shift), full(scale), full(g),
                  full(w_hi), full(w_lo), full(br)],
        out_specs=[pl.BlockSpec((tm, d), lambda i: (i, 0)), pl.BlockSpec((tm, LANES), lambda i: (i, 0))],
        out_shape=[jax.ShapeDtypeStruct((n, d), BF16), jax.ShapeDtypeStruct((n, LANES), F32)],
        compiler_params=_params("parallel"),
        name="router",
    )(x1, shift, scale, g, w_hi, w_lo, br)


def _new_weights(be_ref):
    i = pl.program_id(1)
    return jnp.logical_or(i == 0, be_ref[i] != be_ref[jnp.maximum(i - 1, 0)])


def _for_valid_rows(nv_ref, out_ref, compute):
    nv = nv_ref[pl.program_id(1)]
    for s in range(out_ref.shape[0] // E_SUB):
        rows = slice(s * E_SUB, (s + 1) * E_SUB)

        @pl.when(nv > s * E_SUB)
        def _():
            out_ref[rows, :] = compute(rows)

        @pl.when(nv <= s * E_SUB)
        def _():
            out_ref[rows, :] = jnp.zeros((E_SUB, out_ref.shape[1]), out_ref.dtype)


def _expert_up_kernel(be_ref, nv_ref, x_ref, wg_ref, wu_ref, h_ref, wg_s, wu_s):
    @pl.when(_new_weights(be_ref))
    def _():
        wg_s[...] = wg_ref[0].astype(BF16)
        wu_s[...] = wu_ref[0].astype(BF16)

    def hidden(rows):
        x = x_ref[rows, :]
        gate = jnp.dot(x, wg_s[...], preferred_element_type=F32)
        up = jnp.dot(x, wu_s[...], preferred_element_type=F32)
        return (_silu(gate) * up).astype(BF16)

    _for_valid_rows(nv_ref, h_ref, hidden)


def _expert_down_kernel(be_ref, nv_ref, h_ref, wd_ref, y_ref, wd_s):
    @pl.when(_new_weights(be_ref))
    def _():
        wd_s[...] = wd_ref[0].astype(BF16)

    _for_valid_rows(nv_ref, y_ref, lambda rows: jnp.dot(h_ref[rows, :], wd_s[...], preferred_element_type=F32))


def _experts(xb, blk_e, n_valid, wg, wu, wd, tm, tf=512, tn=2048):
    rows, d = xb.shape
    f = wg.shape[2]
    nblk = rows // tm
    tf, tn = min(tf, f), min(tn, d)
    assert tm % E_SUB == 0 and f % tf == 0 and d % tn == 0
    hid = pl.pallas_call(
        _expert_up_kernel,
        grid_spec=pltpu.PrefetchScalarGridSpec(
            num_scalar_prefetch=2, grid=(f // tf, nblk),
            in_specs=[pl.BlockSpec((tm, d), lambda j, i, be, nv: (i, 0)),
                      pl.BlockSpec((1, d, tf), lambda j, i, be, nv: (be[i], 0, j)),
                      pl.BlockSpec((1, d, tf), lambda j, i, be, nv: (be[i], 0, j))],
            out_specs=pl.BlockSpec((tm, tf), lambda j, i, be, nv: (i, j)),
            scratch_shapes=[pltpu.VMEM((d, tf), BF16), pltpu.VMEM((d, tf), BF16)]),
        out_shape=jax.ShapeDtypeStruct((rows, f), BF16),
        compiler_params=_params("arbitrary", "arbitrary"),
        name="expert_up",
    )(blk_e, n_valid, xb, wg, wu)
    return pl.pallas_call(
        _expert_down_kernel,
        grid_spec=pltpu.PrefetchScalarGridSpec(
            num_scalar_prefetch=2, grid=(d // tn, nblk),
            in_specs=[pl.BlockSpec((tm, f), lambda j, i, be, nv: (i, 0)),
                      pl.BlockSpec((1, f, tn), lambda j, i, be, nv: (be[i], 0, j))],
            out_specs=pl.BlockSpec((tm, tn), lambda j, i, be, nv: (i, j)),
            scratch_shapes=[pltpu.VMEM((f, tn), BF16)]),
        out_shape=jax.ShapeDtypeStruct((rows, d), F32),
        compiler_params=_params("arbitrary", "arbitrary"),
        name="expert_down",
    )(blk_e, n_valid, hid, wd)


def _final_kernel(pos_ref, x_ref, rt_ref, gate_ref, g_ref, y_hbm, o_ref, ybuf, sem, *, cfg, tm, tile0, ntiles):
    i = pl.program_id(0)

    def row_copy(pos, slot, k, r):
        return pltpu.make_async_copy(y_hbm.at[pl.ds(pos, 1), :], ybuf.at[slot, k, pl.ds(r, 1), :], sem.at[slot])

    def fetch(tile, slot):
        base = (tile + tile0) * (tm * TOP_K)

        def body(r, c):
            for k in range(TOP_K):
                row_copy(pos_ref[base + r * TOP_K + k], slot, k, r).start()
            return c

        lax.fori_loop(0, tm, body, 0, unroll=8)

    def wait_all(slot):
        def body(r, c):
            for k in range(TOP_K):
                row_copy(0, slot, k, r).wait()
            return c

        lax.fori_loop(0, tm, body, 0, unroll=8)

    pl.when(i == 0)(lambda: fetch(0, 0))
    pl.when(i + 1 < ntiles)(lambda: fetch(i + 1, (i + 1) % 2))
    slot = i % 2
    wait_all(slot)

    for s in range(tm // CHUNK):
        seq = _seq_of_chunk((i + tile0) * (tm // CHUNK) + s, cfg)
        rows = slice(s * CHUNK, (s + 1) * CHUNK)
        rt = rt_ref[rows, :]
        moe = ybuf[slot, 0, rows, :] * rt[:, TOP_K:TOP_K + 1]
        for k in range(1, TOP_K):
            moe = moe + ybuf[slot, k, rows, :] * rt[:, TOP_K + k:TOP_K + k + 1]
        x = x_ref[rows, :] + gate_ref[pl.ds(seq, 1), :] * moe
        o_ref[rows, :] = x * lax.rsqrt(jnp.mean(x * x, axis=-1, keepdims=True) + NORM_EPS) * g_ref[...]


def _final(x1, yb, slot_pos, route, gate, g, cfg, row0, nrows, tm=256):
    d = x1.shape[1]
    assert row0 % tm == 0 and nrows % tm == 0
    tile0 = row0 // tm
    ntiles = nrows // tm
    full = lambda a: pl.BlockSpec(a.shape, lambda i, pos: (0,) * a.ndim)
    return pl.pallas_call(
        functools.partial(_final_kernel, cfg=cfg, tm=tm, tile0=tile0, ntiles=ntiles),
        grid_spec=pltpu.PrefetchScalarGridSpec(
            num_scalar_prefetch=1, grid=(ntiles,),
            in_specs=[pl.BlockSpec((tm, d), lambda i, pos: (i + tile0, 0)),
                      pl.BlockSpec((tm, LANES), lambda i, pos: (i + tile0, 0)),
                      full(gate), full(g), pl.BlockSpec(memory_space=pl.ANY)],
            out_specs=pl.BlockSpec((tm, d), lambda i, pos: (i, 0)),
            scratch_shapes=[pltpu.VMEM((2, TOP_K, tm, d), F32), pltpu.SemaphoreType.DMA((2,))]),
        out_shape=jax.ShapeDtypeStruct((nrows, d), F32),
        compiler_params=_params("arbitrary"),
        name="final_norm",
    )(slot_pos.reshape(-1), x1, route, gate, g, yb)


def _dispatch(route, n_experts, tm):
    n = route.shape[0]
    eid = route[:, :TOP_K].astype(jnp.int32)

    m = n * TOP_K
    e_flat = eid.reshape(m)
    order = jnp.argsort(e_flat).astype(jnp.int32)
    rank = jnp.argsort(order).astype(jnp.int32)
    e_sorted = e_flat[order]
    experts = jnp.arange(n_experts, dtype=jnp.int32)
    counts = jnp.sum(e_flat[:, None] == experts[None, :], axis=0, dtype=jnp.int32)
    start = jnp.cumsum(counts) - counts
    padded = (counts + tm - 1) // tm * tm
    pend = jnp.cumsum(padded)
    pstart = pend - padded
    dest_sorted = pstart[e_sorted] + jnp.arange(m, dtype=jnp.int32) - start[e_sorted]
    slot_pos = dest_sorted[rank].reshape(n, TOP_K)
    n_blocks = -(-(m + n_experts * (tm - 1)) // tm)
    rows = n_blocks * tm
    r = jnp.arange(rows, dtype=jnp.int32)
    owner = lambda row: jnp.minimum(jnp.sum(row[:, None] >= pend[None, :], axis=1, dtype=jnp.int32), n_experts - 1)
    e_r = owner(r)
    idx = r - pstart[e_r]
    row_tok = jnp.where(idx < counts[e_r], order[jnp.clip(start[e_r] + idx, 0, m - 1)] // TOP_K, r % n)
    n_used = pend[-1] // tm
    blk = jnp.arange(n_blocks, dtype=jnp.int32)
    blk_e = owner(blk * tm)
    n_valid = jnp.where(blk < n_used, jnp.clip(pstart[blk_e] + counts[blk_e] - blk * tm, 0, tm), 0).astype(jnp.int32)
    last_e = blk_e[jnp.maximum(n_used - 1, 0)]
    blk_e = jnp.where(blk < n_used, blk_e, last_e)
    return row_tok, slot_pos, blk_e, n_valid


def _to_window(a, cfg):
    return jnp.concatenate([a, jnp.zeros(a.shape[:-1] + (cfg.win - cfg.r_cols,), a.dtype)], axis=-1)


def _pad_rows(a, rows):
    return jnp.concatenate([a, jnp.zeros((rows - a.shape[0],) + a.shape[1:], a.dtype)], axis=0)


def kernel(x_prompt, x_sample, state_conv, state_ssm, state_shift, state_wkv, c_prompt, c_sample, norm1_g, w_mod, b_mod, w_in, conv_w, conv_b, dt_bias, a_log, d_skip, m_norm_g, shift_mu, w0, w_up, a0, a_up, g_up, k_k, k_a, r_k, ln_x_g, ln_x_b, w_out, norm2_g, w_grp, b_grp, w_erouter, b_erouter, e_gate, e_up, e_down, final_norm_g):
    assert w_mod.shape[0] == 1, "single-layer trunk"
    bp, tp, d = x_prompt.shape
    bs, ts, _ = x_sample.shape
    mi = m_norm_g.shape[-1]
    rw = w0.shape[-1]
    lw, la, lg = w_up.shape[1], a_up.shape[1], g_up.shape[1]
    cfg = Cfg(d, bp, tp, bs, ts, mi, rw, lw, la, lg)
    nc = cfg.nc
    assert tp % CHUNK == 0 and ts % CHUNK == 0 and tp >= CONV_W - 1 and ts >= CONV_W - 1
    assert w_in.shape[-1] == cfg.o_rw + cfg.mh + cfg.r_cols and cfg.o_rw % LANES == 0
    assert state_conv.shape[-1] == cfg.cd and lw <= LANES and la <= LANES and cfg.mh <= LANES
    n = cfg.n_rows
    n_seq = cfg.n_seq
    mh, rh = cfg.mh, cfg.rh
    npair = rw // PAIR
    n_experts = e_gate.shape[1]

    x_p = x_prompt.reshape(bp * tp, d)
    x_s = x_sample.reshape(bs * ts, d)
    c_all = jnp.concatenate([c_prompt, c_sample], axis=0)

    mod = _modulation(c_all, w_mod[0], b_mod[0])
    shift1, scale1, gate1, shift2, scale2, gate2 = [mod[:, i * d:(i + 1) * d] for i in range(6)]

    h1 = _prenorm(x_p, x_s, shift1, scale1, norm1_g, cfg)
    w_in_t = w_in[0].T
    proj = _inproj(h1, w_in_t, cfg)
    dt_raw = _dtproj(h1, w_in_t, cfg)

    zeros = lambda b, *s: jnp.zeros((b,) + s, F32)
    conv0 = jnp.concatenate([zeros(bp, CONV_W - 1, cfg.cd), state_conv[0]], axis=0)
    conv8 = jnp.concatenate([zeros(n_seq, 8 - (CONV_W - 1), cfg.cd), conv0], axis=1)
    nst = state_ssm.shape[-1]
    ssm0 = jnp.concatenate([zeros(bp, mi, nst), state_ssm[0].reshape(bs, mi, nst)], axis=0)
    lane_pad = lambda a: jnp.concatenate([a.reshape(1, -1), jnp.zeros((1, LANES - a.shape[-1]), F32)], axis=1)
    expand = ((jnp.arange(2 * LANES)[:, None] % LANES) == (jnp.arange(mi)[None, :] // HEAD)).astype(BF16)
    ym, ssm_new = _mamba(proj, dt_raw, conv8, ssm0, conv_w[0], conv_b[0].reshape(1, -1), lane_pad(dt_bias[0]),
                         lane_pad(a_log[0]), jnp.repeat(d_skip[0], HEAD).reshape(1, mi), m_norm_g[0].reshape(1, mi),
                         expand, cfg)

    sh0 = jnp.concatenate([zeros(bp, 1, state_shift.shape[-1]), state_shift[0]], axis=0)
    sh8 = jnp.concatenate([zeros(n_seq, 7, cfg.win), _to_window(sh0, cfg)], axis=1)
    to_pairs = lambda s: s.reshape(-1, npair, 2, HEAD, HEAD).transpose(0, 1, 3, 2, 4).reshape(-1, npair, HEAD, PAIR)
    from_pairs = lambda s: s.reshape(-1, npair, HEAD, 2, HEAD).transpose(0, 1, 3, 2, 4).reshape(-1, rh, HEAD, HEAD)
    wkv0 = jnp.concatenate([zeros(bp, npair, HEAD, PAIR), to_pairs(state_wkv[0])], axis=0)
    row = lambda a: a.reshape(1, -1)
    yr, wkv_new = _rwkv(proj, sh8, wkv0, _to_window(row(shift_mu[0]), cfg), row(w0[0]),
                        _pad_rows(w_up[0], LANES), row(a0[0]), _pad_rows(a_up[0], LANES), g_up[0],
                        row(k_k[0]), row(k_a[0]), row(r_k[0]), row(ln_x_g[0]), row(ln_x_b[0]), cfg)

    wo = w_out[0].astype(BF16)
    x1 = _outproj(ym, yr, wo[:mi], wo[mi:], x_p, x_s, gate1, cfg)
    wr = jnp.concatenate([w_grp[0], w_erouter[0], jnp.zeros((d, LANES - E_GROUPS - n_experts), F32)], axis=1)
    br = lane_pad(jnp.concatenate([b_grp[0], b_erouter[0]]))
    h2, route = _router(x1, shift2, scale2, norm2_g, wr, br, cfg, n_experts)

    tm_e = 2 * E_SUB
    row_tok, slot_pos, blk_e, n_valid = _dispatch(route, n_experts, tm_e)
    yb = _experts(h2[row_tok], blk_e, n_valid, e_gate[0], e_up[0], e_down[0], tm_e)

    np_ = bp * tp
    fg = final_norm_g.reshape(1, d)
    y_prompt = _final(x1, yb, slot_pos, route, gate2, fg, cfg, 0, np_).reshape(bp, tp, d)
    y_sample = _final(x1, yb, slot_pos, route, gate2, fg, cfg, np_, bs * ts).reshape(bs, ts, d)
    tails = lambda b, t, base: jnp.stack(
        [lax.slice(proj, (base + (i + 1) * t - (CONV_W - 1), 0), (base + (i + 1) * t, nc)) for i in range(b)])
    pp = tails(bp, tp, 0)
    ps = tails(bs, ts, np_)
    conv_of = lambda p: p[:, :, mi:mi + cfg.cd][None]
    shift_of = lambda p: p[:, -1:, cfg.o_rw:cfg.o_rw + cfg.r_cols][None]
    ssm_new = ssm_new.reshape(n_seq, mh, HEAD, nst)
    wkv_new = from_pairs(wkv_new)
    return (y_prompt, y_sample,
            conv_of(pp), ssm_new[:bp][None], shift_of(pp), wkv_new[:bp][None],
            conv_of(ps), ssm_new[bp:][None], shift_of(ps), wkv_new[bp:][None])
```

```python
import functools
from typing import NamedTuple

import jax
import jax.numpy as jnp
from jax import lax
from jax.experimental import pallas as pl
from jax.experimental.pallas import tpu as pltpu

F32 = jnp.float32
BF16 = jnp.bfloat16
HI = lax.Precision.HIGHEST

LANES = 128
CHUNK = 64
HEAD = 64
PAIR = 2 * HEAD
SSD_GROUP = 4 * HEAD
NORM_EPS = 1e-6
M_NORM_EPS = 1e-5
R_LN_EPS = 64e-5
CONV_W = 4
E_GROUPS = 4
E_PER_GROUP = 8
TOP_K = 2
VMEM_LIMIT = 56 * 1024 * 1024
E_SUB = 256
EXP_NEG_HALF = 0.6065306597126334
SA_GROUP = 16
SCAN_STEPS = 64


class Cfg(NamedTuple):
    d: int
    bp: int
    tp: int
    bs: int
    ts: int
    mi: int
    rw: int
    lw: int
    la: int
    lg: int

    @property
    def cpp(self):
        return self.tp // CHUNK

    @property
    def cps(self):
        return self.ts // CHUNK

    @property
    def n_chunks(self):
        return self.bp * self.cpp + self.bs * self.cps

    @property
    def n_seq(self):
        return self.bp + self.bs

    @property
    def n_rows(self):
        return self.n_chunks * CHUNK

    @property
    def groups(self):
        return self.mi // SSD_GROUP

    @property
    def nb(self):
        return self.groups * LANES

    @property
    def cd(self):
        return self.mi + 2 * self.nb

    @property
    def mh(self):
        return self.mi // HEAD

    @property
    def rh(self):
        return self.rw // HEAD

    @property
    def o_rw(self):
        return self.mi + self.cd

    @property
    def r_cols(self):
        return 3 * self.rw + self.lw + self.la + self.lg

    @property
    def tail(self):
        return -(-(self.lw + self.la + self.lg) // LANES) * LANES

    @property
    def win(self):
        return 3 * self.rw + self.tail

    @property
    def nc(self):
        return self.o_rw + self.win


def _seq_of_chunk(c, cfg):
    npc = cfg.bp * cfg.cpp
    return jnp.where(c < npc, c // cfg.cpp, cfg.bp + (c - npc) // cfg.cps)


def _is_first_chunk(c, cfg):
    npc = cfg.bp * cfg.cpp
    return jnp.where(c < npc, c % cfg.cpp == 0, (c - npc) % cfg.cps == 0)


def _silu(x):
    return x * jax.nn.sigmoid(x)


def _nt(a, b, **kw):
    return lax.dot_general(a, b, (((1,), (1,)), ((), ())), preferred_element_type=F32, **kw)


def _tn(a, b, **kw):
    return lax.dot_general(a, b, (((0,), (0,)), ((), ())), preferred_element_type=F32, **kw)


def _params(*sem):
    return pltpu.CompilerParams(dimension_semantics=sem, vmem_limit_bytes=VMEM_LIMIT)


def _mod_kernel(c_ref, w_ref, b_ref, o_ref):
    o_ref[...] = jnp.dot(_silu(c_ref[...]), w_ref[...], preferred_element_type=F32) + b_ref[...]


def _modulation(c_all, w_mod, b_mod):
    s, d = c_all.shape
    cols = w_mod.shape[1]
    tn = 512
    return pl.pallas_call(
        _mod_kernel,
        grid=(cols // tn,),
        in_specs=[pl.BlockSpec((s, d), lambda j: (0, 0)),
                  pl.BlockSpec((d, tn), lambda j: (0, j)),
                  pl.BlockSpec((1, tn), lambda j: (0, j))],
        out_specs=pl.BlockSpec((s, tn), lambda j: (0, j)),
        out_shape=jax.ShapeDtypeStruct((s, cols), F32),
        compiler_params=_params("parallel"),
        name="modulation",
    )(c_all, w_mod, b_mod.reshape(1, cols))


def _modulated_norm(x, g, scale, shift):
    y = x * lax.rsqrt(jnp.mean(x * x, axis=-1, keepdims=True) + NORM_EPS) * g
    return y * (1.0 + scale) + shift


def _prompt_tiles(cfg, tm):
    assert (cfg.bp * cfg.tp) % tm == 0 and (cfg.bs * cfg.ts) % tm == 0 and tm % CHUNK == 0
    return cfg.bp * cfg.tp // tm


def _row_specs(cfg, tm, width, col=None):
    npt = _prompt_tiles(cfg, tm)
    c = (lambda *r: 0) if col is None else col
    return [pl.BlockSpec((tm, width), lambda i, *r: (jnp.minimum(i, npt - 1), jnp.where(i < npt, c(*r), 0))),
            pl.BlockSpec((tm, width), lambda i, *r: (jnp.maximum(i - npt, 0), jnp.where(i >= npt, c(*r), 0)))]


def _prenorm_kernel(xp_ref, xs_ref, shift_ref, scale_ref, g_ref, h_ref, *, cfg, tm):
    i = pl.program_id(0)

    def norm_rows(x_ref):
        for s in range(tm // CHUNK):
            seq = _seq_of_chunk(i * (tm // CHUNK) + s, cfg)
            rows = slice(s * CHUNK, (s + 1) * CHUNK)
            h = _modulated_norm(x_ref[rows, :], g_ref[...], scale_ref[pl.ds(seq, 1), :], shift_ref[pl.ds(seq, 1), :])
            h_ref[rows, :] = h.astype(BF16)

    npt = _prompt_tiles(cfg, tm)
    pl.when(i < npt)(lambda: norm_rows(xp_ref))
    pl.when(i >= npt)(lambda: norm_rows(xs_ref))


def _prenorm(x_p, x_s, shift, scale, g, cfg, tm=256):
    d = x_p.shape[1]
    full = lambda a: pl.BlockSpec(a.shape, lambda i: (0,) * a.ndim)
    return pl.pallas_call(
        functools.partial(_prenorm_kernel, cfg=cfg, tm=tm),
        grid=(cfg.n_rows // tm,),
        in_specs=_row_specs(cfg, tm, d) + [full(shift), full(scale), full(g)],
        out_specs=pl.BlockSpec((tm, d), lambda i: (i, 0)),
        out_shape=jax.ShapeDtypeStruct((cfg.n_rows, d), BF16),
        compiler_params=_params("parallel"),
        name="prenorm",
    )(x_p, x_s, shift, scale, g)


def _inproj_kernel(h_ref, w_ref, wn_ref, o_ref, wb_s, *, first_shifted, shift, tn, n_features):
    j = pl.program_id(0)
    new_tile = pl.program_id(1) == 0

    @pl.when(jnp.logical_and(new_tile, j < first_shifted))
    def _():
        wb_s[...] = w_ref[...].astype(BF16)

    @pl.when(jnp.logical_and(new_tile, j >= first_shifted))
    def _():
        w = jnp.concatenate([w_ref[...], wn_ref[...]], axis=0)[shift:shift + tn, :]
        feature = j * tn + shift + lax.broadcasted_iota(jnp.int32, w.shape, 0)
        wb_s[...] = jnp.where(feature < n_features, w, 0.0).astype(BF16)

    o_ref[...] = _nt(h_ref[...], wb_s[...])


def _inproj(h, wt, cfg, tm=1024, tn=512):
    n, d = h.shape
    nout = cfg.o_rw + cfg.win
    assert cfg.o_rw % tn == 0 and nout % tn == 0 and n % tm == 0 and cfg.mh <= LANES and cfg.mh % 8 == 0
    last_next = (wt.shape[0] - 1) // LANES
    return pl.pallas_call(
        functools.partial(_inproj_kernel, first_shifted=cfg.o_rw // tn, shift=cfg.mh, tn=tn, n_features=wt.shape[0]),
        grid=(nout // tn, n // tm),
        in_specs=[pl.BlockSpec((tm, d), lambda j, i: (i, 0)),
                  pl.BlockSpec((tn, d), lambda j, i: (j, 0)),
                  pl.BlockSpec((LANES, d), lambda j, i: (jnp.minimum((j + 1) * (tn // LANES), last_next), 0))],
        out_specs=pl.BlockSpec((tm, tn), lambda j, i: (i, j)),
        out_shape=jax.ShapeDtypeStruct((n, nout), F32),
        scratch_shapes=[pltpu.VMEM((tn, d), BF16)],
        compiler_params=_params("arbitrary", "arbitrary"),
        name="inproj",
    )(h, wt, wt)


def _dtproj_kernel(h_ref, w_ref, o_ref):
    o_ref[...] = _nt(h_ref[...], w_ref[...].astype(BF16))


def _dtproj(h, wt, cfg, tm=1024):
    n, d = h.shape
    return pl.pallas_call(
        _dtproj_kernel,
        grid=(n // tm,),
        in_specs=[pl.BlockSpec((tm, d), lambda i: (i, 0)), pl.BlockSpec((LANES, d), lambda i: (cfg.o_rw // LANES, 0))],
        out_specs=pl.BlockSpec((tm, LANES), lambda i: (i, 0)),
        out_shape=jax.ShapeDtypeStruct((n, LANES), F32),
        compiler_params=_params("parallel"),
        name="dtproj",
    )(h, wt)


def _mamba_kernel(zx_ref, dt_ref, cst_ref, sst_ref, cw_ref, cb_ref, dtb_ref, alog_ref, dsk_ref, ng_ref, exp_ref,
                  ym_ref, ssm_ref, buf, u_scr, *, cfg):
    c = pl.program_id(0)
    mi, nb, cd = cfg.mi, cfg.nb, cfg.cd
    L = CHUNK

    @pl.when(_is_first_chunk(c, cfg))
    def _():
        buf[0:8, :] = cst_ref[0]
        ssm_ref[0] = sst_ref[0]

    xbc = zx_ref[:, mi:mi + cd]
    buf[8:8 + L, :] = xbc
    conv = (cb_ref[...] + buf[5:5 + L, :] * cw_ref[0:1, :] + buf[6:6 + L, :] * cw_ref[1:2, :]
            + buf[7:7 + L, :] * cw_ref[2:3, :] + xbc * cw_ref[3:4, :])
    buf[0:8, :] = buf[L:L + 8, :]
    u_scr[...] = _silu(conv)

    dt = jax.nn.softplus(dt_ref[...] + dtb_ref[...])
    da = dt * (-jnp.exp(alog_ref[...]))
    row = lax.broadcasted_iota(jnp.int32, (L, mi), 0)
    pos = lax.broadcasted_iota(jnp.int32, (L, mi), 1) % HEAD
    tri2 = (lax.broadcasted_iota(jnp.int32, (L, 2 * L), 0)
            >= lax.broadcasted_iota(jnp.int32, (L, 2 * L), 1) % L).astype(BF16)
    lane = lax.broadcasted_iota(jnp.int32, (L, PAIR), 1)
    row16 = lax.broadcasted_iota(jnp.int32, (16, mi), 0)
    ones16 = jnp.ones((16, LANES), BF16)

    dt_hl = jnp.concatenate(_split_hi_lo(dt), axis=1)
    da_hl = jnp.concatenate(_split_hi_lo(da), axis=1)
    dte = jnp.dot(dt_hl, exp_ref[...], preferred_element_type=F32)
    dae = jnp.dot(da_hl, exp_ref[...], preferred_element_type=F32)
    cum = jnp.dot(tri2, jnp.concatenate(_split_hi_lo(dae), axis=0), preferred_element_type=F32)
    cum_row = jnp.sum(jnp.where(row == pos, cum, 0.0), axis=0, keepdims=True)
    last = cum[L - 1:L, :]
    decay = jnp.exp(jnp.where(row >= pos, cum - cum_row, -jnp.inf))
    xdt = u_scr[:, 0:mi] * dte
    x_end = xdt * jnp.exp(last - cum)
    e_cum = jnp.exp(cum)
    e_last = jnp.exp(last)
    e_hi = e_last.astype(BF16).astype(F32)
    e_rows = jnp.where(row16 == 0, e_hi, jnp.where(row16 == 1, e_last - e_hi, 0.0)).astype(BF16)

    groups = range(cfg.groups)
    gs = lambda g: slice(g * SSD_GROUP, (g + 1) * SSD_GROUP)
    b_of = lambda g: u_scr[:, mi + g * LANES:mi + (g + 1) * LANES]
    c_of = lambda g: u_scr[:, mi + nb + g * LANES:mi + nb + (g + 1) * LANES]
    sc2 = [_nt(c_of(g), jnp.concatenate([b_of(g), b_of(g)], axis=0)) for g in groups]
    y_diag = []
    for g in groups:
        ys = []
        for q in range(SSD_GROUP // PAIR):
            ps = slice(g * SSD_GROUP + q * PAIR, g * SSD_GROUP + (q + 1) * PAIR)
            xq = xdt[:, ps]
            rhs = jnp.concatenate([jnp.where(lane < HEAD, xq, 0.0), jnp.where(lane >= HEAD, xq, 0.0)], axis=0)
            ys.append(jnp.dot(decay[:, ps] * sc2[g], rhs, preferred_element_type=F32))
        y_diag.append(jnp.concatenate(ys, axis=1))
    y_off = [_nt(c_of(g), ssm_ref[0, gs(g), :]) for g in groups]
    new = [_tn(x_end[:, gs(g)], b_of(g)) for g in groups]
    dcol = [_tn(e_rows[:, gs(g)], ones16) for g in groups]
    for g in groups:
        cs = gs(g)
        ssm_ref[0, cs, :] = ssm_ref[0, cs, :] * dcol[g] + new[g]
        y = y_diag[g] + y_off[g] * e_cum[:, cs] + u_scr[:, cs] * dsk_ref[:, cs]
        y = y * _silu(zx_ref[:, cs])
        y = y * lax.rsqrt(jnp.mean(y * y, axis=-1, keepdims=True) + M_NORM_EPS)
        ym_ref[:, cs] = (y * ng_ref[:, cs]).astype(BF16)


def _mamba(proj, dt_raw, conv_state8, ssm_state, cw, cb, dtb, alog, dsk, ng, expand, cfg):
    n = proj.shape[0]
    mi, cd = cfg.mi, cfg.cd
    nst = ssm_state.shape[-1]
    seq = lambda c: _seq_of_chunk(c, cfg)
    full = lambda a: pl.BlockSpec(a.shape, lambda c: (0,) * a.ndim)
    return pl.pallas_call(
        functools.partial(_mamba_kernel, cfg=cfg),
        grid=(cfg.n_chunks,),
        in_specs=[pl.BlockSpec((CHUNK, mi + cd), lambda c: (c, 0)),
                  pl.BlockSpec((CHUNK, LANES), lambda c: (c, 0)),
                  pl.BlockSpec((1, 8, cd), lambda c: (seq(c), 0, 0)),
                  pl.BlockSpec((1, mi, nst), lambda c: (seq(c), 0, 0)),
                  full(cw), full(cb), full(dtb), full(alog), full(dsk), full(ng), full(expand)],
        out_specs=[pl.BlockSpec((CHUNK, mi), lambda c: (c, 0)),
                   pl.BlockSpec((1, mi, nst), lambda c: (seq(c), 0, 0))],
        out_shape=[jax.ShapeDtypeStruct((n, mi), BF16),
                   jax.ShapeDtypeStruct((cfg.n_seq, mi, nst), F32)],
        scratch_shapes=[pltpu.VMEM((CHUNK + 8, cd), F32), pltpu.VMEM((CHUNK, cd), F32)],
        compiler_params=_params("arbitrary"),
        name="mamba",
    )(proj, dt_raw, conv_state8, ssm_state, cw, cb, dtb, alog, dsk, ng, expand)


def _seg_sum(x, ob):
    w = ob.shape[0]
    xb = x.astype(BF16)
    parts = [jnp.dot(xb[:, p * w:(p + 1) * w], ob, preferred_element_type=F32) for p in range(x.shape[1] // w)]
    return jnp.concatenate(parts, axis=1)


def _split_hi_lo(x):
    hi = x.astype(BF16)
    lo = (x - hi.astype(F32)).astype(BF16)
    return hi, lo


def _rwkv_kernel(rkv_ref, lora_ref, sh_ref, wkv_in_ref, mu_ref, w0_ref, wup_ref, a0_ref, aup_ref, gup_ref,
                 kk_ref, ka_ref, rk_ref, lng_ref, lnb_ref,
                 yr_ref, wkv_ref, pbuf, lbuf, w_s, a_s, b_s, k_s, y_s, vt_s, ut_s, y0_s, g_s, *, cfg):
    c = pl.program_id(0)
    rw = cfg.rw
    L = CHUNK
    npair = rw // PAIR

    @pl.when(_is_first_chunk(c, cfg))
    def _():
        pbuf[0:8, :] = sh_ref[0, :, 0:3 * rw]
        lbuf[0:8, :] = sh_ref[0, :, 3 * rw:]
        wkv_ref[0] = wkv_in_ref[0]

    pr = rkv_ref[...]
    pl_ = lora_ref[...]
    pbuf[8:8 + L, :] = pr
    lbuf[8:8 + L, :] = pl_
    xs = pr + (pbuf[7:7 + L, :] - pr) * mu_ref[:, 0:3 * rw]
    xl = pl_ + (lbuf[7:7 + L, :] - pl_) * mu_ref[:, 3 * rw:]
    pbuf[0:8, :] = pbuf[L:L + 8, :]
    lbuf[0:8, :] = lbuf[L:L + 8, :]

    r = xs[:, 0:rw]
    k = xs[:, rw:2 * rw]
    v = xs[:, 2 * rw:3 * rw]
    wd = xl[:, 0:LANES]
    ad = xl[:, cfg.lw:cfg.lw + LANES]
    gd = xl[:, cfg.lw + cfg.la:cfg.lw + cfg.la + cfg.lg]

    lane = lax.broadcasted_iota(jnp.int32, (PAIR, PAIR), 1)
    rowi = lax.broadcasted_iota(jnp.int32, (PAIR, PAIR), 0)
    o_row = lax.broadcasted_iota(jnp.int32, (2 * PAIR, 2 * PAIR), 0)
    o_lane = lax.broadcasted_iota(jnp.int32, (2 * PAIR, 2 * PAIR), 1)
    ob = ((o_row // HEAD) == (o_lane // HEAD)).astype(BF16)

    w_z = w0_ref[...] + jnp.dot(jnp.tanh(wd), wup_ref[...], preferred_element_type=F32)
    logw = -(jax.nn.sigmoid(w_z) * EXP_NEG_HALF)
    w_s[...] = jnp.exp(logw)
    a = jax.nn.sigmoid(a0_ref[...] + jnp.dot(ad, aup_ref[...], preferred_element_type=F32))
    kk = k * kk_ref[...]
    kk = kk * lax.rsqrt(jnp.maximum(_seg_sum(kk * kk, ob), 1e-24))
    kh = k * (1.0 + (a - 1.0) * ka_ref[...])
    a_s[...] = -kk
    b_s[...] = kk * a
    k_s[...] = kh

    tri2 = (lax.broadcasted_iota(jnp.int32, (L, 2 * L), 0)
            >= lax.broadcasted_iota(jnp.int32, (L, 2 * L), 1) % L).astype(BF16)
    cum = jnp.dot(tri2, jnp.concatenate(_split_hi_lo(logw), axis=0), preferred_element_type=F32)
    p_inv = jnp.exp(-cum)
    r_p = r * jnp.exp(cum)
    b_p = b_s[...] * p_inv
    k_p = kh * p_inv
    lo_lanes = lane < HEAD
    g_row = lax.broadcasted_iota(jnp.int32, (2 * PAIR, PAIR), 0)
    g_lane = lax.broadcasted_iota(jnp.int32, (2 * PAIR, PAIR), 1)
    g_keep = (g_row % HEAD) <= (g_lane % HEAD)

    def by_head(x):
        x2 = jnp.concatenate([x, x], axis=0)
        return jnp.where(lo_lanes == (rowi < HEAD), x2, 0.0).astype(BF16)

    pairs = range(npair)
    lanes_of = lambda p: slice(p * PAIR, (p + 1) * PAIR)
    rows_of = lambda p: slice(p * HEAD, (p + 1) * HEAD)
    r_bd = [by_head(r_p[:, lanes_of(p)]) for p in pairs]
    y0 = [_nt(wkv_ref[0, p].astype(BF16), r_bd[p]) for p in pairs]
    gram = [_nt(jnp.concatenate([by_head(b_p[:, lanes_of(p)]), by_head(k_p[:, lanes_of(p)])], axis=0), r_bd[p])
            for p in pairs]
    vts = [v[:, lanes_of(p)].T for p in pairs]
    for p in pairs:
        y0_s[rows_of(p), :] = y0[p]
        g_s[p] = jnp.where(g_keep, gram[p], 0.0).astype(BF16)
        vjt = jnp.concatenate([vts[p][0:HEAD], vts[p][HEAD:]], axis=1)
        hi = vjt.astype(BF16).astype(F32)
        vt_s[rows_of(p), :] = jnp.concatenate([hi, vjt - hi], axis=1).astype(BF16)

    lane_t = lax.broadcasted_iota(jnp.int32, (HEAD, PAIR), 1) % HEAD
    crow = lax.broadcasted_iota(jnp.int32, (2 * PAIR, 2 * PAIR), 0)
    clane = lax.broadcasted_iota(jnp.int32, (2 * PAIR, 2 * PAIR), 1)
    same_head = ((crow % PAIR) // HEAD) == ((clane % PAIR) // HEAD)
    c_dt = (crow % HEAD) - clane // PAIR

    def steps(blk, carry):
        t0 = pl.multiple_of(blk * SCAN_STEPS, SCAN_STEPS)
        rows = pl.ds(t0, SCAN_STEPS)
        w8, a8, b8, k8 = w_s[rows, :], a_s[rows, :], b_s[rows, :], k_s[rows, :]
        s = [wkv_ref[0, p] for p in range(npair)]
        vcol2 = None
        for i in range(SCAN_STEPS):
            row = lambda x8, p: x8[i:i + 1, p * PAIR:(p + 1) * PAIR]
            this_step = lane_t == t0 + i
            if i % 2 == 0:
                sel = (same_head & (c_dt == t0 + i)).astype(BF16)
                vcol2 = jnp.dot(vt_s[...], sel, preferred_element_type=F32)
            vcol = vcol2[:, (i % 2) * PAIR:(i % 2 + 1) * PAIR]
            for p0 in range(0, npair, SA_GROUP):
                group = range(p0, min(p0 + SA_GROUP, npair))
                lhs = []
                for p in group:
                    prod = (s[p] * row(a8, p)).astype(BF16)
                    lhs.append(jnp.concatenate([prod[0:HEAD // 2], prod[HEAD // 2:]], axis=1))
                sa = jnp.dot(jnp.concatenate(lhs, axis=0), ob, preferred_element_type=F32)
                for q, p in enumerate(group):
                    half = sa[q * (HEAD // 2):(q + 1) * (HEAD // 2)]
                    sa_p = jnp.concatenate([half[:, 0:PAIR], half[:, PAIR:]], axis=0)
                    s[p] = s[p] * row(w8, p) + sa_p * row(b8, p) + vcol[p * HEAD:(p + 1) * HEAD] * row(k8, p)
                    pltpu.store(ut_s.at[p * HEAD:(p + 1) * HEAD, :], sa_p, mask=this_step)
        for p in range(npair):
            wkv_ref[0, p] = s[p]
        return carry

    ut_s[...] = jnp.zeros_like(ut_s)
    lax.fori_loop(0, L // SCAN_STEPS, steps, 0)

    yts = [jnp.dot(jnp.concatenate([ut_s[rows_of(p), :].astype(BF16), vt_s[rows_of(p), 0:PAIR]], axis=1), g_s[p],
                   preferred_element_type=F32) for p in pairs]
    ytt = [(y0_s[rows_of(p), :] + yts[p]).T for p in pairs]
    for p in pairs:
        y_s[:, lanes_of(p)] = jnp.concatenate([ytt[p][0:L], ytt[p][L:]], axis=1)

    y = y_s[...]
    mu = _seg_sum(y, ob) * (1.0 / HEAD)
    dlt = y - mu
    var = _seg_sum(dlt * dlt, ob) * (1.0 / HEAD)
    yn = dlt * lax.rsqrt(var + R_LN_EPS) * lng_ref[...] + lnb_ref[...]
    bonus = _seg_sum(r * k_s[...] * rk_ref[...], ob) * v
    g = jnp.dot(jax.nn.sigmoid(gd), gup_ref[...], preferred_element_type=F32)
    yr_ref[...] = ((yn + bonus) * g).astype(BF16)


def _rwkv(proj, shift8, wkv_pairs, mu, w0, wup, a0, aup, gup, kk, ka, rk, lng, lnb, cfg):
    n = proj.shape[0]
    rw, lp = cfg.rw, cfg.tail
    assert cfg.o_rw % (3 * rw) == 0 and (cfg.o_rw + 3 * rw) % lp == 0
    npair = rw // PAIR
    seq = lambda c: _seq_of_chunk(c, cfg)
    full = lambda a: pl.BlockSpec(a.shape, lambda c: (0,) * a.ndim)
    row_scr = pltpu.VMEM((CHUNK, rw), F32)
    return pl.pallas_call(
        functools.partial(_rwkv_kernel, cfg=cfg),
        grid=(cfg.n_chunks,),
        in_specs=[pl.BlockSpec((CHUNK, 3 * rw), lambda c: (c, cfg.o_rw // (3 * rw))),
                  pl.BlockSpec((CHUNK, lp), lambda c: (c, (cfg.o_rw + 3 * rw) // lp)),
                  pl.BlockSpec((1, 8, 3 * rw + lp), lambda c: (seq(c), 0, 0)),
                  pl.BlockSpec((1, npair, HEAD, PAIR), lambda c: (seq(c), 0, 0, 0)),
                  full(mu), full(w0), full(wup), full(a0), full(aup), full(gup),
                  full(kk), full(ka), full(rk), full(lng), full(lnb)],
        out_specs=[pl.BlockSpec((CHUNK, rw), lambda c: (c, 0)),
                   pl.BlockSpec((1, npair, HEAD, PAIR), lambda c: (seq(c), 0, 0, 0))],
        out_shape=[jax.ShapeDtypeStruct((n, rw), BF16),
                   jax.ShapeDtypeStruct((cfg.n_seq, npair, HEAD, PAIR), F32)],
        scratch_shapes=[pltpu.VMEM((CHUNK + 8, 3 * rw), F32), pltpu.VMEM((CHUNK + 8, lp), F32),
                        row_scr, row_scr, row_scr, row_scr, row_scr,
                        pltpu.VMEM((npair * HEAD, 2 * PAIR), BF16), pltpu.VMEM((npair * HEAD, PAIR), F32),
                        pltpu.VMEM((npair * HEAD, PAIR), F32), pltpu.VMEM((npair, 2 * PAIR, PAIR), BF16)],
        compiler_params=_params("arbitrary"),
        name="rwkv",
    )(proj, proj, shift8, wkv_pairs, mu, w0, wup, a0, aup, gup, kk, ka, rk, lng, lnb)


def _outproj_kernel(ym_ref, yr_ref, wa_ref, wb_ref, xp_ref, xs_ref, gate_ref, o_ref, *, cfg, tm):
    i = pl.program_id(0)
    acc = (jnp.dot(ym_ref[...], wa_ref[...], preferred_element_type=F32)
           + jnp.dot(yr_ref[...], wb_ref[...], preferred_element_type=F32))

    def residual(x_ref):
        for s in range(tm // CHUNK):
            seq = _seq_of_chunk(i * (tm // CHUNK) + s, cfg)
            rows = slice(s * CHUNK, (s + 1) * CHUNK)
            o_ref[rows, :] = x_ref[rows, :] + gate_ref[pl.ds(seq, 1), :] * acc[rows, :]

    npt = _prompt_tiles(cfg, tm)
    pl.when(i < npt)(lambda: residual(xp_ref))
    pl.when(i >= npt)(lambda: residual(xs_ref))


def _outproj(ym, yr, wa, wb, x_p, x_s, gate, cfg, tm=512, tn=1024):
    n, d = cfg.n_rows, x_p.shape[1]
    ka, kb = ym.shape[1], yr.shape[1]
    s = gate.shape[0]
    return pl.pallas_call(
        functools.partial(_outproj_kernel, cfg=cfg, tm=tm),
        grid=(n // tm, d // tn),
        in_specs=[pl.BlockSpec((tm, ka), lambda i, j: (i, 0)),
                  pl.BlockSpec((tm, kb), lambda i, j: (i, 0)),
                  pl.BlockSpec((ka, tn), lambda i, j: (0, j)),
                  pl.BlockSpec((kb, tn), lambda i, j: (0, j))]
                 + _row_specs(cfg, tm, tn, lambda j: j)
                 + [pl.BlockSpec((s, tn), lambda i, j: (0, j))],
        out_specs=pl.BlockSpec((tm, tn), lambda i, j: (i, j)),
        out_shape=jax.ShapeDtypeStruct((n, d), F32),
        compiler_params=_params("parallel", "parallel"),
        name="outproj",
    )(ym, yr, wa, wb, x_p, x_s, gate)


def _route(lg, n_experts):
    lane = lax.broadcasted_iota(jnp.int32, lg.shape, 1)
    first = lambda mask: jnp.min(jnp.where(mask, lane, LANES), axis=-1, keepdims=True)
    is_grp = lane < E_GROUPS
    gmax = jnp.max(jnp.where(is_grp, lg, -jnp.inf), axis=-1, keepdims=True)
    gsel = first(is_grp & (lg == gmax))
    pg = 1.0 / jnp.sum(jnp.where(is_grp, jnp.exp(lg - gmax), 0.0), axis=-1, keepdims=True)
    e_lane = lane - E_GROUPS
    in_grp = (e_lane >= 0) & (e_lane < n_experts) & (e_lane // E_PER_GROUP == gsel)
    emax = jnp.max(jnp.where(in_grp, lg, -jnp.inf), axis=-1, keepdims=True)
    p = jnp.where(in_grp, jnp.exp(lg - emax), 0.0)
    eprob = p / jnp.sum(p, axis=-1, keepdims=True)
    v1 = jnp.max(jnp.where(in_grp, eprob, -1.0), axis=-1, keepdims=True)
    i1 = first(in_grp & (eprob == v1))
    rest = in_grp & (lane != i1)
    v2 = jnp.max(jnp.where(rest, eprob, -1.0), axis=-1, keepdims=True)
    i2 = first(rest & (eprob == v2))
    tot = v1 + v2
    cols = [(i1 - E_GROUPS).astype(F32), (i2 - E_GROUPS).astype(F32), v1 / tot * pg, v2 / tot * pg]
    out = jnp.zeros(lg.shape, F32)
    for c, val in enumerate(cols):
        out = jnp.where(lane == c, val, out)
    return out


def _router_kernel(x_ref, shift_ref, scale_ref, g_ref, wh_ref, wl_ref, br_ref, h_ref, rt_ref, *, cfg, tm, n_experts):
    i = pl.program_id(0)
    for s in range(tm // CHUNK):
        seq = _seq_of_chunk(i * (tm // CHUNK) + s, cfg)
        rows = slice(s * CHUNK, (s + 1) * CHUNK)
        h = _modulated_norm(x_ref[rows, :], g_ref[...], scale_ref[pl.ds(seq, 1), :], shift_ref[pl.ds(seq, 1), :])
        h_hi, h_lo = _split_hi_lo(h)
        h_ref[rows, :] = h_hi
        dot = lambda a, b: jnp.dot(a, b[...], preferred_element_type=F32)
        logits = dot(h_hi, wh_ref) + (dot(h_hi, wl_ref) + dot(h_lo, wh_ref)) + br_ref[...]
        rt_ref[rows, :] = _route(logits, n_experts)


def _router(x1, shift, scale, g, wr, br, cfg, n_experts, tm=256):
    n, d = x1.shape
    assert E_GROUPS + n_experts <= LANES and n_experts == E_GROUPS * E_PER_GROUP
    full = lambda a: pl.BlockSpec(a.shape, lambda i: (0,) * a.ndim)
    w_hi, w_lo = _split_hi_lo(wr)
    return pl.pallas_call(
        functools.partial(_router_kernel, cfg=cfg, tm=tm, n_experts=n_experts),
        grid=(n // tm,),
        in_specs=[pl.BlockSpec((tm, d), lambda i: (i, 0)), full(shift), full(scale), full(g),
                  full(w_hi), full(w_lo), full(br)],
        out_specs=[pl.BlockSpec((tm, d), lambda i: (i, 0)), pl.BlockSpec((tm, LANES), lambda i: (i, 0))],
        out_shape=[jax.ShapeDtypeStruct((n, d), BF16), jax.ShapeDtypeStruct((n, LANES), F32)],
        compiler_params=_params("parallel"),
        name="router",
    )(x1, shift, scale, g, w_hi, w_lo, br)


def _new_weights(be_ref):
    i = pl.program_id(1)
    return jnp.logical_or(i == 0, be_ref[i] != be_ref[jnp.maximum(i - 1, 0)])


def _for_valid_rows(nv_ref, out_ref, compute):
    nv = nv_ref[pl.program_id(1)]
    for s in range(out_ref.shape[0] // E_SUB):
        rows = slice(s * E_SUB, (s + 1) * E_SUB)

        @pl.when(nv > s * E_SUB)
        def _():
            out_ref[rows, :] = compute(rows)

        @pl.when(nv <= s * E_SUB)
        def _():
            out_ref[rows, :] = jnp.zeros((E_SUB, out_ref.shape[1]), out_ref.dtype)


def _expert_up_kernel(be_ref, nv_ref, x_ref, wg_ref, wu_ref, h_ref, wg_s, wu_s):
    @pl.when(_new_weights(be_ref))
    def _():
        wg_s[...] = wg_ref[0].astype(BF16)
        wu_s[...] = wu_ref[0].astype(BF16)

    def hidden(rows):
        x = x_ref[rows, :]
        gate = jnp.dot(x, wg_s[...], preferred_element_type=F32)
        up = jnp.dot(x, wu_s[...], preferred_element_type=F32)
        return (_silu(gate) * up).astype(BF16)

    _for_valid_rows(nv_ref, h_ref, hidden)


def _expert_down_kernel(be_ref, nv_ref, h_ref, wd_ref, y_ref, wd_s):
    @pl.when(_new_weights(be_ref))
    def _():
        wd_s[...] = wd_ref[0].astype(BF16)

    _for_valid_rows(nv_ref, y_ref, lambda rows: jnp.dot(h_ref[rows, :], wd_s[...], preferred_element_type=F32))


def _experts(xb, blk_e, n_valid, wg, wu, wd, tm, tf=512, tn=2048):
    rows, d = xb.shape
    f = wg.shape[2]
    nblk = rows // tm
    tf, tn = min(tf, f), min(tn, d)
    assert tm % E_SUB == 0 and f % tf == 0 and d % tn == 0
    hid = pl.pallas_call(
        _expert_up_kernel,
        grid_spec=pltpu.PrefetchScalarGridSpec(
            num_scalar_prefetch=2, grid=(f // tf, nblk),
            in_specs=[pl.BlockSpec((tm, d), lambda j, i, be, nv: (i, 0)),
                      pl.BlockSpec((1, d, tf), lambda j, i, be, nv: (be[i], 0, j)),
                      pl.BlockSpec((1, d, tf), lambda j, i, be, nv: (be[i], 0, j))],
            out_specs=pl.BlockSpec((tm, tf), lambda j, i, be, nv: (i, j)),
            scratch_shapes=[pltpu.VMEM((d, tf), BF16), pltpu.VMEM((d, tf), BF16)]),
        out_shape=jax.ShapeDtypeStruct((rows, f), BF16),
        compiler_params=_params("arbitrary", "arbitrary"),
        name="expert_up",
    )(blk_e, n_valid, xb, wg, wu)
    return pl.pallas_call(
        _expert_down_kernel,
        grid_spec=pltpu.PrefetchScalarGridSpec(
            num_scalar_prefetch=2, grid=(d // tn, nblk),
            in_specs=[pl.BlockSpec((tm, f), lambda j, i, be, nv: (i, 0)),
                      pl.BlockSpec((1, f, tn), lambda j, i, be, nv: (be[i], 0, j))],
            out_specs=pl.BlockSpec((tm, tn), lambda j, i, be, nv: (i, j)),
            scratch_shapes=[pltpu.VMEM((f, tn), BF16)]),
        out_shape=jax.ShapeDtypeStruct((rows, d), F32),
        compiler_params=_params("arbitrary", "arbitrary"),
        name="expert_down",
    )(blk_e, n_valid, hid, wd)


def _final_kernel(pos_ref, x_ref, rt_ref, gate_ref, g_ref, y_hbm, o_ref, ybuf, sem, *, cfg, tm, tile0, ntiles):
    i = pl.program_id(0)

    def row_copy(pos, slot, k, r):
        return pltpu.make_async_copy(y_hbm.at[pl.ds(pos, 1), :], ybuf.at[slot, k, pl.ds(r, 1), :], sem.at[slot])

    def fetch(tile, slot):
        base = (tile + tile0) * (tm * TOP_K)

        def body(r, c):
            for k in range(TOP_K):
                row_copy(pos_ref[base + r * TOP_K + k], slot, k, r).start()
            return c

        lax.fori_loop(0, tm, body, 0, unroll=8)

    def wait_all(slot):
        def body(r, c):
            for k in range(TOP_K):
                row_copy(0, slot, k, r).wait()
            return c

        lax.fori_loop(0, tm, body, 0, unroll=8)

    pl.when(i == 0)(lambda: fetch(0, 0))
    pl.when(i + 1 < ntiles)(lambda: fetch(i + 1, (i + 1) % 2))
    slot = i % 2
    wait_all(slot)

    for s in range(tm // CHUNK):
        seq = _seq_of_chunk((i + tile0) * (tm // CHUNK) + s, cfg)
        rows = slice(s * CHUNK, (s + 1) * CHUNK)
        rt = rt_ref[rows, :]
        moe = ybuf[slot, 0, rows, :] * rt[:, TOP_K:TOP_K + 1]
        for k in range(1, TOP_K):
            moe = moe + ybuf[slot, k, rows, :] * rt[:, TOP_K + k:TOP_K + k + 1]
        x = x_ref[rows, :] + gate_ref[pl.ds(seq, 1), :] * moe
        o_ref[rows, :] = x * lax.rsqrt(jnp.mean(x * x, axis=-1, keepdims=True) + NORM_EPS) * g_ref[...]


def _final(x1, yb, slot_pos, route, gate, g, cfg, row0, nrows, tm=256):
    d = x1.shape[1]
    assert row0 % tm == 0 and nrows % tm == 0
    tile0 = row0 // tm
    ntiles = nrows // tm
    full = lambda a: pl.BlockSpec(a.shape, lambda i, pos: (0,) * a.ndim)
    return pl.pallas_call(
        functools.partial(_final_kernel, cfg=cfg, tm=tm, tile0=tile0, ntiles=ntiles),
        grid_spec=pltpu.PrefetchScalarGridSpec(
            num_scalar_prefetch=1, grid=(ntiles,),
            in_specs=[pl.BlockSpec((tm, d), lambda i, pos: (i + tile0, 0)),
                      pl.BlockSpec((tm, LANES), lambda i, pos: (i + tile0, 0)),
                      full(gate), full(g), pl.BlockSpec(memory_space=pl.ANY)],
            out_specs=pl.BlockSpec((tm, d), lambda i, pos: (i, 0)),
            scratch_shapes=[pltpu.VMEM((2, TOP_K, tm, d), F32), pltpu.SemaphoreType.DMA((2,))]),
        out_shape=jax.ShapeDtypeStruct((nrows, d), F32),
        compiler_params=_params("arbitrary"),
        name="final_norm",
    )(slot_pos.reshape(-1), x1, route, gate, g, yb)


def _dispatch(route, n_experts, tm):
    n = route.shape[0]
    eid = route[:, :TOP_K].astype(jnp.int32)

    m = n * TOP_K
    e_flat = eid.reshape(m)
    order = jnp.argsort(e_flat).astype(jnp.int32)
    rank = jnp.argsort(order).astype(jnp.int32)
    e_sorted = e_flat[order]
    experts = jnp.arange(n_experts, dtype=jnp.int32)
    counts = jnp.sum(e_flat[:, None] == experts[None, :], axis=0, dtype=jnp.int32)
    start = jnp.cumsum(counts) - counts
    padded = (counts + tm - 1) // tm * tm
    pend = jnp.cumsum(padded)
    pstart = pend - padded
    dest_sorted = pstart[e_sorted] + jnp.arange(m, dtype=jnp.int32) - start[e_sorted]
    slot_pos = dest_sorted[rank].reshape(n, TOP_K)
    n_blocks = -(-(m + n_experts * (tm - 1)) // tm)
    rows = n_blocks * tm
    r = jnp.arange(rows, dtype=jnp.int32)
    owner = lambda row: jnp.minimum(jnp.sum(row[:, None] >= pend[None, :], axis=1, dtype=jnp.int32), n_experts - 1)
    e_r = owner(r)
    idx = r - pstart[e_r]
    row_tok = jnp.where(idx < counts[e_r], order[jnp.clip(start[e_r] + idx, 0, m - 1)] // TOP_K, r % n)
    n_used = pend[-1] // tm
    blk = jnp.arange(n_blocks, dtype=jnp.int32)
    blk_e = owner(blk * tm)
    n_valid = jnp.where(blk < n_used, jnp.clip(pstart[blk_e] + counts[blk_e] - blk * tm, 0, tm), 0).astype(jnp.int32)
    last_e = blk_e[jnp.maximum(n_used - 1, 0)]
    blk_e = jnp.where(blk < n_used, blk_e, last_e)
    return row_tok, slot_pos, blk_e, n_valid


def _to_window(a, cfg):
    return jnp.concatenate([a, jnp.zeros(a.shape[:-1] + (cfg.win - cfg.r_cols,), a.dtype)], axis=-1)


def _pad_rows(a, rows):
    return jnp.concatenate([a, jnp.zeros((rows - a.shape[0],) + a.shape[1:], a.dtype)], axis=0)


def kernel(x_prompt, x_sample, state_conv, state_ssm, state_shift, state_wkv, c_prompt, c_sample, norm1_g, w_mod, b_mod, w_in, conv_w, conv_b, dt_bias, a_log, d_skip, m_norm_g, shift_mu, w0, w_up, a0, a_up, g_up, k_k, k_a, r_k, ln_x_g, ln_x_b, w_out, norm2_g, w_grp, b_grp, w_erouter, b_erouter, e_gate, e_up, e_down, final_norm_g):
    assert w_mod.shape[0] == 1, "single-layer trunk"
    bp, tp, d = x_prompt.shape
    bs, ts, _ = x_sample.shape
    mi = m_norm_g.shape[-1]
    rw = w0.shape[-1]
    lw, la, lg = w_up.shape[1], a_up.shape[1], g_up.shape[1]
    cfg = Cfg(d, bp, tp, bs, ts, mi, rw, lw, la, lg)
    nc = cfg.nc
    assert tp % CHUNK == 0 and ts % CHUNK == 0 and tp >= CONV_W - 1 and ts >= CONV_W - 1
    assert w_in.shape[-1] == cfg.o_rw + cfg.mh + cfg.r_cols and cfg.o_rw % LANES == 0
    assert state_conv.shape[-1] == cfg.cd and lw <= LANES and la <= LANES and cfg.mh <= LANES
    n = cfg.n_rows
    n_seq = cfg.n_seq
    mh, rh = cfg.mh, cfg.rh
    npair = rw // PAIR
    n_experts = e_gate.shape[1]

    x_p = x_prompt.reshape(bp * tp, d)
    x_s = x_sample.reshape(bs * ts, d)
    c_all = jnp.concatenate([c_prompt, c_sample], axis=0)

    mod = _modulation(c_all, w_mod[0], b_mod[0])
    shift1, scale1, gate1, shift2, scale2, gate2 = [mod[:, i * d:(i + 1) * d] for i in range(6)]

    h1 = _prenorm(x_p, x_s, shift1, scale1, norm1_g, cfg)
    w_in_t = w_in[0].T
    proj = _inproj(h1, w_in_t, cfg)
    dt_raw = _dtproj(h1, w_in_t, cfg)

    zeros = lambda b, *s: jnp.zeros((b,) + s, F32)
    conv0 = jnp.concatenate([zeros(bp, CONV_W - 1, cfg.cd), state_conv[0]], axis=0)
    conv8 = jnp.concatenate([zeros(n_seq, 8 - (CONV_W - 1), cfg.cd), conv0], axis=1)
    nst = state_ssm.shape[-1]
    ssm0 = jnp.concatenate([zeros(bp, mi, nst), state_ssm[0].reshape(bs, mi, nst)], axis=0)
    lane_pad = lambda a: jnp.concatenate([a.reshape(1, -1), jnp.zeros((1, LANES - a.shape[-1]), F32)], axis=1)
    expand = ((jnp.arange(2 * LANES)[:, None] % LANES) == (jnp.arange(mi)[None, :] // HEAD)).astype(BF16)
    ym, ssm_new = _mamba(proj, dt_raw, conv8, ssm0, conv_w[0], conv_b[0].reshape(1, -1), lane_pad(dt_bias[0]),
                         lane_pad(a_log[0]), jnp.repeat(d_skip[0], HEAD).reshape(1, mi), m_norm_g[0].reshape(1, mi),
                         expand, cfg)

    sh0 = jnp.concatenate([zeros(bp, 1, state_shift.shape[-1]), state_shift[0]], axis=0)
    sh8 = jnp.concatenate([zeros(n_seq, 7, cfg.win), _to_window(sh0, cfg)], axis=1)
    to_pairs = lambda s: s.reshape(-1, npair, 2, HEAD, HEAD).transpose(0, 1, 3, 2, 4).reshape(-1, npair, HEAD, PAIR)
    from_pairs = lambda s: s.reshape(-1, npair, HEAD, 2, HEAD).transpose(0, 1, 3, 2, 4).reshape(-1, rh, HEAD, HEAD)
    wkv0 = jnp.concatenate([zeros(bp, npair, HEAD, PAIR), to_pairs(state_wkv[0])], axis=0)
    row = lambda a: a.reshape(1, -1)
    yr, wkv_new = _rwkv(proj, sh8, wkv0, _to_window(row(shift_mu[0]), cfg), row(w0[0]),
                        _pad_rows(w_up[0], LANES), row(a0[0]), _pad_rows(a_up[0], LANES), g_up[0],
                        row(k_k[0]), row(k_a[0]), row(r_k[0]), row(ln_x_g[0]), row(ln_x_b[0]), cfg)

    wo = w_out[0].astype(BF16)
    x1 = _outproj(ym, yr, wo[:mi], wo[mi:], x_p, x_s, gate1, cfg)
    wr = jnp.concatenate([w_grp[0], w_erouter[0], jnp.zeros((d, LANES - E_GROUPS - n_experts), F32)], axis=1)
    br = lane_pad(jnp.concatenate([b_grp[0], b_erouter[0]]))
    h2, route = _router(x1, shift2, scale2, norm2_g, wr, br, cfg, n_experts)

    tm_e = 2 * E_SUB
    row_tok, slot_pos, blk_e, n_valid = _dispatch(route, n_experts, tm_e)
    yb = _experts(h2[row_tok], blk_e, n_valid, e_gate[0], e_up[0], e_down[0], tm_e)

    np_ = bp * tp
    fg = final_norm_g.reshape(1, d)
    y_prompt = _final(x1, yb, slot_pos, route, gate2, fg, cfg, 0, np_).reshape(bp, tp, d)
    y_sample = _final(x1, yb, slot_pos, route, gate2, fg, cfg, np_, bs * ts).reshape(bs, ts, d)
    tails = lambda b, t, base: jnp.stack(
        [lax.slice(proj, (base + (i + 1) * t - (CONV_W - 1), 0), (base + (i + 1) * t, nc)) for i in range(b)])
    pp = tails(bp, tp, 0)
    ps = tails(bs, ts, np_)
    conv_of = lambda p: p[:, :, mi:mi + cfg.cd][None]
    shift_of = lambda p: p[:, -1:, cfg.o_rw:cfg.o_rw + cfg.r_cols][None]
    ssm_new = ssm_new.reshape(n_seq, mh, HEAD, nst)
    wkv_new = from_pairs(wkv_new)
    return (y_prompt, y_sample,
            conv_of(pp), ssm_new[:bp][None], shift_of(pp), wkv_new[:bp][None],
            conv_of(ps), ssm_new[bp:][None], shift_of(ps), wkv_new[bp:][None])
```

```python
import functools
from typing import NamedTuple

import jax
import jax.numpy as jnp
from jax import lax
from jax.experimental import pallas as pl
from jax.experimental.pallas import tpu as pltpu

F32 = jnp.float32
BF16 = jnp.bfloat16
HI = lax.Precision.HIGHEST

LANES = 128
CHUNK = 64
HEAD = 64
PAIR = 2 * HEAD
SSD_GROUP = 4 * HEAD
NORM_EPS = 1e-6
M_NORM_EPS = 1e-5
R_LN_EPS = 64e-5
CONV_W = 4
E_GROUPS = 4
E_PER_GROUP = 8
TOP_K = 2
VMEM_LIMIT = 56 * 1024 * 1024
E_SUB = 256
EXP_NEG_HALF = 0.6065306597126334
SA_GROUP = 16
SCAN_STEPS = 64


class Cfg(NamedTuple):
    d: int
    bp: int
    tp: int
    bs: int
    ts: int
    mi: int
    rw: int
    lw: int
    la: int
    lg: int

    @property
    def cpp(self):
        return self.tp // CHUNK

    @property
    def cps(self):
        return self.ts // CHUNK

    @property
    def n_chunks(self):
        return self.bp * self.cpp + self.bs * self.cps

    @property
    def n_seq(self):
        return self.bp + self.bs

    @property
    def n_rows(self):
        return self.n_chunks * CHUNK

    @property
    def groups(self):
        return self.mi // SSD_GROUP

    @property
    def nb(self):
        return self.groups * LANES

    @property
    def cd(self):
        return self.mi + 2 * self.nb

    @property
    def mh(self):
        return self.mi // HEAD

    @property
    def rh(self):
        return self.rw // HEAD

    @property
    def o_rw(self):
        return self.mi + self.cd

    @property
    def r_cols(self):
        return 3 * self.rw + self.lw + self.la + self.lg

    @property
    def tail(self):
        return -(-(self.lw + self.la + self.lg) // LANES) * LANES

    @property
    def win(self):
        return 3 * self.rw + self.tail

    @property
    def nc(self):
        return self.o_rw + self.win


def _seq_of_chunk(c, cfg):
    npc = cfg.bp * cfg.cpp
    return jnp.where(c < npc, c // cfg.cpp, cfg.bp + (c - npc) // cfg.cps)


def _is_first_chunk(c, cfg):
    npc = cfg.bp * cfg.cpp
    return jnp.where(c < npc, c % cfg.cpp == 0, (c - npc) % cfg.cps == 0)


def _silu(x):
    return x * jax.nn.sigmoid(x)


def _nt(a, b, **kw):
    return lax.dot_general(a, b, (((1,), (1,)), ((), ())), preferred_element_type=F32, **kw)


def _tn(a, b, **kw):
    return lax.dot_general(a, b, (((0,), (0,)), ((), ())), preferred_element_type=F32, **kw)


def _params(*sem):
    return pltpu.CompilerParams(dimension_semantics=sem, vmem_limit_bytes=VMEM_LIMIT)


def _mod_kernel(c_ref, w_ref, b_ref, o_ref):
    o_ref[...] = jnp.dot(_silu(c_ref[...]), w_ref[...], preferred_element_type=F32) + b_ref[...]


def _modulation(c_all, w_mod, b_mod):
    s, d = c_all.shape
    cols = w_mod.shape[1]
    tn = 512
    return pl.pallas_call(
        _mod_kernel,
        grid=(cols // tn,),
        in_specs=[pl.BlockSpec((s, d), lambda j: (0, 0)),
                  pl.BlockSpec((d, tn), lambda j: (0, j)),
                  pl.BlockSpec((1, tn), lambda j: (0, j))],
        out_specs=pl.BlockSpec((s, tn), lambda j: (0, j)),
        out_shape=jax.ShapeDtypeStruct((s, cols), F32),
        compiler_params=_params("parallel"),
        name="modulation",
    )(c_all, w_mod, b_mod.reshape(1, cols))


def _modulated_norm(x, g, scale, shift):
    y = x * lax.rsqrt(jnp.mean(x * x, axis=-1, keepdims=True) + NORM_EPS) * g
    return y * (1.0 + scale) + shift


def _prompt_tiles(cfg, tm):
    assert (cfg.bp * cfg.tp) % tm == 0 and (cfg.bs * cfg.ts) % tm == 0 and tm % CHUNK == 0
    return cfg.bp * cfg.tp // tm


def _row_specs(cfg, tm, width, col=None):
    npt = _prompt_tiles(cfg, tm)
    c = (lambda *r: 0) if col is None else col
    return [pl.BlockSpec((tm, width), lambda i, *r: (jnp.minimum(i, npt - 1), jnp.where(i < npt, c(*r), 0))),
            pl.BlockSpec((tm, width), lambda i, *r: (jnp.maximum(i - npt, 0), jnp.where(i >= npt, c(*r), 0)))]


def _prenorm_kernel(xp_ref, xs_ref, shift_ref, scale_ref, g_ref, h_ref, *, cfg, tm):
    i = pl.program_id(0)

    def norm_rows(x_ref):
        for s in range(tm // CHUNK):
            seq = _seq_of_chunk(i * (tm // CHUNK) + s, cfg)
            rows = slice(s * CHUNK, (s + 1) * CHUNK)
            h = _modulated_norm(x_ref[rows, :], g_ref[...], scale_ref[pl.ds(seq, 1), :], shift_ref[pl.ds(seq, 1), :])
            h_ref[rows, :] = h.astype(BF16)

    npt = _prompt_tiles(cfg, tm)
    pl.when(i < npt)(lambda: norm_rows(xp_ref))
    pl.when(i >= npt)(lambda: norm_rows(xs_ref))


def _prenorm(x_p, x_s, shift, scale, g, cfg, tm=256):
    d = x_p.shape[1]
    full = lambda a: pl.BlockSpec(a.shape, lambda i: (0,) * a.ndim)
    return pl.pallas_call(
        functools.partial(_prenorm_kernel, cfg=cfg, tm=tm),
        grid=(cfg.n_rows // tm,),
        in_specs=_row_specs(cfg, tm, d) + [full(shift), full(scale), full(g)],
        out_specs=pl.BlockSpec((tm, d), lambda i: (i, 0)),
        out_shape=jax.ShapeDtypeStruct((cfg.n_rows, d), BF16),
        compiler_params=_params("parallel"),
        name="prenorm",
    )(x_p, x_s, shift, scale, g)


def _inproj_kernel(h_ref, w_ref, wn_ref, o_ref, wb_s, *, first_shifted, shift, tn, n_features):
    j = pl.program_id(0)
    new_tile = pl.program_id(1) == 0

    @pl.when(jnp.logical_and(new_tile, j < first_shifted))
    def _():
        wb_s[...] = w_ref[...].astype(BF16)

    @pl.when(jnp.logical_and(new_tile, j >= first_shifted))
    def _():
        w = jnp.concatenate([w_ref[...], wn_ref[...]], axis=0)[shift:shift + tn, :]
        feature = j * tn + shift + lax.broadcasted_iota(jnp.int32, w.shape, 0)
        wb_s[...] = jnp.where(feature < n_features, w, 0.0).astype(BF16)

    o_ref[...] = _nt(h_ref[...], wb_s[...])


def _inproj(h, wt, cfg, tm=1024, tn=512):
    n, d = h.shape
    nout = cfg.o_rw + cfg.win
    assert cfg.o_rw % tn == 0 and nout % tn == 0 and n % tm == 0 and cfg.mh <= LANES and cfg.mh % 8 == 0
    last_next = (wt.shape[0] - 1) // LANES
    return pl.pallas_call(
        functools.partial(_inproj_kernel, first_shifted=cfg.o_rw // tn, shift=cfg.mh, tn=tn, n_features=wt.shape[0]),
        grid=(nout // tn, n // tm),
        in_specs=[pl.BlockSpec((tm, d), lambda j, i: (i, 0)),
                  pl.BlockSpec((tn, d), lambda j, i: (j, 0)),
                  pl.BlockSpec((LANES, d), lambda j, i: (jnp.minimum((j + 1) * (tn // LANES), last_next), 0))],
        out_specs=pl.BlockSpec((tm, tn), lambda j, i: (i, j)),
        out_shape=jax.ShapeDtypeStruct((n, nout), F32),
        scratch_shapes=[pltpu.VMEM((tn, d), BF16)],
        compiler_params=_params("arbitrary", "arbitrary"),
        name="inproj",
    )(h, wt, wt)


def _dtproj_kernel(h_ref, w_ref, o_ref):
    o_ref[...] = _nt(h_ref[...], w_ref[...].astype(BF16))


def _dtproj(h, wt, cfg, tm=1024):
    n, d = h.shape
    return pl.pallas_call(
        _dtproj_kernel,
        grid=(n // tm,),
        in_specs=[pl.BlockSpec((tm, d), lambda i: (i, 0)), pl.BlockSpec((LANES, d), lambda i: (cfg.o_rw // LANES, 0))],
        out_specs=pl.BlockSpec((tm, LANES), lambda i: (i, 0)),
        out_shape=jax.ShapeDtypeStruct((n, LANES), F32),
        compiler_params=_params("parallel"),
        name="dtproj",
    )(h, wt)


def _mamba_init(cst_ref, sst_ref, ssm_ref, buf):
    buf[0:8, :] = cst_ref[0]
    ssm_ref[0] = sst_ref[0]


def _mamba_kernel(zx_ref, dt_ref, cst_ref, sst_ref, cw_ref, cb_ref, dtb_ref, alog_ref, dsk_ref, ng_ref, exp_ref,
                  ym_ref, ssm_ref, buf, u_scr, *, cfg, init=True):
    mi, nb, cd = cfg.mi, cfg.nb, cfg.cd
    L = CHUNK
    if init:
        pl.when(_is_first_chunk(pl.program_id(0), cfg))(lambda: _mamba_init(cst_ref, sst_ref, ssm_ref, buf))

    xbc = zx_ref[:, mi:mi + cd]
    buf[8:8 + L, :] = xbc
    conv = (cb_ref[...] + buf[5:5 + L, :] * cw_ref[0:1, :] + buf[6:6 + L, :] * cw_ref[1:2, :]
            + buf[7:7 + L, :] * cw_ref[2:3, :] + xbc * cw_ref[3:4, :])
    buf[0:8, :] = buf[L:L + 8, :]
    u_scr[...] = _silu(conv)

    dt = jax.nn.softplus(dt_ref[...] + dtb_ref[...])
    da = dt * (-jnp.exp(alog_ref[...]))
    row = lax.broadcasted_iota(jnp.int32, (L, mi), 0)
    pos = lax.broadcasted_iota(jnp.int32, (L, mi), 1) % HEAD
    tri2 = (lax.broadcasted_iota(jnp.int32, (L, 2 * L), 0)
            >= lax.broadcasted_iota(jnp.int32, (L, 2 * L), 1) % L).astype(BF16)
    lane = lax.broadcasted_iota(jnp.int32, (L, PAIR), 1)
    row16 = lax.broadcasted_iota(jnp.int32, (16, mi), 0)
    ones16 = jnp.ones((16, LANES), BF16)

    dt_hl = jnp.concatenate(_split_hi_lo(dt), axis=1)
    da_hl = jnp.concatenate(_split_hi_lo(da), axis=1)
    dte = jnp.dot(dt_hl, exp_ref[...], preferred_element_type=F32)
    dae = jnp.dot(da_hl, exp_ref[...], preferred_element_type=F32)
    cum = jnp.dot(tri2, jnp.concatenate(_split_hi_lo(dae), axis=0), preferred_element_type=F32)
    cum_row = jnp.sum(jnp.where(row == pos, cum, 0.0), axis=0, keepdims=True)
    last = cum[L - 1:L, :]
    decay = jnp.exp(jnp.where(row >= pos, cum - cum_row, -jnp.inf))
    xdt = u_scr[:, 0:mi] * dte
    x_end = xdt * jnp.exp(last - cum)
    e_cum = jnp.exp(cum)
    e_last = jnp.exp(last)
    e_hi = e_last.astype(BF16).astype(F32)
    e_rows = jnp.where(row16 == 0, e_hi, jnp.where(row16 == 1, e_last - e_hi, 0.0)).astype(BF16)

    groups = range(cfg.groups)
    gs = lambda g: slice(g * SSD_GROUP, (g + 1) * SSD_GROUP)
    b_of = lambda g: u_scr[:, mi + g * LANES:mi + (g + 1) * LANES]
    c_of = lambda g: u_scr[:, mi + nb + g * LANES:mi + nb + (g + 1) * LANES]
    sc2 = [_nt(c_of(g), jnp.concatenate([b_of(g), b_of(g)], axis=0)) for g in groups]
    y_diag = []
    for g in groups:
        ys = []
        for q in range(SSD_GROUP // PAIR):
            ps = slice(g * SSD_GROUP + q * PAIR, g * SSD_GROUP + (q + 1) * PAIR)
            xq = xdt[:, ps]
            rhs = jnp.concatenate([jnp.where(lane < HEAD, xq, 0.0), jnp.where(lane >= HEAD, xq, 0.0)], axis=0)
            ys.append(jnp.dot(decay[:, ps] * sc2[g], rhs, preferred_element_type=F32))
        y_diag.append(jnp.concatenate(ys, axis=1))
    y_off = [_nt(c_of(g), ssm_ref[0, gs(g), :]) for g in groups]
    new = [_tn(x_end[:, gs(g)], b_of(g)) for g in groups]
    dcol = [_tn(e_rows[:, gs(g)], ones16) for g in groups]
    for g in groups:
        cs = gs(g)
        ssm_ref[0, cs, :] = ssm_ref[0, cs, :] * dcol[g] + new[g]
        y = y_diag[g] + y_off[g] * e_cum[:, cs] + u_scr[:, cs] * dsk_ref[:, cs]
        y = y * _silu(zx_ref[:, cs])
        y = y * lax.rsqrt(jnp.mean(y * y, axis=-1, keepdims=True) + M_NORM_EPS)
        ym_ref[:, cs] = (y * ng_ref[:, cs]).astype(BF16)


def _mamba_plan(proj, dt_raw, conv_state8, ssm_state, cw, cb, dtb, alog, dsk, ng, expand, cfg):
    n = proj.shape[0]
    mi, cd = cfg.mi, cfg.cd
    nst = ssm_state.shape[-1]
    seq = lambda c: _seq_of_chunk(c, cfg)
    full = lambda a: pl.BlockSpec(a.shape, lambda c: (0,) * a.ndim)
    return _Plan(
        operands=(proj, dt_raw, conv_state8, ssm_state, cw, cb, dtb, alog, dsk, ng, expand),
        in_specs=[pl.BlockSpec((CHUNK, mi + cd), lambda c: (c, 0)),
                  pl.BlockSpec((CHUNK, LANES), lambda c: (c, 0)),
                  pl.BlockSpec((1, 8, cd), lambda c: (seq(c), 0, 0)),
                  pl.BlockSpec((1, mi, nst), lambda c: (seq(c), 0, 0)),
                  full(cw), full(cb), full(dtb), full(alog), full(dsk), full(ng), full(expand)],
        out_specs=[pl.BlockSpec((CHUNK, mi), lambda c: (c, 0)),
                   pl.BlockSpec((1, mi, nst), lambda c: (seq(c), 0, 0))],
        out_shape=[jax.ShapeDtypeStruct((n, mi), BF16),
                   jax.ShapeDtypeStruct((cfg.n_seq, mi, nst), F32)],
        scratch=[pltpu.VMEM((CHUNK + 8, cd), F32), pltpu.VMEM((CHUNK, cd), F32)])


def _seg_sum(x, ob):
    w = ob.shape[0]
    xb = x.astype(BF16)
    parts = [jnp.dot(xb[:, p * w:(p + 1) * w], ob, preferred_element_type=F32) for p in range(x.shape[1] // w)]
    return jnp.concatenate(parts, axis=1)


def _split_hi_lo(x):
    hi = x.astype(BF16)
    lo = (x - hi.astype(F32)).astype(BF16)
    return hi, lo


def _rwkv_init(sh_ref, wkv_in_ref, wkv_ref, pbuf, lbuf, cfg):
    pbuf[0:8, :] = sh_ref[0, :, 0:3 * cfg.rw]
    lbuf[0:8, :] = sh_ref[0, :, 3 * cfg.rw:]
    wkv_ref[0] = wkv_in_ref[0]


def _rwkv_kernel(rkv_ref, lora_ref, sh_ref, wkv_in_ref, mu_ref, w0_ref, wup_ref, a0_ref, aup_ref, gup_ref,
                 kk_ref, ka_ref, rk_ref, lng_ref, lnb_ref,
                 yr_ref, wkv_ref, pbuf, lbuf, w_s, a_s, b_s, k_s, y_s, vt_s, ut_s, y0_s, g_s, *, cfg, init=True):
    rw = cfg.rw
    L = CHUNK
    npair = rw // PAIR
    if init:
        pl.when(_is_first_chunk(pl.program_id(0), cfg))(
            lambda: _rwkv_init(sh_ref, wkv_in_ref, wkv_ref, pbuf, lbuf, cfg))

    pr = rkv_ref[...]
    pl_ = lora_ref[...]
    pbuf[8:8 + L, :] = pr
    lbuf[8:8 + L, :] = pl_
    xs = pr + (pbuf[7:7 + L, :] - pr) * mu_ref[:, 0:3 * rw]
    xl = pl_ + (lbuf[7:7 + L, :] - pl_) * mu_ref[:, 3 * rw:]
    pbuf[0:8, :] = pbuf[L:L + 8, :]
    lbuf[0:8, :] = lbuf[L:L + 8, :]

    r = xs[:, 0:rw]
    k = xs[:, rw:2 * rw]
    v = xs[:, 2 * rw:3 * rw]
    wd = xl[:, 0:LANES]
    ad = xl[:, cfg.lw:cfg.lw + LANES]
    gd = xl[:, cfg.lw + cfg.la:cfg.lw + cfg.la + cfg.lg]

    lane = lax.broadcasted_iota(jnp.int32, (PAIR, PAIR), 1)
    rowi = lax.broadcasted_iota(jnp.int32, (PAIR, PAIR), 0)
    o_row = lax.broadcasted_iota(jnp.int32, (2 * PAIR, 2 * PAIR), 0)
    o_lane = lax.broadcasted_iota(jnp.int32, (2 * PAIR, 2 * PAIR), 1)
    ob = ((o_row // HEAD) == (o_lane // HEAD)).astype(BF16)

    w_z = w0_ref[...] + jnp.dot(jnp.tanh(wd), wup_ref[...], preferred_element_type=F32)
    logw = -(jax.nn.sigmoid(w_z) * EXP_NEG_HALF)
    w_s[...] = jnp.exp(logw)
    a = jax.nn.sigmoid(a0_ref[...] + jnp.dot(ad, aup_ref[...], preferred_element_type=F32))
    kk = k * kk_ref[...]
    kk = kk * lax.rsqrt(jnp.maximum(_seg_sum(kk * kk, ob), 1e-24))
    kh = k * (1.0 + (a - 1.0) * ka_ref[...])
    a_s[...] = -kk
    b_s[...] = kk * a
    k_s[...] = kh

    tri2 = (lax.broadcasted_iota(jnp.int32, (L, 2 * L), 0)
            >= lax.broadcasted_iota(jnp.int32, (L, 2 * L), 1) % L).astype(BF16)
    cum = jnp.dot(tri2, jnp.concatenate(_split_hi_lo(logw), axis=0), preferred_element_type=F32)
    p_inv = jnp.exp(-cum)
    r_p = r * jnp.exp(cum)
    b_p = b_s[...] * p_inv
    k_p = kh * p_inv
    lo_lanes = lane < HEAD
    g_row = lax.broadcasted_iota(jnp.int32, (2 * PAIR, PAIR), 0)
    g_lane = lax.broadcasted_iota(jnp.int32, (2 * PAIR, PAIR), 1)
    g_keep = (g_row % HEAD) <= (g_lane % HEAD)

    def by_head(x):
        x2 = jnp.concatenate([x, x], axis=0)
        return jnp.where(lo_lanes == (rowi < HEAD), x2, 0.0).astype(BF16)

    pairs = range(npair)
    lanes_of = lambda p: slice(p * PAIR, (p + 1) * PAIR)
    rows_of = lambda p: slice(p * HEAD, (p + 1) * HEAD)
    r_bd = [by_head(r_p[:, lanes_of(p)]) for p in pairs]
    y0 = [_nt(wkv_ref[0, p].astype(BF16), r_bd[p]) for p in pairs]
    gram = [_nt(jnp.concatenate([by_head(b_p[:, lanes_of(p)]), by_head(k_p[:, lanes_of(p)])], axis=0), r_bd[p])
            for p in pairs]
    vts = [v[:, lanes_of(p)].T for p in pairs]
    for p in pairs:
        y0_s[rows_of(p), :] = y0[p]
        g_s[p] = jnp.where(g_keep, gram[p], 0.0).astype(BF16)
        vjt = jnp.concatenate([vts[p][0:HEAD], vts[p][HEAD:]], axis=1)
        hi = vjt.astype(BF16).astype(F32)
        vt_s[rows_of(p), :] = jnp.concatenate([hi, vjt - hi], axis=1).astype(BF16)

    lane_t = lax.broadcasted_iota(jnp.int32, (HEAD, PAIR), 1) % HEAD
    crow = lax.broadcasted_iota(jnp.int32, (2 * PAIR, 2 * PAIR), 0)
    clane = lax.broadcasted_iota(jnp.int32, (2 * PAIR, 2 * PAIR), 1)
    same_head = ((crow % PAIR) // HEAD) == ((clane % PAIR) // HEAD)
    c_dt = (crow % HEAD) - clane // PAIR

    def steps(blk, carry):
        t0 = pl.multiple_of(blk * SCAN_STEPS, SCAN_STEPS)
        rows = pl.ds(t0, SCAN_STEPS)
        w8, a8, b8, k8 = w_s[rows, :], a_s[rows, :], b_s[rows, :], k_s[rows, :]
        s = [wkv_ref[0, p] for p in range(npair)]
        vcol2 = None
        for i in range(SCAN_STEPS):
            row = lambda x8, p: x8[i:i + 1, p * PAIR:(p + 1) * PAIR]
            this_step = lane_t == t0 + i
            if i % 2 == 0:
                sel = (same_head & (c_dt == t0 + i)).astype(BF16)
                vcol2 = jnp.dot(vt_s[...], sel, preferred_element_type=F32)
            vcol = vcol2[:, (i % 2) * PAIR:(i % 2 + 1) * PAIR]
            for p0 in range(0, npair, SA_GROUP):
                group = range(p0, min(p0 + SA_GROUP, npair))
                lhs = []
                for p in group:
                    prod = (s[p] * row(a8, p)).astype(BF16)
                    lhs.append(jnp.concatenate([prod[0:HEAD // 2], prod[HEAD // 2:]], axis=1))
                sa = jnp.dot(jnp.concatenate(lhs, axis=0), ob, preferred_element_type=F32)
                for q, p in enumerate(group):
                    half = sa[q * (HEAD // 2):(q + 1) * (HEAD // 2)]
                    sa_p = jnp.concatenate([half[:, 0:PAIR], half[:, PAIR:]], axis=0)
                    s[p] = s[p] * row(w8, p) + sa_p * row(b8, p) + vcol[p * HEAD:(p + 1) * HEAD] * row(k8, p)
                    pltpu.store(ut_s.at[p * HEAD:(p + 1) * HEAD, :], sa_p, mask=this_step)
        for p in range(npair):
            wkv_ref[0, p] = s[p]
        return carry

    ut_s[...] = jnp.zeros_like(ut_s)
    lax.fori_loop(0, L // SCAN_STEPS, steps, 0)

    yts = [jnp.dot(jnp.concatenate([ut_s[rows_of(p), :].astype(BF16), vt_s[rows_of(p), 0:PAIR]], axis=1), g_s[p],
                   preferred_element_type=F32) for p in pairs]
    ytt = [(y0_s[rows_of(p), :] + yts[p]).T for p in pairs]
    for p in pairs:
        y_s[:, lanes_of(p)] = jnp.concatenate([ytt[p][0:L], ytt[p][L:]], axis=1)

    y = y_s[...]
    mu = _seg_sum(y, ob) * (1.0 / HEAD)
    dlt = y - mu
    var = _seg_sum(dlt * dlt, ob) * (1.0 / HEAD)
    yn = dlt * lax.rsqrt(var + R_LN_EPS) * lng_ref[...] + lnb_ref[...]
    bonus = _seg_sum(r * k_s[...] * rk_ref[...], ob) * v
    g = jnp.dot(jax.nn.sigmoid(gd), gup_ref[...], preferred_element_type=F32)
    yr_ref[...] = ((yn + bonus) * g).astype(BF16)


class _Plan(NamedTuple):
    operands: tuple
    in_specs: list
    out_specs: list
    out_shape: list
    scratch: list


def _rwkv_plan(proj, shift8, wkv_pairs, mu, w0, wup, a0, aup, gup, kk, ka, rk, lng, lnb, cfg):
    n = proj.shape[0]
    rw, lp = cfg.rw, cfg.tail
    assert cfg.o_rw % (3 * rw) == 0 and (cfg.o_rw + 3 * rw) % lp == 0
    npair = rw // PAIR
    seq = lambda c: _seq_of_chunk(c, cfg)
    full = lambda a: pl.BlockSpec(a.shape, lambda c: (0,) * a.ndim)
    row_scr = pltpu.VMEM((CHUNK, rw), F32)
    return _Plan(
        operands=(proj, proj, shift8, wkv_pairs, mu, w0, wup, a0, aup, gup, kk, ka, rk, lng, lnb),
        in_specs=[pl.BlockSpec((CHUNK, 3 * rw), lambda c: (c, cfg.o_rw // (3 * rw))),
                  pl.BlockSpec((CHUNK, lp), lambda c: (c, (cfg.o_rw + 3 * rw) // lp)),
                  pl.BlockSpec((1, 8, 3 * rw + lp), lambda c: (seq(c), 0, 0)),
                  pl.BlockSpec((1, npair, HEAD, PAIR), lambda c: (seq(c), 0, 0, 0)),
                  full(mu), full(w0), full(wup), full(a0), full(aup), full(gup),
                  full(kk), full(ka), full(rk), full(lng), full(lnb)],
        out_specs=[pl.BlockSpec((CHUNK, rw), lambda c: (c, 0)),
                   pl.BlockSpec((1, npair, HEAD, PAIR), lambda c: (seq(c), 0, 0, 0))],
        out_shape=[jax.ShapeDtypeStruct((n, rw), BF16),
                   jax.ShapeDtypeStruct((cfg.n_seq, npair, HEAD, PAIR), F32)],
        scratch=[pltpu.VMEM((CHUNK + 8, 3 * rw), F32), pltpu.VMEM((CHUNK + 8, lp), F32),
                 row_scr, row_scr, row_scr, row_scr, row_scr,
                 pltpu.VMEM((npair * HEAD, 2 * PAIR), BF16), pltpu.VMEM((npair * HEAD, PAIR), F32),
                 pltpu.VMEM((npair * HEAD, PAIR), F32), pltpu.VMEM((npair, 2 * PAIR, PAIR), BF16)])


def _mixers_kernel(*refs, cfg, split):
    m_in, r_in, m_out, r_out, m_scr, r_scr = [refs[a:b] for a, b in zip(split[:-1], split[1:])]

    @pl.when(_is_first_chunk(pl.program_id(0), cfg))
    def _():
        _mamba_init(m_in[2], m_in[3], m_out[1], m_scr[0])
        _rwkv_init(r_in[2], r_in[3], r_out[1], r_scr[0], r_scr[1], cfg)

    _mamba_kernel(*m_in, *m_out, *m_scr, cfg=cfg, init=False)
    _rwkv_kernel(*r_in, *r_out, *r_scr, cfg=cfg, init=False)


def _mixers(mamba, rwkv, cfg):
    sizes = [len(mamba.operands), len(rwkv.operands), len(mamba.out_specs), len(rwkv.out_specs),
             len(mamba.scratch), len(rwkv.scratch)]
    split = [sum(sizes[:i]) for i in range(len(sizes) + 1)]
    return pl.pallas_call(
        functools.partial(_mixers_kernel, cfg=cfg, split=split),
        grid=(cfg.n_chunks,),
        in_specs=mamba.in_specs + rwkv.in_specs,
        out_specs=mamba.out_specs + rwkv.out_specs,
        out_shape=mamba.out_shape + rwkv.out_shape,
        scratch_shapes=mamba.scratch + rwkv.scratch,
        compiler_params=_params("arbitrary"),
        name="mixers",
    )(*mamba.operands, *rwkv.operands)


def _outproj_kernel(ym_ref, yr_ref, wa_ref, wb_ref, xp_ref, xs_ref, gate_ref, o_ref, *, cfg, tm):
    i = pl.program_id(0)
    acc = (jnp.dot(ym_ref[...], wa_ref[...], preferred_element_type=F32)
           + jnp.dot(yr_ref[...], wb_ref[...], preferred_element_type=F32))

    def residual(x_ref):
        for s in range(tm // CHUNK):
            seq = _seq_of_chunk(i * (tm // CHUNK) + s, cfg)
            rows = slice(s * CHUNK, (s + 1) * CHUNK)
            o_ref[rows, :] = x_ref[rows, :] + gate_ref[pl.ds(seq, 1), :] * acc[rows, :]

    npt = _prompt_tiles(cfg, tm)
    pl.when(i < npt)(lambda: residual(xp_ref))
    pl.when(i >= npt)(lambda: residual(xs_ref))


def _outproj(ym, yr, wa, wb, x_p, x_s, gate, cfg, tm=512, tn=1024):
    n, d = cfg.n_rows, x_p.shape[1]
    ka, kb = ym.shape[1], yr.shape[1]
    s = gate.shape[0]
    return pl.pallas_call(
        functools.partial(_outproj_kernel, cfg=cfg, tm=tm),
        grid=(n // tm, d // tn),
        in_specs=[pl.BlockSpec((tm, ka), lambda i, j: (i, 0)),
                  pl.BlockSpec((tm, kb), lambda i, j: (i, 0)),
                  pl.BlockSpec((ka, tn), lambda i, j: (0, j)),
                  pl.BlockSpec((kb, tn), lambda i, j: (0, j))]
                 + _row_specs(cfg, tm, tn, lambda j: j)
                 + [pl.BlockSpec((s, tn), lambda i, j: (0, j))],
        out_specs=pl.BlockSpec((tm, tn), lambda i, j: (i, j)),
        out_shape=jax.ShapeDtypeStruct((n, d), F32),
        compiler_params=_params("parallel", "parallel"),
        name="outproj",
    )(ym, yr, wa, wb, x_p, x_s, gate)


def _route(lg, n_experts):
    lane = lax.broadcasted_iota(jnp.int32, lg.shape, 1)
    first = lambda mask: jnp.min(jnp.where(mask, lane, LANES), axis=-1, keepdims=True)
    is_grp = lane < E_GROUPS
    gmax = jnp.max(jnp.where(is_grp, lg, -jnp.inf), axis=-1, keepdims=True)
    gsel = first(is_grp & (lg == gmax))
    pg = 1.0 / jnp.sum(jnp.where(is_grp, jnp.exp(lg - gmax), 0.0), axis=-1, keepdims=True)
    e_lane = lane - E_GROUPS
    in_grp = (e_lane >= 0) & (e_lane < n_experts) & (e_lane // E_PER_GROUP == gsel)
    emax = jnp.max(jnp.where(in_grp, lg, -jnp.inf), axis=-1, keepdims=True)
    p = jnp.where(in_grp, jnp.exp(lg - emax), 0.0)
    eprob = p / jnp.sum(p, axis=-1, keepdims=True)
    v1 = jnp.max(jnp.where(in_grp, eprob, -1.0), axis=-1, keepdims=True)
    i1 = first(in_grp & (eprob == v1))
    rest = in_grp & (lane != i1)
    v2 = jnp.max(jnp.where(rest, eprob, -1.0), axis=-1, keepdims=True)
    i2 = first(rest & (eprob == v2))
    tot = v1 + v2
    cols = [(i1 - E_GROUPS).astype(F32), (i2 - E_GROUPS).astype(F32), v1 / tot * pg, v2 / tot * pg]
    out = jnp.zeros(lg.shape, F32)
    for c, val in enumerate(cols):
        out = jnp.where(lane == c, val, out)
    return out


def _router_kernel(x_ref, shift_ref, scale_ref, g_ref, wh_ref, wl_ref, br_ref, h_ref, rt_ref, *, cfg, tm, n_experts):
    i = pl.program_id(0)
    for s in range(tm // CHUNK):
        seq = _seq_of_chunk(i * (tm // CHUNK) + s, cfg)
        rows = slice(s * CHUNK, (s + 1) * CHUNK)
        h = _modulated_norm(x_ref[rows, :], g_ref[...], scale_ref[pl.ds(seq, 1), :], shift_ref[pl.ds(seq, 1), :])
        h_hi, h_lo = _split_hi_lo(h)
        h_ref[rows, :] = h_hi
        dot = lambda a, b: jnp.dot(a, b[...], preferred_element_type=F32)
        logits = dot(h_hi, wh_ref) + (dot(h_hi, wl_ref) + dot(h_lo, wh_ref)) + br_ref[...]
        rt_ref[rows, :] = _route(logits, n_experts)


def _router(x1, shift, scale, g, wr, br, cfg, n_experts, tm=256):
    n, d = x1.shape
    assert E_GROUPS + n_experts <= LANES and n_experts == E_GROUPS * E_PER_GROUP
    full = lambda a: pl.BlockSpec(a.shape, lambda i: (0,) * a.ndim)
    w_hi, w_lo = _split_hi_lo(wr)
    return pl.pallas_call(
        functools.partial(_router_kernel, cfg=cfg, tm=tm, n_experts=n_experts),
        grid=(n // tm,),
        in_specs=[pl.BlockSpec((tm, d), lambda i: (i, 0)), full(shift), full(scale), full(g),
                  full(w_hi), full(w_lo), full(br)],
        out_specs=[pl.BlockSpec((tm, d), lambda i: (i, 0)), pl.BlockSpec((tm, LANES), lambda i: (i, 0))],
        out_shape=[jax.ShapeDtypeStruct((n, d), BF16), jax.ShapeDtypeStruct((n, LANES), F32)],
        compiler_params=_params("parallel"),
        name="router",
    )(x1, shift, scale, g, w_hi, w_lo, br)


def _new_weights(be_ref):
    i = pl.program_id(1)
    return jnp.logical_or(i == 0, be_ref[i] != be_ref[jnp.maximum(i - 1, 0)])


def _for_valid_rows(nv_ref, out_ref, compute):
    nv = nv_ref[pl.program_id(1)]
    for s in range(out_ref.shape[0] // E_SUB):
        rows = slice(s * E_SUB, (s + 1) * E_SUB)

        @pl.when(nv > s * E_SUB)
        def _():
            out_ref[rows, :] = compute(rows)

        @pl.when(nv <= s * E_SUB)
        def _():
            out_ref[rows, :] = jnp.zeros((E_SUB, out_ref.shape[1]), out_ref.dtype)


def _expert_up_kernel(be_ref, nv_ref, x_ref, wg_ref, wu_ref, h_ref, wg_s, wu_s):
    @pl.when(_new_weights(be_ref))
    def _():
        wg_s[...] = wg_ref[0].astype(BF16)
        wu_s[...] = wu_ref[0].astype(BF16)

    def hidden(rows):
        x = x_ref[rows, :]
        gate = jnp.dot(x, wg_s[...], preferred_element_type=F32)
        up = jnp.dot(x, wu_s[...], preferred_element_type=F32)
        return (_silu(gate) * up).astype(BF16)

    _for_valid_rows(nv_ref, h_ref, hidden)


def _expert_down_kernel(be_ref, nv_ref, h_ref, wd_ref, y_ref, wd_s):
    @pl.when(_new_weights(be_ref))
    def _():
        wd_s[...] = wd_ref[0].astype(BF16)

    _for_valid_rows(nv_ref, y_ref, lambda rows: jnp.dot(h_ref[rows, :], wd_s[...], preferred_element_type=F32))


def _experts(xb, blk_e, n_valid, wg, wu, wd, tm, tf=512, tn=2048):
    rows, d = xb.shape
    f = wg.shape[2]
    nblk = rows // tm
    tf, tn = min(tf, f), min(tn, d)
    assert tm % E_SUB == 0 and f % tf == 0 and d % tn == 0
    hid = pl.pallas_call(
        _expert_up_kernel,
        grid_spec=pltpu.PrefetchScalarGridSpec(
            num_scalar_prefetch=2, grid=(f // tf, nblk),
            in_specs=[pl.BlockSpec((tm, d), lambda j, i, be, nv: (i, 0)),
                      pl.BlockSpec((1, d, tf), lambda j, i, be, nv: (be[i], 0, j)),
                      pl.BlockSpec((1, d, tf), lambda j, i, be, nv: (be[i], 0, j))],
            out_specs=pl.BlockSpec((tm, tf), lambda j, i, be, nv: (i, j)),
            scratch_shapes=[pltpu.VMEM((d, tf), BF16), pltpu.VMEM((d, tf), BF16)]),
        out_shape=jax.ShapeDtypeStruct((rows, f), BF16),
        compiler_params=_params("arbitrary", "arbitrary"),
        name="expert_up",
    )(blk_e, n_valid, xb, wg, wu)
    return pl.pallas_call(
        _expert_down_kernel,
        grid_spec=pltpu.PrefetchScalarGridSpec(
            num_scalar_prefetch=2, grid=(d // tn, nblk),
            in_specs=[pl.BlockSpec((tm, f), lambda j, i, be, nv: (i, 0)),
                      pl.BlockSpec((1, f, tn), lambda j, i, be, nv: (be[i], 0, j))],
            out_specs=pl.BlockSpec((tm, tn), lambda j, i, be, nv: (i, j)),
            scratch_shapes=[pltpu.VMEM((f, tn), BF16)]),
        out_shape=jax.ShapeDtypeStruct((rows, d), F32),
        compiler_params=_params("arbitrary", "arbitrary"),
        name="expert_down",
    )(blk_e, n_valid, hid, wd)


def _final_kernel(pos_ref, x_ref, rt_ref, gate_ref, g_ref, y_hbm, o_ref, ybuf, sem, *, cfg, tm, tile0, ntiles):
    i = pl.program_id(0)

    def row_copy(pos, slot, k, r):
        return pltpu.make_async_copy(y_hbm.at[pl.ds(pos, 1), :], ybuf.at[slot, k, pl.ds(r, 1), :], sem.at[slot])

    def fetch(tile, slot):
        base = (tile + tile0) * (tm * TOP_K)

        def body(r, c):
            for k in range(TOP_K):
                row_copy(pos_ref[base + r * TOP_K + k], slot, k, r).start()
            return c

        lax.fori_loop(0, tm, body, 0, unroll=8)

    def wait_all(slot):
        def body(r, c):
            for k in range(TOP_K):
                row_copy(0, slot, k, r).wait()
            return c

        lax.fori_loop(0, tm, body, 0, unroll=8)

    pl.when(i == 0)(lambda: fetch(0, 0))
    pl.when(i + 1 < ntiles)(lambda: fetch(i + 1, (i + 1) % 2))
    slot = i % 2
    wait_all(slot)

    for s in range(tm // CHUNK):
        seq = _seq_of_chunk((i + tile0) * (tm // CHUNK) + s, cfg)
        rows = slice(s * CHUNK, (s + 1) * CHUNK)
        rt = rt_ref[rows, :]
        moe = ybuf[slot, 0, rows, :] * rt[:, TOP_K:TOP_K + 1]
        for k in range(1, TOP_K):
            moe = moe + ybuf[slot, k, rows, :] * rt[:, TOP_K + k:TOP_K + k + 1]
        x = x_ref[rows, :] + gate_ref[pl.ds(seq, 1), :] * moe
        o_ref[rows, :] = x * lax.rsqrt(jnp.mean(x * x, axis=-1, keepdims=True) + NORM_EPS) * g_ref[...]


def _final(x1, yb, slot_pos, route, gate, g, cfg, row0, nrows, tm=256):
    d = x1.shape[1]
    assert row0 % tm == 0 and nrows % tm == 0
    tile0 = row0 // tm
    ntiles = nrows // tm
    full = lambda a: pl.BlockSpec(a.shape, lambda i, pos: (0,) * a.ndim)
    return pl.pallas_call(
        functools.partial(_final_kernel, cfg=cfg, tm=tm, tile0=tile0, ntiles=ntiles),
        grid_spec=pltpu.PrefetchScalarGridSpec(
            num_scalar_prefetch=1, grid=(ntiles,),
            in_specs=[pl.BlockSpec((tm, d), lambda i, pos: (i + tile0, 0)),
                      pl.BlockSpec((tm, LANES), lambda i, pos: (i + tile0, 0)),
                      full(gate), full(g), pl.BlockSpec(memory_space=pl.ANY)],
            out_specs=pl.BlockSpec((tm, d), lambda i, pos: (i, 0)),
            scratch_shapes=[pltpu.VMEM((2, TOP_K, tm, d), F32), pltpu.SemaphoreType.DMA((2,))]),
        out_shape=jax.ShapeDtypeStruct((nrows, d), F32),
        compiler_params=_params("arbitrary"),
        name="final_norm",
    )(slot_pos.reshape(-1), x1, route, gate, g, yb)


def _dispatch(route, n_experts, tm):
    n = route.shape[0]
    eid = route[:, :TOP_K].astype(jnp.int32)

    m = n * TOP_K
    e_flat = eid.reshape(m)
    order = jnp.argsort(e_flat).astype(jnp.int32)
    rank = jnp.argsort(order).astype(jnp.int32)
    e_sorted = e_flat[order]
    experts = jnp.arange(n_experts, dtype=jnp.int32)
    counts = jnp.sum(e_flat[:, None] == experts[None, :], axis=0, dtype=jnp.int32)
    start = jnp.cumsum(counts) - counts
    padded = (counts + tm - 1) // tm * tm
    pend = jnp.cumsum(padded)
    pstart = pend - padded
    dest_sorted = pstart[e_sorted] + jnp.arange(m, dtype=jnp.int32) - start[e_sorted]
    slot_pos = dest_sorted[rank].reshape(n, TOP_K)
    n_blocks = -(-(m + n_experts * (tm - 1)) // tm)
    rows = n_blocks * tm
    r = jnp.arange(rows, dtype=jnp.int32)
    owner = lambda row: jnp.minimum(jnp.sum(row[:, None] >= pend[None, :], axis=1, dtype=jnp.int32), n_experts - 1)
    e_r = owner(r)
    idx = r - pstart[e_r]
    row_tok = jnp.where(idx < counts[e_r], order[jnp.clip(start[e_r] + idx, 0, m - 1)] // TOP_K, r % n)
    n_used = pend[-1] // tm
    blk = jnp.arange(n_blocks, dtype=jnp.int32)
    blk_e = owner(blk * tm)
    n_valid = jnp.where(blk < n_used, jnp.clip(pstart[blk_e] + counts[blk_e] - blk * tm, 0, tm), 0).astype(jnp.int32)
    last_e = blk_e[jnp.maximum(n_used - 1, 0)]
    blk_e = jnp.where(blk < n_used, blk_e, last_e)
    return row_tok, slot_pos, blk_e, n_valid


def _to_window(a, cfg):
    return jnp.concatenate([a, jnp.zeros(a.shape[:-1] + (cfg.win - cfg.r_cols,), a.dtype)], axis=-1)


def _pad_rows(a, rows):
    return jnp.concatenate([a, jnp.zeros((rows - a.shape[0],) + a.shape[1:], a.dtype)], axis=0)


def kernel(x_prompt, x_sample, state_conv, state_ssm, state_shift, state_wkv, c_prompt, c_sample, norm1_g, w_mod, b_mod, w_in, conv_w, conv_b, dt_bias, a_log, d_skip, m_norm_g, shift_mu, w0, w_up, a0, a_up, g_up, k_k, k_a, r_k, ln_x_g, ln_x_b, w_out, norm2_g, w_grp, b_grp, w_erouter, b_erouter, e_gate, e_up, e_down, final_norm_g):
    assert w_mod.shape[0] == 1, "single-layer trunk"
    bp, tp, d = x_prompt.shape
    bs, ts, _ = x_sample.shape
    mi = m_norm_g.shape[-1]
    rw = w0.shape[-1]
    lw, la, lg = w_up.shape[1], a_up.shape[1], g_up.shape[1]
    cfg = Cfg(d, bp, tp, bs, ts, mi, rw, lw, la, lg)
    nc = cfg.nc
    assert tp % CHUNK == 0 and ts % CHUNK == 0 and tp >= CONV_W - 1 and ts >= CONV_W - 1
    assert w_in.shape[-1] == cfg.o_rw + cfg.mh + cfg.r_cols and cfg.o_rw % LANES == 0
    assert state_conv.shape[-1] == cfg.cd and lw <= LANES and la <= LANES and cfg.mh <= LANES
    n = cfg.n_rows
    n_seq = cfg.n_seq
    mh, rh = cfg.mh, cfg.rh
    npair = rw // PAIR
    n_experts = e_gate.shape[1]

    x_p = x_prompt.reshape(bp * tp, d)
    x_s = x_sample.reshape(bs * ts, d)
    c_all = jnp.concatenate([c_prompt, c_sample], axis=0)

    mod = _modulation(c_all, w_mod[0], b_mod[0])
    shift1, scale1, gate1, shift2, scale2, gate2 = [mod[:, i * d:(i + 1) * d] for i in range(6)]

    h1 = _prenorm(x_p, x_s, shift1, scale1, norm1_g, cfg)
    w_in_t = w_in[0].T
    proj = _inproj(h1, w_in_t, cfg)
    dt_raw = _dtproj(h1, w_in_t, cfg)

    zeros = lambda b, *s: jnp.zeros((b,) + s, F32)
    conv0 = jnp.concatenate([zeros(bp, CONV_W - 1, cfg.cd), state_conv[0]], axis=0)
    conv8 = jnp.concatenate([zeros(n_seq, 8 - (CONV_W - 1), cfg.cd), conv0], axis=1)
    nst = state_ssm.shape[-1]
    ssm0 = jnp.concatenate([zeros(bp, mi, nst), state_ssm[0].reshape(bs, mi, nst)], axis=0)
    lane_pad = lambda a: jnp.concatenate([a.reshape(1, -1), jnp.zeros((1, LANES - a.shape[-1]), F32)], axis=1)
    expand = ((jnp.arange(2 * LANES)[:, None] % LANES) == (jnp.arange(mi)[None, :] // HEAD)).astype(BF16)
    mamba = _mamba_plan(proj, dt_raw, conv8, ssm0, conv_w[0], conv_b[0].reshape(1, -1), lane_pad(dt_bias[0]),
                        lane_pad(a_log[0]), jnp.repeat(d_skip[0], HEAD).reshape(1, mi), m_norm_g[0].reshape(1, mi),
                        expand, cfg)

    sh0 = jnp.concatenate([zeros(bp, 1, state_shift.shape[-1]), state_shift[0]], axis=0)
    sh8 = jnp.concatenate([zeros(n_seq, 7, cfg.win), _to_window(sh0, cfg)], axis=1)
    to_pairs = lambda s: s.reshape(-1, npair, 2, HEAD, HEAD).transpose(0, 1, 3, 2, 4).reshape(-1, npair, HEAD, PAIR)
    from_pairs = lambda s: s.reshape(-1, npair, HEAD, 2, HEAD).transpose(0, 1, 3, 2, 4).reshape(-1, rh, HEAD, HEAD)
    wkv0 = jnp.concatenate([zeros(bp, npair, HEAD, PAIR), to_pairs(state_wkv[0])], axis=0)
    row = lambda a: a.reshape(1, -1)
    rwkv = _rwkv_plan(proj, sh8, wkv0, _to_window(row(shift_mu[0]), cfg), row(w0[0]),
                      _pad_rows(w_up[0], LANES), row(a0[0]), _pad_rows(a_up[0], LANES), g_up[0],
                      row(k_k[0]), row(k_a[0]), row(r_k[0]), row(ln_x_g[0]), row(ln_x_b[0]), cfg)
    ym, ssm_new, yr, wkv_new = _mixers(mamba, rwkv, cfg)

    wo = w_out[0].astype(BF16)
    x1 = _outproj(ym, yr, wo[:mi], wo[mi:], x_p, x_s, gate1, cfg)
    wr = jnp.concatenate([w_grp[0], w_erouter[0], jnp.zeros((d, LANES - E_GROUPS - n_experts), F32)], axis=1)
    br = lane_pad(jnp.concatenate([b_grp[0], b_erouter[0]]))
    h2, route = _router(x1, shift2, scale2, norm2_g, wr, br, cfg, n_experts)

    tm_e = 2 * E_SUB
    row_tok, slot_pos, blk_e, n_valid = _dispatch(route, n_experts, tm_e)
    yb = _experts(h2[row_tok], blk_e, n_valid, e_gate[0], e_up[0], e_down[0], tm_e)

    np_ = bp * tp
    fg = final_norm_g.reshape(1, d)
    y_prompt = _final(x1, yb, slot_pos, route, gate2, fg, cfg, 0, np_).reshape(bp, tp, d)
    y_sample = _final(x1, yb, slot_pos, route, gate2, fg, cfg, np_, bs * ts).reshape(bs, ts, d)
    tails = lambda b, t, base: jnp.stack(
        [lax.slice(proj, (base + (i + 1) * t - (CONV_W - 1), 0), (base + (i + 1) * t, nc)) for i in range(b)])
    pp = tails(bp, tp, 0)
    ps = tails(bs, ts, np_)
    conv_of = lambda p: p[:, :, mi:mi + cfg.cd][None]
    shift_of = lambda p: p[:, -1:, cfg.o_rw:cfg.o_rw + cfg.r_cols][None]
    ssm_new = ssm_new.reshape(n_seq, mh, HEAD, nst)
    wkv_new = from_pairs(wkv_new)
    return (y_prompt, y_sample,
            conv_of(pp), ssm_new[:bp][None], shift_of(pp), wkv_new[:bp][None],
            conv_of(ps), ssm_new[bp:][None], shift_of(ps), wkv_new[bp:][None])
```

```python
import functools
from typing import NamedTuple

import jax
import jax.numpy as jnp
from jax import lax
from jax.experimental import pallas as pl
from jax.experimental.pallas import tpu as pltpu

F32 = jnp.float32
BF16 = jnp.bfloat16
HI = lax.Precision.HIGHEST

LANES = 128
CHUNK = 64
HEAD = 64
PAIR = 2 * HEAD
SSD_GROUP = 4 * HEAD
NORM_EPS = 1e-6
M_NORM_EPS = 1e-5
R_LN_EPS = 64e-5
CONV_W = 4
E_GROUPS = 4
E_PER_GROUP = 8
TOP_K = 2
VMEM_LIMIT = 56 * 1024 * 1024
E_SUB = 256
EXP_NEG_HALF = 0.6065306597126334
SA_GROUP = 16
SCAN_STEPS = 64


class Cfg(NamedTuple):
    d: int
    bp: int
    tp: int
    bs: int
    ts: int
    mi: int
    rw: int
    lw: int
    la: int
    lg: int

    @property
    def cpp(self):
        return self.tp // CHUNK

    @property
    def cps(self):
        return self.ts // CHUNK

    @property
    def n_chunks(self):
        return self.bp * self.cpp + self.bs * self.cps

    @property
    def n_seq(self):
        return self.bp + self.bs

    @property
    def n_rows(self):
        return self.n_chunks * CHUNK

    @property
    def groups(self):
        return self.mi // SSD_GROUP

    @property
    def nb(self):
        return self.groups * LANES

    @property
    def cd(self):
        return self.mi + 2 * self.nb

    @property
    def mh(self):
        return self.mi // HEAD

    @property
    def rh(self):
        return self.rw // HEAD

    @property
    def o_rw(self):
        return self.mi + self.cd

    @property
    def r_cols(self):
        return 3 * self.rw + self.lw + self.la + self.lg

    @property
    def tail(self):
        return -(-(self.lw + self.la + self.lg) // LANES) * LANES

    @property
    def win(self):
        return 3 * self.rw + self.tail

    @property
    def nc(self):
        return self.o_rw + self.win


def _seq_of_chunk(c, cfg):
    npc = cfg.bp * cfg.cpp
    return jnp.where(c < npc, c // cfg.cpp, cfg.bp + (c - npc) // cfg.cps)


def _is_first_chunk(c, cfg):
    npc = cfg.bp * cfg.cpp
    return jnp.where(c < npc, c % cfg.cpp == 0, (c - npc) % cfg.cps == 0)


def _silu(x):
    return x * jax.nn.sigmoid(x)


def _nt(a, b, **kw):
    return lax.dot_general(a, b, (((1,), (1,)), ((), ())), preferred_element_type=F32, **kw)


def _tn(a, b, **kw):
    return lax.dot_general(a, b, (((0,), (0,)), ((), ())), preferred_element_type=F32, **kw)


def _params(*sem):
    return pltpu.CompilerParams(dimension_semantics=sem, vmem_limit_bytes=VMEM_LIMIT)


def _mod_kernel(c_ref, w_ref, b_ref, o_ref):
    o_ref[...] = jnp.dot(_silu(c_ref[...]), w_ref[...], preferred_element_type=F32) + b_ref[...]


def _modulation(c_all, w_mod, b_mod):
    s, d = c_all.shape
    cols = w_mod.shape[1]
    tn = 512
    return pl.pallas_call(
        _mod_kernel,
        grid=(cols // tn,),
        in_specs=[pl.BlockSpec((s, d), lambda j: (0, 0)),
                  pl.BlockSpec((d, tn), lambda j: (0, j)),
                  pl.BlockSpec((1, tn), lambda j: (0, j))],
        out_specs=pl.BlockSpec((s, tn), lambda j: (0, j)),
        out_shape=jax.ShapeDtypeStruct((s, cols), F32),
        compiler_params=_params("parallel"),
        name="modulation",
    )(c_all, w_mod, b_mod.reshape(1, cols))


def _modulated_norm(x, g, scale, shift):
    y = x * lax.rsqrt(jnp.mean(x * x, axis=-1, keepdims=True) + NORM_EPS) * g
    return y * (1.0 + scale) + shift


def _prompt_tiles(cfg, tm):
    assert (cfg.bp * cfg.tp) % tm == 0 and (cfg.bs * cfg.ts) % tm == 0 and tm % CHUNK == 0
    return cfg.bp * cfg.tp // tm


def _row_specs(cfg, tm, width, col=None):
    npt = _prompt_tiles(cfg, tm)
    c = (lambda *r: 0) if col is None else col
    return [pl.BlockSpec((tm, width), lambda i, *r: (jnp.minimum(i, npt - 1), jnp.where(i < npt, c(*r), 0))),
            pl.BlockSpec((tm, width), lambda i, *r: (jnp.maximum(i - npt, 0), jnp.where(i >= npt, c(*r), 0)))]


def _prenorm_kernel(xp_ref, xs_ref, shift_ref, scale_ref, g_ref, h_ref, *, cfg, tm):
    i = pl.program_id(0)

    def norm_rows(x_ref):
        for s in range(tm // CHUNK):
            seq = _seq_of_chunk(i * (tm // CHUNK) + s, cfg)
            rows = slice(s * CHUNK, (s + 1) * CHUNK)
            h = _modulated_norm(x_ref[rows, :], g_ref[...], scale_ref[pl.ds(seq, 1), :], shift_ref[pl.ds(seq, 1), :])
            h_ref[rows, :] = h.astype(BF16)

    npt = _prompt_tiles(cfg, tm)
    pl.when(i < npt)(lambda: norm_rows(xp_ref))
    pl.when(i >= npt)(lambda: norm_rows(xs_ref))


def _prenorm(x_p, x_s, shift, scale, g, cfg, tm=256):
    d = x_p.shape[1]
    full = lambda a: pl.BlockSpec(a.shape, lambda i: (0,) * a.ndim)
    return pl.pallas_call(
        functools.partial(_prenorm_kernel, cfg=cfg, tm=tm),
        grid=(cfg.n_rows // tm,),
        in_specs=_row_specs(cfg, tm, d) + [full(shift), full(scale), full(g)],
        out_specs=pl.BlockSpec((tm, d), lambda i: (i, 0)),
        out_shape=jax.ShapeDtypeStruct((cfg.n_rows, d), BF16),
        compiler_params=_params("parallel"),
        name="prenorm",
    )(x_p, x_s, shift, scale, g)


def _inproj_kernel(h_ref, w_ref, wn_ref, o_ref, wb_s, *, first_shifted, shift, tn, n_features):
    j = pl.program_id(0)
    new_tile = pl.program_id(1) == 0

    @pl.when(jnp.logical_and(new_tile, j < first_shifted))
    def _():
        wb_s[...] = w_ref[...].astype(BF16)

    @pl.when(jnp.logical_and(new_tile, j >= first_shifted))
    def _():
        w = jnp.concatenate([w_ref[...], wn_ref[...]], axis=0)[shift:shift + tn, :]
        feature = j * tn + shift + lax.broadcasted_iota(jnp.int32, w.shape, 0)
        wb_s[...] = jnp.where(feature < n_features, w, 0.0).astype(BF16)

    o_ref[...] = _nt(h_ref[...], wb_s[...])


def _inproj(h, wt, cfg, tm=1024, tn=512):
    n, d = h.shape
    nout = cfg.o_rw + cfg.win
    assert cfg.o_rw % tn == 0 and nout % tn == 0 and n % tm == 0 and cfg.mh <= LANES and cfg.mh % 8 == 0
    last_next = (wt.shape[0] - 1) // LANES
    return pl.pallas_call(
        functools.partial(_inproj_kernel, first_shifted=cfg.o_rw // tn, shift=cfg.mh, tn=tn, n_features=wt.shape[0]),
        grid=(nout // tn, n // tm),
        in_specs=[pl.BlockSpec((tm, d), lambda j, i: (i, 0)),
                  pl.BlockSpec((tn, d), lambda j, i: (j, 0)),
                  pl.BlockSpec((LANES, d), lambda j, i: (jnp.minimum((j + 1) * (tn // LANES), last_next), 0))],
        out_specs=pl.BlockSpec((tm, tn), lambda j, i: (i, j)),
        out_shape=jax.ShapeDtypeStruct((n, nout), F32),
        scratch_shapes=[pltpu.VMEM((tn, d), BF16)],
        compiler_params=_params("arbitrary", "arbitrary"),
        name="inproj",
    )(h, wt, wt)


def _dtproj_kernel(h_ref, w_ref, o_ref):
    o_ref[...] = _nt(h_ref[...], w_ref[...].astype(BF16))


def _dtproj(h, wt, cfg, tm=1024):
    n, d = h.shape
    return pl.pallas_call(
        _dtproj_kernel,
        grid=(n // tm,),
        in_specs=[pl.BlockSpec((tm, d), lambda i: (i, 0)), pl.BlockSpec((LANES, d), lambda i: (cfg.o_rw // LANES, 0))],
        out_specs=pl.BlockSpec((tm, LANES), lambda i: (i, 0)),
        out_shape=jax.ShapeDtypeStruct((n, LANES), F32),
        compiler_params=_params("parallel"),
        name="dtproj",
    )(h, wt)


def _mamba_init(cst_ref, sst_ref, ssm_ref, buf):
    buf[0:8, :] = cst_ref[0]
    ssm_ref[0] = sst_ref[0]


def _mamba_kernel(zx_ref, dt_ref, cst_ref, sst_ref, cw_ref, cb_ref, dtb_ref, alog_ref, dsk_ref, ng_ref, exp_ref,
                  ym_ref, ssm_ref, buf, u_scr, *, cfg, init=True):
    mi, nb, cd = cfg.mi, cfg.nb, cfg.cd
    L = CHUNK
    if init:
        pl.when(_is_first_chunk(pl.program_id(0), cfg))(lambda: _mamba_init(cst_ref, sst_ref, ssm_ref, buf))

    xbc = zx_ref[:, mi:mi + cd]
    buf[8:8 + L, :] = xbc
    conv = (cb_ref[...] + buf[5:5 + L, :] * cw_ref[0:1, :] + buf[6:6 + L, :] * cw_ref[1:2, :]
            + buf[7:7 + L, :] * cw_ref[2:3, :] + xbc * cw_ref[3:4, :])
    buf[0:8, :] = buf[L:L + 8, :]
    u_scr[...] = _silu(conv)

    dt = jax.nn.softplus(dt_ref[...] + dtb_ref[...])
    da = dt * (-jnp.exp(alog_ref[...]))
    row = lax.broadcasted_iota(jnp.int32, (L, mi), 0)
    pos = lax.broadcasted_iota(jnp.int32, (L, mi), 1) % HEAD
    tri2 = (lax.broadcasted_iota(jnp.int32, (L, 2 * L), 0)
            >= lax.broadcasted_iota(jnp.int32, (L, 2 * L), 1) % L).astype(BF16)
    lane = lax.broadcasted_iota(jnp.int32, (L, PAIR), 1)
    row16 = lax.broadcasted_iota(jnp.int32, (16, mi), 0)
    ones16 = jnp.ones((16, LANES), BF16)

    dt_hl = jnp.concatenate(_split_hi_lo(dt), axis=1)
    da_hl = jnp.concatenate(_split_hi_lo(da), axis=1)
    dte = jnp.dot(dt_hl, exp_ref[...], preferred_element_type=F32)
    dae = jnp.dot(da_hl, exp_ref[...], preferred_element_type=F32)
    cum = jnp.dot(tri2, jnp.concatenate(_split_hi_lo(dae), axis=0), preferred_element_type=F32)
    cum_row = jnp.sum(jnp.where(row == pos, cum, 0.0), axis=0, keepdims=True)
    last = cum[L - 1:L, :]
    decay = jnp.exp(jnp.where(row >= pos, cum - cum_row, -jnp.inf))
    xdt = u_scr[:, 0:mi] * dte
    x_end = xdt * jnp.exp(last - cum)
    e_cum = jnp.exp(cum)
    e_last = jnp.exp(last)
    e_hi = e_last.astype(BF16).astype(F32)
    e_rows = jnp.where(row16 == 0, e_hi, jnp.where(row16 == 1, e_last - e_hi, 0.0)).astype(BF16)

    groups = range(cfg.groups)
    gs = lambda g: slice(g * SSD_GROUP, (g + 1) * SSD_GROUP)
    b_of = lambda g: u_scr[:, mi + g * LANES:mi + (g + 1) * LANES]
    c_of = lambda g: u_scr[:, mi + nb + g * LANES:mi + nb + (g + 1) * LANES]
    sc2 = [_nt(c_of(g), jnp.concatenate([b_of(g), b_of(g)], axis=0)) for g in groups]
    y_diag = []
    for g in groups:
        ys = []
        for q in range(SSD_GROUP // PAIR):
            ps = slice(g * SSD_GROUP + q * PAIR, g * SSD_GROUP + (q + 1) * PAIR)
            xq = xdt[:, ps]
            rhs = jnp.concatenate([jnp.where(lane < HEAD, xq, 0.0), jnp.where(lane >= HEAD, xq, 0.0)], axis=0)
            ys.append(jnp.dot(decay[:, ps] * sc2[g], rhs, preferred_element_type=F32))
        y_diag.append(jnp.concatenate(ys, axis=1))
    y_off = [_nt(c_of(g), ssm_ref[0, gs(g), :]) for g in groups]
    new = [_tn(x_end[:, gs(g)], b_of(g)) for g in groups]
    dcol = [_tn(e_rows[:, gs(g)], ones16) for g in groups]
    for g in groups:
        cs = gs(g)
        ssm_ref[0, cs, :] = ssm_ref[0, cs, :] * dcol[g] + new[g]
        y = y_diag[g] + y_off[g] * e_cum[:, cs] + u_scr[:, cs] * dsk_ref[:, cs]
        y = y * _silu(zx_ref[:, cs])
        y = y * lax.rsqrt(jnp.mean(y * y, axis=-1, keepdims=True) + M_NORM_EPS)
        ym_ref[:, cs] = (y * ng_ref[:, cs]).astype(BF16)


def _mamba_plan(proj, dt_raw, conv_state8, ssm_state, cw, cb, dtb, alog, dsk, ng, expand, cfg):
    n = proj.shape[0]
    mi, cd = cfg.mi, cfg.cd
    nst = ssm_state.shape[-1]
    seq = lambda c: _seq_of_chunk(c, cfg)
    full = lambda a: pl.BlockSpec(a.shape, lambda c: (0,) * a.ndim)
    return _Plan(
        operands=(proj, dt_raw, conv_state8, ssm_state, cw, cb, dtb, alog, dsk, ng, expand),
        in_specs=[pl.BlockSpec((CHUNK, mi + cd), lambda c: (c, 0)),
                  pl.BlockSpec((CHUNK, LANES), lambda c: (c, 0)),
                  pl.BlockSpec((1, 8, cd), lambda c: (seq(c), 0, 0)),
                  pl.BlockSpec((1, mi, nst), lambda c: (seq(c), 0, 0)),
                  full(cw), full(cb), full(dtb), full(alog), full(dsk), full(ng), full(expand)],
        out_specs=[pl.BlockSpec((CHUNK, mi), lambda c: (c, 0)),
                   pl.BlockSpec((1, mi, nst), lambda c: (seq(c), 0, 0))],
        out_shape=[jax.ShapeDtypeStruct((n, mi), BF16),
                   jax.ShapeDtypeStruct((cfg.n_seq, mi, nst), F32)],
        scratch=[pltpu.VMEM((CHUNK + 8, cd), F32), pltpu.VMEM((CHUNK, cd), F32)])


def _seg_sum(x, ob):
    w = ob.shape[0]
    xb = x.astype(BF16)
    parts = [jnp.dot(xb[:, p * w:(p + 1) * w], ob, preferred_element_type=F32) for p in range(x.shape[1] // w)]
    return jnp.concatenate(parts, axis=1)


def _split_hi_lo(x):
    hi = x.astype(BF16)
    lo = (x - hi.astype(F32)).astype(BF16)
    return hi, lo


def _rwkv_init(sh_ref, wkv_in_ref, wkv_ref, pbuf, lbuf, cfg):
    pbuf[0:8, :] = sh_ref[0, :, 0:3 * cfg.rw]
    lbuf[0:8, :] = sh_ref[0, :, 3 * cfg.rw:]
    wkv_ref[0] = wkv_in_ref[0]


def _rwkv_kernel(rkv_ref, lora_ref, sh_ref, wkv_in_ref, mu_ref, w0_ref, wup_ref, a0_ref, aup_ref, gup_ref,
                 kk_ref, ka_ref, rk_ref, lng_ref, lnb_ref,
                 yr_ref, wkv_ref, pbuf, lbuf, w_s, a_s, b_s, k_s, y_s, vt_s, ut_s, y0_s, g_s, *, cfg, init=True):
    rw = cfg.rw
    L = CHUNK
    npair = rw // PAIR
    if init:
        pl.when(_is_first_chunk(pl.program_id(0), cfg))(
            lambda: _rwkv_init(sh_ref, wkv_in_ref, wkv_ref, pbuf, lbuf, cfg))

    pr = rkv_ref[...]
    pl_ = lora_ref[...]
    pbuf[8:8 + L, :] = pr
    lbuf[8:8 + L, :] = pl_
    xs = pr + (pbuf[7:7 + L, :] - pr) * mu_ref[:, 0:3 * rw]
    xl = pl_ + (lbuf[7:7 + L, :] - pl_) * mu_ref[:, 3 * rw:]
    pbuf[0:8, :] = pbuf[L:L + 8, :]
    lbuf[0:8, :] = lbuf[L:L + 8, :]

    r = xs[:, 0:rw]
    k = xs[:, rw:2 * rw]
    v = xs[:, 2 * rw:3 * rw]
    wd = xl[:, 0:LANES]
    ad = xl[:, cfg.lw:cfg.lw + LANES]
    gd = xl[:, cfg.lw + cfg.la:cfg.lw + cfg.la + cfg.lg]

    lane = lax.broadcasted_iota(jnp.int32, (PAIR, PAIR), 1)
    rowi = lax.broadcasted_iota(jnp.int32, (PAIR, PAIR), 0)
    o_row = lax.broadcasted_iota(jnp.int32, (2 * PAIR, 2 * PAIR), 0)
    o_lane = lax.broadcasted_iota(jnp.int32, (2 * PAIR, 2 * PAIR), 1)
    ob = ((o_row // HEAD) == (o_lane // HEAD)).astype(BF16)

    w_z = w0_ref[...] + jnp.dot(jnp.tanh(wd), wup_ref[...], preferred_element_type=F32)
    logw = -(jax.nn.sigmoid(w_z) * EXP_NEG_HALF)
    w_s[...] = jnp.exp(logw)
    a = jax.nn.sigmoid(a0_ref[...] + jnp.dot(ad, aup_ref[...], preferred_element_type=F32))
    kk = k * kk_ref[...]
    kk = kk * lax.rsqrt(jnp.maximum(_seg_sum(kk * kk, ob), 1e-24))
    kh = k * (1.0 + (a - 1.0) * ka_ref[...])
    a_s[...] = -kk
    b_s[...] = kk * a
    k_s[...] = kh

    tri2 = (lax.broadcasted_iota(jnp.int32, (L, 2 * L), 0)
            >= lax.broadcasted_iota(jnp.int32, (L, 2 * L), 1) % L).astype(BF16)
    cum = jnp.dot(tri2, jnp.concatenate(_split_hi_lo(logw), axis=0), preferred_element_type=F32)
    p_inv = jnp.exp(-cum)
    r_p = r * jnp.exp(cum)
    b_p = b_s[...] * p_inv
    k_p = kh * p_inv
    lo_lanes = lane < HEAD
    g_row = lax.broadcasted_iota(jnp.int32, (2 * PAIR, PAIR), 0)
    g_lane = lax.broadcasted_iota(jnp.int32, (2 * PAIR, PAIR), 1)
    g_keep = (g_row % HEAD) <= (g_lane % HEAD)

    def by_head(x):
        x2 = jnp.concatenate([x, x], axis=0)
        return jnp.where(lo_lanes == (rowi < HEAD), x2, 0.0).astype(BF16)

    pairs = range(npair)
    lanes_of = lambda p: slice(p * PAIR, (p + 1) * PAIR)
    rows_of = lambda p: slice(p * HEAD, (p + 1) * HEAD)
    r_bd = [by_head(r_p[:, lanes_of(p)]) for p in pairs]
    y0 = [_nt(wkv_ref[0, p].astype(BF16), r_bd[p]) for p in pairs]
    gram = [_nt(jnp.concatenate([by_head(b_p[:, lanes_of(p)]), by_head(k_p[:, lanes_of(p)])], axis=0), r_bd[p])
            for p in pairs]
    vts = [v[:, lanes_of(p)].T for p in pairs]
    for p in pairs:
        y0_s[rows_of(p), :] = y0[p]
        g_s[p] = jnp.where(g_keep, gram[p], 0.0).astype(BF16)
        vjt = jnp.concatenate([vts[p][0:HEAD], vts[p][HEAD:]], axis=1)
        hi = vjt.astype(BF16).astype(F32)
        vt_s[rows_of(p), :] = jnp.concatenate([hi, vjt - hi], axis=1).astype(BF16)

    lane_t = lax.broadcasted_iota(jnp.int32, (HEAD, PAIR), 1) % HEAD
    crow = lax.broadcasted_iota(jnp.int32, (2 * PAIR, 2 * PAIR), 0)
    clane = lax.broadcasted_iota(jnp.int32, (2 * PAIR, 2 * PAIR), 1)
    same_head = ((crow % PAIR) // HEAD) == ((clane % PAIR) // HEAD)
    c_dt = (crow % HEAD) - clane // PAIR

    def steps(blk, carry):
        t0 = pl.multiple_of(blk * SCAN_STEPS, SCAN_STEPS)
        rows = pl.ds(t0, SCAN_STEPS)
        w8, a8, b8, k8 = w_s[rows, :], a_s[rows, :], b_s[rows, :], k_s[rows, :]
        s = [wkv_ref[0, p] for p in range(npair)]
        vcol2 = None
        for i in range(SCAN_STEPS):
            row = lambda x8, p: x8[i:i + 1, p * PAIR:(p + 1) * PAIR]
            this_step = lane_t == t0 + i
            if i % 2 == 0:
                sel = (same_head & (c_dt == t0 + i)).astype(BF16)
                vcol2 = jnp.dot(vt_s[...], sel, preferred_element_type=F32)
            vcol = vcol2[:, (i % 2) * PAIR:(i % 2 + 1) * PAIR]
            for p0 in range(0, npair, SA_GROUP):
                group = range(p0, min(p0 + SA_GROUP, npair))
                lhs = []
                for p in group:
                    prod = (s[p] * row(a8, p)).astype(BF16)
                    lhs.append(jnp.concatenate([prod[0:HEAD // 2], prod[HEAD // 2:]], axis=1))
                sa = jnp.dot(jnp.concatenate(lhs, axis=0), ob, preferred_element_type=F32)
                for q, p in enumerate(group):
                    half = sa[q * (HEAD // 2):(q + 1) * (HEAD // 2)]
                    sa_p = jnp.concatenate([half[:, 0:PAIR], half[:, PAIR:]], axis=0)
                    s[p] = s[p] * row(w8, p) + sa_p * row(b8, p) + vcol[p * HEAD:(p + 1) * HEAD] * row(k8, p)
                    pltpu.store(ut_s.at[p * HEAD:(p + 1) * HEAD, :], sa_p, mask=this_step)
        for p in range(npair):
            wkv_ref[0, p] = s[p]
        return carry

    ut_s[...] = jnp.zeros_like(ut_s)
    lax.fori_loop(0, L // SCAN_STEPS, steps, 0)

    yts = [jnp.dot(jnp.concatenate([ut_s[rows_of(p), :].astype(BF16), vt_s[rows_of(p), 0:PAIR]], axis=1), g_s[p],
                   preferred_element_type=F32) for p in pairs]
    ytt = [(y0_s[rows_of(p), :] + yts[p]).T for p in pairs]
    for p in pairs:
        y_s[:, lanes_of(p)] = jnp.concatenate([ytt[p][0:L], ytt[p][L:]], axis=1)

    y = y_s[...]
    mu = _seg_sum(y, ob) * (1.0 / HEAD)
    dlt = y - mu
    var = _seg_sum(dlt * dlt, ob) * (1.0 / HEAD)
    yn = dlt * lax.rsqrt(var + R_LN_EPS) * lng_ref[...] + lnb_ref[...]
    bonus = _seg_sum(r * k_s[...] * rk_ref[...], ob) * v
    g = jnp.dot(jax.nn.sigmoid(gd), gup_ref[...], preferred_element_type=F32)
    yr_ref[...] = ((yn + bonus) * g).astype(BF16)


class _Plan(NamedTuple):
    operands: tuple
    in_specs: list
    out_specs: list
    out_shape: list
    scratch: list


def _rwkv_plan(proj, shift8, wkv_pairs, mu, w0, wup, a0, aup, gup, kk, ka, rk, lng, lnb, cfg):
    n = proj.shape[0]
    rw, lp = cfg.rw, cfg.tail
    assert cfg.o_rw % (3 * rw) == 0 and (cfg.o_rw + 3 * rw) % lp == 0
    npair = rw // PAIR
    seq = lambda c: _seq_of_chunk(c, cfg)
    full = lambda a: pl.BlockSpec(a.shape, lambda c: (0,) * a.ndim)
    row_scr = pltpu.VMEM((CHUNK, rw), F32)
    return _Plan(
        operands=(proj, proj, shift8, wkv_pairs, mu, w0, wup, a0, aup, gup, kk, ka, rk, lng, lnb),
        in_specs=[pl.BlockSpec((CHUNK, 3 * rw), lambda c: (c, cfg.o_rw // (3 * rw))),
                  pl.BlockSpec((CHUNK, lp), lambda c: (c, (cfg.o_rw + 3 * rw) // lp)),
                  pl.BlockSpec((1, 8, 3 * rw + lp), lambda c: (seq(c), 0, 0)),
                  pl.BlockSpec((1, npair, HEAD, PAIR), lambda c: (seq(c), 0, 0, 0)),
                  full(mu), full(w0), full(wup), full(a0), full(aup), full(gup),
                  full(kk), full(ka), full(rk), full(lng), full(lnb)],
        out_specs=[pl.BlockSpec((CHUNK, rw), lambda c: (c, 0)),
                   pl.BlockSpec((1, npair, HEAD, PAIR), lambda c: (seq(c), 0, 0, 0))],
        out_shape=[jax.ShapeDtypeStruct((n, rw), BF16),
                   jax.ShapeDtypeStruct((cfg.n_seq, npair, HEAD, PAIR), F32)],
        scratch=[pltpu.VMEM((CHUNK + 8, 3 * rw), F32), pltpu.VMEM((CHUNK + 8, lp), F32),
                 row_scr, row_scr, row_scr, row_scr, row_scr,
                 pltpu.VMEM((npair * HEAD, 2 * PAIR), BF16), pltpu.VMEM((npair * HEAD, PAIR), F32),
                 pltpu.VMEM((npair * HEAD, PAIR), F32), pltpu.VMEM((npair, 2 * PAIR, PAIR), BF16)])


def _mixers_kernel(*refs, cfg, split):
    m_in, r_in, m_out, r_out, m_scr, r_scr = [refs[a:b] for a, b in zip(split[:-1], split[1:])]

    @pl.when(_is_first_chunk(pl.program_id(0), cfg))
    def _():
        _mamba_init(m_in[2], m_in[3], m_out[1], m_scr[0])
        _rwkv_init(r_in[2], r_in[3], r_out[1], r_scr[0], r_scr[1], cfg)

    _mamba_kernel(*m_in, *m_out, *m_scr, cfg=cfg, init=False)
    _rwkv_kernel(*r_in, *r_out, *r_scr, cfg=cfg, init=False)


def _mixers(mamba, rwkv, cfg):
    sizes = [len(mamba.operands), len(rwkv.operands), len(mamba.out_specs), len(rwkv.out_specs),
             len(mamba.scratch), len(rwkv.scratch)]
    split = [sum(sizes[:i]) for i in range(len(sizes) + 1)]
    return pl.pallas_call(
        functools.partial(_mixers_kernel, cfg=cfg, split=split),
        grid=(cfg.n_chunks,),
        in_specs=mamba.in_specs + rwkv.in_specs,
        out_specs=mamba.out_specs + rwkv.out_specs,
        out_shape=mamba.out_shape + rwkv.out_shape,
        scratch_shapes=mamba.scratch + rwkv.scratch,
        compiler_params=_params("arbitrary"),
        name="mixers",
    )(*mamba.operands, *rwkv.operands)


def _outproj_kernel(ym_ref, yr_ref, wa_ref, wb_ref, xp_ref, xs_ref, gate_ref, o_ref, *, cfg, tm):
    i = pl.program_id(0)
    acc = (jnp.dot(ym_ref[...], wa_ref[...], preferred_element_type=F32)
           + jnp.dot(yr_ref[...], wb_ref[...], preferred_element_type=F32))

    def residual(x_ref):
        for s in range(tm // CHUNK):
            seq = _seq_of_chunk(i * (tm // CHUNK) + s, cfg)
            rows = slice(s * CHUNK, (s + 1) * CHUNK)
            o_ref[rows, :] = x_ref[rows, :] + gate_ref[pl.ds(seq, 1), :] * acc[rows, :]

    npt = _prompt_tiles(cfg, tm)
    pl.when(i < npt)(lambda: residual(xp_ref))
    pl.when(i >= npt)(lambda: residual(xs_ref))


def _outproj(ym, yr, wa, wb, x_p, x_s, gate, cfg, tm=512, tn=1024):
    n, d = cfg.n_rows, x_p.shape[1]
    ka, kb = ym.shape[1], yr.shape[1]
    s = gate.shape[0]
    return pl.pallas_call(
        functools.partial(_outproj_kernel, cfg=cfg, tm=tm),
        grid=(n // tm, d // tn),
        in_specs=[pl.BlockSpec((tm, ka), lambda i, j: (i, 0)),
                  pl.BlockSpec((tm, kb), lambda i, j: (i, 0)),
                  pl.BlockSpec((ka, tn), lambda i, j: (0, j)),
                  pl.BlockSpec((kb, tn), lambda i, j: (0, j))]
                 + _row_specs(cfg, tm, tn, lambda j: j)
                 + [pl.BlockSpec((s, tn), lambda i, j: (0, j))],
        out_specs=pl.BlockSpec((tm, tn), lambda i, j: (i, j)),
        out_shape=jax.ShapeDtypeStruct((n, d), F32),
        compiler_params=_params("parallel", "parallel"),
        name="outproj",
    )(ym, yr, wa, wb, x_p, x_s, gate)


def _route(lg, n_experts):
    lane = lax.broadcasted_iota(jnp.int32, lg.shape, 1)
    first = lambda mask: jnp.min(jnp.where(mask, lane, LANES), axis=-1, keepdims=True)
    is_grp = lane < E_GROUPS
    gmax = jnp.max(jnp.where(is_grp, lg, -jnp.inf), axis=-1, keepdims=True)
    gsel = first(is_grp & (lg == gmax))
    pg = 1.0 / jnp.sum(jnp.where(is_grp, jnp.exp(lg - gmax), 0.0), axis=-1, keepdims=True)
    e_lane = lane - E_GROUPS
    in_grp = (e_lane >= 0) & (e_lane < n_experts) & (e_lane // E_PER_GROUP == gsel)
    emax = jnp.max(jnp.where(in_grp, lg, -jnp.inf), axis=-1, keepdims=True)
    p = jnp.where(in_grp, jnp.exp(lg - emax), 0.0)
    eprob = p / jnp.sum(p, axis=-1, keepdims=True)
    v1 = jnp.max(jnp.where(in_grp, eprob, -1.0), axis=-1, keepdims=True)
    i1 = first(in_grp & (eprob == v1))
    rest = in_grp & (lane != i1)
    v2 = jnp.max(jnp.where(rest, eprob, -1.0), axis=-1, keepdims=True)
    i2 = first(rest & (eprob == v2))
    tot = v1 + v2
    cols = [(i1 - E_GROUPS).astype(F32), (i2 - E_GROUPS).astype(F32), v1 / tot * pg, v2 / tot * pg]
    out = jnp.zeros(lg.shape, F32)
    for c, val in enumerate(cols):
        out = jnp.where(lane == c, val, out)
    return out


def _router_kernel(x_ref, shift_ref, scale_ref, g_ref, wh_ref, wl_ref, br_ref, h_ref, rt_ref, *, cfg, tm, n_experts):
    i = pl.program_id(0)
    for s in range(tm // CHUNK):
        seq = _seq_of_chunk(i * (tm // CHUNK) + s, cfg)
        rows = slice(s * CHUNK, (s + 1) * CHUNK)
        h = _modulated_norm(x_ref[rows, :], g_ref[...], scale_ref[pl.ds(seq, 1), :], shift_ref[pl.ds(seq, 1), :])
        h_hi, h_lo = _split_hi_lo(h)
        h_ref[rows, :] = h_hi
        dot = lambda a, b: jnp.dot(a, b[...], preferred_element_type=F32)
        logits = dot(h_hi, wh_ref) + (dot(h_hi, wl_ref) + dot(h_lo, wh_ref)) + br_ref[...]
        rt_ref[rows, :] = _route(logits, n_experts)


def _router(x1, shift, scale, g, wr, br, cfg, n_experts, tm=256):
    n, d = x1.shape
    assert E_GROUPS + n_experts <= LANES and n_experts == E_GROUPS * E_PER_GROUP
    full = lambda a: pl.BlockSpec(a.shape, lambda i: (0,) * a.ndim)
    w_hi, w_lo = _split_hi_lo(wr)
    return pl.pallas_call(
        functools.partial(_router_kernel, cfg=cfg, tm=tm, n_experts=n_experts),
        grid=(n // tm,),
        in_specs=[pl.BlockSpec((tm, d), lambda i: (i, 0)), full(shift), full(scale), full(g),
                  full(w_hi), full(w_lo), full(br)],
        out_specs=[pl.BlockSpec((tm, d), lambda i: (i, 0)), pl.BlockSpec((tm, LANES), lambda i: (i, 0))],
        out_shape=[jax.ShapeDtypeStruct((n, d), BF16), jax.ShapeDtypeStruct((n, LANES), F32)],
        compiler_params=_params("parallel"),
        name="router",
    )(x1, shift, scale, g, w_hi, w_lo, br)


def _new_weights(be_ref):
    i = pl.program_id(1)
    return jnp.logical_or(i == 0, be_ref[i] != be_ref[jnp.maximum(i - 1, 0)])


def _for_valid_rows(nv_ref, out_ref, compute):
    nv = nv_ref[pl.program_id(1)]
    for s in range(out_ref.shape[0] // E_SUB):
        rows = slice(s * E_SUB, (s + 1) * E_SUB)

        @pl.when(nv > s * E_SUB)
        def _():
            out_ref[rows, :] = compute(rows)

        @pl.when(nv <= s * E_SUB)
        def _():
            out_ref[rows, :] = jnp.zeros((E_SUB, out_ref.shape[1]), out_ref.dtype)


def _expert_up_kernel(be_ref, nv_ref, src_ref, x_ref, wg_ref, wu_ref, h_ref, wg_s, wu_s):
    @pl.when(_new_weights(be_ref))
    def _():
        wg_s[...] = wg_ref[0].astype(BF16)
        wu_s[...] = wu_ref[0].astype(BF16)

    def hidden(rows):
        x = x_ref[rows, :]
        gate = jnp.dot(x, wg_s[...], preferred_element_type=F32)
        up = jnp.dot(x, wu_s[...], preferred_element_type=F32)
        return (_silu(gate) * up).astype(BF16)

    _for_valid_rows(nv_ref, h_ref, hidden)


def _expert_down_kernel(be_ref, nv_ref, src_ref, h_ref, wd_ref, y_ref, wd_s):
    @pl.when(_new_weights(be_ref))
    def _():
        wd_s[...] = wd_ref[0].astype(BF16)

    _for_valid_rows(nv_ref, y_ref, lambda rows: jnp.dot(h_ref[rows, :], wd_s[...], preferred_element_type=F32))


def _experts(xb, blk_e, n_valid, src_blk, wg, wu, wd, tm, tf=512, tn=2048):
    rows, d = xb.shape
    f = wg.shape[2]
    nblk = rows // tm
    tf, tn = min(tf, f), min(tn, d)
    assert tm % E_SUB == 0 and f % tf == 0 and d % tn == 0
    hid = pl.pallas_call(
        _expert_up_kernel,
        grid_spec=pltpu.PrefetchScalarGridSpec(
            num_scalar_prefetch=3, grid=(f // tf, nblk),
            in_specs=[pl.BlockSpec((tm, d), lambda j, i, be, nv, src: (src[i], 0)),
                      pl.BlockSpec((1, d, tf), lambda j, i, be, nv, src: (be[i], 0, j)),
                      pl.BlockSpec((1, d, tf), lambda j, i, be, nv, src: (be[i], 0, j))],
            out_specs=pl.BlockSpec((tm, tf), lambda j, i, be, nv, src: (i, j)),
            scratch_shapes=[pltpu.VMEM((d, tf), BF16), pltpu.VMEM((d, tf), BF16)]),
        out_shape=jax.ShapeDtypeStruct((rows, f), BF16),
        compiler_params=_params("arbitrary", "arbitrary"),
        name="expert_up",
    )(blk_e, n_valid, src_blk, xb, wg, wu)
    return pl.pallas_call(
        _expert_down_kernel,
        grid_spec=pltpu.PrefetchScalarGridSpec(
            num_scalar_prefetch=3, grid=(d // tn, nblk),
            in_specs=[pl.BlockSpec((tm, f), lambda j, i, be, nv, src: (src[i], 0)),
                      pl.BlockSpec((1, f, tn), lambda j, i, be, nv, src: (be[i], 0, j))],
            out_specs=pl.BlockSpec((tm, tn), lambda j, i, be, nv, src: (i, j)),
            scratch_shapes=[pltpu.VMEM((f, tn), BF16)]),
        out_shape=jax.ShapeDtypeStruct((rows, d), F32),
        compiler_params=_params("arbitrary", "arbitrary"),
        name="expert_down",
    )(blk_e, n_valid, src_blk, hid, wd)


def _final_kernel(pos_ref, x_ref, rt_ref, gate_ref, g_ref, y_hbm, o_ref, ybuf, sem, *, cfg, tm, tile0, ntiles):
    i = pl.program_id(0)

    def row_copy(pos, slot, k, r):
        return pltpu.make_async_copy(y_hbm.at[pl.ds(pos, 1), :], ybuf.at[slot, k, pl.ds(r, 1), :], sem.at[slot])

    def fetch(tile, slot):
        base = (tile + tile0) * (tm * TOP_K)

        def body(r, c):
            for k in range(TOP_K):
                row_copy(pos_ref[base + r * TOP_K + k], slot, k, r).start()
            return c

        lax.fori_loop(0, tm, body, 0, unroll=8)

    def wait_all(slot):
        def body(r, c):
            for k in range(TOP_K):
                row_copy(0, slot, k, r).wait()
            return c

        lax.fori_loop(0, tm, body, 0, unroll=8)

    pl.when(i == 0)(lambda: fetch(0, 0))
    pl.when(i + 1 < ntiles)(lambda: fetch(i + 1, (i + 1) % 2))
    slot = i % 2
    wait_all(slot)

    for s in range(tm // CHUNK):
        seq = _seq_of_chunk((i + tile0) * (tm // CHUNK) + s, cfg)
        rows = slice(s * CHUNK, (s + 1) * CHUNK)
        rt = rt_ref[rows, :]
        moe = ybuf[slot, 0, rows, :] * rt[:, TOP_K:TOP_K + 1]
        for k in range(1, TOP_K):
            moe = moe + ybuf[slot, k, rows, :] * rt[:, TOP_K + k:TOP_K + k + 1]
        x = x_ref[rows, :] + gate_ref[pl.ds(seq, 1), :] * moe
        o_ref[rows, :] = x * lax.rsqrt(jnp.mean(x * x, axis=-1, keepdims=True) + NORM_EPS) * g_ref[...]


def _final(x1, yb, slot_pos, route, gate, g, cfg, row0, nrows, tm=256):
    d = x1.shape[1]
    assert row0 % tm == 0 and nrows % tm == 0
    tile0 = row0 // tm
    ntiles = nrows // tm
    full = lambda a: pl.BlockSpec(a.shape, lambda i, pos: (0,) * a.ndim)
    return pl.pallas_call(
        functools.partial(_final_kernel, cfg=cfg, tm=tm, tile0=tile0, ntiles=ntiles),
        grid_spec=pltpu.PrefetchScalarGridSpec(
            num_scalar_prefetch=1, grid=(ntiles,),
            in_specs=[pl.BlockSpec((tm, d), lambda i, pos: (i + tile0, 0)),
                      pl.BlockSpec((tm, LANES), lambda i, pos: (i + tile0, 0)),
                      full(gate), full(g), pl.BlockSpec(memory_space=pl.ANY)],
            out_specs=pl.BlockSpec((tm, d), lambda i, pos: (i, 0)),
            scratch_shapes=[pltpu.VMEM((2, TOP_K, tm, d), F32), pltpu.SemaphoreType.DMA((2,))]),
        out_shape=jax.ShapeDtypeStruct((nrows, d), F32),
        compiler_params=_params("arbitrary"),
        name="final_norm",
    )(slot_pos.reshape(-1), x1, route, gate, g, yb)


def _dispatch(route, n_experts, tm):
    n = route.shape[0]
    eid = route[:, :TOP_K].astype(jnp.int32)

    m = n * TOP_K
    e_flat = eid.reshape(m)
    order = jnp.argsort(e_flat).astype(jnp.int32)
    rank = jnp.argsort(order).astype(jnp.int32)
    e_sorted = e_flat[order]
    experts = jnp.arange(n_experts, dtype=jnp.int32)
    counts = jnp.sum(e_flat[:, None] == experts[None, :], axis=0, dtype=jnp.int32)
    start = jnp.cumsum(counts) - counts
    padded = (counts + tm - 1) // tm * tm
    pend = jnp.cumsum(padded)
    pstart = pend - padded
    dest_sorted = pstart[e_sorted] + jnp.arange(m, dtype=jnp.int32) - start[e_sorted]
    slot_pos = dest_sorted[rank].reshape(n, TOP_K)
    n_blocks = -(-(m + n_experts * (tm - 1)) // tm)
    rows = n_blocks * tm
    r = jnp.arange(rows, dtype=jnp.int32)
    owner = lambda row: jnp.minimum(jnp.sum(row[:, None] >= pend[None, :], axis=1, dtype=jnp.int32), n_experts - 1)
    e_r = owner(r)
    idx = r - pstart[e_r]
    row_tok = jnp.where(idx < counts[e_r], order[jnp.clip(start[e_r] + idx, 0, m - 1)] // TOP_K, r % n)
    n_used = pend[-1] // tm
    blk = jnp.arange(n_blocks, dtype=jnp.int32)
    blk_e = owner(blk * tm)
    n_valid = jnp.where(blk < n_used, jnp.clip(pstart[blk_e] + counts[blk_e] - blk * tm, 0, tm), 0).astype(jnp.int32)
    last_e = blk_e[jnp.maximum(n_used - 1, 0)]
    blk_e = jnp.where(blk < n_used, blk_e, last_e)
    src_blk = jnp.minimum(blk, jnp.maximum(n_used - 1, 0)).astype(jnp.int32)
    return row_tok, slot_pos, blk_e, n_valid, src_blk


def _to_window(a, cfg):
    return jnp.concatenate([a, jnp.zeros(a.shape[:-1] + (cfg.win - cfg.r_cols,), a.dtype)], axis=-1)


def _pad_rows(a, rows):
    return jnp.concatenate([a, jnp.zeros((rows - a.shape[0],) + a.shape[1:], a.dtype)], axis=0)


def kernel(x_prompt, x_sample, state_conv, state_ssm, state_shift, state_wkv, c_prompt, c_sample, norm1_g, w_mod, b_mod, w_in, conv_w, conv_b, dt_bias, a_log, d_skip, m_norm_g, shift_mu, w0, w_up, a0, a_up, g_up, k_k, k_a, r_k, ln_x_g, ln_x_b, w_out, norm2_g, w_grp, b_grp, w_erouter, b_erouter, e_gate, e_up, e_down, final_norm_g):
    assert w_mod.shape[0] == 1, "single-layer trunk"
    bp, tp, d = x_prompt.shape
    bs, ts, _ = x_sample.shape
    mi = m_norm_g.shape[-1]
    rw = w0.shape[-1]
    lw, la, lg = w_up.shape[1], a_up.shape[1], g_up.shape[1]
    cfg = Cfg(d, bp, tp, bs, ts, mi, rw, lw, la, lg)
    nc = cfg.nc
    assert tp % CHUNK == 0 and ts % CHUNK == 0 and tp >= CONV_W - 1 and ts >= CONV_W - 1
    assert w_in.shape[-1] == cfg.o_rw + cfg.mh + cfg.r_cols and cfg.o_rw % LANES == 0
    assert state_conv.shape[-1] == cfg.cd and lw <= LANES and la <= LANES and cfg.mh <= LANES
    n = cfg.n_rows
    n_seq = cfg.n_seq
    mh, rh = cfg.mh, cfg.rh
    npair = rw // PAIR
    n_experts = e_gate.shape[1]

    x_p = x_prompt.reshape(bp * tp, d)
    x_s = x_sample.reshape(bs * ts, d)
    c_all = jnp.concatenate([c_prompt, c_sample], axis=0)

    mod = _modulation(c_all, w_mod[0], b_mod[0])
    shift1, scale1, gate1, shift2, scale2, gate2 = [mod[:, i * d:(i + 1) * d] for i in range(6)]

    h1 = _prenorm(x_p, x_s, shift1, scale1, norm1_g, cfg)
    w_in_t = w_in[0].T
    proj = _inproj(h1, w_in_t, cfg)
    dt_raw = _dtproj(h1, w_in_t, cfg)

    zeros = lambda b, *s: jnp.zeros((b,) + s, F32)
    conv0 = jnp.concatenate([zeros(bp, CONV_W - 1, cfg.cd), state_conv[0]], axis=0)
    conv8 = jnp.concatenate([zeros(n_seq, 8 - (CONV_W - 1), cfg.cd), conv0], axis=1)
    nst = state_ssm.shape[-1]
    ssm0 = jnp.concatenate([zeros(bp, mi, nst), state_ssm[0].reshape(bs, mi, nst)], axis=0)
    lane_pad = lambda a: jnp.concatenate([a.reshape(1, -1), jnp.zeros((1, LANES - a.shape[-1]), F32)], axis=1)
    expand = ((jnp.arange(2 * LANES)[:, None] % LANES) == (jnp.arange(mi)[None, :] // HEAD)).astype(BF16)
    mamba = _mamba_plan(proj, dt_raw, conv8, ssm0, conv_w[0], conv_b[0].reshape(1, -1), lane_pad(dt_bias[0]),
                        lane_pad(a_log[0]), jnp.repeat(d_skip[0], HEAD).reshape(1, mi), m_norm_g[0].reshape(1, mi),
                        expand, cfg)

    sh0 = jnp.concatenate([zeros(bp, 1, state_shift.shape[-1]), state_shift[0]], axis=0)
    sh8 = jnp.concatenate([zeros(n_seq, 7, cfg.win), _to_window(sh0, cfg)], axis=1)
    to_pairs = lambda s: s.reshape(-1, npair, 2, HEAD, HEAD).transpose(0, 1, 3, 2, 4).reshape(-1, npair, HEAD, PAIR)
    from_pairs = lambda s: s.reshape(-1, npair, HEAD, 2, HEAD).transpose(0, 1, 3, 2, 4).reshape(-1, rh, HEAD, HEAD)
    wkv0 = jnp.concatenate([zeros(bp, npair, HEAD, PAIR), to_pairs(state_wkv[0])], axis=0)
    row = lambda a: a.reshape(1, -1)
    rwkv = _rwkv_plan(proj, sh8, wkv0, _to_window(row(shift_mu[0]), cfg), row(w0[0]),
                      _pad_rows(w_up[0], LANES), row(a0[0]), _pad_rows(a_up[0], LANES), g_up[0],
                      row(k_k[0]), row(k_a[0]), row(r_k[0]), row(ln_x_g[0]), row(ln_x_b[0]), cfg)
    ym, ssm_new, yr, wkv_new = _mixers(mamba, rwkv, cfg)

    wo = w_out[0].astype(BF16)
    x1 = _outproj(ym, yr, wo[:mi], wo[mi:], x_p, x_s, gate1, cfg)
    wr = jnp.concatenate([w_grp[0], w_erouter[0], jnp.zeros((d, LANES - E_GROUPS - n_experts), F32)], axis=1)
    br = lane_pad(jnp.concatenate([b_grp[0], b_erouter[0]]))
    h2, route = _router(x1, shift2, scale2, norm2_g, wr, br, cfg, n_experts)

    tm_e = 2 * E_SUB
    row_tok, slot_pos, blk_e, n_valid, src_blk = _dispatch(route, n_experts, tm_e)
    yb = _experts(h2[row_tok], blk_e, n_valid, src_blk, e_gate[0], e_up[0], e_down[0], tm_e)

    np_ = bp * tp
    fg = final_norm_g.reshape(1, d)
    y_prompt = _final(x1, yb, slot_pos, route, gate2, fg, cfg, 0, np_).reshape(bp, tp, d)
    y_sample = _final(x1, yb, slot_pos, route, gate2, fg, cfg, np_, bs * ts).reshape(bs, ts, d)
    tails = lambda b, t, base: jnp.stack(
        [lax.slice(proj, (base + (i + 1) * t - (CONV_W - 1), 0), (base + (i + 1) * t, nc)) for i in range(b)])
    pp = tails(bp, tp, 0)
    ps = tails(bs, ts, np_)
    conv_of = lambda p: p[:, :, mi:mi + cfg.cd][None]
    shift_of = lambda p: p[:, -1:, cfg.o_rw:cfg.o_rw + cfg.r_cols][None]
    ssm_new = ssm_new.reshape(n_seq, mh, HEAD, nst)
    wkv_new = from_pairs(wkv_new)
    return (y_prompt, y_sample,
            conv_of(pp), ssm_new[:bp][None], shift_of(pp), wkv_new[:bp][None],
            conv_of(ps), ssm_new[bp:][None], shift_of(ps), wkv_new[bp:][None])
```
